```python
import math
import jax
import jax.numpy as jnp
from jax import lax
import numpy as np

D_MODEL = 4096
BATCH = 4
SEQ = 4096
DEPTH = 2

GRID_W = 64
CTX_LEN = 256

N_BRANCH = 4
MIX_WIDTH = D_MODEL // 4
HEAD_DIM = 128

A_HEADS = MIX_WIDTH // HEAD_DIM
A_QK_DIM = HEAD_DIM // 2
A_V_DIM = HEAD_DIM
B_HEADS = MIX_WIDTH // HEAD_DIM
B_Q_RANK = 3 * D_MODEL // 16
B_KV_RANK = D_MODEL // 8
B_NOPE = 128
B_ROPE = 64
B_V = 128
C_HEADS = MIX_WIDTH // HEAD_DIM
C_EXPAND = 128
C_V = MIX_WIDTH // C_HEADS
C_CHUNK = 64
D_HEADS = MIX_WIDTH // HEAD_DIM
D_HEAD = HEAD_DIM
NA_ROWS = 8
NA_COLS = 16
D_FF = 85 * D_MODEL // 32
CONV_W = 3

Q_BLOCK = 128
ROPE_DIM = 64
ROPE_THETA = 10000.0
NORM_EPS = 1e-6
NEG_INF = -1e30

IN_GROUPS = (
    ('a_q', A_HEADS * 2 * A_QK_DIM),
    ('a_k', A_HEADS * 2 * A_QK_DIM),
    ('a_v', A_HEADS * A_V_DIM),
    ('b_qa', B_Q_RANK),
    ('b_kva', B_KV_RANK + B_ROPE),
    ('c_q', C_HEADS * C_EXPAND),
    ('c_i', C_HEADS * C_V),
    ('c_ff', C_HEADS * C_EXPAND),
    ('c_fb', C_HEADS * C_EXPAND),
    ('c_g', C_HEADS * C_V),
    ('d_q', D_HEADS * D_HEAD),
    ('d_k', D_HEADS * D_HEAD),
    ('d_v', D_HEADS * D_HEAD),
    ('gate', N_BRANCH * D_MODEL),
)
ALL_GROUPS = tuple(name for name, _ in IN_GROUPS)
CTX_KV_GROUPS = ('a_k', 'a_v', 'b_kva', 'c_i', 'c_ff', 'c_fb', 'd_k', 'd_v')
IN_WIDTH = sum(size for _, size in IN_GROUPS)

kernel_name = 'hybrid_gated_branch_diffusion_trunk'


def _rmsnorm(x, g):
    xf = x.astype(jnp.float32)
    y = xf * lax.rsqrt(jnp.mean(xf * xf, axis=-1, keepdims=True) + NORM_EPS)
    return (y * g.astype(jnp.float32)).astype(x.dtype)


def _modulate(h, shift, scale):
    return h * (1.0 + scale) + shift


def _heads(x, n_heads):
    b, t, e = x.shape
    return x.reshape(b, t, n_heads, e // n_heads).transpose(0, 2, 1, 3)


def _merge_heads(x):
    b, h, t, d = x.shape
    return x.transpose(0, 2, 1, 3).reshape(b, t, h * d)


def _in_proj(h, w_in, names):
    offsets = {}
    start = 0
    for name, size in IN_GROUPS:
        offsets[name] = (start, size)
        start += size
    if names == ALL_GROUPS:
        w = w_in
    else:
        w = jnp.concatenate([w_in[:, offsets[nm][0]:offsets[nm][0] + offsets[nm][1]] for nm in names], axis=1)
    y = jnp.einsum('btd,de->bte', h, w)
    cuts = [int(v) for v in np.cumsum([offsets[nm][1] for nm in names])[:-1]]
    return dict(zip(names, jnp.split(y, cuts, axis=-1)))


def _rope_tables(n):
    t = jnp.arange(n, dtype=jnp.int32)
    row = (t // GRID_W).astype(jnp.float32)
    col = (t % GRID_W).astype(jnp.float32)
    quarter = ROPE_DIM // 4
    inv = ROPE_THETA ** (-jnp.arange(quarter, dtype=jnp.float32) / quarter)
    ar = row[:, None] * inv
    ac = col[:, None] * inv
    ang = jnp.concatenate([ar, ar, ac, ac], axis=-1)
    return jnp.cos(ang), jnp.sin(ang)


def _rope(x, cos, sin):
    x1, x2, x3, x4 = jnp.split(x, 4, axis=-1)
    rot = jnp.concatenate([-x2, x1, -x4, x3], axis=-1)
    return (x * cos + rot * sin).astype(x.dtype)


def _map_query_blocks(fn, q):
    shp = q.shape
    n, d = shp[-2], shp[-1]
    nb = n // Q_BLOCK
    qb = jnp.moveaxis(q.reshape(shp[:-2] + (nb, Q_BLOCK, d)), -3, 0)
    ob = jnp.moveaxis(lax.map(fn, qb), 0, -3)
    return ob.reshape(ob.shape[:-3] + (n, ob.shape[-1]))


def _sdpa(q, k, v, scale):
    s = jnp.einsum('bhqd,bhkd->bhqk', q, k).astype(jnp.float32) * scale
    p = jax.nn.softmax(s, axis=-1)
    return jnp.einsum('bhqk,bhkd->bhqd', p.astype(v.dtype), v)


def _diff_heads(x):
    b, t, _ = x.shape
    return x.reshape(b, t, A_HEADS, 2, A_QK_DIM).transpose(0, 2, 3, 1, 4)


def _diff_core(q, k, v, lam):
    s = jnp.einsum('bhcqd,bhckd->bhcqk', q, k).astype(jnp.float32) * (A_QK_DIM ** -0.5)
    p = jax.nn.softmax(s, axis=-1)
    w = p[:, :, 0] - lam * p[:, :, 1]
    return jnp.einsum('bhqk,bhkd->bhqd', w.astype(v.dtype), v)


def _diff_attention(pl, pc, lam_par, subln, layer, cos, sin, with_ctx_out):
    lam_init = 0.8 - 0.6 * math.exp(-0.3 * layer)
    lp = lam_par.astype(jnp.float32)
    lam = jnp.exp(jnp.sum(lp[0] * lp[1])) - jnp.exp(jnp.sum(lp[2] * lp[3])) + lam_init
    q = _rope(_diff_heads(pl['a_q']), cos, sin)
    k = _rope(_diff_heads(pl['a_k']), cos, sin)
    v = _heads(pl['a_v'], A_HEADS)
    kc = _diff_heads(pc['a_k'])
    vc = _heads(pc['a_v'], A_HEADS)
    k_all = jnp.concatenate([k, kc], axis=3)
    v_all = jnp.concatenate([v, vc], axis=2)

    def post(o):
        return _merge_heads(_rmsnorm(o, subln) * (1.0 - lam_init))

    o_lat = post(_map_query_blocks(lambda qb: _diff_core(qb, k_all, v_all, lam), q))
    o_ctx = post(_diff_core(_diff_heads(pc['a_q']), kc, vc, lam)) if with_ctx_out else None
    return o_lat, o_ctx


def _mla_q(pqa, q_norm, w_qb):
    q = _heads(jnp.einsum('btr,re->bte', _rmsnorm(pqa, q_norm), w_qb), B_HEADS)
    return q[..., :B_NOPE], q[..., B_NOPE:]


def _mla_kv(pkva, kv_norm, w_kvb):
    ckv, k_rope = pkva[..., :B_KV_RANK], pkva[..., B_KV_RANK:]
    kv = _heads(jnp.einsum('btr,re->bte', _rmsnorm(ckv, kv_norm), w_kvb), B_HEADS)
    return kv[..., :B_NOPE], kv[..., B_NOPE:], k_rope[:, None]


def _mla(pl, pc, q_norm, w_qb, kv_norm, w_kvb, cos, sin, with_ctx_out):
    scale = (B_NOPE + B_ROPE) ** -0.5
    qn, qr = _mla_q(pl['b_qa'], q_norm, w_qb)
    kn, v, kr = _mla_kv(pl['b_kva'], kv_norm, w_kvb)
    q = jnp.concatenate([qn, _rope(qr, cos, sin)], axis=-1)
    k = jnp.concatenate([kn, jnp.broadcast_to(_rope(kr, cos, sin), kn.shape[:-1] + (B_ROPE,))], axis=-1)
    knc, vc, krc = _mla_kv(pc['b_kva'], kv_norm, w_kvb)
    kc = jnp.concatenate([knc, jnp.broadcast_to(krc, knc.shape[:-1] + (B_ROPE,))], axis=-1)
    k_all = jnp.concatenate([k, kc], axis=2)
    v_all = jnp.concatenate([v, vc], axis=2)
    o_lat = _merge_heads(_map_query_blocks(lambda qb: _sdpa(qb, k_all, v_all, scale), q))
    o_ctx = None
    if with_ctx_out:
        qnc, qrc = _mla_q(pc['b_qa'], q_norm, w_qb)
        o_ctx = _merge_heads(_sdpa(jnp.concatenate([qnc, qrc], axis=-1), kc, vc, scale))
    return o_lat, o_ctx


def _gla_chunk_scan(q, k, v, logf, s0):
    b, h, t, dk = q.shape
    dv = v.shape[-1]
    nc = t // C_CHUNK

    def split(a):
        return a.reshape(b, h, nc, C_CHUNK, a.shape[-1]).transpose(2, 0, 1, 3, 4)

    causal = jnp.tril(jnp.ones((C_CHUNK, C_CHUNK), dtype=bool))

    def step(s, inp):
        qc, kc, vc, gc = inp
        cum = jnp.cumsum(gc, axis=2)
        o_inter = jnp.einsum('bhtd,bhdv->bhtv', qc * jnp.exp(cum), s)
        rel = cum[:, :, :, None, :] - cum[:, :, None, :, :]
        decay = jnp.where(causal[:, :, None], jnp.exp(jnp.minimum(rel, 0.0)), 0.0)
        a = jnp.einsum('bhtd,bhsd,bhtsd->bhts', qc, kc, decay)
        o = o_inter + jnp.einsum('bhts,bhsv->bhtv', a, vc)
        last = cum[:, :, -1:, :]
        s_new = jnp.exp(last[:, :, 0])[..., None] * s + jnp.einsum('bhsd,bhsv->bhdv', kc * jnp.exp(last - cum), vc)
        return s_new, o

    s_fin, o = lax.scan(step, s0, (split(q), split(k), split(v), split(logf)))
    return o.transpose(1, 2, 0, 3, 4).reshape(b, h, t, dv), s_fin


def _gla_final_state(k, v, logf):
    cum = jnp.cumsum(logf, axis=2)
    return jnp.einsum('bhtd,bhtv->bhdv', k * jnp.exp(cum[:, :, -1:] - cum), v)


def _hgrn_decay(z, lb):
    f = lb + (1.0 - lb) * jax.nn.sigmoid(z.astype(jnp.float32))
    return _heads(1.0 - f, C_HEADS), _heads(jnp.log(f), C_HEADS)


def _hgrn2(pl, pc, lb, norm_g, with_ctx_out):
    dt = pl['c_i'].dtype
    scale = C_EXPAND ** -0.5
    q = _heads(jax.nn.silu(pl['c_q']), C_HEADS) * scale
    v = _heads(pl['c_i'], C_HEADS)
    vc = _heads(pc['c_i'], C_HEADS)
    s_zero = jnp.zeros((v.shape[0], C_HEADS, C_EXPAND, C_V), jnp.float32)
    qc = _heads(jax.nn.silu(pc['c_q']), C_HEADS) * scale if with_ctx_out else None
    lat_outs, ctx_outs = [], []
    for d, name in enumerate(('c_ff', 'c_fb')):
        order = (lambda a: jnp.flip(a, axis=2)) if d == 1 else (lambda a: a)
        k, logf = _hgrn_decay(pl[name], lb[d])
        kc, logfc = _hgrn_decay(pc[name], lb[d])
        if with_ctx_out:
            oc, s_ctx = _gla_chunk_scan(order(qc), order(kc), order(vc), order(logfc), s_zero)
            ctx_outs.append(order(oc))
        else:
            s_ctx = _gla_final_state(order(kc), order(vc), order(logfc))
        ol, _ = _gla_chunk_scan(order(q), order(k), order(v), order(logf), s_ctx)
        lat_outs.append(order(ol))

    def readout(outs, gate):
        o = (outs[0] + outs[1]).astype(dt)
        return _merge_heads(_rmsnorm(o, norm_g)) * jax.nn.silu(gate)

    o_lat = readout(lat_outs, pl['c_g'])
    o_ctx = readout(ctx_outs, pc['c_g']) if with_ctx_out else None
    return o_lat, o_ctx


def _na(pl, pc, rpb, with_ctx_out):
    q = _heads(pl['d_q'], D_HEADS)
    k = _heads(pl['d_k'], D_HEADS)
    v = _heads(pl['d_v'], D_HEADS)
    kc = _heads(pc['d_k'], D_HEADS)
    vc = _heads(pc['d_v'], D_HEADS)
    b, h, n, dh = q.shape
    rows = n // GRID_W
    wr = min(NA_ROWS, rows)
    scale = dh ** -0.5
    qg = q.reshape(b, h, rows, GRID_W, dh)
    kg = k.reshape(b, h, rows, GRID_W, dh)
    vg = v.reshape(b, h, rows, GRID_W, dh)
    col = jnp.arange(GRID_W, dtype=jnp.int32)
    cs = jnp.clip(col - NA_COLS // 2, 0, GRID_W - NA_COLS)
    colmask = (col[None, :] >= cs[:, None]) & (col[None, :] < cs[:, None] + NA_COLS)
    dc = jnp.clip(col[None, :] - col[:, None] + NA_COLS - 1, 0, 2 * NA_COLS - 2)
    rpb_c = rpb[:, :, dc]

    def one_row(args):
        r, qr = args
        rs = jnp.clip(r - wr // 2, 0, rows - wr)
        kr = lax.dynamic_slice_in_dim(kg, rs, wr, axis=2)
        vr = lax.dynamic_slice_in_dim(vg, rs, wr, axis=2)
        dr = rs + jnp.arange(wr, dtype=jnp.int32) - r + NA_ROWS - 1
        bias = jnp.moveaxis(rpb_c[:, dr], 1, 2)
        s_loc = jnp.einsum('bhqd,bhjkd->bhqjk', qr, kr).astype(jnp.float32) * scale + bias
        s_loc = jnp.where(colmask[:, None, :], s_loc, NEG_INF)
        s_ctx = jnp.einsum('bhqd,bhkd->bhqk', qr, kc).astype(jnp.float32) * scale
        s = jnp.concatenate([s_loc.reshape(b, h, GRID_W, wr * GRID_W), s_ctx], axis=-1)
        p = jax.nn.softmax(s, axis=-1).astype(v.dtype)
        o = jnp.einsum('bhqk,bhkd->bhqd', p[..., :wr * GRID_W], vr.reshape(b, h, wr * GRID_W, dh))
        return o + jnp.einsum('bhqk,bhkd->bhqd', p[..., wr * GRID_W:], vc)

    og = lax.map(one_row, (jnp.arange(rows, dtype=jnp.int32), jnp.moveaxis(qg, 2, 0)))
    o_lat = _merge_heads(jnp.moveaxis(og, 0, 2).reshape(b, h, n, dh))
    o_ctx = _merge_heads(_sdpa(_heads(pc['d_q'], D_HEADS), kc, vc, scale)) if with_ctx_out else None
    return o_lat, o_ctx


def _merge_branches(gate_logits, outs, w_branch, w_out):
    g = jax.nn.sigmoid(gate_logits.astype(jnp.float32)).astype(gate_logits.dtype)
    merged = sum(g[..., i * D_MODEL:(i + 1) * D_MODEL] * jnp.einsum('btc,cd->btd', o, w_branch[i])
                 for i, o in enumerate(outs))
    return jnp.einsum('btd,de->bte', merged, w_out)


def _conv_ffn(h, w_up, conv_w, conv_b, w_down):
    u = jnp.einsum('btd,df->btf', h, w_up)
    up = jnp.pad(u, ((0, 0), (1, 1), (0, 0)))
    u = up[:, :-2] * conv_w[0] + up[:, 1:-1] * conv_w[1] + up[:, 2:] * conv_w[2] + conv_b
    gate, val = jnp.split(u, 2, axis=-1)
    return jnp.einsum('btf,fd->btd', jax.nn.silu(gate) * val, w_down)


def setup_inputs(seed: int = 0) -> dict:
    key = jax.random.key(seed)
    ks = jax.random.split(key, 26)
    D = D_MODEL

    def nrm(k, shape, scale):
        return jax.random.normal(k, shape, jnp.float32) * scale

    def gain(k, shape):
        return 1.0 + 0.05 * jax.random.normal(k, shape, jnp.float32)

    return {
        'x': nrm(ks[0], (BATCH, SEQ, D), 1.0),
        'c': nrm(ks[1], (BATCH, D), 1.0),
        'ctx': nrm(ks[2], (BATCH, CTX_LEN, D), 1.0),
        'c_ctx': nrm(ks[3], (D,), 1.0),
        'ada_w': nrm(ks[4], (DEPTH, D, 6 * D), 0.5 * D ** -0.5),
        'ada_b': nrm(ks[5], (DEPTH, 6 * D), 0.02),
        'norm_mix_pre': gain(ks[6], (DEPTH, D)),
        'norm_mix_post': gain(ks[7], (DEPTH, D)),
        'norm_ffn_pre': gain(ks[8], (DEPTH, D)),
        'norm_ffn_post': gain(ks[9], (DEPTH, D)),
        'w_in': nrm(ks[10], (DEPTH, D, IN_WIDTH), D ** -0.5),
        'diff_lambda': nrm(ks[11], (DEPTH, 4, A_QK_DIM), 0.1),
        'diff_subln': gain(ks[12], (DEPTH, A_V_DIM)),
        'mla_q_norm': gain(ks[13], (DEPTH, B_Q_RANK)),
        'mla_w_qb': nrm(ks[14], (DEPTH, B_Q_RANK, B_HEADS * (B_NOPE + B_ROPE)), B_Q_RANK ** -0.5),
        'mla_kv_norm': gain(ks[15], (DEPTH, B_KV_RANK)),
        'mla_w_kvb': nrm(ks[16], (DEPTH, B_KV_RANK, B_HEADS * (B_NOPE + B_V)), B_KV_RANK ** -0.5),
        'hgrn_lb_logits': nrm(ks[17], (DEPTH, 2, C_HEADS * C_EXPAND), 1.0),
        'hgrn_norm': gain(ks[18], (DEPTH, C_V)),
        'na_rpb': nrm(ks[19], (DEPTH, D_HEADS, 2 * NA_ROWS - 1, 2 * NA_COLS - 1), 0.1),
        'w_branch': nrm(ks[20], (DEPTH, N_BRANCH, MIX_WIDTH, D), MIX_WIDTH ** -0.5),
        'w_out': nrm(ks[21], (DEPTH, D, D), D ** -0.5),
        'ffn_w_up': nrm(ks[22], (DEPTH, D, 2 * D_FF), D ** -0.5),
        'ffn_conv_w': nrm(ks[23], (DEPTH, CONV_W, 2 * D_FF), CONV_W ** -0.5),
        'ffn_conv_b': nrm(ks[24], (DEPTH, 2 * D_FF), 0.02),
        'ffn_w_down': nrm(ks[25], (DEPTH, D_FF, D), D_FF ** -0.5),
    }


def reference(x, c, ctx, c_ctx, ada_w, ada_b, norm_mix_pre, norm_mix_post, norm_ffn_pre, norm_ffn_post,
              w_in, diff_lambda, diff_subln, mla_q_norm, mla_w_qb, mla_kv_norm, mla_w_kvb,
              hgrn_lb_logits, hgrn_norm, na_rpb, w_branch, w_out, ffn_w_up, ffn_conv_w, ffn_conv_b, ffn_w_down):
    n = x.shape[1]
    cos, sin = _rope_tables(n)
    lb_sm = jax.nn.softmax(hgrn_lb_logits.astype(jnp.float32), axis=0)
    lower_bounds = jnp.cumsum(lb_sm, axis=0) - lb_sm[0]
    c_act = jax.nn.silu(c)
    cc_act = jax.nn.silu(c_ctx)
    xl, xc = x, ctx
    for l in range(DEPTH):
        with_ctx = l < DEPTH - 1
        mod = jnp.einsum('bd,de->be', c_act, ada_w[l]) + ada_b[l]
        sh1, sc1, g1, sh2, sc2, g2 = [m[:, None] for m in jnp.split(mod, 6, axis=-1)]
        if with_ctx:
            modc = jnp.einsum('d,de->e', cc_act, ada_w[l]) + ada_b[l]
            csh1, csc1, cg1, csh2, csc2, cg2 = jnp.split(modc, 6)
        else:
            modc = jnp.einsum('d,de->e', cc_act, ada_w[l][:, :2 * D_MODEL]) + ada_b[l][:2 * D_MODEL]
            csh1, csc1 = jnp.split(modc, 2)

        hl = _modulate(_rmsnorm(xl, norm_mix_pre[l]), sh1, sc1)
        hc = _modulate(_rmsnorm(xc, norm_mix_pre[l]), csh1, csc1)
        pl = _in_proj(hl, w_in[l], ALL_GROUPS)
        pc = _in_proj(hc, w_in[l], ALL_GROUPS if with_ctx else CTX_KV_GROUPS)
        oa = _diff_attention(pl, pc, diff_lambda[l], diff_subln[l], l, cos, sin, with_ctx)
        ob = _mla(pl, pc, mla_q_norm[l], mla_w_qb[l], mla_kv_norm[l], mla_w_kvb[l], cos, sin, with_ctx)
        oc = _hgrn2(pl, pc, lower_bounds[l], hgrn_norm[l], with_ctx)
        od = _na(pl, pc, na_rpb[l], with_ctx)
        mix_l = _merge_branches(pl['gate'], (oa[0], ob[0], oc[0], od[0]), w_branch[l], w_out[l])
        xl_new = xl + g1 * _rmsnorm(mix_l, norm_mix_post[l])
        hl2 = _modulate(_rmsnorm(xl_new, norm_ffn_pre[l]), sh2, sc2)
        ffn_l = _conv_ffn(hl2, ffn_w_up[l], ffn_conv_w[l], ffn_conv_b[l], ffn_w_down[l])
        xl_new = xl_new + g2 * _rmsnorm(ffn_l, norm_ffn_post[l])
        if with_ctx:
            mix_c = _merge_branches(pc['gate'], (oa[1], ob[1], oc[1], od[1]), w_branch[l], w_out[l])
            xc = xc + cg1 * _rmsnorm(mix_c, norm_mix_post[l])
            hc2 = _modulate(_rmsnorm(xc, norm_ffn_pre[l]), csh2, csc2)
            ffn_c = _conv_ffn(hc2, ffn_w_up[l], ffn_conv_w[l], ffn_conv_b[l], ffn_w_down[l])
            xc = xc + cg2 * _rmsnorm(ffn_c, norm_ffn_post[l])
        xl = xl_new
    return xl
```

```python
import functools
import math

import numpy as np
import jax
import jax.numpy as jnp
from jax import lax
from jax.experimental import pallas as pl
from jax.experimental.pallas import tpu as pltpu

F32 = jnp.float32
BF16 = jnp.bfloat16

HEAD_DIM = 128
QK_HALF = 64
MLA_NOPE = 128
MLA_ROPE = 64
MLA_V = 128
ROPE_DIM = 64
ROPE_THETA = 10000.0
GRID_W = 64
NA_ROWS = 8
NA_COLS = 16
CHUNK = 64
NORM_EPS = 1e-6
NEG_INF = -1e30
N_BRANCH = 4

LANE = 128
VMEM_LIMIT_BYTES = 56 * 1024 * 1024

NT_DIMS = (((1,), (1,)), ((), ()))
TN_DIMS = (((0,), (0,)), ((), ()))


def _params(*sem):
    return pltpu.CompilerParams(dimension_semantics=sem, vmem_limit_bytes=VMEM_LIMIT_BYTES)


def _sigmoid(x):
    return 1.0 / (1.0 + jnp.exp(-x))


def _pick(total, candidates):
    for c in candidates:
        if total % c == 0:
            return c
    raise ValueError(f"no tile for {total} in {candidates}")


def _ada_kernel(c_ref, w_ref, b_ref, o_ref):
    c = c_ref[...]
    a = (c * _sigmoid(c)).astype(BF16)
    o_ref[...] = jnp.dot(a, w_ref[...].astype(BF16), preferred_element_type=F32) + b_ref[...]


def ada_modulation(c_rows, ada_w, ada_b, layer):
    rows, d = c_rows.shape
    width = ada_w.shape[2]
    tn = _pick(width, (512, 256, 128))
    b3 = ada_b.reshape(ada_b.shape[0], 1, width)
    return pl.pallas_call(
        _ada_kernel,
        grid=(width // tn,),
        in_specs=[
            pl.BlockSpec((rows, d), lambda j: (0, 0)),
            pl.BlockSpec((None, d, tn), lambda j: (layer, 0, j)),
            pl.BlockSpec((None, 1, tn), lambda j: (layer, 0, j)),
        ],
        out_specs=pl.BlockSpec((rows, tn), lambda j: (0, j)),
        out_shape=jax.ShapeDtypeStruct((rows, width), F32),
        compiler_params=_params("parallel"),
        name="ada_modulation",
    )(c_rows, ada_w, b3)


def _rms(x, gain):
    return x * lax.rsqrt(jnp.mean(x * x, axis=-1, keepdims=True) + NORM_EPS) * gain


def _norm_mod_kernel(x_ref, g_ref, sh_ref, sc_ref, h_ref):
    h = _rms(x_ref[0], g_ref[...])
    h_ref[0] = (h * (1.0 + sc_ref[0]) + sh_ref[0]).astype(h_ref.dtype)


def _resid_kernel(x_ref, y_ref, gate_ref, gpost_ref, xo_ref):
    xo_ref[0] = x_ref[0] + gate_ref[0] * _rms(y_ref[0], gpost_ref[...])


def _resid_norm_mod_kernel(x_ref, y_ref, gate_ref, gpost_ref, gpre_ref, sh_ref, sc_ref, xo_ref, h_ref):
    xn = x_ref[0] + gate_ref[0] * _rms(y_ref[0], gpost_ref[...])
    xo_ref[0] = xn
    h = _rms(xn, gpre_ref[...])
    h_ref[0] = (h * (1.0 + sc_ref[0]) + sh_ref[0]).astype(h_ref.dtype)


def _row_specs(dims, tr, n_lat_tiles):
    s, d = dims
    xspec = pl.BlockSpec((1, tr, d), lambda b, t: (b, t, 0))
    gspec = pl.BlockSpec((1, d), lambda b, t: (0, 0))
    mspec = pl.BlockSpec((1, 1, d), lambda b, t: (2 * b + t // n_lat_tiles, 0, 0))
    return xspec, gspec, mspec


def norm_modulate(x, gain, shift, scale, n_lat, tr=256):
    bsz, s, d = x.shape
    xspec, gspec, mspec = _row_specs((s, d), tr, n_lat // tr)
    return pl.pallas_call(
        _norm_mod_kernel,
        grid=(bsz, s // tr),
        in_specs=[xspec, gspec, mspec, mspec],
        out_specs=xspec,
        out_shape=jax.ShapeDtypeStruct(x.shape, BF16),
        compiler_params=_params("parallel", "parallel"),
        name="norm_modulate",
    )(x, gain.reshape(1, d), shift, scale)


def residual(x, y, gate, gain_post, n_lat, tr=256):
    bsz, s, d = x.shape
    xspec, gspec, mspec = _row_specs((s, d), tr, n_lat // tr)
    return pl.pallas_call(
        _resid_kernel,
        grid=(bsz, s // tr),
        in_specs=[xspec, xspec, mspec, gspec],
        out_specs=xspec,
        out_shape=jax.ShapeDtypeStruct(x.shape, F32),
        compiler_params=_params("parallel", "parallel"),
        name="residual",
    )(x, y, gate, gain_post.reshape(1, d))


def residual_norm_modulate(x, y, gate, gain_post, gain_pre, shift, scale, n_lat, tr=256):
    bsz, s, d = x.shape
    xspec, gspec, mspec = _row_specs((s, d), tr, n_lat // tr)
    return pl.pallas_call(
        _resid_norm_mod_kernel,
        grid=(bsz, s // tr),
        in_specs=[xspec, xspec, mspec, gspec, gspec, mspec, mspec],
        out_specs=[xspec, xspec],
        out_shape=[jax.ShapeDtypeStruct(x.shape, F32), jax.ShapeDtypeStruct(x.shape, BF16)],
        compiler_params=_params("parallel", "parallel"),
        name="residual_norm_modulate",
    )(x, y, gate, gain_post.reshape(1, d), gain_pre.reshape(1, d), shift, scale)


def _mm_kernel(a_ref, b_ref, o_ref):
    o_ref[...] = jnp.dot(a_ref[...], b_ref[...], preferred_element_type=F32).astype(o_ref.dtype)


def _mm_acc_kernel(a_ref, b_ref, o_ref, acc_ref):
    k = pl.program_id(2)

    @pl.when(k == 0)
    def _():
        acc_ref[...] = jnp.zeros_like(acc_ref)

    acc_ref[...] += jnp.dot(a_ref[...], b_ref[...], preferred_element_type=F32)

    @pl.when(k == pl.num_programs(2) - 1)
    def _():
        o_ref[...] = acc_ref[...].astype(o_ref.dtype)


MAX_SINGLE_K = 4096


def matmul(a, b, out_dtype, name):
    m, k = a.shape
    n = b.shape[1]
    bm = _pick(m, (1024, 512, 256, 128))
    bn = _pick(n, (1024, 768, 640, 512, 384, 256, 128))
    if k <= MAX_SINGLE_K:
        return pl.pallas_call(
            _mm_kernel,
            grid=(m // bm, n // bn),
            in_specs=[pl.BlockSpec((bm, k), lambda i, j: (i, 0)),
                      pl.BlockSpec((k, bn), lambda i, j: (0, j))],
            out_specs=pl.BlockSpec((bm, bn), lambda i, j: (i, j)),
            out_shape=jax.ShapeDtypeStruct((m, n), out_dtype),
            compiler_params=_params("parallel", "parallel"),
            name=name,
        )(a, b)
    nk = -(-k // MAX_SINGLE_K)
    while k % nk or (k // nk) % LANE:
        nk += 1
    bk = k // nk
    return pl.pallas_call(
        _mm_acc_kernel,
        grid=(m // bm, n // bn, nk),
        in_specs=[pl.BlockSpec((bm, bk), lambda i, j, kk: (i, kk)),
                  pl.BlockSpec((bk, bn), lambda i, j, kk: (kk, j))],
        out_specs=pl.BlockSpec((bm, bn), lambda i, j, kk: (i, j)),
        out_shape=jax.ShapeDtypeStruct((m, n), out_dtype),
        scratch_shapes=[pltpu.VMEM((bm, bn), F32)],
        compiler_params=_params("parallel", "parallel", "arbitrary"),
        name=name,
    )(a, b)


def _rope_tables(n_lat, n_ctx):
    t = jnp.arange(n_lat, dtype=jnp.int32)
    row = (t // GRID_W).astype(F32)
    col = (t % GRID_W).astype(F32)
    quarter = ROPE_DIM // 4
    inv = ROPE_THETA ** (-jnp.arange(quarter, dtype=F32) / quarter)
    ar = row[:, None] * inv
    ac = col[:, None] * inv
    ang = jnp.concatenate([ar, ar, ac, ac], axis=-1)
    cos = jnp.concatenate([jnp.cos(ang), jnp.ones((n_ctx, ROPE_DIM), F32)], axis=0)
    sin = jnp.concatenate([jnp.sin(ang), jnp.zeros((n_ctx, ROPE_DIM), F32)], axis=0)
    first = (jnp.arange(ROPE_DIM) // quarter) % 2 == 0
    sin_next = jnp.where(first, -sin, 0.0)
    sin_prev = jnp.where(first, 0.0, sin)
    reps = LANE // ROPE_DIM
    return tuple(jnp.tile(a, (1, reps)) for a in (cos, sin_next, sin_prev))


def _rope_apply(x, cos, sin_next, sin_prev):
    quarter = ROPE_DIM // 4
    return (x * cos + pltpu.roll(x, LANE - quarter, 1) * sin_next
            + pltpu.roll(x, quarter, 1) * sin_prev)


def _rope_kernel(x_ref, cos_ref, sn_ref, sp_ref, o_ref):
    cos, sn, sp = cos_ref[...], sn_ref[...], sp_ref[...]
    for c in range(x_ref.shape[1] // LANE):
        sl = slice(c * LANE, (c + 1) * LANE)
        o_ref[:, sl] = _rope_apply(x_ref[:, sl].astype(F32), cos, sn, sp).astype(o_ref.dtype)


def rope(x, tables, col_block, n_col_blocks, s, width=1024, tr=256):
    r = x.shape[0]
    width = min(width, x.shape[1])
    tiles_per_seq = s // tr
    tspec = pl.BlockSpec((tr, LANE), lambda i, j: (i % tiles_per_seq, 0))
    return pl.pallas_call(
        _rope_kernel,
        grid=(r // tr, n_col_blocks),
        in_specs=[pl.BlockSpec((tr, width), lambda i, j: (i, col_block + j)), tspec, tspec, tspec],
        out_specs=pl.BlockSpec((tr, width), lambda i, j: (i, j)),
        out_shape=jax.ShapeDtypeStruct((r, width * n_col_blocks), BF16),
        compiler_params=_params("parallel", "parallel"),
        name="rope",
    )(x, *tables)


def _diff_attn_kernel(q_ref, k_ref, v_ref, lam_ref, g_ref, o_ref, *, lam_init):
    q = q_ref[...]
    tq = q.shape[0]
    lane = lax.broadcasted_iota(jnp.int32, q.shape, 1)
    zero = jnp.zeros_like(q)
    qq = jnp.concatenate([jnp.where(lane < QK_HALF, q, zero), jnp.where(lane >= QK_HALF, q, zero)], axis=0)
    s = lax.dot_general(qq, k_ref[...], NT_DIMS, preferred_element_type=F32) * (QK_HALF ** -0.5)
    e = jnp.exp(s - jnp.max(s, axis=-1, keepdims=True))
    inv = 1.0 / jnp.sum(e, axis=-1, keepdims=True)
    lp = lam_ref[...]
    lam = (jnp.exp(jnp.sum(lp[0:1] * lp[1:2], axis=-1, keepdims=True))
           - jnp.exp(jnp.sum(lp[2:3] * lp[3:4], axis=-1, keepdims=True)) + lam_init)
    w = e[:tq] * inv[:tq] - e[tq:] * (lam * inv[tq:])
    o = jnp.dot(w.astype(BF16), v_ref[...], preferred_element_type=F32)
    o_ref[...] = (_rms(o, g_ref[...]) * (1.0 - lam_init)).astype(o_ref.dtype)


def _sdpa_kernel(*refs, n_qk, scale):
    q_refs, k_refs = refs[:n_qk], refs[n_qk:2 * n_qk]
    v_ref, o_ref = refs[2 * n_qk:]
    s = None
    for q_ref, k_ref in zip(q_refs, k_refs):
        part = lax.dot_general(q_ref[...], k_ref[...], NT_DIMS, preferred_element_type=F32)
        s = part if s is None else s + part
    s = s * scale
    e = jnp.exp(s - jnp.max(s, axis=-1, keepdims=True))
    inv = 1.0 / jnp.sum(e, axis=-1, keepdims=True)
    o = jnp.dot(e.astype(BF16), v_ref[...], preferred_element_type=F32)
    o_ref[...] = (o * inv).astype(o_ref.dtype)


def _attention_call(kernel, q_list, k_list, v, extras, *, bsz, heads, s, q_range, k_range, tq, name):
    q_start, q_len = q_range
    k_start, k_len = k_range
    nq = q_len // tq
    q_off = q_start // tq
    q_tiles = s // tq
    k_tiles = s // k_len
    k_off = k_start // k_len

    def qspec(col0):
        return pl.BlockSpec((tq, HEAD_DIM), lambda b, h, i: (b * q_tiles + q_off + i, col0 + h))

    def kspec(col0, stride, shared=False):
        if shared:
            return pl.BlockSpec((k_len, HEAD_DIM), lambda b, h, i: (b * k_tiles + k_off, col0))
        return pl.BlockSpec((k_len, HEAD_DIM), lambda b, h, i: (b * k_tiles + k_off, col0 + stride * h))

    in_specs, args = [], []
    for arr, col0 in q_list:
        in_specs.append(qspec(col0)); args.append(arr)
    for arr, col0, stride, shared in k_list:
        in_specs.append(kspec(col0, stride, shared)); args.append(arr)
    arr, col0, stride = v
    in_specs.append(kspec(col0, stride)); args.append(arr)
    for e in extras:
        in_specs.append(pl.BlockSpec(e.shape, lambda b, h, i: (0, 0))); args.append(e)
    return pl.pallas_call(
        kernel,
        grid=(bsz, heads, nq),
        in_specs=in_specs,
        out_specs=pl.BlockSpec((tq, HEAD_DIM), lambda b, h, i: (b * nq + i, h)),
        out_shape=jax.ShapeDtypeStruct((bsz * q_len, heads * HEAD_DIM), BF16),
        compiler_params=_params("parallel", "parallel", "arbitrary"),
        name=name,
    )(*args)


def _attend_lat_and_ctx(call, n_lat, n_ctx, with_ctx):
    s = n_lat + n_ctx
    o_lat = call((0, n_lat), (0, s), "lat")
    bsz = o_lat.shape[0] // n_lat
    o_lat = o_lat.reshape(bsz, n_lat, -1)
    if with_ctx:
        o_ctx = call((n_lat, n_ctx), (n_lat, n_ctx), "ctx").reshape(bsz, n_ctx, -1)
    else:
        o_ctx = jnp.zeros((bsz, n_ctx, o_lat.shape[-1]), o_lat.dtype)
    return jnp.concatenate([o_lat, o_ctx], axis=1).reshape(bsz * s, -1)


def _na_bias_table(rpb):
    col = jnp.arange(GRID_W, dtype=jnp.int32)
    cs = jnp.clip(col - NA_COLS // 2, 0, GRID_W - NA_COLS)
    colmask = (col[None, :] >= cs[:, None]) & (col[None, :] < cs[:, None] + NA_COLS)
    dc = jnp.clip(col[None, :] - col[:, None] + NA_COLS - 1, 0, 2 * NA_COLS - 2)
    delta = jnp.arange(NA_ROWS, dtype=jnp.int32)
    jj = jnp.arange(NA_ROWS, dtype=jnp.int32)
    dr = jj[None, :] - delta[:, None] + NA_ROWS - 1
    bias = rpb[:, dr][:, :, :, dc]
    bias = jnp.where(colmask[None, None, None], bias, NEG_INF)
    bias = jnp.transpose(bias, (0, 1, 3, 2, 4))
    return bias.reshape(rpb.shape[0], NA_ROWS, GRID_W, NA_ROWS * GRID_W).astype(F32)


def _na_kernel(q_ref, k_ref, v_ref, bias_ref, o_ref, *, rows_per_step, n_lat, grid_rows):
    g = pl.program_id(2)
    scale = HEAD_DIM ** -0.5
    win = NA_ROWS * GRID_W
    kc = k_ref[n_lat:, :]
    vc = v_ref[n_lat:, :]
    for i in range(rows_per_step):
        r = g * rows_per_step + i
        rs = jnp.clip(r - NA_ROWS // 2, 0, grid_rows - NA_ROWS)
        start = pl.multiple_of(rs * GRID_W, GRID_W)
        q = q_ref[i * GRID_W:(i + 1) * GRID_W, :]
        kw = k_ref[pl.ds(start, win), :]
        vw = v_ref[pl.ds(start, win), :]
        sl = lax.dot_general(q, kw, NT_DIMS, preferred_element_type=F32) * scale + bias_ref[r - rs]
        sc = lax.dot_general(q, kc, NT_DIMS, preferred_element_type=F32) * scale
        m = jnp.maximum(jnp.max(sl, axis=-1, keepdims=True), jnp.max(sc, axis=-1, keepdims=True))
        el = jnp.exp(sl - m)
        ec = jnp.exp(sc - m)
        inv = 1.0 / (jnp.sum(el, axis=-1, keepdims=True) + jnp.sum(ec, axis=-1, keepdims=True))
        o = (jnp.dot(el.astype(BF16), vw, preferred_element_type=F32)
             + jnp.dot(ec.astype(BF16), vc, preferred_element_type=F32))
        o_ref[i * GRID_W:(i + 1) * GRID_W, :] = (o * inv).astype(o_ref.dtype)


def neighbourhood_attention(pd, bias, bsz, heads, n_lat, n_ctx, rows_per_step=4):
    s = n_lat + n_ctx
    grid_rows = n_lat // GRID_W
    tq = rows_per_step * GRID_W
    assert grid_rows >= NA_ROWS and grid_rows % rows_per_step == 0 and s % tq == 0
    steps = grid_rows // rows_per_step
    q_tiles = s // tq
    return pl.pallas_call(
        functools.partial(_na_kernel, rows_per_step=rows_per_step, n_lat=n_lat, grid_rows=grid_rows),
        grid=(bsz, heads, steps),
        in_specs=[
            pl.BlockSpec((tq, HEAD_DIM), lambda b, h, g: (b * q_tiles + g, h)),
            pl.BlockSpec((s, HEAD_DIM), lambda b, h, g: (b, heads + h)),
            pl.BlockSpec((s, HEAD_DIM), lambda b, h, g: (b, 2 * heads + h)),
            pl.BlockSpec((None, NA_ROWS, GRID_W, NA_ROWS * GRID_W), lambda b, h, g: (h, 0, 0, 0)),
        ],
        out_specs=pl.BlockSpec((tq, HEAD_DIM), lambda b, h, g: (b * steps + g, h)),
        out_shape=jax.ShapeDtypeStruct((bsz * n_lat, heads * HEAD_DIM), BF16),
        compiler_params=_params("parallel", "parallel", "arbitrary"),
        name="neighbourhood_attention",
    )(pd, pd, pd, bias)


def _hgrn_constants(reverse):
    c = CHUNK
    t = np.arange(c)
    tri = (t[None, :] >= t[:, None]) if reverse else (t[None, :] <= t[:, None])
    mats = [tri.astype(np.float32)]
    masks = [np.eye(c, dtype=np.float32)]
    m = c // 2
    while m >= 1:
        node = (t // (2 * m)) * (2 * m)
        ref = node + (m if reverse else m - 1)
        mats.append(tri[ref].astype(np.float32))
        upper = (t % (2 * m)) >= m
        same = node[:, None] == node[None, :]
        pair = same & (upper[:, None] & ~upper[None, :])
        masks.append((pair.T if reverse else pair).astype(np.float32))
        m //= 2
    return np.concatenate(mats, axis=0), np.stack(masks)


def _hgrn_kernel(cq_ref, ci_ref, z_ref, lb_ref, ms_ref, mk_ref, o_ref, st_ref, *, heads, reverse):
    @pl.when(pl.program_id(1) == 0)
    def _():
        st_ref[...] = jnp.zeros_like(st_ref)

    c = CHUNK
    n_levels = mk_ref.shape[0]
    ms = ms_ref[...].astype(BF16)
    scale = HEAD_DIM ** -0.5
    tot_row = 0 if reverse else c - 1
    for h in range(heads):
        sl = slice(h * HEAD_DIM, (h + 1) * HEAD_DIM)
        lb = lb_ref[:, sl]
        f = lb + (1.0 - lb) * _sigmoid(z_ref[:, sl].astype(F32))
        kk = 1.0 - f
        g = jnp.log(f)
        cq = cq_ref[:, sl].astype(F32)
        qq = cq * _sigmoid(cq) * scale
        v = ci_ref[:, sl]
        g_hi = g.astype(BF16)
        r1 = g - g_hi.astype(F32)
        g_mid = r1.astype(BF16)
        g_lo = (r1 - g_mid.astype(F32)).astype(BF16)
        cs3 = jnp.dot(ms, jnp.concatenate([g_hi, g_mid, g_lo], axis=1), preferred_element_type=F32)
        cs = cs3[:, :HEAD_DIM] + cs3[:, HEAD_DIM:2 * HEAD_DIM] + cs3[:, 2 * HEAD_DIM:]
        cum = cs[:c]
        tot = cs[tot_row:tot_row + 1]
        st = st_ref[h]
        o = lax.dot_general((qq * jnp.exp(cum)).astype(BF16), st.astype(BF16), NT_DIMS,
                            preferred_element_type=F32)
        a = mk_ref[0] * lax.dot_general(qq.astype(BF16), kk.astype(BF16), NT_DIMS,
                                        preferred_element_type=F32)
        for lev in range(1, n_levels):
            e = jnp.exp(-jnp.abs(cum - cs[lev * c:(lev + 1) * c]))
            a = a + mk_ref[lev] * lax.dot_general((qq * e).astype(BF16), (kk * e).astype(BF16), NT_DIMS,
                                                  preferred_element_type=F32)
        o = o + jnp.dot(a.astype(BF16), v, preferred_element_type=F32)
        o_ref[:, sl] = o
        ke = (kk * jnp.exp(tot - cum)).astype(BF16)
        st_ref[h] = st * jnp.exp(tot) + lax.dot_general(v, ke, TN_DIMS, preferred_element_type=F32)


def hgrn_scan(pc, lb, bsz, heads, n_lat, n_ctx, reverse):
    s = n_lat + n_ctx
    chunks = s // CHUNK
    lat_chunks = n_lat // CHUNK
    width = heads * HEAD_DIM
    ms, mk = _hgrn_constants(reverse)

    def chunk_of(step):
        return (chunks - 1 - step) if reverse else lax.rem(step + lat_chunks, chunks)

    def spec(col):
        return pl.BlockSpec((CHUNK, width), lambda b, t: (b * chunks + chunk_of(t), col))

    return pl.pallas_call(
        functools.partial(_hgrn_kernel, heads=heads, reverse=reverse),
        grid=(bsz, chunks),
        in_specs=[spec(0), spec(1), spec(3 if reverse else 2),
                  pl.BlockSpec((1, width), lambda b, t: (0, 0)),
                  pl.BlockSpec(ms.shape, lambda b, t: (0, 0)),
                  pl.BlockSpec(mk.shape, lambda b, t: (0, 0, 0))],
        out_specs=spec(0),
        out_shape=jax.ShapeDtypeStruct((bsz * s, width), F32),
        scratch_shapes=[pltpu.VMEM((heads, HEAD_DIM, HEAD_DIM), F32)],
        compiler_params=_params("parallel", "arbitrary"),
        name="hgrn_scan_bwd" if reverse else "hgrn_scan_fwd",
    )(pc, pc, pc, lb.reshape(1, width), jnp.asarray(ms), jnp.asarray(mk))


def _hgrn_readout_kernel(of_ref, ob_ref, gate_ref, g_ref, o_ref):
    gain = g_ref[...]
    for h in range(of_ref.shape[1] // HEAD_DIM):
        sl = slice(h * HEAD_DIM, (h + 1) * HEAD_DIM)
        gate = gate_ref[:, sl].astype(F32)
        o = _rms(of_ref[:, sl] + ob_ref[:, sl], gain)
        o_ref[:, sl] = (o * (gate * _sigmoid(gate))).astype(o_ref.dtype)


def hgrn_readout(o_fwd, o_bwd, pc, gain, tr=256):
    r, width = o_fwd.shape
    ospec = pl.BlockSpec((tr, width), lambda i: (i, 0))
    return pl.pallas_call(
        _hgrn_readout_kernel,
        grid=(r // tr,),
        in_specs=[ospec, ospec, pl.BlockSpec((tr, width), lambda i: (i, 4)),
                  pl.BlockSpec((1, HEAD_DIM), lambda i: (0, 0))],
        out_specs=ospec,
        out_shape=jax.ShapeDtypeStruct((r, width), BF16),
        compiler_params=_params("parallel"),
        name="hgrn_readout",
    )(o_fwd, o_bwd, pc, gain.reshape(1, HEAD_DIM))


def _mla_prep_kernel(p_ref, qg_ref, kg_ref, cos_ref, sn_ref, sp_ref, qn_ref, cn_ref, kr_ref, *, q_rank, kv_rank):
    qn_ref[...] = _rms(p_ref[:, :q_rank].astype(F32), qg_ref[...]).astype(qn_ref.dtype)
    cn_ref[...] = _rms(p_ref[:, q_rank:q_rank + kv_rank].astype(F32), kg_ref[...]).astype(cn_ref.dtype)
    kr = p_ref[:, q_rank + kv_rank:q_rank + kv_rank + LANE].astype(F32)
    kr_ref[...] = _rope_apply(kr, cos_ref[...], sn_ref[...], sp_ref[...]).astype(kr_ref.dtype)


def mla_prep(pb, q_gain, kv_gain, tables, s, tr=256):
    r, width = pb.shape
    q_rank, kv_rank = q_gain.shape[0], kv_gain.shape[0]
    assert width >= q_rank + kv_rank + LANE and q_rank % LANE == 0 and kv_rank % LANE == 0
    tiles_per_seq = s // tr
    tspec = pl.BlockSpec((tr, LANE), lambda i: (i % tiles_per_seq, 0))

    def ospec(w):
        return pl.BlockSpec((tr, w), lambda i: (i, 0))

    return pl.pallas_call(
        functools.partial(_mla_prep_kernel, q_rank=q_rank, kv_rank=kv_rank),
        grid=(r // tr,),
        in_specs=[ospec(width), pl.BlockSpec((1, q_rank), lambda i: (0, 0)),
                  pl.BlockSpec((1, kv_rank), lambda i: (0, 0)), tspec, tspec, tspec],
        out_specs=[ospec(q_rank), ospec(kv_rank), ospec(LANE)],
        out_shape=[jax.ShapeDtypeStruct((r, q_rank), BF16), jax.ShapeDtypeStruct((r, kv_rank), BF16),
                   jax.ShapeDtypeStruct((r, LANE), BF16)],
        compiler_params=_params("parallel"),
        name="mla_prep",
    )(pb, q_gain.reshape(1, q_rank), kv_gain.reshape(1, kv_rank), *tables)


def _merge_kernel(oa_ref, ob_ref, oc_ref, od_ref, ga_ref, gb_ref, gc_ref, gd_ref, w_ref, o_ref):
    acc = None
    for i, (o_ref_i, g_ref_i) in enumerate(((oa_ref, ga_ref), (ob_ref, gb_ref), (oc_ref, gc_ref), (od_ref, gd_ref))):
        y = jnp.dot(o_ref_i[...], w_ref[i], preferred_element_type=F32)
        term = _sigmoid(g_ref_i[...].astype(F32)) * y
        acc = term if acc is None else acc + term
    o_ref[...] = acc.astype(o_ref.dtype)


def merge_branches(outs, gate_logits, w_branch):
    r, c = outs[0].shape
    d = w_branch.shape[2]
    bm = _pick(r, (512, 256, 128))
    bn = _pick(d, (1024, 512, 256, 128))
    nb = d // bn
    ospec = pl.BlockSpec((bm, c), lambda i, j: (i, 0))

    def gspec(branch):
        return pl.BlockSpec((bm, bn), lambda i, j: (i, branch * nb + j))

    return pl.pallas_call(
        _merge_kernel,
        grid=(r // bm, nb),
        in_specs=[ospec] * N_BRANCH + [gspec(i) for i in range(N_BRANCH)]
        + [pl.BlockSpec((N_BRANCH, c, bn), lambda i, j: (0, 0, j))],
        out_specs=pl.BlockSpec((bm, bn), lambda i, j: (i, j)),
        out_shape=jax.ShapeDtypeStruct((r, d), BF16),
        compiler_params=_params("parallel", "parallel"),
        name="merge_branches",
    )(*outs, gate_logits, gate_logits, gate_logits, gate_logits, w_branch)


HALO = 16


def _ffn_up_kernel(h_ref, hp_ref, hn_ref, wg_ref, wv_ref, cwg_ref, cwv_ref, cbg_ref, cbv_ref, o_ref, hbuf_ref,
                   *, bm, s, n_lat):
    i = pl.program_id(0)

    @pl.when(pl.program_id(1) == 0)
    def _():
        hbuf_ref[0:HALO, :] = hp_ref[...]
        hbuf_ref[HALO:HALO + bm, :] = h_ref[...]
        hbuf_ref[HALO + bm:, :] = hn_ref[...]

    pos = lax.rem(i * bm + lax.broadcasted_iota(jnp.int32, (bm, 1), 0), s)
    has_prev = jnp.where((pos == 0) | (pos == n_lat), 0.0, 1.0)
    has_next = jnp.where((pos == n_lat - 1) | (pos == s - 1), 0.0, 1.0)
    rows = bm + 2 * HALO

    def conv(w_ref, cw_ref, cb_ref):
        u = jnp.dot(hbuf_ref[...], w_ref[...], preferred_element_type=F32)
        prev = pltpu.roll(u, 1, 0)[HALO:HALO + bm]
        nxt = pltpu.roll(u, rows - 1, 0)[HALO:HALO + bm]
        cw = cw_ref[...]
        return (prev * has_prev * cw[0:1] + u[HALO:HALO + bm] * cw[1:2] + nxt * has_next * cw[2:3]
                + cb_ref[...])

    gate = conv(wg_ref, cwg_ref, cbg_ref)
    val = conv(wv_ref, cwv_ref, cbv_ref)
    o_ref[...] = (gate * _sigmoid(gate) * val).astype(o_ref.dtype)


def ffn_up(h, w_up, conv_w, conv_b, layer, s, n_lat):
    r, d = h.shape
    f = w_up.shape[1] // 2
    bm = _pick(r, (512, 256, 128))
    bn = _pick(f, (640, 512, 384, 256, 128))
    nb = f // bn
    halo_blocks = r // HALO
    per_tile = bm // HALO
    cb3 = conv_b.reshape(conv_b.shape[0], 1, 2 * f)
    return pl.pallas_call(
        functools.partial(_ffn_up_kernel, bm=bm, s=s, n_lat=n_lat),
        grid=(r // bm, nb),
        in_specs=[
            pl.BlockSpec((bm, d), lambda i, j: (i, 0)),
            pl.BlockSpec((HALO, d), lambda i, j: (jnp.maximum(i * per_tile - 1, 0), 0)),
            pl.BlockSpec((HALO, d), lambda i, j: (jnp.minimum((i + 1) * per_tile, halo_blocks - 1), 0)),
            pl.BlockSpec((d, bn), lambda i, j: (0, j)),
            pl.BlockSpec((d, bn), lambda i, j: (0, nb + j)),
            pl.BlockSpec((None, 3, bn), lambda i, j: (layer, 0, j)),
            pl.BlockSpec((None, 3, bn), lambda i, j: (layer, 0, nb + j)),
            pl.BlockSpec((None, 1, bn), lambda i, j: (layer, 0, j)),
            pl.BlockSpec((None, 1, bn), lambda i, j: (layer, 0, nb + j)),
        ],
        out_specs=pl.BlockSpec((bm, bn), lambda i, j: (i, j)),
        out_shape=jax.ShapeDtypeStruct((r, f), BF16),
        scratch_shapes=[pltpu.VMEM((bm + 2 * HALO, d), BF16)],
        compiler_params=_params("parallel", "arbitrary"),
        name="ffn_up",
    )(h, h, h, w_up, w_up, conv_w, conv_w, cb3, cb3)


def _split_w_in(w, d, heads):
    mix = heads * HEAD_DIM
    q_rank = 3 * d // 16
    kv_rank = d // 8
    sizes = [('a', 3 * mix), ('b_qa', q_rank), ('b_kva', kv_rank + MLA_ROPE), ('c', 5 * mix), ('d', 3 * mix),
             ('gate', N_BRANCH * d)]
    out, start = {}, 0
    for name, size in sizes:
        out[name] = (start, size)
        start += size
    assert start == w.shape[1]

    def cols(name):
        a, n = out[name]
        return w[:, a:a + n].astype(BF16)

    used = q_rank + kv_rank + LANE
    pad = -used % 512 + LANE - MLA_ROPE
    w_b = jnp.concatenate([cols('b_qa'), cols('b_kva'), jnp.zeros((d, pad), BF16)], axis=1)
    return cols('a'), w_b, cols('c'), cols('d'), cols('gate'), q_rank, kv_rank


def _mla_q_weight(w_qb, heads):
    rq = w_qb.shape[0]
    w = w_qb.reshape(rq, heads, MLA_NOPE + MLA_ROPE)
    nope = w[:, :, :MLA_NOPE].reshape(rq, heads * MLA_NOPE)
    ropep = jnp.pad(w[:, :, MLA_NOPE:], ((0, 0), (0, 0), (0, LANE - MLA_ROPE))).reshape(rq, heads * LANE)
    return jnp.concatenate([nope, ropep], axis=1).astype(BF16)


def kernel(x, c, ctx, c_ctx, ada_w, ada_b, norm_mix_pre, norm_mix_post, norm_ffn_pre, norm_ffn_post, w_in,
           diff_lambda, diff_subln, mla_q_norm, mla_w_qb, mla_kv_norm, mla_w_kvb, hgrn_lb_logits, hgrn_norm,
           na_rpb, w_branch, w_out, ffn_w_up, ffn_conv_w, ffn_conv_b, ffn_w_down):
    bsz, n_lat, d = x.shape
    n_ctx = ctx.shape[1]
    s = n_lat + n_ctx
    r = bsz * s
    depth = ada_w.shape[0]
    heads = d // (N_BRANCH * HEAD_DIM)
    mix = heads * HEAD_DIM
    assert n_lat % 256 == 0 and n_ctx % 256 == 0 and s % n_ctx == 0 and n_lat % GRID_W == 0

    tables = _rope_tables(n_lat, n_ctx)
    lb_sm = jax.nn.softmax(hgrn_lb_logits.astype(F32), axis=0)
    lower_bounds = jnp.cumsum(lb_sm, axis=0) - lb_sm[0]

    c_rows = jnp.zeros((8, d), F32).at[:bsz].set(c).at[bsz].set(c_ctx)
    xa = jnp.concatenate([x, ctx], axis=1)

    def mod_rows(mod, idx):
        part = mod[:, idx * d:(idx + 1) * d]
        both = jnp.stack([part[:bsz], jnp.broadcast_to(part[bsz], (bsz, d))], axis=1)
        return both.reshape(2 * bsz, 1, d)

    mods = []
    for l in range(depth):
        mod = ada_modulation(c_rows, ada_w, ada_b, l)
        mods.append([mod_rows(mod, i) for i in range(6)])

    h = norm_modulate(xa, norm_mix_pre[0], mods[0][0], mods[0][1], n_lat)
    for l in range(depth):
        with_ctx = l < depth - 1
        sh1, sc1, g1, sh2, sc2, g2 = mods[l]
        h2d = h.reshape(r, d)

        w_a, w_b, w_c, w_d, w_g, q_rank, kv_rank = _split_w_in(w_in[l], d, heads)
        pa = matmul(h2d, w_a, BF16, "in_proj_a")
        pb = matmul(h2d, w_b, BF16, "in_proj_b")
        pc = matmul(h2d, w_c, BF16, "in_proj_c")
        pd = matmul(h2d, w_d, BF16, "in_proj_d")
        pg = matmul(h2d, w_g, BF16, "in_proj_gate")

        qk = rope(pa, tables, 0, 2, s, width=mix)
        lam_init = 0.8 - 0.6 * math.exp(-0.3 * l)
        extras_a = [diff_lambda[l].astype(F32), diff_subln[l].reshape(1, HEAD_DIM).astype(F32)]

        def call_a(q_range, k_range, tag):
            return _attention_call(
                functools.partial(_diff_attn_kernel, lam_init=lam_init),
                [(qk, 0)], [(qk, heads, 1, False)], (pa, 2 * heads, 1), extras_a,
                bsz=bsz, heads=heads, s=s, q_range=q_range, k_range=k_range, tq=128, name="diff_attn_" + tag)

        oa = _attend_lat_and_ctx(call_a, n_lat, n_ctx, with_ctx)

        qn, cn, kr = mla_prep(pb, mla_q_norm[l], mla_kv_norm[l], tables, s)
        qb = matmul(qn, _mla_q_weight(mla_w_qb[l], heads), BF16, "mla_q_up")
        qb_rope = rope(qb, tables, 1, 1, s, width=mix)
        kvb = matmul(cn, mla_w_kvb[l].astype(BF16), BF16, "mla_kv_up")

        def call_b(q_range, k_range, tag):
            return _attention_call(
                functools.partial(_sdpa_kernel, n_qk=2, scale=(MLA_NOPE + MLA_ROPE) ** -0.5),
                [(qb, 0), (qb_rope, 0)], [(kvb, 0, 2, False), (kr, 0, 0, True)], (kvb, 1, 2), [],
                bsz=bsz, heads=heads, s=s, q_range=q_range, k_range=k_range, tq=128, name="mla_attn_" + tag)

        ob = _attend_lat_and_ctx(call_b, n_lat, n_ctx, with_ctx)

        o_fwd = hgrn_scan(pc, lower_bounds[l, 0], bsz, heads, n_lat, n_ctx, reverse=False)
        o_bwd = hgrn_scan(pc, lower_bounds[l, 1], bsz, heads, n_lat, n_ctx, reverse=True)
        oc = hgrn_readout(o_fwd, o_bwd, pc, hgrn_norm[l])

        od_lat = neighbourhood_attention(pd, _na_bias_table(na_rpb[l]), bsz, heads, n_lat, n_ctx)
        od_lat = od_lat.reshape(bsz, n_lat, mix)
        if with_ctx:
            od_ctx = _attention_call(
                functools.partial(_sdpa_kernel, n_qk=1, scale=HEAD_DIM ** -0.5),
                [(pd, 0)], [(pd, heads, 1, False)], (pd, 2 * heads, 1), [],
                bsz=bsz, heads=heads, s=s, q_range=(n_lat, n_ctx), k_range=(n_lat, n_ctx), tq=128,
                name="na_ctx_attn").reshape(bsz, n_ctx, mix)
        else:
            od_ctx = jnp.zeros((bsz, n_ctx, mix), BF16)
        od = jnp.concatenate([od_lat, od_ctx], axis=1).reshape(r, mix)

        merged = merge_branches((oa, ob, oc, od), pg, w_branch[l].astype(BF16))
        mix_out = matmul(merged, w_out[l].astype(BF16), F32, "out_proj").reshape(bsz, s, d)
        xa, h2 = residual_norm_modulate(xa, mix_out, g1, norm_mix_post[l], norm_ffn_pre[l], sh2, sc2, n_lat)

        gv = ffn_up(h2.reshape(r, d), ffn_w_up[l].astype(BF16), ffn_conv_w, ffn_conv_b, l, s, n_lat)
        ffn = matmul(gv, ffn_w_down[l].astype(BF16), F32, "ffn_down").reshape(bsz, s, d)
        if with_ctx:
            xa, h = residual_norm_modulate(xa, ffn, g2, norm_ffn_post[l], norm_mix_pre[l + 1],
                                           mods[l + 1][0], mods[l + 1][1], n_lat)
        else:
            xa = residual(xa, ffn, g2, norm_ffn_post[l], n_lat)
    return xa[:, :n_lat]
```

```python
import functools
import math

import numpy as np
import jax
import jax.numpy as jnp
from jax import lax
from jax.experimental import pallas as pl
from jax.experimental.pallas import tpu as pltpu

F32 = jnp.float32
BF16 = jnp.bfloat16

HEAD_DIM = 128
QK_HALF = 64
MLA_NOPE = 128
MLA_ROPE = 64
MLA_V = 128
ROPE_DIM = 64
ROPE_THETA = 10000.0
GRID_W = 64
NA_ROWS = 8
NA_COLS = 16
CHUNK = 64
NORM_EPS = 1e-6
NEG_INF = -1e30
N_BRANCH = 4

LANE = 128
VMEM_LIMIT_BYTES = 56 * 1024 * 1024

NT_DIMS = (((1,), (1,)), ((), ()))
TN_DIMS = (((0,), (0,)), ((), ()))


def _params(*sem):
    return pltpu.CompilerParams(dimension_semantics=sem, vmem_limit_bytes=VMEM_LIMIT_BYTES)


def _sigmoid(x):
    return 1.0 / (1.0 + jnp.exp(-x))


def _pick(total, candidates):
    for c in candidates:
        if total % c == 0:
            return c
    raise ValueError(f"no tile for {total} in {candidates}")


def _ada_kernel(c_ref, w_ref, b_ref, o_ref):
    c = c_ref[...]
    a = (c * _sigmoid(c)).astype(BF16)
    o_ref[...] = jnp.dot(a, w_ref[...].astype(BF16), preferred_element_type=F32) + b_ref[...]


def ada_modulation(c_rows, ada_w, ada_b, layer):
    rows, d = c_rows.shape
    width = ada_w.shape[2]
    tn = _pick(width, (512, 256, 128))
    b3 = ada_b.reshape(ada_b.shape[0], 1, width)
    return pl.pallas_call(
        _ada_kernel,
        grid=(width // tn,),
        in_specs=[
            pl.BlockSpec((rows, d), lambda j: (0, 0)),
            pl.BlockSpec((None, d, tn), lambda j: (layer, 0, j)),
            pl.BlockSpec((None, 1, tn), lambda j: (layer, 0, j)),
        ],
        out_specs=pl.BlockSpec((rows, tn), lambda j: (0, j)),
        out_shape=jax.ShapeDtypeStruct((rows, width), F32),
        compiler_params=_params("parallel"),
        name="ada_modulation",
    )(c_rows, ada_w, b3)


def _rms(x, gain):
    return x * lax.rsqrt(jnp.mean(x * x, axis=-1, keepdims=True) + NORM_EPS) * gain


def _norm_mod_kernel(x_ref, g_ref, sh_ref, sc_ref, h_ref):
    h = _rms(x_ref[0], g_ref[...])
    h_ref[0] = (h * (1.0 + sc_ref[0]) + sh_ref[0]).astype(h_ref.dtype)


def _resid_kernel(x_ref, y_ref, gate_ref, gpost_ref, xo_ref):
    xo_ref[0] = x_ref[0] + gate_ref[0] * _rms(y_ref[0], gpost_ref[...])


def _resid_norm_mod_kernel(x_ref, y_ref, gate_ref, gpost_ref, gpre_ref, sh_ref, sc_ref, xo_ref, h_ref):
    xn = x_ref[0] + gate_ref[0] * _rms(y_ref[0], gpost_ref[...])
    xo_ref[0] = xn
    h = _rms(xn, gpre_ref[...])
    h_ref[0] = (h * (1.0 + sc_ref[0]) + sh_ref[0]).astype(h_ref.dtype)


def _row_specs(dims, tr, n_lat_tiles):
    s, d = dims
    xspec = pl.BlockSpec((1, tr, d), lambda b, t: (b, t, 0))
    gspec = pl.BlockSpec((1, d), lambda b, t: (0, 0))
    mspec = pl.BlockSpec((1, 1, d), lambda b, t: (2 * b + t // n_lat_tiles, 0, 0))
    return xspec, gspec, mspec


def norm_modulate(x, gain, shift, scale, n_lat, tr=256):
    bsz, s, d = x.shape
    xspec, gspec, mspec = _row_specs((s, d), tr, n_lat // tr)
    return pl.pallas_call(
        _norm_mod_kernel,
        grid=(bsz, s // tr),
        in_specs=[xspec, gspec, mspec, mspec],
        out_specs=xspec,
        out_shape=jax.ShapeDtypeStruct(x.shape, BF16),
        compiler_params=_params("parallel", "parallel"),
        name="norm_modulate",
    )(x, gain.reshape(1, d), shift, scale)


def residual(x, y, gate, gain_post, n_lat, tr=256):
    bsz, s, d = x.shape
    rows = y.shape[1]
    xspec, gspec, mspec = _row_specs((s, d), tr, n_lat // tr)
    return pl.pallas_call(
        _resid_kernel,
        grid=(bsz, rows // tr),
        in_specs=[xspec, xspec, mspec, gspec],
        out_specs=xspec,
        out_shape=jax.ShapeDtypeStruct(y.shape, F32),
        compiler_params=_params("parallel", "parallel"),
        name="residual",
    )(x, y, gate, gain_post.reshape(1, d))


def residual_norm_modulate(x, y, gate, gain_post, gain_pre, shift, scale, n_lat, tr=256):
    bsz, s, d = x.shape
    rows = y.shape[1]
    xspec, gspec, mspec = _row_specs((s, d), tr, n_lat // tr)
    return pl.pallas_call(
        _resid_norm_mod_kernel,
        grid=(bsz, rows // tr),
        in_specs=[xspec, xspec, mspec, gspec, gspec, mspec, mspec],
        out_specs=[xspec, xspec],
        out_shape=[jax.ShapeDtypeStruct(y.shape, F32), jax.ShapeDtypeStruct(y.shape, BF16)],
        compiler_params=_params("parallel", "parallel"),
        name="residual_norm_modulate",
    )(x, y, gate, gain_post.reshape(1, d), gain_pre.reshape(1, d), shift, scale)


def _mm_kernel(a_ref, b_ref, o_ref):
    o_ref[...] = jnp.dot(a_ref[...], b_ref[...], preferred_element_type=F32).astype(o_ref.dtype)


def _mm_acc_kernel(a_ref, b_ref, o_ref, acc_ref):
    k = pl.program_id(2)

    @pl.when(k == 0)
    def _():
        acc_ref[...] = jnp.zeros_like(acc_ref)

    acc_ref[...] += jnp.dot(a_ref[...], b_ref[...], preferred_element_type=F32)

    @pl.when(k == pl.num_programs(2) - 1)
    def _():
        o_ref[...] = acc_ref[...].astype(o_ref.dtype)


MAX_SINGLE_K = 4096


def matmul_leading_rows(a, b, rows, out_dtype, name):
    bsz, _, k = a.shape
    n = b.shape[1]
    assert k <= MAX_SINGLE_K
    bm = _pick(rows, (1024, 512, 256, 128))
    bn = _pick(n, (1024, 768, 640, 512, 384, 256, 128))
    tiles = rows // bm
    return pl.pallas_call(
        _mm_kernel,
        grid=(bsz, tiles, n // bn),
        in_specs=[pl.BlockSpec((None, bm, k), lambda s, i, j: (s, i, 0)),
                  pl.BlockSpec((k, bn), lambda s, i, j: (0, j))],
        out_specs=pl.BlockSpec((bm, bn), lambda s, i, j: (s * tiles + i, j)),
        out_shape=jax.ShapeDtypeStruct((bsz * rows, n), out_dtype),
        compiler_params=_params("parallel", "parallel", "parallel"),
        name=name,
    )(a, b)


def matmul(a, b, out_dtype, name):
    m, k = a.shape
    n = b.shape[1]
    bm = _pick(m, (1024, 512, 256, 128))
    bn = _pick(n, (1024, 768, 640, 512, 384, 256, 128))
    if k <= MAX_SINGLE_K:
        return pl.pallas_call(
            _mm_kernel,
            grid=(m // bm, n // bn),
            in_specs=[pl.BlockSpec((bm, k), lambda i, j: (i, 0)),
                      pl.BlockSpec((k, bn), lambda i, j: (0, j))],
            out_specs=pl.BlockSpec((bm, bn), lambda i, j: (i, j)),
            out_shape=jax.ShapeDtypeStruct((m, n), out_dtype),
            compiler_params=_params("parallel", "parallel"),
            name=name,
        )(a, b)
    nk = -(-k // MAX_SINGLE_K)
    while k % nk or (k // nk) % LANE:
        nk += 1
    bk = k // nk
    return pl.pallas_call(
        _mm_acc_kernel,
        grid=(m // bm, n // bn, nk),
        in_specs=[pl.BlockSpec((bm, bk), lambda i, j, kk: (i, kk)),
                  pl.BlockSpec((bk, bn), lambda i, j, kk: (kk, j))],
        out_specs=pl.BlockSpec((bm, bn), lambda i, j, kk: (i, j)),
        out_shape=jax.ShapeDtypeStruct((m, n), out_dtype),
        scratch_shapes=[pltpu.VMEM((bm, bn), F32)],
        compiler_params=_params("parallel", "parallel", "arbitrary"),
        name=name,
    )(a, b)


def _rope_tables(n_lat, n_ctx):
    t = jnp.arange(n_lat, dtype=jnp.int32)
    row = (t // GRID_W).astype(F32)
    col = (t % GRID_W).astype(F32)
    quarter = ROPE_DIM // 4
    inv = ROPE_THETA ** (-jnp.arange(quarter, dtype=F32) / quarter)
    ar = row[:, None] * inv
    ac = col[:, None] * inv
    ang = jnp.concatenate([ar, ar, ac, ac], axis=-1)
    cos = jnp.concatenate([jnp.cos(ang), jnp.ones((n_ctx, ROPE_DIM), F32)], axis=0)
    sin = jnp.concatenate([jnp.sin(ang), jnp.zeros((n_ctx, ROPE_DIM), F32)], axis=0)
    first = (jnp.arange(ROPE_DIM) // quarter) % 2 == 0
    sin_next = jnp.where(first, -sin, 0.0)
    sin_prev = jnp.where(first, 0.0, sin)
    reps = LANE // ROPE_DIM
    return tuple(jnp.tile(a, (1, reps)) for a in (cos, sin_next, sin_prev))


def _rope_apply(x, cos, sin_next, sin_prev):
    quarter = ROPE_DIM // 4
    return (x * cos + pltpu.roll(x, LANE - quarter, 1) * sin_next
            + pltpu.roll(x, quarter, 1) * sin_prev)


def _rope_kernel(x_ref, cos_ref, sn_ref, sp_ref, o_ref):
    cos, sn, sp = cos_ref[...], sn_ref[...], sp_ref[...]
    for c in range(x_ref.shape[1] // LANE):
        sl = slice(c * LANE, (c + 1) * LANE)
        o_ref[:, sl] = _rope_apply(x_ref[:, sl].astype(F32), cos, sn, sp).astype(o_ref.dtype)


def rope(x, tables, col_block, n_col_blocks, s, width=1024, tr=256):
    r = x.shape[0]
    width = min(width, x.shape[1])
    tiles_per_seq = s // tr
    tspec = pl.BlockSpec((tr, LANE), lambda i, j: (i % tiles_per_seq, 0))
    return pl.pallas_call(
        _rope_kernel,
        grid=(r // tr, n_col_blocks),
        in_specs=[pl.BlockSpec((tr, width), lambda i, j: (i, col_block + j)), tspec, tspec, tspec],
        out_specs=pl.BlockSpec((tr, width), lambda i, j: (i, j)),
        out_shape=jax.ShapeDtypeStruct((r, width * n_col_blocks), BF16),
        compiler_params=_params("parallel", "parallel"),
        name="rope",
    )(x, *tables)


LOG2E = math.log2(math.e)
KEY_CHUNK = 256


def _attend_tiles(k_ref, q_tiles, vt_ref, s_ref, e_ref):
    m_keys, n = s_ref.shape[1:]
    kc = min(KEY_CHUNK, m_keys)
    chunks = [slice(c * kc, (c + 1) * kc) for c in range(m_keys // kc)]

    def scores(t):
        s_ref[t % 2] = lax.dot_general(k_ref[...], q_tiles[t], NT_DIMS, preferred_element_type=F32)

    def column_max(t):
        m8 = None
        for sl in chunks:
            part = jnp.max(s_ref[t % 2, sl, :].reshape(kc // 8, 8, n), axis=0)
            m8 = part if m8 is None else jnp.maximum(m8, part)
        return jnp.max(m8, axis=0, keepdims=True)

    scores(0)
    m = column_max(0)
    outs = []
    for t in range(len(q_tiles)):
        if t + 1 < len(q_tiles):
            scores(t + 1)
        l8 = jnp.zeros((8, n), F32)
        for sl in chunks:
            e = jnp.exp2(s_ref[t % 2, sl, :] - m)
            l8 = l8 + jnp.sum(e.reshape(kc // 8, 8, n), axis=0)
            e_ref[t % 2, sl, :] = e.astype(e_ref.dtype)
        o_t = jnp.dot(vt_ref[...], e_ref[t % 2], preferred_element_type=F32)
        outs.append((o_t, jnp.sum(l8, axis=0, keepdims=True)))
        if t + 1 < len(q_tiles):
            m = column_max(t + 1)
    return outs


def _load_v_transposed(v_ref, vt_ref):
    @pl.when(pl.program_id(2) == 0)
    def _():
        vt_ref[...] = v_ref[...].astype(F32).T.astype(vt_ref.dtype)


def _diff_attn_kernel(q_ref, k_ref, v_ref, lam_ref, g_ref, o_ref, vt_ref, s_ref, e_ref, *, lam_init):
    _load_v_transposed(v_ref, vt_ref)
    tq = s_ref.shape[2] // 2
    q = (q_ref[...].astype(F32) * (QK_HALF ** -0.5 * LOG2E)).astype(k_ref.dtype)
    lane = lax.broadcasted_iota(jnp.int32, q.shape, 1)
    zero = jnp.zeros_like(q)
    q0 = jnp.where(lane < QK_HALF, q, zero)
    q1 = jnp.where(lane >= QK_HALF, q, zero)
    tiles = [slice(t * tq, (t + 1) * tq) for t in range(q.shape[0] // tq)]
    outs = _attend_tiles(k_ref, [jnp.concatenate([q0[sl], q1[sl]], axis=0) for sl in tiles], vt_ref, s_ref, e_ref)
    lp = lam_ref[...]
    lam = (jnp.exp(jnp.sum(lp[0:1] * lp[1:2], axis=-1, keepdims=True))
           - jnp.exp(jnp.sum(lp[2:3] * lp[3:4], axis=-1, keepdims=True)) + lam_init)
    gain = g_ref[...] * (1.0 - lam_init)
    for sl, (o_t, l) in zip(tiles, outs):
        inv = 1.0 / l
        c_t = o_t[:, :tq] * inv[:, :tq] - o_t[:, tq:] * (lam * inv[:, tq:])
        c_t = c_t * lax.rsqrt(jnp.mean(c_t * c_t, axis=0, keepdims=True) + NORM_EPS)
        o_ref[sl, :] = (c_t.T * gain).astype(o_ref.dtype)


def _sdpa_kernel(*refs, n_qk, scale):
    q_refs, k_refs = refs[:n_qk], refs[n_qk:2 * n_qk]
    v_ref, o_ref, vt_ref, s_ref, e_ref = refs[2 * n_qk:2 * n_qk + 5]
    _load_v_transposed(v_ref, vt_ref)
    if n_qk == 1:
        kcat_ref = k_refs[0]
    else:
        kcat_ref = refs[2 * n_qk + 5]

        @pl.when(pl.program_id(2) == 0)
        def _():
            for i, k_ref in enumerate(k_refs):
                kcat_ref[:, i * HEAD_DIM:(i + 1) * HEAD_DIM] = k_ref[...]

    tq = s_ref.shape[2]
    q = jnp.concatenate([q_ref[...] for q_ref in q_refs], axis=1)
    q = (q.astype(F32) * (scale * LOG2E)).astype(kcat_ref.dtype)
    tiles = [slice(t * tq, (t + 1) * tq) for t in range(q.shape[0] // tq)]
    outs = _attend_tiles(kcat_ref, [q[sl] for sl in tiles], vt_ref, s_ref, e_ref)
    for sl, (o_t, l) in zip(tiles, outs):
        o_ref[sl, :] = (o_t * (1.0 / l)).T.astype(o_ref.dtype)


def _attention_call(kernel, q_list, k_list, v, extras, *, bsz, heads, s, q_range, k_range, tq, lanes, name):
    q_start, q_len = q_range
    k_start, k_len = k_range
    nq = q_len // tq
    q_off = q_start // tq
    k_off = k_start // k_len
    assert q_start % tq == 0 and q_len % tq == 0 and k_start % k_len == 0

    def qspec(col0):
        return pl.BlockSpec((None, tq, HEAD_DIM), lambda b, h, i: (b, q_off + i, col0 + h))

    def kspec(col0, stride, shared=False):
        if shared:
            return pl.BlockSpec((None, k_len, HEAD_DIM), lambda b, h, i: (b, k_off, col0))
        return pl.BlockSpec((None, k_len, HEAD_DIM), lambda b, h, i: (b, k_off, col0 + stride * h))

    def per_sample(arr):
        return arr.reshape(bsz, s, arr.shape[-1])

    in_specs, args = [], []
    for arr, col0 in q_list:
        in_specs.append(qspec(col0)); args.append(per_sample(arr))
    for arr, col0, stride, shared in k_list:
        in_specs.append(kspec(col0, stride, shared)); args.append(per_sample(arr))
    arr, col0, stride = v
    in_specs.append(kspec(col0, stride)); args.append(per_sample(arr))
    for e in extras:
        in_specs.append(pl.BlockSpec(e.shape, lambda b, h, i: (0, 0))); args.append(e)
    return pl.pallas_call(
        kernel,
        grid=(bsz, heads, nq),
        in_specs=in_specs,
        out_specs=pl.BlockSpec((tq, HEAD_DIM), lambda b, h, i: (b * nq + i, h)),
        out_shape=jax.ShapeDtypeStruct((bsz * q_len, heads * HEAD_DIM), BF16),
        scratch_shapes=[pltpu.VMEM((HEAD_DIM, k_len), BF16), pltpu.VMEM((2, k_len, lanes), F32),
                        pltpu.VMEM((2, k_len, lanes), BF16)]
        + ([pltpu.VMEM((k_len, len(k_list) * HEAD_DIM), BF16)] if len(k_list) > 1 else []),
        compiler_params=_params("parallel", "parallel", "arbitrary"),
        name=name,
    )(*args)


def _attend_lat_and_ctx(call, n_lat, n_ctx, with_ctx):
    s = n_lat + n_ctx
    o_lat = call((0, n_lat), (0, s), "lat")
    if not with_ctx:
        return o_lat
    bsz = o_lat.shape[0] // n_lat
    o_ctx = call((n_lat, n_ctx), (n_lat, n_ctx), "ctx").reshape(bsz, n_ctx, -1)
    return jnp.concatenate([o_lat.reshape(bsz, n_lat, -1), o_ctx], axis=1).reshape(bsz * s, -1)


def _na_bias_table(rpb):
    col = jnp.arange(GRID_W, dtype=jnp.int32)
    cs = jnp.clip(col - NA_COLS // 2, 0, GRID_W - NA_COLS)
    colmask = (col[None, :] >= cs[:, None]) & (col[None, :] < cs[:, None] + NA_COLS)
    dc = jnp.clip(col[None, :] - col[:, None] + NA_COLS - 1, 0, 2 * NA_COLS - 2)
    delta = jnp.arange(NA_ROWS, dtype=jnp.int32)
    jj = jnp.arange(NA_ROWS, dtype=jnp.int32)
    dr = jj[None, :] - delta[:, None] + NA_ROWS - 1
    bias = rpb[:, dr][:, :, :, dc]
    bias = jnp.where(colmask[None, None, None], bias, NEG_INF)
    bias = jnp.transpose(bias, (0, 1, 3, 2, 4))
    return bias.reshape(rpb.shape[0], NA_ROWS, GRID_W, NA_ROWS * GRID_W).astype(F32)


def _na_kernel(q_ref, k_ref, v_ref, bias_ref, o_ref, *, rows_per_step, n_lat, grid_rows):
    g = pl.program_id(2)
    scale = HEAD_DIM ** -0.5
    win = NA_ROWS * GRID_W
    kc = k_ref[n_lat:, :]
    vc = v_ref[n_lat:, :]
    for i in range(rows_per_step):
        r = g * rows_per_step + i
        rs = jnp.clip(r - NA_ROWS // 2, 0, grid_rows - NA_ROWS)
        start = pl.multiple_of(rs * GRID_W, GRID_W)
        q = q_ref[i * GRID_W:(i + 1) * GRID_W, :]
        kw = k_ref[pl.ds(start, win), :]
        vw = v_ref[pl.ds(start, win), :]
        sl = lax.dot_general(q, kw, NT_DIMS, preferred_element_type=F32) * scale + bias_ref[r - rs]
        sc = lax.dot_general(q, kc, NT_DIMS, preferred_element_type=F32) * scale
        m = jnp.maximum(jnp.max(sl, axis=-1, keepdims=True), jnp.max(sc, axis=-1, keepdims=True))
        el = jnp.exp(sl - m)
        ec = jnp.exp(sc - m)
        inv = 1.0 / (jnp.sum(el, axis=-1, keepdims=True) + jnp.sum(ec, axis=-1, keepdims=True))
        o = (jnp.dot(el.astype(BF16), vw, preferred_element_type=F32)
             + jnp.dot(ec.astype(BF16), vc, preferred_element_type=F32))
        o_ref[i * GRID_W:(i + 1) * GRID_W, :] = (o * inv).astype(o_ref.dtype)


def neighbourhood_attention(pd, bias, bsz, heads, n_lat, n_ctx, rows_per_step=4):
    s = n_lat + n_ctx
    grid_rows = n_lat // GRID_W
    tq = rows_per_step * GRID_W
    assert grid_rows >= NA_ROWS and grid_rows % rows_per_step == 0 and s % tq == 0
    steps = grid_rows // rows_per_step
    q_tiles = s // tq
    return pl.pallas_call(
        functools.partial(_na_kernel, rows_per_step=rows_per_step, n_lat=n_lat, grid_rows=grid_rows),
        grid=(bsz, heads, steps),
        in_specs=[
            pl.BlockSpec((tq, HEAD_DIM), lambda b, h, g: (b * q_tiles + g, h)),
            pl.BlockSpec((s, HEAD_DIM), lambda b, h, g: (b, heads + h)),
            pl.BlockSpec((s, HEAD_DIM), lambda b, h, g: (b, 2 * heads + h)),
            pl.BlockSpec((None, NA_ROWS, GRID_W, NA_ROWS * GRID_W), lambda b, h, g: (h, 0, 0, 0)),
        ],
        out_specs=pl.BlockSpec((tq, HEAD_DIM), lambda b, h, g: (b * steps + g, h)),
        out_shape=jax.ShapeDtypeStruct((bsz * n_lat, heads * HEAD_DIM), BF16),
        compiler_params=_params("parallel", "parallel", "arbitrary"),
        name="neighbourhood_attention",
    )(pd, pd, pd, bias)


def _hgrn_constants(reverse):
    c = CHUNK
    t = np.arange(c)
    tri = (t[None, :] >= t[:, None]) if reverse else (t[None, :] <= t[:, None])
    mats = [tri.astype(np.float32)]
    masks = [np.eye(c, dtype=np.float32)]
    m = c // 2
    while m >= 1:
        node = (t // (2 * m)) * (2 * m)
        ref = node + (m if reverse else m - 1)
        mats.append(tri[ref].astype(np.float32))
        upper = (t % (2 * m)) >= m
        same = node[:, None] == node[None, :]
        pair = same & (upper[:, None] & ~upper[None, :])
        masks.append((pair.T if reverse else pair).astype(np.float32))
        m //= 2
    return np.concatenate(mats, axis=0), np.stack(masks)


def _hgrn_kernel(cq_ref, ci_ref, z_ref, lb_ref, ms_ref, mk_ref, o_ref, st_ref, *, heads, reverse):
    @pl.when(pl.program_id(1) == 0)
    def _():
        st_ref[...] = jnp.zeros_like(st_ref)

    c = CHUNK
    n_levels = mk_ref.shape[0]
    ms = ms_ref[...].astype(BF16)
    scale = HEAD_DIM ** -0.5
    tot_row = 0 if reverse else c - 1
    for h in range(heads):
        sl = slice(h * HEAD_DIM, (h + 1) * HEAD_DIM)
        lb = lb_ref[:, sl]
        f = lb + (1.0 - lb) * _sigmoid(z_ref[:, sl].astype(F32))
        kk = 1.0 - f
        g = jnp.log(f)
        cq = cq_ref[:, sl].astype(F32)
        qq = cq * _sigmoid(cq) * scale
        v = ci_ref[:, sl]
        g_hi = g.astype(BF16)
        r1 = g - g_hi.astype(F32)
        g_mid = r1.astype(BF16)
        g_lo = (r1 - g_mid.astype(F32)).astype(BF16)
        cs3 = jnp.dot(ms, jnp.concatenate([g_hi, g_mid, g_lo], axis=1), preferred_element_type=F32)
        cs = cs3[:, :HEAD_DIM] + cs3[:, HEAD_DIM:2 * HEAD_DIM] + cs3[:, 2 * HEAD_DIM:]
        cum = cs[:c]
        tot = cs[tot_row:tot_row + 1]
        st = st_ref[h]
        o = lax.dot_general((qq * jnp.exp(cum)).astype(BF16), st.astype(BF16), NT_DIMS,
                            preferred_element_type=F32)
        a = mk_ref[0] * lax.dot_general(qq.astype(BF16), kk.astype(BF16), NT_DIMS,
                                        preferred_element_type=F32)
        for lev in range(1, n_levels):
            e = jnp.exp(-jnp.abs(cum - cs[lev * c:(lev + 1) * c]))
            a = a + mk_ref[lev] * lax.dot_general((qq * e).astype(BF16), (kk * e).astype(BF16), NT_DIMS,
                                                  preferred_element_type=F32)
        o = o + jnp.dot(a.astype(BF16), v, preferred_element_type=F32)
        o_ref[:, sl] = o
        ke = (kk * jnp.exp(tot - cum)).astype(BF16)
        st_ref[h] = st * jnp.exp(tot) + lax.dot_general(v, ke, TN_DIMS, preferred_element_type=F32)


def hgrn_scan(pc, lb, bsz, heads, n_lat, n_ctx, reverse):
    s = n_lat + n_ctx
    chunks = s // CHUNK
    lat_chunks = n_lat // CHUNK
    width = heads * HEAD_DIM
    ms, mk = _hgrn_constants(reverse)

    def chunk_of(step):
        return (chunks - 1 - step) if reverse else lax.rem(step + lat_chunks, chunks)

    def spec(col):
        return pl.BlockSpec((CHUNK, width), lambda b, t: (b * chunks + chunk_of(t), col))

    return pl.pallas_call(
        functools.partial(_hgrn_kernel, heads=heads, reverse=reverse),
        grid=(bsz, chunks),
        in_specs=[spec(0), spec(1), spec(3 if reverse else 2),
                  pl.BlockSpec((1, width), lambda b, t: (0, 0)),
                  pl.BlockSpec(ms.shape, lambda b, t: (0, 0)),
                  pl.BlockSpec(mk.shape, lambda b, t: (0, 0, 0))],
        out_specs=spec(0),
        out_shape=jax.ShapeDtypeStruct((bsz * s, width), F32),
        scratch_shapes=[pltpu.VMEM((heads, HEAD_DIM, HEAD_DIM), F32)],
        compiler_params=_params("parallel", "arbitrary"),
        name="hgrn_scan_bwd" if reverse else "hgrn_scan_fwd",
    )(pc, pc, pc, lb.reshape(1, width), jnp.asarray(ms), jnp.asarray(mk))


def _hgrn_readout_kernel(of_ref, ob_ref, gate_ref, g_ref, o_ref):
    gain = g_ref[...]
    for h in range(of_ref.shape[1] // HEAD_DIM):
        sl = slice(h * HEAD_DIM, (h + 1) * HEAD_DIM)
        gate = gate_ref[:, sl].astype(F32)
        o = _rms(of_ref[:, sl] + ob_ref[:, sl], gain)
        o_ref[:, sl] = (o * (gate * _sigmoid(gate))).astype(o_ref.dtype)


def hgrn_readout(o_fwd, o_bwd, pc, gain, bsz, s, rows, tr=256):
    width = o_fwd.shape[1]
    in_tiles, out_tiles = s // tr, rows // tr
    ispec = pl.BlockSpec((tr, width), lambda b, i: (b * in_tiles + i, 0))
    return pl.pallas_call(
        _hgrn_readout_kernel,
        grid=(bsz, out_tiles),
        in_specs=[ispec, ispec, pl.BlockSpec((tr, width), lambda b, i: (b * in_tiles + i, 4)),
                  pl.BlockSpec((1, HEAD_DIM), lambda b, i: (0, 0))],
        out_specs=pl.BlockSpec((tr, width), lambda b, i: (b * out_tiles + i, 0)),
        out_shape=jax.ShapeDtypeStruct((bsz * rows, width), BF16),
        compiler_params=_params("parallel", "parallel"),
        name="hgrn_readout",
    )(o_fwd, o_bwd, pc, gain.reshape(1, HEAD_DIM))


def _mla_prep_kernel(p_ref, qg_ref, kg_ref, cos_ref, sn_ref, sp_ref, qn_ref, cn_ref, kr_ref, *, q_rank, kv_rank):
    qn_ref[...] = _rms(p_ref[:, :q_rank].astype(F32), qg_ref[...]).astype(qn_ref.dtype)
    cn_ref[...] = _rms(p_ref[:, q_rank:q_rank + kv_rank].astype(F32), kg_ref[...]).astype(cn_ref.dtype)
    kr = p_ref[:, q_rank + kv_rank:q_rank + kv_rank + LANE].astype(F32)
    kr_ref[...] = _rope_apply(kr, cos_ref[...], sn_ref[...], sp_ref[...]).astype(kr_ref.dtype)


def mla_prep(pb, q_gain, kv_gain, tables, s, tr=256):
    r, width = pb.shape
    q_rank, kv_rank = q_gain.shape[0], kv_gain.shape[0]
    assert width >= q_rank + kv_rank + LANE and q_rank % LANE == 0 and kv_rank % LANE == 0
    tiles_per_seq = s // tr
    tspec = pl.BlockSpec((tr, LANE), lambda i: (i % tiles_per_seq, 0))

    def ospec(w):
        return pl.BlockSpec((tr, w), lambda i: (i, 0))

    return pl.pallas_call(
        functools.partial(_mla_prep_kernel, q_rank=q_rank, kv_rank=kv_rank),
        grid=(r // tr,),
        in_specs=[ospec(width), pl.BlockSpec((1, q_rank), lambda i: (0, 0)),
                  pl.BlockSpec((1, kv_rank), lambda i: (0, 0)), tspec, tspec, tspec],
        out_specs=[ospec(q_rank), ospec(kv_rank), ospec(LANE)],
        out_shape=[jax.ShapeDtypeStruct((r, q_rank), BF16), jax.ShapeDtypeStruct((r, kv_rank), BF16),
                   jax.ShapeDtypeStruct((r, LANE), BF16)],
        compiler_params=_params("parallel"),
        name="mla_prep",
    )(pb, q_gain.reshape(1, q_rank), kv_gain.reshape(1, kv_rank), *tables)


def _merge_kernel(oa_ref, ob_ref, oc_ref, od_ref, ga_ref, gb_ref, gc_ref, gd_ref, w_ref, o_ref):
    acc = None
    for i, (o_ref_i, g_ref_i) in enumerate(((oa_ref, ga_ref), (ob_ref, gb_ref), (oc_ref, gc_ref), (od_ref, gd_ref))):
        y = jnp.dot(o_ref_i[...], w_ref[i], preferred_element_type=F32)
        term = _sigmoid(g_ref_i[...].astype(F32)) * y
        acc = term if acc is None else acc + term
    o_ref[...] = acc.astype(o_ref.dtype)


def merge_branches(outs, gate_logits, w_branch):
    r, c = outs[0].shape
    d = w_branch.shape[2]
    bm = _pick(r, (512, 256, 128))
    bn = _pick(d, (1024, 512, 256, 128))
    nb = d // bn
    ospec = pl.BlockSpec((bm, c), lambda i, j: (i, 0))

    def gspec(branch):
        return pl.BlockSpec((bm, bn), lambda i, j: (i, branch * nb + j))

    return pl.pallas_call(
        _merge_kernel,
        grid=(r // bm, nb),
        in_specs=[ospec] * N_BRANCH + [gspec(i) for i in range(N_BRANCH)]
        + [pl.BlockSpec((N_BRANCH, c, bn), lambda i, j: (0, 0, j))],
        out_specs=pl.BlockSpec((bm, bn), lambda i, j: (i, j)),
        out_shape=jax.ShapeDtypeStruct((r, d), BF16),
        compiler_params=_params("parallel", "parallel"),
        name="merge_branches",
    )(*outs, gate_logits, gate_logits, gate_logits, gate_logits, w_branch)


HALO = 16


FFN_BN = 640


def _ffn_tile_pairs(a, bn):
    f = a.shape[-1] // 2
    lead = a.shape[:-1]
    pairs = jnp.stack([a[..., :f].reshape(*lead, f // bn, bn), a[..., f:].reshape(*lead, f // bn, bn)], axis=-2)
    return pairs.reshape(*lead, 2 * f)


def _ffn_up_kernel(h_ref, hp_ref, hn_ref, w_ref, cw_ref, cb_ref, o_ref, hbuf_ref, *, bm, s, n_lat):
    i = pl.program_id(0)
    bn = o_ref.shape[1]

    @pl.when(pl.program_id(1) == 0)
    def _():
        hbuf_ref[0:HALO, :] = hp_ref[...]
        hbuf_ref[HALO:HALO + bm, :] = h_ref[...]
        hbuf_ref[HALO + bm:, :] = hn_ref[...]

    pos = lax.rem(i * bm + lax.broadcasted_iota(jnp.int32, (bm, 1), 0), s)
    has_prev = jnp.where((pos == 0) | (pos == n_lat), 0.0, 1.0)
    has_next = jnp.where((pos == n_lat - 1) | (pos == s - 1), 0.0, 1.0)
    rows = bm + 2 * HALO
    u = jnp.dot(hbuf_ref[...], w_ref[...], preferred_element_type=F32)
    prev = pltpu.roll(u, 1, 0)[HALO:HALO + bm]
    nxt = pltpu.roll(u, rows - 1, 0)[HALO:HALO + bm]
    cw = cw_ref[...]
    y = prev * has_prev * cw[0:1] + u[HALO:HALO + bm] * cw[1:2] + nxt * has_next * cw[2:3] + cb_ref[...]
    gate, val = y[:, :bn], y[:, bn:]
    o_ref[...] = (gate * _sigmoid(gate) * val).astype(o_ref.dtype)


def ffn_up(h, w_up, conv_w, conv_b, s, n_lat):
    r, d = h.shape
    f = w_up.shape[1] // 2
    bm = _pick(r, (512, 256, 128))
    bn = FFN_BN
    assert f % bn == 0
    halo_blocks = r // HALO
    per_tile = bm // HALO
    return pl.pallas_call(
        functools.partial(_ffn_up_kernel, bm=bm, s=s, n_lat=n_lat),
        grid=(r // bm, f // bn),
        in_specs=[
            pl.BlockSpec((bm, d), lambda i, j: (i, 0)),
            pl.BlockSpec((HALO, d), lambda i, j: (jnp.maximum(i * per_tile - 1, 0), 0)),
            pl.BlockSpec((HALO, d), lambda i, j: (jnp.minimum((i + 1) * per_tile, halo_blocks - 1), 0)),
            pl.BlockSpec((d, 2 * bn), lambda i, j: (0, j)),
            pl.BlockSpec((3, 2 * bn), lambda i, j: (0, j)),
            pl.BlockSpec((1, 2 * bn), lambda i, j: (0, j)),
        ],
        out_specs=pl.BlockSpec((bm, bn), lambda i, j: (i, j)),
        out_shape=jax.ShapeDtypeStruct((r, f), BF16),
        scratch_shapes=[pltpu.VMEM((bm + 2 * HALO, d), BF16)],
        compiler_params=_params("parallel", "arbitrary"),
        name="ffn_up",
    )(h, h, h, w_up, conv_w, conv_b)


def _split_w_in(w, d, heads):
    mix = heads * HEAD_DIM
    q_rank = 3 * d // 16
    kv_rank = d // 8
    sizes = [('a', 3 * mix), ('b_qa', q_rank), ('b_kva', kv_rank + MLA_ROPE), ('c', 5 * mix), ('d', 3 * mix),
             ('gate', N_BRANCH * d)]
    out, start = {}, 0
    for name, size in sizes:
        out[name] = (start, size)
        start += size
    assert start == w.shape[1]

    def cols(name):
        a, n = out[name]
        return w[:, a:a + n].astype(BF16)

    used = q_rank + kv_rank + LANE
    pad = -used % 512 + LANE - MLA_ROPE
    w_b = jnp.concatenate([cols('b_qa'), cols('b_kva'), jnp.zeros((d, pad), BF16)], axis=1)
    return cols('a'), w_b, cols('c'), cols('d'), cols('gate'), q_rank, kv_rank


def _mla_q_weight(w_qb, heads):
    rq = w_qb.shape[0]
    w = w_qb.reshape(rq, heads, MLA_NOPE + MLA_ROPE)
    nope = w[:, :, :MLA_NOPE].reshape(rq, heads * MLA_NOPE)
    ropep = jnp.pad(w[:, :, MLA_NOPE:], ((0, 0), (0, 0), (0, LANE - MLA_ROPE))).reshape(rq, heads * LANE)
    return jnp.concatenate([nope, ropep], axis=1).astype(BF16)


def kernel(x, c, ctx, c_ctx, ada_w, ada_b, norm_mix_pre, norm_mix_post, norm_ffn_pre, norm_ffn_post, w_in,
           diff_lambda, diff_subln, mla_q_norm, mla_w_qb, mla_kv_norm, mla_w_kvb, hgrn_lb_logits, hgrn_norm,
           na_rpb, w_branch, w_out, ffn_w_up, ffn_conv_w, ffn_conv_b, ffn_w_down):
    bsz, n_lat, d = x.shape
    n_ctx = ctx.shape[1]
    s = n_lat + n_ctx
    r = bsz * s
    depth = ada_w.shape[0]
    heads = d // (N_BRANCH * HEAD_DIM)
    mix = heads * HEAD_DIM
    assert n_lat % 256 == 0 and n_ctx % 256 == 0 and s % n_ctx == 0 and n_lat % GRID_W == 0

    tables = _rope_tables(n_lat, n_ctx)
    lb_sm = jax.nn.softmax(hgrn_lb_logits.astype(F32), axis=0)
    lower_bounds = jnp.cumsum(lb_sm, axis=0) - lb_sm[0]

    c_rows = jnp.zeros((8, d), F32).at[:bsz].set(c).at[bsz].set(c_ctx)
    xa = jnp.concatenate([x, ctx], axis=1)

    def mod_rows(mod, idx):
        part = mod[:, idx * d:(idx + 1) * d]
        both = jnp.stack([part[:bsz], jnp.broadcast_to(part[bsz], (bsz, d))], axis=1)
        return both.reshape(2 * bsz, 1, d)

    mods = []
    for l in range(depth):
        mod = ada_modulation(c_rows, ada_w, ada_b, l)
        mods.append([mod_rows(mod, i) for i in range(6)])

    h = norm_modulate(xa, norm_mix_pre[0], mods[0][0], mods[0][1], n_lat)
    for l in range(depth):
        with_ctx = l < depth - 1
        sh1, sc1, g1, sh2, sc2, g2 = mods[l]
        h2d = h.reshape(r, d)
        rows = s if with_ctx else n_lat

        w_a, w_b, w_c, w_d, w_g, q_rank, kv_rank = _split_w_in(w_in[l], d, heads)
        pa = matmul(h2d, w_a, BF16, "in_proj_a")
        pb = matmul(h2d, w_b, BF16, "in_proj_b")
        pc = matmul(h2d, w_c, BF16, "in_proj_c")
        pd = matmul(h2d, w_d, BF16, "in_proj_d")
        if with_ctx:
            pg = matmul(h2d, w_g, BF16, "in_proj_gate")
        else:
            pg = matmul_leading_rows(h, w_g, rows, BF16, "in_proj_gate")

        qk = rope(pa, tables, 0, 2, s, width=mix)
        lam_init = 0.8 - 0.6 * math.exp(-0.3 * l)
        extras_a = [diff_lambda[l].astype(F32), diff_subln[l].reshape(1, HEAD_DIM).astype(F32)]

        def call_a(q_range, k_range, tag):
            return _attention_call(
                functools.partial(_diff_attn_kernel, lam_init=lam_init),
                [(qk, 0)], [(qk, heads, 1, False)], (pa, 2 * heads, 1), extras_a,
                bsz=bsz, heads=heads, s=s, q_range=q_range, k_range=k_range, tq=min(512, q_range[1]), lanes=256,
                name="diff_attn_" + tag)

        oa = _attend_lat_and_ctx(call_a, n_lat, n_ctx, with_ctx)

        qn, cn, kr = mla_prep(pb, mla_q_norm[l], mla_kv_norm[l], tables, s)
        qb = matmul(qn, _mla_q_weight(mla_w_qb[l], heads), BF16, "mla_q_up")
        qb_rope = rope(qb, tables, 1, 1, s, width=mix)
        kvb = matmul(cn, mla_w_kvb[l].astype(BF16), BF16, "mla_kv_up")

        def call_b(q_range, k_range, tag):
            return _attention_call(
                functools.partial(_sdpa_kernel, n_qk=2, scale=(MLA_NOPE + MLA_ROPE) ** -0.5),
                [(qb, 0), (qb_rope, 0)], [(kvb, 0, 2, False), (kr, 0, 0, True)], (kvb, 1, 2), [],
                bsz=bsz, heads=heads, s=s, q_range=q_range, k_range=k_range, tq=min(1024, q_range[1]), lanes=256,
                name="mla_attn_" + tag)

        ob = _attend_lat_and_ctx(call_b, n_lat, n_ctx, with_ctx)

        o_fwd = hgrn_scan(pc, lower_bounds[l, 0], bsz, heads, n_lat, n_ctx, reverse=False)
        o_bwd = hgrn_scan(pc, lower_bounds[l, 1], bsz, heads, n_lat, n_ctx, reverse=True)
        oc = hgrn_readout(o_fwd, o_bwd, pc, hgrn_norm[l], bsz, s, rows)

        od = neighbourhood_attention(pd, _na_bias_table(na_rpb[l]), bsz, heads, n_lat, n_ctx)
        if with_ctx:
            od_ctx = _attention_call(
                functools.partial(_sdpa_kernel, n_qk=1, scale=HEAD_DIM ** -0.5),
                [(pd, 0)], [(pd, heads, 1, False)], (pd, 2 * heads, 1), [],
                bsz=bsz, heads=heads, s=s, q_range=(n_lat, n_ctx), k_range=(n_lat, n_ctx), tq=256, lanes=256,
                name="na_ctx_attn").reshape(bsz, n_ctx, mix)
            od = jnp.concatenate([od.reshape(bsz, n_lat, mix), od_ctx], axis=1).reshape(r, mix)

        merged = merge_branches((oa, ob, oc, od), pg, w_branch[l].astype(BF16))
        mix_out = matmul(merged, w_out[l].astype(BF16), F32, "out_proj").reshape(bsz, rows, d)
        xa, h2 = residual_norm_modulate(xa, mix_out, g1, norm_mix_post[l], norm_ffn_pre[l], sh2, sc2, n_lat)

        gv = ffn_up(h2.reshape(bsz * rows, d), _ffn_tile_pairs(ffn_w_up[l], FFN_BN).astype(BF16),
                    _ffn_tile_pairs(ffn_conv_w[l], FFN_BN), _ffn_tile_pairs(ffn_conv_b[l][None], FFN_BN),
                    rows, n_lat)
        ffn = matmul(gv, ffn_w_down[l].astype(BF16), F32, "ffn_down").reshape(bsz, rows, d)
        if with_ctx:
            xa, h = residual_norm_modulate(xa, ffn, g2, norm_ffn_post[l], norm_mix_pre[l + 1],
                                           mods[l + 1][0], mods[l + 1][1], n_lat)
        else:
            xa = residual(xa, ffn, g2, norm_ffn_post[l], n_lat)
    return xa
```

```python
import functools
import math

import numpy as np
import jax
import jax.numpy as jnp
from jax import lax
from jax.experimental import pallas as pl
from jax.experimental.pallas import tpu as pltpu

F32 = jnp.float32
BF16 = jnp.bfloat16

HEAD_DIM = 128
QK_HALF = 64
MLA_NOPE = 128
MLA_ROPE = 64
MLA_V = 128
ROPE_DIM = 64
ROPE_THETA = 10000.0
GRID_W = 64
NA_ROWS = 8
NA_COLS = 16
CHUNK = 128
NORM_EPS = 1e-6
NEG_INF = -1e30
N_BRANCH = 4

LANE = 128
VMEM_LIMIT_BYTES = 56 * 1024 * 1024

NT_DIMS = (((1,), (1,)), ((), ()))
TN_DIMS = (((0,), (0,)), ((), ()))


def _params(*sem):
    return pltpu.CompilerParams(dimension_semantics=sem, vmem_limit_bytes=VMEM_LIMIT_BYTES)


def _sigmoid(x):
    return 0.5 * jnp.tanh(0.5 * x) + 0.5


def _pick(total, candidates):
    for c in candidates:
        if total % c == 0:
            return c
    raise ValueError(f"no tile for {total} in {candidates}")


def _ada_kernel(c_ref, w_ref, b_ref, o_ref):
    c = c_ref[...]
    a = (c * _sigmoid(c)).astype(BF16)
    o_ref[...] = jnp.dot(a, w_ref[...].astype(BF16), preferred_element_type=F32) + b_ref[...]


def ada_modulation(c_rows, ada_w, ada_b, layer):
    rows, d = c_rows.shape
    width = ada_w.shape[2]
    tn = _pick(width, (512, 256, 128))
    b3 = ada_b.reshape(ada_b.shape[0], 1, width)
    return pl.pallas_call(
        _ada_kernel,
        grid=(width // tn,),
        in_specs=[
            pl.BlockSpec((rows, d), lambda j: (0, 0)),
            pl.BlockSpec((None, d, tn), lambda j: (layer, 0, j)),
            pl.BlockSpec((None, 1, tn), lambda j: (layer, 0, j)),
        ],
        out_specs=pl.BlockSpec((rows, tn), lambda j: (0, j)),
        out_shape=jax.ShapeDtypeStruct((rows, width), F32),
        compiler_params=_params("parallel"),
        name="ada_modulation",
    )(c_rows, ada_w, b3)


def _rms(x, gain):
    return x * lax.rsqrt(jnp.mean(x * x, axis=-1, keepdims=True) + NORM_EPS) * gain


def _norm_mod_kernel(x_ref, g_ref, sh_ref, sc_ref, h_ref):
    h = _rms(x_ref[0], g_ref[...])
    h_ref[0] = (h * (1.0 + sc_ref[0]) + sh_ref[0]).astype(h_ref.dtype)


def _resid_kernel(x_ref, y_ref, gate_ref, gpost_ref, xo_ref):
    xo_ref[0] = x_ref[0] + gate_ref[0] * _rms(y_ref[0], gpost_ref[...])


def _resid_norm_mod_kernel(x_ref, y_ref, gate_ref, gpost_ref, gpre_ref, sh_ref, sc_ref, xo_ref, h_ref):
    xn = x_ref[0] + gate_ref[0] * _rms(y_ref[0], gpost_ref[...])
    xo_ref[0] = xn
    h = _rms(xn, gpre_ref[...])
    h_ref[0] = (h * (1.0 + sc_ref[0]) + sh_ref[0]).astype(h_ref.dtype)


def _row_specs(dims, tr, n_lat_tiles):
    s, d = dims
    xspec = pl.BlockSpec((1, tr, d), lambda b, t: (b, t, 0))
    gspec = pl.BlockSpec((1, d), lambda b, t: (0, 0))
    mspec = pl.BlockSpec((1, 1, d), lambda b, t: (2 * b + t // n_lat_tiles, 0, 0))
    return xspec, gspec, mspec


def norm_modulate(x, gain, shift, scale, n_lat, tr=256):
    bsz, s, d = x.shape
    xspec, gspec, mspec = _row_specs((s, d), tr, n_lat // tr)
    return pl.pallas_call(
        _norm_mod_kernel,
        grid=(bsz, s // tr),
        in_specs=[xspec, gspec, mspec, mspec],
        out_specs=xspec,
        out_shape=jax.ShapeDtypeStruct(x.shape, BF16),
        compiler_params=_params("parallel", "parallel"),
        name="norm_modulate",
    )(x, gain.reshape(1, d), shift, scale)


def residual(x, y, gate, gain_post, n_lat, tr=256):
    bsz, s, d = x.shape
    rows = y.shape[1]
    xspec, gspec, mspec = _row_specs((s, d), tr, n_lat // tr)
    return pl.pallas_call(
        _resid_kernel,
        grid=(bsz, rows // tr),
        in_specs=[xspec, xspec, mspec, gspec],
        out_specs=xspec,
        out_shape=jax.ShapeDtypeStruct(y.shape, F32),
        compiler_params=_params("parallel", "parallel"),
        name="residual",
    )(x, y, gate, gain_post.reshape(1, d))


def residual_norm_modulate(x, y, gate, gain_post, gain_pre, shift, scale, n_lat, tr=256):
    bsz, s, d = x.shape
    rows = y.shape[1]
    xspec, gspec, mspec = _row_specs((s, d), tr, n_lat // tr)
    return pl.pallas_call(
        _resid_norm_mod_kernel,
        grid=(bsz, rows // tr),
        in_specs=[xspec, xspec, mspec, gspec, gspec, mspec, mspec],
        out_specs=[xspec, xspec],
        out_shape=[jax.ShapeDtypeStruct(y.shape, F32), jax.ShapeDtypeStruct(y.shape, BF16)],
        compiler_params=_params("parallel", "parallel"),
        name="residual_norm_modulate",
    )(x, y, gate, gain_post.reshape(1, d), gain_pre.reshape(1, d), shift, scale)


def _mm_kernel(a_ref, b_ref, o_ref):
    o_ref[...] = jnp.dot(a_ref[...], b_ref[...], preferred_element_type=F32).astype(o_ref.dtype)


def _mm_acc_kernel(a_ref, b_ref, o_ref, acc_ref):
    k = pl.program_id(2)

    @pl.when(k == 0)
    def _():
        acc_ref[...] = jnp.zeros_like(acc_ref)

    acc_ref[...] += jnp.dot(a_ref[...], b_ref[...], preferred_element_type=F32)

    @pl.when(k == pl.num_programs(2) - 1)
    def _():
        o_ref[...] = acc_ref[...].astype(o_ref.dtype)


MAX_SINGLE_K = 4096


def matmul_leading_rows(a, b, rows, out_dtype, name):
    bsz, _, k = a.shape
    n = b.shape[1]
    assert k <= MAX_SINGLE_K
    bm = _pick(rows, (1024, 512, 256, 128))
    bn = _pick(n, (1024, 768, 640, 512, 384, 256, 128))
    tiles = rows // bm
    return pl.pallas_call(
        _mm_kernel,
        grid=(bsz, tiles, n // bn),
        in_specs=[pl.BlockSpec((None, bm, k), lambda s, i, j: (s, i, 0)),
                  pl.BlockSpec((k, bn), lambda s, i, j: (0, j))],
        out_specs=pl.BlockSpec((bm, bn), lambda s, i, j: (s * tiles + i, j)),
        out_shape=jax.ShapeDtypeStruct((bsz * rows, n), out_dtype),
        compiler_params=_params("parallel", "parallel", "parallel"),
        name=name,
    )(a, b)


def matmul(a, b, out_dtype, name):
    m, k = a.shape
    n = b.shape[1]
    bm = _pick(m, (1024, 512, 256, 128))
    bn = _pick(n, (1024, 768, 640, 512, 384, 256, 128))
    if k <= MAX_SINGLE_K:
        return pl.pallas_call(
            _mm_kernel,
            grid=(m // bm, n // bn),
            in_specs=[pl.BlockSpec((bm, k), lambda i, j: (i, 0)),
                      pl.BlockSpec((k, bn), lambda i, j: (0, j))],
            out_specs=pl.BlockSpec((bm, bn), lambda i, j: (i, j)),
            out_shape=jax.ShapeDtypeStruct((m, n), out_dtype),
            compiler_params=_params("parallel", "parallel"),
            name=name,
        )(a, b)
    nk = -(-k // MAX_SINGLE_K)
    while k % nk or (k // nk) % LANE:
        nk += 1
    bk = k // nk
    return pl.pallas_call(
        _mm_acc_kernel,
        grid=(m // bm, n // bn, nk),
        in_specs=[pl.BlockSpec((bm, bk), lambda i, j, kk: (i, kk)),
                  pl.BlockSpec((bk, bn), lambda i, j, kk: (kk, j))],
        out_specs=pl.BlockSpec((bm, bn), lambda i, j, kk: (i, j)),
        out_shape=jax.ShapeDtypeStruct((m, n), out_dtype),
        scratch_shapes=[pltpu.VMEM((bm, bn), F32)],
        compiler_params=_params("parallel", "parallel", "arbitrary"),
        name=name,
    )(a, b)


def _rope_tables(n_lat, n_ctx):
    t = jnp.arange(n_lat, dtype=jnp.int32)
    row = (t // GRID_W).astype(F32)
    col = (t % GRID_W).astype(F32)
    quarter = ROPE_DIM // 4
    inv = ROPE_THETA ** (-jnp.arange(quarter, dtype=F32) / quarter)
    ar = row[:, None] * inv
    ac = col[:, None] * inv
    ang = jnp.concatenate([ar, ar, ac, ac], axis=-1)
    cos = jnp.concatenate([jnp.cos(ang), jnp.ones((n_ctx, ROPE_DIM), F32)], axis=0)
    sin = jnp.concatenate([jnp.sin(ang), jnp.zeros((n_ctx, ROPE_DIM), F32)], axis=0)
    first = (jnp.arange(ROPE_DIM) // quarter) % 2 == 0
    sin_next = jnp.where(first, -sin, 0.0)
    sin_prev = jnp.where(first, 0.0, sin)
    reps = LANE // ROPE_DIM
    return tuple(jnp.tile(a, (1, reps)) for a in (cos, sin_next, sin_prev))


def _rope_apply(x, cos, sin_next, sin_prev):
    quarter = ROPE_DIM // 4
    return (x * cos + pltpu.roll(x, LANE - quarter, 1) * sin_next
            + pltpu.roll(x, quarter, 1) * sin_prev)


def _rope_kernel(x_ref, cos_ref, sn_ref, sp_ref, o_ref):
    cos, sn, sp = cos_ref[...], sn_ref[...], sp_ref[...]
    for c in range(x_ref.shape[1] // LANE):
        sl = slice(c * LANE, (c + 1) * LANE)
        o_ref[:, sl] = _rope_apply(x_ref[:, sl].astype(F32), cos, sn, sp).astype(o_ref.dtype)


def rope(x, tables, col_block, n_col_blocks, s, width=1024, tr=256):
    r = x.shape[0]
    width = min(width, x.shape[1])
    tiles_per_seq = s // tr
    tspec = pl.BlockSpec((tr, LANE), lambda i, j: (i % tiles_per_seq, 0))
    return pl.pallas_call(
        _rope_kernel,
        grid=(r // tr, n_col_blocks),
        in_specs=[pl.BlockSpec((tr, width), lambda i, j: (i, col_block + j)), tspec, tspec, tspec],
        out_specs=pl.BlockSpec((tr, width), lambda i, j: (i, j)),
        out_shape=jax.ShapeDtypeStruct((r, width * n_col_blocks), BF16),
        compiler_params=_params("parallel", "parallel"),
        name="rope",
    )(x, *tables)


LOG2E = math.log2(math.e)
KEY_CHUNK = 256


def _attend_tiles(k_ref, q_tiles, vt_ref, s_ref, e_ref):
    m_keys, n = s_ref.shape[1:]
    kc = min(KEY_CHUNK, m_keys)
    chunks = [slice(c * kc, (c + 1) * kc) for c in range(m_keys // kc)]

    def scores(t):
        s_ref[t % 2] = lax.dot_general(k_ref[...], q_tiles[t], NT_DIMS, preferred_element_type=F32)

    def column_max(t):
        m8 = None
        for sl in chunks:
            part = jnp.max(s_ref[t % 2, sl, :].reshape(kc // 8, 8, n), axis=0)
            m8 = part if m8 is None else jnp.maximum(m8, part)
        return jnp.max(m8, axis=0, keepdims=True)

    scores(0)
    m = column_max(0)
    outs = []
    for t in range(len(q_tiles)):
        if t + 1 < len(q_tiles):
            scores(t + 1)
        l8 = jnp.zeros((8, n), F32)
        for sl in chunks:
            e = jnp.exp2(s_ref[t % 2, sl, :] - m)
            l8 = l8 + jnp.sum(e.reshape(kc // 8, 8, n), axis=0)
            e_ref[t % 2, sl, :] = e.astype(e_ref.dtype)
        o_t = jnp.dot(vt_ref[...], e_ref[t % 2], preferred_element_type=F32)
        outs.append((o_t, jnp.sum(l8, axis=0, keepdims=True)))
        if t + 1 < len(q_tiles):
            m = column_max(t + 1)
    return outs


def _load_v_transposed(v_ref, vt_ref):
    @pl.when(pl.program_id(2) == 0)
    def _():
        vt_ref[...] = v_ref[...].astype(F32).T.astype(vt_ref.dtype)


def _diff_attn_kernel(q_ref, k_ref, v_ref, lam_ref, g_ref, o_ref, vt_ref, s_ref, e_ref, *, lam_init):
    _load_v_transposed(v_ref, vt_ref)
    tq = s_ref.shape[2] // 2
    q = (q_ref[...].astype(F32) * (QK_HALF ** -0.5 * LOG2E)).astype(k_ref.dtype)
    lane = lax.broadcasted_iota(jnp.int32, q.shape, 1)
    zero = jnp.zeros_like(q)
    q0 = jnp.where(lane < QK_HALF, q, zero)
    q1 = jnp.where(lane >= QK_HALF, q, zero)
    tiles = [slice(t * tq, (t + 1) * tq) for t in range(q.shape[0] // tq)]
    outs = _attend_tiles(k_ref, [jnp.concatenate([q0[sl], q1[sl]], axis=0) for sl in tiles], vt_ref, s_ref, e_ref)
    lp = lam_ref[...]
    lam = (jnp.exp(jnp.sum(lp[0:1] * lp[1:2], axis=-1, keepdims=True))
           - jnp.exp(jnp.sum(lp[2:3] * lp[3:4], axis=-1, keepdims=True)) + lam_init)
    gain = g_ref[...] * (1.0 - lam_init)
    for sl, (o_t, l) in zip(tiles, outs):
        inv = 1.0 / l
        c_t = o_t[:, :tq] * inv[:, :tq] - o_t[:, tq:] * (lam * inv[:, tq:])
        c_t = c_t * lax.rsqrt(jnp.mean(c_t * c_t, axis=0, keepdims=True) + NORM_EPS)
        o_ref[sl, :] = (c_t.T * gain).astype(o_ref.dtype)


def _sdpa_kernel(*refs, n_qk, scale):
    q_refs, k_refs = refs[:n_qk], refs[n_qk:2 * n_qk]
    v_ref, o_ref, vt_ref, s_ref, e_ref = refs[2 * n_qk:2 * n_qk + 5]
    _load_v_transposed(v_ref, vt_ref)
    if n_qk == 1:
        kcat_ref = k_refs[0]
    else:
        kcat_ref = refs[2 * n_qk + 5]

        @pl.when(pl.program_id(2) == 0)
        def _():
            for i, k_ref in enumerate(k_refs):
                kcat_ref[:, i * HEAD_DIM:(i + 1) * HEAD_DIM] = k_ref[...]

    tq = s_ref.shape[2]
    q = jnp.concatenate([q_ref[...] for q_ref in q_refs], axis=1)
    q = (q.astype(F32) * (scale * LOG2E)).astype(kcat_ref.dtype)
    tiles = [slice(t * tq, (t + 1) * tq) for t in range(q.shape[0] // tq)]
    outs = _attend_tiles(kcat_ref, [q[sl] for sl in tiles], vt_ref, s_ref, e_ref)
    for sl, (o_t, l) in zip(tiles, outs):
        o_ref[sl, :] = (o_t * (1.0 / l)).T.astype(o_ref.dtype)


def _attention_call(kernel, q_list, k_list, v, extras, *, bsz, heads, s, q_range, k_range, tq, lanes, name):
    q_start, q_len = q_range
    k_start, k_len = k_range
    nq = q_len // tq
    q_off = q_start // tq
    k_off = k_start // k_len
    assert q_start % tq == 0 and q_len % tq == 0 and k_start % k_len == 0

    def qspec(col0):
        return pl.BlockSpec((None, tq, HEAD_DIM), lambda b, h, i: (b, q_off + i, col0 + h))

    def kspec(col0, stride, shared=False):
        if shared:
            return pl.BlockSpec((None, k_len, HEAD_DIM), lambda b, h, i: (b, k_off, col0))
        return pl.BlockSpec((None, k_len, HEAD_DIM), lambda b, h, i: (b, k_off, col0 + stride * h))

    def per_sample(arr):
        return arr.reshape(bsz, s, arr.shape[-1])

    in_specs, args = [], []
    for arr, col0 in q_list:
        in_specs.append(qspec(col0)); args.append(per_sample(arr))
    for arr, col0, stride, shared in k_list:
        in_specs.append(kspec(col0, stride, shared)); args.append(per_sample(arr))
    arr, col0, stride = v
    in_specs.append(kspec(col0, stride)); args.append(per_sample(arr))
    for e in extras:
        in_specs.append(pl.BlockSpec(e.shape, lambda b, h, i: (0, 0))); args.append(e)
    return pl.pallas_call(
        kernel,
        grid=(bsz, heads, nq),
        in_specs=in_specs,
        out_specs=pl.BlockSpec((tq, HEAD_DIM), lambda b, h, i: (b * nq + i, h)),
        out_shape=jax.ShapeDtypeStruct((bsz * q_len, heads * HEAD_DIM), BF16),
        scratch_shapes=[pltpu.VMEM((HEAD_DIM, k_len), BF16), pltpu.VMEM((2, k_len, lanes), F32),
                        pltpu.VMEM((2, k_len, lanes), BF16)]
        + ([pltpu.VMEM((k_len, len(k_list) * HEAD_DIM), BF16)] if len(k_list) > 1 else []),
        compiler_params=_params("parallel", "parallel", "arbitrary"),
        name=name,
    )(*args)


def _attend_lat_and_ctx(call, n_lat, n_ctx, with_ctx):
    s = n_lat + n_ctx
    o_lat = call((0, n_lat), (0, s), "lat")
    if not with_ctx:
        return o_lat
    bsz = o_lat.shape[0] // n_lat
    o_ctx = call((n_lat, n_ctx), (n_lat, n_ctx), "ctx").reshape(bsz, n_ctx, -1)
    return jnp.concatenate([o_lat.reshape(bsz, n_lat, -1), o_ctx], axis=1).reshape(bsz * s, -1)


def _na_bias_table(rpb):
    col = jnp.arange(GRID_W, dtype=jnp.int32)
    cs = jnp.clip(col - NA_COLS // 2, 0, GRID_W - NA_COLS)
    colmask = (col[None, :] >= cs[:, None]) & (col[None, :] < cs[:, None] + NA_COLS)
    dc = jnp.clip(col[None, :] - col[:, None] + NA_COLS - 1, 0, 2 * NA_COLS - 2)
    delta = jnp.arange(NA_ROWS, dtype=jnp.int32)
    jj = jnp.arange(NA_ROWS, dtype=jnp.int32)
    dr = jj[None, :] - delta[:, None] + NA_ROWS - 1
    bias = rpb[:, dr][:, :, :, dc]
    bias = jnp.where(colmask[None, None, None], bias, NEG_INF)
    bias = jnp.transpose(bias, (0, 1, 3, 2, 4))
    return bias.reshape(rpb.shape[0], NA_ROWS, GRID_W, NA_ROWS * GRID_W).astype(F32)


def _na_kernel(q_ref, k_ref, v_ref, bias_ref, o_ref, *, rows_per_step, n_lat, grid_rows):
    g = pl.program_id(2)
    scale = HEAD_DIM ** -0.5
    win = NA_ROWS * GRID_W
    kc = k_ref[n_lat:, :]
    vc = v_ref[n_lat:, :]
    for i in range(rows_per_step):
        r = g * rows_per_step + i
        rs = jnp.clip(r - NA_ROWS // 2, 0, grid_rows - NA_ROWS)
        start = pl.multiple_of(rs * GRID_W, GRID_W)
        q = q_ref[i * GRID_W:(i + 1) * GRID_W, :]
        kw = k_ref[pl.ds(start, win), :]
        vw = v_ref[pl.ds(start, win), :]
        sl = lax.dot_general(q, kw, NT_DIMS, preferred_element_type=F32) * scale + bias_ref[r - rs]
        sc = lax.dot_general(q, kc, NT_DIMS, preferred_element_type=F32) * scale
        m = jnp.maximum(jnp.max(sl, axis=-1, keepdims=True), jnp.max(sc, axis=-1, keepdims=True))
        el = jnp.exp(sl - m)
        ec = jnp.exp(sc - m)
        inv = 1.0 / (jnp.sum(el, axis=-1, keepdims=True) + jnp.sum(ec, axis=-1, keepdims=True))
        o = (jnp.dot(el.astype(BF16), vw, preferred_element_type=F32)
             + jnp.dot(ec.astype(BF16), vc, preferred_element_type=F32))
        o_ref[i * GRID_W:(i + 1) * GRID_W, :] = (o * inv).astype(o_ref.dtype)


def neighbourhood_attention(pd, bias, bsz, heads, n_lat, n_ctx, rows_per_step=8):
    s = n_lat + n_ctx
    grid_rows = n_lat // GRID_W
    tq = rows_per_step * GRID_W
    assert grid_rows >= NA_ROWS and grid_rows % rows_per_step == 0
    steps = grid_rows // rows_per_step
    pd3 = pd.reshape(bsz, s, pd.shape[-1])
    return pl.pallas_call(
        functools.partial(_na_kernel, rows_per_step=rows_per_step, n_lat=n_lat, grid_rows=grid_rows),
        grid=(bsz, heads, steps),
        in_specs=[
            pl.BlockSpec((None, tq, HEAD_DIM), lambda b, h, g: (b, g, h)),
            pl.BlockSpec((None, s, HEAD_DIM), lambda b, h, g: (b, 0, heads + h)),
            pl.BlockSpec((None, s, HEAD_DIM), lambda b, h, g: (b, 0, 2 * heads + h)),
            pl.BlockSpec((None, NA_ROWS, GRID_W, NA_ROWS * GRID_W), lambda b, h, g: (h, 0, 0, 0)),
        ],
        out_specs=pl.BlockSpec((tq, HEAD_DIM), lambda b, h, g: (b * steps + g, h)),
        out_shape=jax.ShapeDtypeStruct((bsz * n_lat, heads * HEAD_DIM), BF16),
        compiler_params=_params("parallel", "parallel", "arbitrary"),
        name="neighbourhood_attention",
    )(pd3, pd3, pd3, bias)


def _hgrn_constants(reverse):
    c = CHUNK
    t = np.arange(c)
    tri = (t[None, :] >= t[:, None]) if reverse else (t[None, :] <= t[:, None])
    mats = [tri.astype(np.float32)]
    masks = [np.eye(c, dtype=np.float32)]
    m = c // 2
    while m >= 1:
        node = (t // (2 * m)) * (2 * m)
        ref = node + (m if reverse else m - 1)
        mats.append(tri[ref].astype(np.float32))
        upper = (t % (2 * m)) >= m
        same = node[:, None] == node[None, :]
        pair = same & (upper[:, None] & ~upper[None, :])
        masks.append((pair.T if reverse else pair).astype(np.float32))
        m //= 2
    return np.concatenate(mats, axis=0), np.stack(masks)


def _hgrn_kernel(cq_ref, ci_ref, z_ref, lb_ref, ms_ref, mk_ref, o_ref, st_ref, *, heads, reverse):
    @pl.when(pl.program_id(1) == 0)
    def _():
        st_ref[...] = jnp.zeros_like(st_ref)

    c = CHUNK
    n_levels = mk_ref.shape[0]
    ms = ms_ref[...].astype(BF16)
    scale = HEAD_DIM ** -0.5
    tot_row = 0 if reverse else c - 1
    for h in range(heads):
        sl = slice(h * HEAD_DIM, (h + 1) * HEAD_DIM)
        lb = lb_ref[:, sl]
        f = lb + (1.0 - lb) * _sigmoid(z_ref[:, sl].astype(F32))
        kk = 1.0 - f
        g = jnp.log(f)
        cq = cq_ref[:, sl].astype(F32)
        qq = cq * _sigmoid(cq) * scale
        v = ci_ref[:, sl]
        g_hi = g.astype(BF16)
        r1 = g - g_hi.astype(F32)
        g_mid = r1.astype(BF16)
        g_lo = (r1 - g_mid.astype(F32)).astype(BF16)
        cs3 = jnp.dot(ms, jnp.concatenate([g_hi, g_mid, g_lo], axis=1), preferred_element_type=F32)
        cs = cs3[:, :HEAD_DIM] + cs3[:, HEAD_DIM:2 * HEAD_DIM] + cs3[:, 2 * HEAD_DIM:]
        cum = cs[:c]
        tot = cs[tot_row:tot_row + 1]
        st = st_ref[h]
        o = lax.dot_general((qq * jnp.exp(cum)).astype(BF16), st.astype(BF16), NT_DIMS,
                            preferred_element_type=F32)
        a = mk_ref[0] * lax.dot_general(qq.astype(BF16), kk.astype(BF16), NT_DIMS,
                                        preferred_element_type=F32)
        for lev in range(1, n_levels):
            e = jnp.exp(-jnp.abs(cum - cs[lev * c:(lev + 1) * c]))
            a = a + mk_ref[lev] * lax.dot_general((qq * e).astype(BF16), (kk * e).astype(BF16), NT_DIMS,
                                                  preferred_element_type=F32)
        o = o + jnp.dot(a.astype(BF16), v, preferred_element_type=F32)
        o_ref[:, sl] = o
        ke = (kk * jnp.exp(tot - cum)).astype(BF16)
        st_ref[h] = st * jnp.exp(tot) + lax.dot_general(v, ke, TN_DIMS, preferred_element_type=F32)


def hgrn_scan(pc, lb, bsz, heads, n_lat, n_ctx, reverse):
    s = n_lat + n_ctx
    chunks = s // CHUNK
    lat_chunks = n_lat // CHUNK
    width = heads * HEAD_DIM
    ms, mk = _hgrn_constants(reverse)

    def chunk_of(step):
        return (chunks - 1 - step) if reverse else lax.rem(step + lat_chunks, chunks)

    def spec(col):
        return pl.BlockSpec((CHUNK, width), lambda b, t: (b * chunks + chunk_of(t), col))

    return pl.pallas_call(
        functools.partial(_hgrn_kernel, heads=heads, reverse=reverse),
        grid=(bsz, chunks),
        in_specs=[spec(0), spec(1), spec(3 if reverse else 2),
                  pl.BlockSpec((1, width), lambda b, t: (0, 0)),
                  pl.BlockSpec(ms.shape, lambda b, t: (0, 0)),
                  pl.BlockSpec(mk.shape, lambda b, t: (0, 0, 0))],
        out_specs=spec(0),
        out_shape=jax.ShapeDtypeStruct((bsz * s, width), F32),
        scratch_shapes=[pltpu.VMEM((heads, HEAD_DIM, HEAD_DIM), F32)],
        compiler_params=_params("parallel", "arbitrary"),
        name="hgrn_scan_bwd" if reverse else "hgrn_scan_fwd",
    )(pc, pc, pc, lb.reshape(1, width), jnp.asarray(ms), jnp.asarray(mk))


def _hgrn_readout_kernel(of_ref, ob_ref, gate_ref, g_ref, o_ref):
    gain = g_ref[...]
    for h in range(of_ref.shape[1] // HEAD_DIM):
        sl = slice(h * HEAD_DIM, (h + 1) * HEAD_DIM)
        gate = gate_ref[:, sl].astype(F32)
        o = _rms(of_ref[:, sl] + ob_ref[:, sl], gain)
        o_ref[:, sl] = (o * (gate * _sigmoid(gate))).astype(o_ref.dtype)


def hgrn_readout(o_fwd, o_bwd, pc, gain, bsz, s, rows, tr=256):
    width = o_fwd.shape[1]
    in_tiles, out_tiles = s // tr, rows // tr
    ispec = pl.BlockSpec((tr, width), lambda b, i: (b * in_tiles + i, 0))
    return pl.pallas_call(
        _hgrn_readout_kernel,
        grid=(bsz, out_tiles),
        in_specs=[ispec, ispec, pl.BlockSpec((tr, width), lambda b, i: (b * in_tiles + i, 4)),
                  pl.BlockSpec((1, HEAD_DIM), lambda b, i: (0, 0))],
        out_specs=pl.BlockSpec((tr, width), lambda b, i: (b * out_tiles + i, 0)),
        out_shape=jax.ShapeDtypeStruct((bsz * rows, width), BF16),
        compiler_params=_params("parallel", "parallel"),
        name="hgrn_readout",
    )(o_fwd, o_bwd, pc, gain.reshape(1, HEAD_DIM))


def _mla_prep_kernel(p_ref, qg_ref, kg_ref, cos_ref, sn_ref, sp_ref, qn_ref, cn_ref, kr_ref, *, q_rank, kv_rank):
    qn_ref[...] = _rms(p_ref[:, :q_rank].astype(F32), qg_ref[...]).astype(qn_ref.dtype)
    cn_ref[...] = _rms(p_ref[:, q_rank:q_rank + kv_rank].astype(F32), kg_ref[...]).astype(cn_ref.dtype)
    kr = p_ref[:, q_rank + kv_rank:q_rank + kv_rank + LANE].astype(F32)
    kr_ref[...] = _rope_apply(kr, cos_ref[...], sn_ref[...], sp_ref[...]).astype(kr_ref.dtype)


def mla_prep(pb, q_gain, kv_gain, tables, s, tr=256):
    r, width = pb.shape
    q_rank, kv_rank = q_gain.shape[0], kv_gain.shape[0]
    assert width >= q_rank + kv_rank + LANE and q_rank % LANE == 0 and kv_rank % LANE == 0
    tiles_per_seq = s // tr
    tspec = pl.BlockSpec((tr, LANE), lambda i: (i % tiles_per_seq, 0))

    def ospec(w):
        return pl.BlockSpec((tr, w), lambda i: (i, 0))

    return pl.pallas_call(
        functools.partial(_mla_prep_kernel, q_rank=q_rank, kv_rank=kv_rank),
        grid=(r // tr,),
        in_specs=[ospec(width), pl.BlockSpec((1, q_rank), lambda i: (0, 0)),
                  pl.BlockSpec((1, kv_rank), lambda i: (0, 0)), tspec, tspec, tspec],
        out_specs=[ospec(q_rank), ospec(kv_rank), ospec(LANE)],
        out_shape=[jax.ShapeDtypeStruct((r, q_rank), BF16), jax.ShapeDtypeStruct((r, kv_rank), BF16),
                   jax.ShapeDtypeStruct((r, LANE), BF16)],
        compiler_params=_params("parallel"),
        name="mla_prep",
    )(pb, q_gain.reshape(1, q_rank), kv_gain.reshape(1, kv_rank), *tables)


def _merge_kernel(oa_ref, ob_ref, oc_ref, od_ref, ga_ref, gb_ref, gc_ref, gd_ref, w_ref, o_ref):
    acc = None
    for i, (o_ref_i, g_ref_i) in enumerate(((oa_ref, ga_ref), (ob_ref, gb_ref), (oc_ref, gc_ref), (od_ref, gd_ref))):
        y = jnp.dot(o_ref_i[...], w_ref[i], preferred_element_type=F32)
        term = _sigmoid(g_ref_i[...].astype(F32)) * y
        acc = term if acc is None else acc + term
    o_ref[...] = acc.astype(o_ref.dtype)


def merge_branches(outs, gate_logits, w_branch):
    r, c = outs[0].shape
    d = w_branch.shape[2]
    bm = _pick(r, (512, 256, 128))
    bn = _pick(d, (1024, 512, 256, 128))
    nb = d // bn
    ospec = pl.BlockSpec((bm, c), lambda i, j: (i, 0))

    def gspec(branch):
        return pl.BlockSpec((bm, bn), lambda i, j: (i, branch * nb + j))

    return pl.pallas_call(
        _merge_kernel,
        grid=(r // bm, nb),
        in_specs=[ospec] * N_BRANCH + [gspec(i) for i in range(N_BRANCH)]
        + [pl.BlockSpec((N_BRANCH, c, bn), lambda i, j: (0, 0, j))],
        out_specs=pl.BlockSpec((bm, bn), lambda i, j: (i, j)),
        out_shape=jax.ShapeDtypeStruct((r, d), BF16),
        compiler_params=_params("parallel", "parallel"),
        name="merge_branches",
    )(*outs, gate_logits, gate_logits, gate_logits, gate_logits, w_branch)


HALO = 16


FFN_BN = 640


def _ffn_tile_pairs(a, bn):
    f = a.shape[-1] // 2
    lead = a.shape[:-1]
    pairs = jnp.stack([a[..., :f].reshape(*lead, f // bn, bn), a[..., f:].reshape(*lead, f // bn, bn)], axis=-2)
    return pairs.reshape(*lead, 2 * f)


def _cast_pair_kernel(g_ref, v_ref, o_ref):
    bn = g_ref.shape[1]
    o_ref[:, :bn] = g_ref[...].astype(o_ref.dtype)
    o_ref[:, bn:] = v_ref[...].astype(o_ref.dtype)


def ffn_up_weight(w_up, layer, bn):
    _, d, two_f = w_up.shape
    nb = two_f // 2 // bn
    tk = _pick(d, (1024, 512, 256, 128))
    return pl.pallas_call(
        _cast_pair_kernel,
        grid=(d // tk, nb),
        in_specs=[pl.BlockSpec((None, tk, bn), lambda i, j: (layer, i, j)),
                  pl.BlockSpec((None, tk, bn), lambda i, j: (layer, i, nb + j))],
        out_specs=pl.BlockSpec((tk, 2 * bn), lambda i, j: (i, j)),
        out_shape=jax.ShapeDtypeStruct((d, two_f), BF16),
        compiler_params=_params("parallel", "parallel"),
        name="ffn_up_weight",
    )(w_up, w_up)


def _ffn_up_kernel(h_ref, hp_ref, hn_ref, w_ref, cw_ref, cb_ref, o_ref, hbuf_ref, *, bm, s, n_lat):
    i = pl.program_id(0)
    bn = o_ref.shape[1]

    @pl.when(pl.program_id(1) == 0)
    def _():
        hbuf_ref[0:HALO, :] = hp_ref[...]
        hbuf_ref[HALO:HALO + bm, :] = h_ref[...]
        hbuf_ref[HALO + bm:, :] = hn_ref[...]

    pos = lax.rem(i * bm + lax.broadcasted_iota(jnp.int32, (bm, 1), 0), s)
    has_prev = jnp.where((pos == 0) | (pos == n_lat), 0.0, 1.0)
    has_next = jnp.where((pos == n_lat - 1) | (pos == s - 1), 0.0, 1.0)
    rows = bm + 2 * HALO
    u = jnp.dot(hbuf_ref[...], w_ref[...], preferred_element_type=F32)
    prev = pltpu.roll(u, 1, 0)[HALO:HALO + bm]
    nxt = pltpu.roll(u, rows - 1, 0)[HALO:HALO + bm]
    cw = cw_ref[...]
    y = prev * has_prev * cw[0:1] + u[HALO:HALO + bm] * cw[1:2] + nxt * has_next * cw[2:3] + cb_ref[...]
    gate, val = y[:, :bn], y[:, bn:]
    o_ref[...] = (gate * _sigmoid(gate) * val).astype(o_ref.dtype)


def ffn_up(h, w_up, conv_w, conv_b, s, n_lat):
    r, d = h.shape
    f = w_up.shape[1] // 2
    bm = _pick(r, (512, 256, 128))
    bn = FFN_BN
    assert f % bn == 0
    halo_blocks = r // HALO
    per_tile = bm // HALO
    return pl.pallas_call(
        functools.partial(_ffn_up_kernel, bm=bm, s=s, n_lat=n_lat),
        grid=(r // bm, f // bn),
        in_specs=[
            pl.BlockSpec((bm, d), lambda i, j: (i, 0)),
            pl.BlockSpec((HALO, d), lambda i, j: (jnp.maximum(i * per_tile - 1, 0), 0)),
            pl.BlockSpec((HALO, d), lambda i, j: (jnp.minimum((i + 1) * per_tile, halo_blocks - 1), 0)),
            pl.BlockSpec((d, 2 * bn), lambda i, j: (0, j)),
            pl.BlockSpec((3, 2 * bn), lambda i, j: (0, j)),
            pl.BlockSpec((1, 2 * bn), lambda i, j: (0, j)),
        ],
        out_specs=pl.BlockSpec((bm, bn), lambda i, j: (i, j)),
        out_shape=jax.ShapeDtypeStruct((r, f), BF16),
        scratch_shapes=[pltpu.VMEM((bm + 2 * HALO, d), BF16)],
        compiler_params=_params("parallel", "arbitrary"),
        name="ffn_up",
    )(h, h, h, w_up, conv_w, conv_b)


def _split_w_in(w, d, heads):
    mix = heads * HEAD_DIM
    q_rank = 3 * d // 16
    kv_rank = d // 8
    sizes = [('a', 3 * mix), ('b_qa', q_rank), ('b_kva', kv_rank + MLA_ROPE), ('c', 5 * mix), ('d', 3 * mix),
             ('gate', N_BRANCH * d)]
    out, start = {}, 0
    for name, size in sizes:
        out[name] = (start, size)
        start += size
    assert start == w.shape[1]

    def cols(name):
        a, n = out[name]
        return w[:, a:a + n].astype(BF16)

    used = q_rank + kv_rank + LANE
    pad = -used % 512 + LANE - MLA_ROPE
    w_b = jnp.concatenate([cols('b_qa'), cols('b_kva'), jnp.zeros((d, pad), BF16)], axis=1)
    return cols('a'), w_b, cols('c'), cols('d'), cols('gate'), q_rank, kv_rank


def _mla_q_weight(w_qb, heads):
    rq = w_qb.shape[0]
    w = w_qb.reshape(rq, heads, MLA_NOPE + MLA_ROPE)
    nope = w[:, :, :MLA_NOPE].reshape(rq, heads * MLA_NOPE)
    ropep = jnp.pad(w[:, :, MLA_NOPE:], ((0, 0), (0, 0), (0, LANE - MLA_ROPE))).reshape(rq, heads * LANE)
    return jnp.concatenate([nope, ropep], axis=1).astype(BF16)


def kernel(x, c, ctx, c_ctx, ada_w, ada_b, norm_mix_pre, norm_mix_post, norm_ffn_pre, norm_ffn_post, w_in,
           diff_lambda, diff_subln, mla_q_norm, mla_w_qb, mla_kv_norm, mla_w_kvb, hgrn_lb_logits, hgrn_norm,
           na_rpb, w_branch, w_out, ffn_w_up, ffn_conv_w, ffn_conv_b, ffn_w_down):
    bsz, n_lat, d = x.shape
    n_ctx = ctx.shape[1]
    s = n_lat + n_ctx
    r = bsz * s
    depth = ada_w.shape[0]
    heads = d // (N_BRANCH * HEAD_DIM)
    mix = heads * HEAD_DIM
    assert n_lat % 256 == 0 and n_ctx % 256 == 0 and s % n_ctx == 0 and n_lat % GRID_W == 0

    tables = _rope_tables(n_lat, n_ctx)
    lb_sm = jax.nn.softmax(hgrn_lb_logits.astype(F32), axis=0)
    lower_bounds = jnp.cumsum(lb_sm, axis=0) - lb_sm[0]

    c_rows = jnp.zeros((8, d), F32).at[:bsz].set(c).at[bsz].set(c_ctx)
    xa = jnp.concatenate([x, ctx], axis=1)

    def mod_rows(mod, idx):
        part = mod[:, idx * d:(idx + 1) * d]
        both = jnp.stack([part[:bsz], jnp.broadcast_to(part[bsz], (bsz, d))], axis=1)
        return both.reshape(2 * bsz, 1, d)

    mods = []
    for l in range(depth):
        mod = ada_modulation(c_rows, ada_w, ada_b, l)
        mods.append([mod_rows(mod, i) for i in range(6)])

    h = norm_modulate(xa, norm_mix_pre[0], mods[0][0], mods[0][1], n_lat)
    for l in range(depth):
        with_ctx = l < depth - 1
        sh1, sc1, g1, sh2, sc2, g2 = mods[l]
        h2d = h.reshape(r, d)
        rows = s if with_ctx else n_lat

        w_a, w_b, w_c, w_d, w_g, q_rank, kv_rank = _split_w_in(w_in[l], d, heads)
        pa = matmul(h2d, w_a, BF16, "in_proj_a")
        pb = matmul(h2d, w_b, BF16, "in_proj_b")
        pc = matmul(h2d, w_c, BF16, "in_proj_c")
        pd = matmul(h2d, w_d, BF16, "in_proj_d")
        if with_ctx:
            pg = matmul(h2d, w_g, BF16, "in_proj_gate")
        else:
            pg = matmul_leading_rows(h, w_g, rows, BF16, "in_proj_gate")

        qk = rope(pa, tables, 0, 2, s, width=mix)
        lam_init = 0.8 - 0.6 * math.exp(-0.3 * l)
        extras_a = [diff_lambda[l].astype(F32), diff_subln[l].reshape(1, HEAD_DIM).astype(F32)]

        def call_a(q_range, k_range, tag):
            return _attention_call(
                functools.partial(_diff_attn_kernel, lam_init=lam_init),
                [(qk, 0)], [(qk, heads, 1, False)], (pa, 2 * heads, 1), extras_a,
                bsz=bsz, heads=heads, s=s, q_range=q_range, k_range=k_range, tq=min(512, q_range[1]), lanes=256,
                name="diff_attn_" + tag)

        oa = _attend_lat_and_ctx(call_a, n_lat, n_ctx, with_ctx)

        qn, cn, kr = mla_prep(pb, mla_q_norm[l], mla_kv_norm[l], tables, s)
        qb = matmul(qn, _mla_q_weight(mla_w_qb[l], heads), BF16, "mla_q_up")
        qb_rope = rope(qb, tables, 1, 1, s, width=mix)
        kvb = matmul(cn, mla_w_kvb[l].astype(BF16), BF16, "mla_kv_up")

        def call_b(q_range, k_range, tag):
            return _attention_call(
                functools.partial(_sdpa_kernel, n_qk=2, scale=(MLA_NOPE + MLA_ROPE) ** -0.5),
                [(qb, 0), (qb_rope, 0)], [(kvb, 0, 2, False), (kr, 0, 0, True)], (kvb, 1, 2), [],
                bsz=bsz, heads=heads, s=s, q_range=q_range, k_range=k_range, tq=min(1024, q_range[1]), lanes=256,
                name="mla_attn_" + tag)

        ob = _attend_lat_and_ctx(call_b, n_lat, n_ctx, with_ctx)

        o_fwd = hgrn_scan(pc, lower_bounds[l, 0], bsz, heads, n_lat, n_ctx, reverse=False)
        o_bwd = hgrn_scan(pc, lower_bounds[l, 1], bsz, heads, n_lat, n_ctx, reverse=True)
        oc = hgrn_readout(o_fwd, o_bwd, pc, hgrn_norm[l], bsz, s, rows)

        od = neighbourhood_attention(pd, _na_bias_table(na_rpb[l]), bsz, heads, n_lat, n_ctx)
        if with_ctx:
            od_ctx = _attention_call(
                functools.partial(_sdpa_kernel, n_qk=1, scale=HEAD_DIM ** -0.5),
                [(pd, 0)], [(pd, heads, 1, False)], (pd, 2 * heads, 1), [],
                bsz=bsz, heads=heads, s=s, q_range=(n_lat, n_ctx), k_range=(n_lat, n_ctx), tq=256, lanes=256,
                name="na_ctx_attn").reshape(bsz, n_ctx, mix)
            od = jnp.concatenate([od.reshape(bsz, n_lat, mix), od_ctx], axis=1).reshape(r, mix)

        merged = merge_branches((oa, ob, oc, od), pg, w_branch[l].astype(BF16))
        mix_out = matmul(merged, w_out[l].astype(BF16), F32, "out_proj").reshape(bsz, rows, d)
        xa, h2 = residual_norm_modulate(xa, mix_out, g1, norm_mix_post[l], norm_ffn_pre[l], sh2, sc2, n_lat)

        gv = ffn_up(h2.reshape(bsz * rows, d), ffn_up_weight(ffn_w_up, l, FFN_BN),
                    _ffn_tile_pairs(ffn_conv_w[l], FFN_BN), _ffn_tile_pairs(ffn_conv_b[l][None], FFN_BN),
                    rows, n_lat)
        ffn = matmul(gv, ffn_w_down[l].astype(BF16), F32, "ffn_down").reshape(bsz, rows, d)
        if with_ctx:
            xa, h = residual_norm_modulate(xa, ffn, g2, norm_ffn_post[l], norm_mix_pre[l + 1],
                                           mods[l + 1][0], mods[l + 1][1], n_lat)
        else:
            xa = residual(xa, ffn, g2, norm_ffn_post[l], n_lat)
    return xa
```

```python
import functools
import math

import numpy as np
import jax
import jax.numpy as jnp
from jax import lax
from jax.experimental import pallas as pl
from jax.experimental.pallas import tpu as pltpu

F32 = jnp.float32
BF16 = jnp.bfloat16

HEAD_DIM = 128
QK_HALF = 64
MLA_NOPE = 128
MLA_ROPE = 64
MLA_V = 128
ROPE_DIM = 64
ROPE_THETA = 10000.0
GRID_W = 64
NA_ROWS = 8
NA_COLS = 16
CHUNK = 128
NORM_EPS = 1e-6
NEG_INF = -1e30
N_BRANCH = 4

LANE = 128
VMEM_LIMIT_BYTES = 56 * 1024 * 1024

NT_DIMS = (((1,), (1,)), ((), ()))
TN_DIMS = (((0,), (0,)), ((), ()))


def _params(*sem):
    return pltpu.CompilerParams(dimension_semantics=sem, vmem_limit_bytes=VMEM_LIMIT_BYTES)


def _sigmoid(x):
    return 0.5 * jnp.tanh(0.5 * x) + 0.5


def _pick(total, candidates):
    for c in candidates:
        if total % c == 0:
            return c
    raise ValueError(f"no tile for {total} in {candidates}")


def _ada_kernel(c_ref, w_ref, b_ref, o_ref):
    c = c_ref[...]
    a = (c * _sigmoid(c)).astype(BF16)
    o_ref[...] = jnp.dot(a, w_ref[...].astype(BF16), preferred_element_type=F32) + b_ref[...]


def ada_modulation(c_rows, ada_w, ada_b, layer):
    rows, d = c_rows.shape
    width = ada_w.shape[2]
    tn = _pick(width, (512, 256, 128))
    b3 = ada_b.reshape(ada_b.shape[0], 1, width)
    return pl.pallas_call(
        _ada_kernel,
        grid=(width // tn,),
        in_specs=[
            pl.BlockSpec((rows, d), lambda j: (0, 0)),
            pl.BlockSpec((None, d, tn), lambda j: (layer, 0, j)),
            pl.BlockSpec((None, 1, tn), lambda j: (layer, 0, j)),
        ],
        out_specs=pl.BlockSpec((rows, tn), lambda j: (0, j)),
        out_shape=jax.ShapeDtypeStruct((rows, width), F32),
        compiler_params=_params("parallel"),
        name="ada_modulation",
    )(c_rows, ada_w, b3)


def _rms(x, gain):
    return x * lax.rsqrt(jnp.mean(x * x, axis=-1, keepdims=True) + NORM_EPS) * gain


def _norm_mod_kernel(x_ref, g_ref, sh_ref, sc_ref, h_ref):
    h = _rms(x_ref[0], g_ref[...])
    h_ref[0] = (h * (1.0 + sc_ref[0]) + sh_ref[0]).astype(h_ref.dtype)


def _resid_kernel(x_ref, y_ref, gate_ref, gpost_ref, xo_ref):
    xo_ref[0] = x_ref[0] + gate_ref[0] * _rms(y_ref[0].astype(F32), gpost_ref[...])


def _resid_norm_mod_kernel(x_ref, y_ref, gate_ref, gpost_ref, gpre_ref, sh_ref, sc_ref, xo_ref, h_ref):
    xn = x_ref[0] + gate_ref[0] * _rms(y_ref[0].astype(F32), gpost_ref[...])
    xo_ref[0] = xn
    h = _rms(xn, gpre_ref[...])
    h_ref[0] = (h * (1.0 + sc_ref[0]) + sh_ref[0]).astype(h_ref.dtype)


def _row_specs(dims, tr, n_lat_tiles):
    s, d = dims
    xspec = pl.BlockSpec((1, tr, d), lambda b, t: (b, t, 0))
    gspec = pl.BlockSpec((1, d), lambda b, t: (0, 0))
    mspec = pl.BlockSpec((1, 1, d), lambda b, t: (2 * b + t // n_lat_tiles, 0, 0))
    return xspec, gspec, mspec


def norm_modulate(x, gain, shift, scale, n_lat, tr=256):
    bsz, s, d = x.shape
    xspec, gspec, mspec = _row_specs((s, d), tr, n_lat // tr)
    return pl.pallas_call(
        _norm_mod_kernel,
        grid=(bsz, s // tr),
        in_specs=[xspec, gspec, mspec, mspec],
        out_specs=xspec,
        out_shape=jax.ShapeDtypeStruct(x.shape, BF16),
        compiler_params=_params("parallel", "parallel"),
        name="norm_modulate",
    )(x, gain.reshape(1, d), shift, scale)


def residual(x, y, gate, gain_post, n_lat, tr=256):
    bsz, s, d = x.shape
    rows = y.shape[1]
    xspec, gspec, mspec = _row_specs((s, d), tr, n_lat // tr)
    return pl.pallas_call(
        _resid_kernel,
        grid=(bsz, rows // tr),
        in_specs=[xspec, xspec, mspec, gspec],
        out_specs=xspec,
        out_shape=jax.ShapeDtypeStruct(y.shape, F32),
        compiler_params=_params("parallel", "parallel"),
        name="residual",
    )(x, y, gate, gain_post.reshape(1, d))


def residual_norm_modulate(x, y, gate, gain_post, gain_pre, shift, scale, n_lat, tr=256):
    bsz, s, d = x.shape
    rows = y.shape[1]
    xspec, gspec, mspec = _row_specs((s, d), tr, n_lat // tr)
    return pl.pallas_call(
        _resid_norm_mod_kernel,
        grid=(bsz, rows // tr),
        in_specs=[xspec, xspec, mspec, gspec, gspec, mspec, mspec],
        out_specs=[xspec, xspec],
        out_shape=[jax.ShapeDtypeStruct(y.shape, F32), jax.ShapeDtypeStruct(y.shape, BF16)],
        compiler_params=_params("parallel", "parallel"),
        name="residual_norm_modulate",
    )(x, y, gate, gain_post.reshape(1, d), gain_pre.reshape(1, d), shift, scale)


def _mm_kernel(a_ref, b_ref, o_ref):
    o_ref[...] = jnp.dot(a_ref[...], b_ref[...], preferred_element_type=F32).astype(o_ref.dtype)


def _mm_acc_kernel(a_ref, b_ref, o_ref, acc_ref):
    k = pl.program_id(2)

    @pl.when(k == 0)
    def _():
        acc_ref[...] = jnp.zeros_like(acc_ref)

    acc_ref[...] += jnp.dot(a_ref[...], b_ref[...], preferred_element_type=F32)

    @pl.when(k == pl.num_programs(2) - 1)
    def _():
        o_ref[...] = acc_ref[...].astype(o_ref.dtype)


MAX_SINGLE_K = 4096
MATMUL_VMEM_BUDGET = VMEM_LIMIT_BYTES * 9 // 10


def matmul_leading_rows(a, b, rows, out_dtype, name):
    bsz, _, k = a.shape
    n = b.shape[1]
    assert k <= MAX_SINGLE_K
    bm = _pick(rows, (1024, 512, 256, 128))
    bn = _pick(n, (1024, 768, 640, 512, 384, 256, 128))
    tiles = rows // bm
    return pl.pallas_call(
        _mm_kernel,
        grid=(bsz, tiles, n // bn),
        in_specs=[pl.BlockSpec((None, bm, k), lambda s, i, j: (s, i, 0)),
                  pl.BlockSpec((k, bn), lambda s, i, j: (0, j))],
        out_specs=pl.BlockSpec((bm, bn), lambda s, i, j: (s * tiles + i, j)),
        out_shape=jax.ShapeDtypeStruct((bsz * rows, n), out_dtype),
        compiler_params=_params("parallel", "parallel", "parallel"),
        name=name,
    )(a, b)


def matmul(a, b, out_dtype, name):
    m, k = a.shape
    n = b.shape[1]
    out_bytes = jnp.dtype(out_dtype).itemsize
    full_k = None
    for cap in (1024, 512):
        bm = _pick(m, tuple(c for c in (1024, 512, 256, 128) if c <= cap))
        bn = _pick(n, tuple(c for c in (1024, 768, 640, 512, 384, 256, 128) if c <= cap))
        need = 2 * 2 * (bm * k + k * bn) + 2 * bm * bn * out_bytes + 4 * bm * bn
        if need <= MATMUL_VMEM_BUDGET:
            full_k = (bm, bn)
            break
    if full_k is not None:
        bm, bn = full_k
        return pl.pallas_call(
            _mm_kernel,
            grid=(m // bm, n // bn),
            in_specs=[pl.BlockSpec((bm, k), lambda i, j: (i, 0)),
                      pl.BlockSpec((k, bn), lambda i, j: (0, j))],
            out_specs=pl.BlockSpec((bm, bn), lambda i, j: (i, j)),
            out_shape=jax.ShapeDtypeStruct((m, n), out_dtype),
            compiler_params=_params("parallel", "parallel"),
            name=name,
        )(a, b)
    bm = _pick(m, (1024, 512, 256, 128))
    bn = _pick(n, (1024, 768, 640, 512, 384, 256, 128))
    nk = -(-k // MAX_SINGLE_K)
    while k % nk or (k // nk) % LANE:
        nk += 1
    bk = k // nk
    return pl.pallas_call(
        _mm_acc_kernel,
        grid=(m // bm, n // bn, nk),
        in_specs=[pl.BlockSpec((bm, bk), lambda i, j, kk: (i, kk)),
                  pl.BlockSpec((bk, bn), lambda i, j, kk: (kk, j))],
        out_specs=pl.BlockSpec((bm, bn), lambda i, j, kk: (i, j)),
        out_shape=jax.ShapeDtypeStruct((m, n), out_dtype),
        scratch_shapes=[pltpu.VMEM((bm, bn), F32)],
        compiler_params=_params("parallel", "parallel", "arbitrary"),
        name=name,
    )(a, b)


def _rope_tables(n_lat, n_ctx):
    t = jnp.arange(n_lat, dtype=jnp.int32)
    row = (t // GRID_W).astype(F32)
    col = (t % GRID_W).astype(F32)
    quarter = ROPE_DIM // 4
    inv = ROPE_THETA ** (-jnp.arange(quarter, dtype=F32) / quarter)
    ar = row[:, None] * inv
    ac = col[:, None] * inv
    ang = jnp.concatenate([ar, ar, ac, ac], axis=-1)
    cos = jnp.concatenate([jnp.cos(ang), jnp.ones((n_ctx, ROPE_DIM), F32)], axis=0)
    sin = jnp.concatenate([jnp.sin(ang), jnp.zeros((n_ctx, ROPE_DIM), F32)], axis=0)
    first = (jnp.arange(ROPE_DIM) // quarter) % 2 == 0
    sin_next = jnp.where(first, -sin, 0.0)
    sin_prev = jnp.where(first, 0.0, sin)
    reps = LANE // ROPE_DIM
    return tuple(jnp.tile(a, (1, reps)) for a in (cos, sin_next, sin_prev))


def _rope_apply(x, cos, sin_next, sin_prev):
    quarter = ROPE_DIM // 4
    return (x * cos + pltpu.roll(x, LANE - quarter, 1) * sin_next
            + pltpu.roll(x, quarter, 1) * sin_prev)


def _rope_kernel(x_ref, cos_ref, sn_ref, sp_ref, o_ref):
    cos, sn, sp = cos_ref[...], sn_ref[...], sp_ref[...]
    for c in range(x_ref.shape[1] // LANE):
        sl = slice(c * LANE, (c + 1) * LANE)
        o_ref[:, sl] = _rope_apply(x_ref[:, sl].astype(F32), cos, sn, sp).astype(o_ref.dtype)


def rope(x, tables, col_block, n_col_blocks, s, width=1024, tr=256):
    r = x.shape[0]
    width = min(width, x.shape[1])
    tiles_per_seq = s // tr
    tspec = pl.BlockSpec((tr, LANE), lambda i, j: (i % tiles_per_seq, 0))
    return pl.pallas_call(
        _rope_kernel,
        grid=(r // tr, n_col_blocks),
        in_specs=[pl.BlockSpec((tr, width), lambda i, j: (i, col_block + j)), tspec, tspec, tspec],
        out_specs=pl.BlockSpec((tr, width), lambda i, j: (i, j)),
        out_shape=jax.ShapeDtypeStruct((r, width * n_col_blocks), BF16),
        compiler_params=_params("parallel", "parallel"),
        name="rope",
    )(x, *tables)


LOG2E = math.log2(math.e)
KEY_CHUNK = 256


def _attend_tiles(k_ref, q_tiles, vt_ref, s_ref, e_ref):
    m_keys, n = s_ref.shape[1:]
    kc = min(KEY_CHUNK, m_keys)
    chunks = [slice(c * kc, (c + 1) * kc) for c in range(m_keys // kc)]

    def scores(t):
        s_ref[t % 2] = lax.dot_general(k_ref[...], q_tiles[t], NT_DIMS, preferred_element_type=F32)

    def column_max(t):
        m8 = None
        for sl in chunks:
            part = jnp.max(s_ref[t % 2, sl, :].reshape(kc // 8, 8, n), axis=0)
            m8 = part if m8 is None else jnp.maximum(m8, part)
        return jnp.max(m8, axis=0, keepdims=True)

    scores(0)
    m = column_max(0)
    outs = []
    for t in range(len(q_tiles)):
        if t + 1 < len(q_tiles):
            scores(t + 1)
        l8 = jnp.zeros((8, n), F32)
        for sl in chunks:
            e = jnp.exp2(s_ref[t % 2, sl, :] - m)
            l8 = l8 + jnp.sum(e.reshape(kc // 8, 8, n), axis=0)
            e_ref[t % 2, sl, :] = e.astype(e_ref.dtype)
        o_t = jnp.dot(vt_ref[...], e_ref[t % 2], preferred_element_type=F32)
        outs.append((o_t, jnp.sum(l8, axis=0, keepdims=True)))
        if t + 1 < len(q_tiles):
            m = column_max(t + 1)
    return outs


def _load_v_transposed(v_ref, vt_ref):
    @pl.when(pl.program_id(2) == 0)
    def _():
        vt_ref[...] = v_ref[...].astype(F32).T.astype(vt_ref.dtype)


def _diff_attn_kernel(q_ref, k_ref, v_ref, lam_ref, g_ref, o_ref, vt_ref, s_ref, e_ref, *, lam_init):
    _load_v_transposed(v_ref, vt_ref)
    tq = s_ref.shape[2] // 2
    q = (q_ref[...].astype(F32) * (QK_HALF ** -0.5 * LOG2E)).astype(k_ref.dtype)
    lane = lax.broadcasted_iota(jnp.int32, q.shape, 1)
    zero = jnp.zeros_like(q)
    q0 = jnp.where(lane < QK_HALF, q, zero)
    q1 = jnp.where(lane >= QK_HALF, q, zero)
    tiles = [slice(t * tq, (t + 1) * tq) for t in range(q.shape[0] // tq)]
    outs = _attend_tiles(k_ref, [jnp.concatenate([q0[sl], q1[sl]], axis=0) for sl in tiles], vt_ref, s_ref, e_ref)
    lp = lam_ref[...]
    lam = (jnp.exp(jnp.sum(lp[0:1] * lp[1:2], axis=-1, keepdims=True))
           - jnp.exp(jnp.sum(lp[2:3] * lp[3:4], axis=-1, keepdims=True)) + lam_init)
    gain = g_ref[...] * (1.0 - lam_init)
    for sl, (o_t, l) in zip(tiles, outs):
        inv = 1.0 / l
        c_t = o_t[:, :tq] * inv[:, :tq] - o_t[:, tq:] * (lam * inv[:, tq:])
        c_t = c_t * lax.rsqrt(jnp.mean(c_t * c_t, axis=0, keepdims=True) + NORM_EPS)
        o_ref[sl, :] = (c_t.T * gain).astype(o_ref.dtype)


def _sdpa_kernel(*refs, n_qk, scale):
    q_refs, k_refs = refs[:n_qk], refs[n_qk:2 * n_qk]
    v_ref, o_ref, vt_ref, s_ref, e_ref = refs[2 * n_qk:2 * n_qk + 5]
    _load_v_transposed(v_ref, vt_ref)
    if n_qk == 1:
        kcat_ref = k_refs[0]
    else:
        kcat_ref = refs[2 * n_qk + 5]

        @pl.when(pl.program_id(2) == 0)
        def _():
            for i, k_ref in enumerate(k_refs):
                kcat_ref[:, i * HEAD_DIM:(i + 1) * HEAD_DIM] = k_ref[...]

    tq = s_ref.shape[2]
    q = jnp.concatenate([q_ref[...] for q_ref in q_refs], axis=1)
    q = (q.astype(F32) * (scale * LOG2E)).astype(kcat_ref.dtype)
    tiles = [slice(t * tq, (t + 1) * tq) for t in range(q.shape[0] // tq)]
    outs = _attend_tiles(kcat_ref, [q[sl] for sl in tiles], vt_ref, s_ref, e_ref)
    for sl, (o_t, l) in zip(tiles, outs):
        o_ref[sl, :] = (o_t * (1.0 / l)).T.astype(o_ref.dtype)


def _attention_call(kernel, q_list, k_list, v, extras, *, bsz, heads, s, q_range, k_range, tq, lanes, name):
    q_start, q_len = q_range
    k_start, k_len = k_range
    nq = q_len // tq
    q_off = q_start // tq
    k_off = k_start // k_len
    assert q_start % tq == 0 and q_len % tq == 0 and k_start % k_len == 0

    def qspec(col0):
        return pl.BlockSpec((None, tq, HEAD_DIM), lambda b, h, i: (b, q_off + i, col0 + h))

    def kspec(col0, stride, shared=False):
        if shared:
            return pl.BlockSpec((None, k_len, HEAD_DIM), lambda b, h, i: (b, k_off, col0))
        return pl.BlockSpec((None, k_len, HEAD_DIM), lambda b, h, i: (b, k_off, col0 + stride * h))

    def per_sample(arr):
        return arr.reshape(bsz, s, arr.shape[-1])

    in_specs, args = [], []
    for arr, col0 in q_list:
        in_specs.append(qspec(col0)); args.append(per_sample(arr))
    for arr, col0, stride, shared in k_list:
        in_specs.append(kspec(col0, stride, shared)); args.append(per_sample(arr))
    arr, col0, stride = v
    in_specs.append(kspec(col0, stride)); args.append(per_sample(arr))
    for e in extras:
        in_specs.append(pl.BlockSpec(e.shape, lambda b, h, i: (0, 0))); args.append(e)
    return pl.pallas_call(
        kernel,
        grid=(bsz, heads, nq),
        in_specs=in_specs,
        out_specs=pl.BlockSpec((tq, HEAD_DIM), lambda b, h, i: (b * nq + i, h)),
        out_shape=jax.ShapeDtypeStruct((bsz * q_len, heads * HEAD_DIM), BF16),
        scratch_shapes=[pltpu.VMEM((HEAD_DIM, k_len), BF16), pltpu.VMEM((2, k_len, lanes), F32),
                        pltpu.VMEM((2, k_len, lanes), BF16)]
        + ([pltpu.VMEM((k_len, len(k_list) * HEAD_DIM), BF16)] if len(k_list) > 1 else []),
        compiler_params=_params("parallel", "parallel", "arbitrary"),
        name=name,
    )(*args)


def _attend_lat_and_ctx(call, n_lat, n_ctx, with_ctx):
    s = n_lat + n_ctx
    o_lat = call((0, n_lat), (0, s), "lat")
    if not with_ctx:
        return o_lat
    bsz = o_lat.shape[0] // n_lat
    o_ctx = call((n_lat, n_ctx), (n_lat, n_ctx), "ctx").reshape(bsz, n_ctx, -1)
    return jnp.concatenate([o_lat.reshape(bsz, n_lat, -1), o_ctx], axis=1).reshape(bsz * s, -1)


def _na_bias_table(rpb):
    col = jnp.arange(GRID_W, dtype=jnp.int32)
    cs = jnp.clip(col - NA_COLS // 2, 0, GRID_W - NA_COLS)
    colmask = (col[None, :] >= cs[:, None]) & (col[None, :] < cs[:, None] + NA_COLS)
    dc = jnp.clip(col[None, :] - col[:, None] + NA_COLS - 1, 0, 2 * NA_COLS - 2)
    delta = jnp.arange(NA_ROWS, dtype=jnp.int32)
    jj = jnp.arange(NA_ROWS, dtype=jnp.int32)
    dr = jj[None, :] - delta[:, None] + NA_ROWS - 1
    bias = rpb[:, dr][:, :, :, dc]
    bias = jnp.where(colmask[None, None, None], bias, NEG_INF)
    bias = jnp.transpose(bias, (0, 1, 3, 2, 4))
    return bias.reshape(rpb.shape[0], NA_ROWS, GRID_W, NA_ROWS * GRID_W).astype(F32)


def _na_kernel(q_ref, k_ref, v_ref, bias_ref, o_ref, *, rows_per_step, n_lat, grid_rows):
    g = pl.program_id(2)
    scale = HEAD_DIM ** -0.5
    win = NA_ROWS * GRID_W
    kc = k_ref[n_lat:, :]
    vc = v_ref[n_lat:, :]
    for i in range(rows_per_step):
        r = g * rows_per_step + i
        rs = jnp.clip(r - NA_ROWS // 2, 0, grid_rows - NA_ROWS)
        start = pl.multiple_of(rs * GRID_W, GRID_W)
        q = q_ref[i * GRID_W:(i + 1) * GRID_W, :]
        kw = k_ref[pl.ds(start, win), :]
        vw = v_ref[pl.ds(start, win), :]
        sl = lax.dot_general(q, kw, NT_DIMS, preferred_element_type=F32) * scale + bias_ref[r - rs]
        sc = lax.dot_general(q, kc, NT_DIMS, preferred_element_type=F32) * scale
        m = jnp.maximum(jnp.max(sl, axis=-1, keepdims=True), jnp.max(sc, axis=-1, keepdims=True))
        el = jnp.exp(sl - m)
        ec = jnp.exp(sc - m)
        inv = 1.0 / (jnp.sum(el, axis=-1, keepdims=True) + jnp.sum(ec, axis=-1, keepdims=True))
        o = (jnp.dot(el.astype(BF16), vw, preferred_element_type=F32)
             + jnp.dot(ec.astype(BF16), vc, preferred_element_type=F32))
        o_ref[i * GRID_W:(i + 1) * GRID_W, :] = (o * inv).astype(o_ref.dtype)


def neighbourhood_attention(pd, bias, bsz, heads, n_lat, n_ctx, rows_per_step=8):
    s = n_lat + n_ctx
    grid_rows = n_lat // GRID_W
    tq = rows_per_step * GRID_W
    assert grid_rows >= NA_ROWS and grid_rows % rows_per_step == 0
    steps = grid_rows // rows_per_step
    pd3 = pd.reshape(bsz, s, pd.shape[-1])
    return pl.pallas_call(
        functools.partial(_na_kernel, rows_per_step=rows_per_step, n_lat=n_lat, grid_rows=grid_rows),
        grid=(bsz, heads, steps),
        in_specs=[
            pl.BlockSpec((None, tq, HEAD_DIM), lambda b, h, g: (b, g, h)),
            pl.BlockSpec((None, s, HEAD_DIM), lambda b, h, g: (b, 0, heads + h)),
            pl.BlockSpec((None, s, HEAD_DIM), lambda b, h, g: (b, 0, 2 * heads + h)),
            pl.BlockSpec((None, NA_ROWS, GRID_W, NA_ROWS * GRID_W), lambda b, h, g: (h, 0, 0, 0)),
        ],
        out_specs=pl.BlockSpec((tq, HEAD_DIM), lambda b, h, g: (b * steps + g, h)),
        out_shape=jax.ShapeDtypeStruct((bsz * n_lat, heads * HEAD_DIM), BF16),
        compiler_params=_params("parallel", "parallel", "arbitrary"),
        name="neighbourhood_attention",
    )(pd3, pd3, pd3, bias)


def _hgrn_constants(reverse):
    c = CHUNK
    t = np.arange(c)
    tri = (t[None, :] >= t[:, None]) if reverse else (t[None, :] <= t[:, None])
    mats = [tri.astype(np.float32)]
    masks = [np.eye(c, dtype=np.float32)]
    m = c // 2
    while m >= 1:
        node = (t // (2 * m)) * (2 * m)
        ref = node + (m if reverse else m - 1)
        mats.append(tri[ref].astype(np.float32))
        upper = (t % (2 * m)) >= m
        same = node[:, None] == node[None, :]
        pair = same & (upper[:, None] & ~upper[None, :])
        masks.append((pair.T if reverse else pair).astype(np.float32))
        m //= 2
    return np.concatenate(mats, axis=0), np.stack(masks)


def _hgrn_kernel(cq_ref, ci_ref, z_ref, lb_ref, ms_ref, mk_ref, o_ref, st_ref, *, heads, reverse):
    @pl.when(pl.program_id(1) == 0)
    def _():
        st_ref[...] = jnp.zeros_like(st_ref)

    c = CHUNK
    n_levels = mk_ref.shape[0]
    ms = ms_ref[...].astype(BF16)
    scale = HEAD_DIM ** -0.5
    tot_row = 0 if reverse else c - 1
    for h in range(heads):
        sl = slice(h * HEAD_DIM, (h + 1) * HEAD_DIM)
        lb = lb_ref[:, sl]
        f = lb + (1.0 - lb) * _sigmoid(z_ref[:, sl].astype(F32))
        kk = 1.0 - f
        g = jnp.log(f)
        cq = cq_ref[:, sl].astype(F32)
        qq = cq * _sigmoid(cq) * scale
        v = ci_ref[:, sl]
        g_hi = g.astype(BF16)
        r1 = g - g_hi.astype(F32)
        g_mid = r1.astype(BF16)
        g_lo = (r1 - g_mid.astype(F32)).astype(BF16)
        cs3 = jnp.dot(ms, jnp.concatenate([g_hi, g_mid, g_lo], axis=1), preferred_element_type=F32)
        cs = cs3[:, :HEAD_DIM] + cs3[:, HEAD_DIM:2 * HEAD_DIM] + cs3[:, 2 * HEAD_DIM:]
        cum = cs[:c]
        tot = cs[tot_row:tot_row + 1]
        st = st_ref[h]
        o = lax.dot_general((qq * jnp.exp(cum)).astype(BF16), st.astype(BF16), NT_DIMS,
                            preferred_element_type=F32)
        a = mk_ref[0] * lax.dot_general(qq.astype(BF16), kk.astype(BF16), NT_DIMS,
                                        preferred_element_type=F32)
        for lev in range(1, n_levels):
            e = jnp.exp(-jnp.abs(cum - cs[lev * c:(lev + 1) * c]))
            a = a + mk_ref[lev] * lax.dot_general((qq * e).astype(BF16), (kk * e).astype(BF16), NT_DIMS,
                                                  preferred_element_type=F32)
        o = o + jnp.dot(a.astype(BF16), v, preferred_element_type=F32)
        o_ref[:, sl] = o
        ke = (kk * jnp.exp(tot - cum)).astype(BF16)
        st_ref[h] = st * jnp.exp(tot) + lax.dot_general(v, ke, TN_DIMS, preferred_element_type=F32)


def hgrn_scan(pc, lb, bsz, heads, n_lat, n_ctx, reverse):
    s = n_lat + n_ctx
    chunks = s // CHUNK
    lat_chunks = n_lat // CHUNK
    width = heads * HEAD_DIM
    ms, mk = _hgrn_constants(reverse)

    def chunk_of(step):
        return (chunks - 1 - step) if reverse else lax.rem(step + lat_chunks, chunks)

    def spec(col):
        return pl.BlockSpec((CHUNK, width), lambda b, t: (b * chunks + chunk_of(t), col))

    return pl.pallas_call(
        functools.partial(_hgrn_kernel, heads=heads, reverse=reverse),
        grid=(bsz, chunks),
        in_specs=[spec(0), spec(1), spec(3 if reverse else 2),
                  pl.BlockSpec((1, width), lambda b, t: (0, 0)),
                  pl.BlockSpec(ms.shape, lambda b, t: (0, 0)),
                  pl.BlockSpec(mk.shape, lambda b, t: (0, 0, 0))],
        out_specs=spec(0),
        out_shape=jax.ShapeDtypeStruct((bsz * s, width), F32),
        scratch_shapes=[pltpu.VMEM((heads, HEAD_DIM, HEAD_DIM), F32)],
        compiler_params=_params("parallel", "arbitrary"),
        name="hgrn_scan_bwd" if reverse else "hgrn_scan_fwd",
    )(pc, pc, pc, lb.reshape(1, width), jnp.asarray(ms), jnp.asarray(mk))


def _hgrn_readout_kernel(of_ref, ob_ref, gate_ref, g_ref, o_ref):
    gain = g_ref[...]
    for h in range(of_ref.shape[1] // HEAD_DIM):
        sl = slice(h * HEAD_DIM, (h + 1) * HEAD_DIM)
        gate = gate_ref[:, sl].astype(F32)
        o = _rms(of_ref[:, sl] + ob_ref[:, sl], gain)
        o_ref[:, sl] = (o * (gate * _sigmoid(gate))).astype(o_ref.dtype)


def hgrn_readout(o_fwd, o_bwd, pc, gain, bsz, s, rows, tr=256):
    width = o_fwd.shape[1]
    in_tiles, out_tiles = s // tr, rows // tr
    ispec = pl.BlockSpec((tr, width), lambda b, i: (b * in_tiles + i, 0))
    return pl.pallas_call(
        _hgrn_readout_kernel,
        grid=(bsz, out_tiles),
        in_specs=[ispec, ispec, pl.BlockSpec((tr, width), lambda b, i: (b * in_tiles + i, 4)),
                  pl.BlockSpec((1, HEAD_DIM), lambda b, i: (0, 0))],
        out_specs=pl.BlockSpec((tr, width), lambda b, i: (b * out_tiles + i, 0)),
        out_shape=jax.ShapeDtypeStruct((bsz * rows, width), BF16),
        compiler_params=_params("parallel", "parallel"),
        name="hgrn_readout",
    )(o_fwd, o_bwd, pc, gain.reshape(1, HEAD_DIM))


def _mla_prep_kernel(p_ref, qg_ref, kg_ref, cos_ref, sn_ref, sp_ref, qn_ref, cn_ref, kr_ref, *, q_rank, kv_rank):
    qn_ref[...] = _rms(p_ref[:, :q_rank].astype(F32), qg_ref[...]).astype(qn_ref.dtype)
    cn_ref[...] = _rms(p_ref[:, q_rank:q_rank + kv_rank].astype(F32), kg_ref[...]).astype(cn_ref.dtype)
    kr = p_ref[:, q_rank + kv_rank:q_rank + kv_rank + LANE].astype(F32)
    kr_ref[...] = _rope_apply(kr, cos_ref[...], sn_ref[...], sp_ref[...]).astype(kr_ref.dtype)


def mla_prep(pb, q_gain, kv_gain, tables, s, tr=256):
    r, width = pb.shape
    q_rank, kv_rank = q_gain.shape[0], kv_gain.shape[0]
    assert width >= q_rank + kv_rank + LANE and q_rank % LANE == 0 and kv_rank % LANE == 0
    tiles_per_seq = s // tr
    tspec = pl.BlockSpec((tr, LANE), lambda i: (i % tiles_per_seq, 0))

    def ospec(w):
        return pl.BlockSpec((tr, w), lambda i: (i, 0))

    return pl.pallas_call(
        functools.partial(_mla_prep_kernel, q_rank=q_rank, kv_rank=kv_rank),
        grid=(r // tr,),
        in_specs=[ospec(width), pl.BlockSpec((1, q_rank), lambda i: (0, 0)),
                  pl.BlockSpec((1, kv_rank), lambda i: (0, 0)), tspec, tspec, tspec],
        out_specs=[ospec(q_rank), ospec(kv_rank), ospec(LANE)],
        out_shape=[jax.ShapeDtypeStruct((r, q_rank), BF16), jax.ShapeDtypeStruct((r, kv_rank), BF16),
                   jax.ShapeDtypeStruct((r, LANE), BF16)],
        compiler_params=_params("parallel"),
        name="mla_prep",
    )(pb, q_gain.reshape(1, q_rank), kv_gain.reshape(1, kv_rank), *tables)


def _merge_kernel(oa_ref, ob_ref, oc_ref, od_ref, ga_ref, gb_ref, gc_ref, gd_ref, w_ref, o_ref):
    acc = None
    for i, (o_ref_i, g_ref_i) in enumerate(((oa_ref, ga_ref), (ob_ref, gb_ref), (oc_ref, gc_ref), (od_ref, gd_ref))):
        y = jnp.dot(o_ref_i[...], w_ref[i], preferred_element_type=F32)
        term = _sigmoid(g_ref_i[...].astype(F32)) * y
        acc = term if acc is None else acc + term
    o_ref[...] = acc.astype(o_ref.dtype)


def merge_branches(outs, gate_logits, w_branch):
    r, c = outs[0].shape
    d = w_branch.shape[2]
    bm = _pick(r, (512, 256, 128))
    bn = _pick(d, (1024, 512, 256, 128))
    nb = d // bn
    ospec = pl.BlockSpec((bm, c), lambda i, j: (i, 0))

    def gspec(branch):
        return pl.BlockSpec((bm, bn), lambda i, j: (i, branch * nb + j))

    return pl.pallas_call(
        _merge_kernel,
        grid=(r // bm, nb),
        in_specs=[ospec] * N_BRANCH + [gspec(i) for i in range(N_BRANCH)]
        + [pl.BlockSpec((N_BRANCH, c, bn), lambda i, j: (0, 0, j))],
        out_specs=pl.BlockSpec((bm, bn), lambda i, j: (i, j)),
        out_shape=jax.ShapeDtypeStruct((r, d), BF16),
        compiler_params=_params("parallel", "parallel"),
        name="merge_branches",
    )(*outs, gate_logits, gate_logits, gate_logits, gate_logits, w_branch)


HALO = 16


FFN_BN = 640


def _ffn_tile_pairs(a, bn):
    f = a.shape[-1] // 2
    lead = a.shape[:-1]
    pairs = jnp.stack([a[..., :f].reshape(*lead, f // bn, bn), a[..., f:].reshape(*lead, f // bn, bn)], axis=-2)
    return pairs.reshape(*lead, 2 * f)


def _cast_pair_kernel(g_ref, v_ref, o_ref):
    bn = g_ref.shape[1]
    o_ref[:, :bn] = g_ref[...].astype(o_ref.dtype)
    o_ref[:, bn:] = v_ref[...].astype(o_ref.dtype)


def ffn_up_weight(w_up, layer, bn):
    _, d, two_f = w_up.shape
    nb = two_f // 2 // bn
    tk = _pick(d, (1024, 512, 256, 128))
    return pl.pallas_call(
        _cast_pair_kernel,
        grid=(d // tk, nb),
        in_specs=[pl.BlockSpec((None, tk, bn), lambda i, j: (layer, i, j)),
                  pl.BlockSpec((None, tk, bn), lambda i, j: (layer, i, nb + j))],
        out_specs=pl.BlockSpec((tk, 2 * bn), lambda i, j: (i, j)),
        out_shape=jax.ShapeDtypeStruct((d, two_f), BF16),
        compiler_params=_params("parallel", "parallel"),
        name="ffn_up_weight",
    )(w_up, w_up)


def _ffn_up_kernel(h_ref, hp_ref, hn_ref, w_ref, cw_ref, cb_ref, o_ref, hbuf_ref, *, bm, s, n_lat):
    i = pl.program_id(0)
    bn = o_ref.shape[1]

    @pl.when(pl.program_id(1) == 0)
    def _():
        hbuf_ref[0:HALO, :] = hp_ref[...]
        hbuf_ref[HALO:HALO + bm, :] = h_ref[...]
        hbuf_ref[HALO + bm:, :] = hn_ref[...]

    pos = lax.rem(i * bm + lax.broadcasted_iota(jnp.int32, (bm, 1), 0), s)
    has_prev = jnp.where((pos == 0) | (pos == n_lat), 0.0, 1.0)
    has_next = jnp.where((pos == n_lat - 1) | (pos == s - 1), 0.0, 1.0)
    rows = bm + 2 * HALO
    u = jnp.dot(hbuf_ref[...], w_ref[...], preferred_element_type=F32)
    prev = pltpu.roll(u, 1, 0)[HALO:HALO + bm]
    nxt = pltpu.roll(u, rows - 1, 0)[HALO:HALO + bm]
    cw = cw_ref[...]
    y = prev * has_prev * cw[0:1] + u[HALO:HALO + bm] * cw[1:2] + nxt * has_next * cw[2:3] + cb_ref[...]
    gate, val = y[:, :bn], y[:, bn:]
    o_ref[...] = (gate * _sigmoid(gate) * val).astype(o_ref.dtype)


def ffn_up(h, w_up, conv_w, conv_b, s, n_lat):
    r, d = h.shape
    f = w_up.shape[1] // 2
    bm = _pick(r, (512, 256, 128))
    bn = FFN_BN
    assert f % bn == 0
    halo_blocks = r // HALO
    per_tile = bm // HALO
    return pl.pallas_call(
        functools.partial(_ffn_up_kernel, bm=bm, s=s, n_lat=n_lat),
        grid=(r // bm, f // bn),
        in_specs=[
            pl.BlockSpec((bm, d), lambda i, j: (i, 0)),
            pl.BlockSpec((HALO, d), lambda i, j: (jnp.maximum(i * per_tile - 1, 0), 0)),
            pl.BlockSpec((HALO, d), lambda i, j: (jnp.minimum((i + 1) * per_tile, halo_blocks - 1), 0)),
            pl.BlockSpec((d, 2 * bn), lambda i, j: (0, j)),
            pl.BlockSpec((3, 2 * bn), lambda i, j: (0, j)),
            pl.BlockSpec((1, 2 * bn), lambda i, j: (0, j)),
        ],
        out_specs=pl.BlockSpec((bm, bn), lambda i, j: (i, j)),
        out_shape=jax.ShapeDtypeStruct((r, f), BF16),
        scratch_shapes=[pltpu.VMEM((bm + 2 * HALO, d), BF16)],
        compiler_params=_params("parallel", "arbitrary"),
        name="ffn_up",
    )(h, h, h, w_up, conv_w, conv_b)


def _split_w_in(w, d, heads):
    mix = heads * HEAD_DIM
    q_rank = 3 * d // 16
    kv_rank = d // 8
    sizes = [('a', 3 * mix), ('b_qa', q_rank), ('b_kva', kv_rank + MLA_ROPE), ('c', 5 * mix), ('d', 3 * mix),
             ('gate', N_BRANCH * d)]
    out, start = {}, 0
    for name, size in sizes:
        out[name] = (start, size)
        start += size
    assert start == w.shape[1]

    def cols(name):
        a, n = out[name]
        return w[:, a:a + n].astype(BF16)

    used = q_rank + kv_rank + LANE
    pad = -used % 512 + LANE - MLA_ROPE
    w_b = jnp.concatenate([cols('b_qa'), cols('b_kva'), jnp.zeros((d, pad), BF16)], axis=1)
    return cols('a'), w_b, cols('c'), cols('d'), cols('gate'), q_rank, kv_rank


def _mla_q_weight(w_qb, heads):
    rq = w_qb.shape[0]
    w = w_qb.reshape(rq, heads, MLA_NOPE + MLA_ROPE)
    nope = w[:, :, :MLA_NOPE].reshape(rq, heads * MLA_NOPE)
    ropep = jnp.pad(w[:, :, MLA_NOPE:], ((0, 0), (0, 0), (0, LANE - MLA_ROPE))).reshape(rq, heads * LANE)
    return jnp.concatenate([nope, ropep], axis=1).astype(BF16)


def kernel(x, c, ctx, c_ctx, ada_w, ada_b, norm_mix_pre, norm_mix_post, norm_ffn_pre, norm_ffn_post, w_in,
           diff_lambda, diff_subln, mla_q_norm, mla_w_qb, mla_kv_norm, mla_w_kvb, hgrn_lb_logits, hgrn_norm,
           na_rpb, w_branch, w_out, ffn_w_up, ffn_conv_w, ffn_conv_b, ffn_w_down):
    bsz, n_lat, d = x.shape
    n_ctx = ctx.shape[1]
    s = n_lat + n_ctx
    r = bsz * s
    depth = ada_w.shape[0]
    heads = d // (N_BRANCH * HEAD_DIM)
    mix = heads * HEAD_DIM
    assert n_lat % 256 == 0 and n_ctx % 256 == 0 and s % n_ctx == 0 and n_lat % GRID_W == 0

    tables = _rope_tables(n_lat, n_ctx)
    lb_sm = jax.nn.softmax(hgrn_lb_logits.astype(F32), axis=0)
    lower_bounds = jnp.cumsum(lb_sm, axis=0) - lb_sm[0]

    c_rows = jnp.zeros((8, d), F32).at[:bsz].set(c).at[bsz].set(c_ctx)
    xa = jnp.concatenate([x, ctx], axis=1)

    def mod_rows(mod, idx):
        part = mod[:, idx * d:(idx + 1) * d]
        both = jnp.stack([part[:bsz], jnp.broadcast_to(part[bsz], (bsz, d))], axis=1)
        return both.reshape(2 * bsz, 1, d)

    mods = []
    for l in range(depth):
        mod = ada_modulation(c_rows, ada_w, ada_b, l)
        mods.append([mod_rows(mod, i) for i in range(6)])

    h = norm_modulate(xa, norm_mix_pre[0], mods[0][0], mods[0][1], n_lat)
    for l in range(depth):
        with_ctx = l < depth - 1
        sh1, sc1, g1, sh2, sc2, g2 = mods[l]
        h2d = h.reshape(r, d)
        rows = s if with_ctx else n_lat

        w_a, w_b, w_c, w_d, w_g, q_rank, kv_rank = _split_w_in(w_in[l], d, heads)
        pa = matmul(h2d, w_a, BF16, "in_proj_a")
        pb = matmul(h2d, w_b, BF16, "in_proj_b")
        pc = matmul(h2d, w_c, BF16, "in_proj_c")
        pd = matmul(h2d, w_d, BF16, "in_proj_d")
        if with_ctx:
            pg = matmul(h2d, w_g, BF16, "in_proj_gate")
        else:
            pg = matmul_leading_rows(h, w_g, rows, BF16, "in_proj_gate")

        qk = rope(pa, tables, 0, 2, s, width=mix)
        lam_init = 0.8 - 0.6 * math.exp(-0.3 * l)
        extras_a = [diff_lambda[l].astype(F32), diff_subln[l].reshape(1, HEAD_DIM).astype(F32)]

        def call_a(q_range, k_range, tag):
            return _attention_call(
                functools.partial(_diff_attn_kernel, lam_init=lam_init),
                [(qk, 0)], [(qk, heads, 1, False)], (pa, 2 * heads, 1), extras_a,
                bsz=bsz, heads=heads, s=s, q_range=q_range, k_range=k_range, tq=min(1024, q_range[1]), lanes=256,
                name="diff_attn_" + tag)

        oa = _attend_lat_and_ctx(call_a, n_lat, n_ctx, with_ctx)

        qn, cn, kr = mla_prep(pb, mla_q_norm[l], mla_kv_norm[l], tables, s)
        qb = matmul(qn, _mla_q_weight(mla_w_qb[l], heads), BF16, "mla_q_up")
        qb_rope = rope(qb, tables, 1, 1, s, width=mix)
        kvb = matmul(cn, mla_w_kvb[l].astype(BF16), BF16, "mla_kv_up")

        def call_b(q_range, k_range, tag):
            return _attention_call(
                functools.partial(_sdpa_kernel, n_qk=2, scale=(MLA_NOPE + MLA_ROPE) ** -0.5),
                [(qb, 0), (qb_rope, 0)], [(kvb, 0, 2, False), (kr, 0, 0, True)], (kvb, 1, 2), [],
                bsz=bsz, heads=heads, s=s, q_range=q_range, k_range=k_range, tq=min(2048, q_range[1]), lanes=256,
                name="mla_attn_" + tag)

        ob = _attend_lat_and_ctx(call_b, n_lat, n_ctx, with_ctx)

        o_fwd = hgrn_scan(pc, lower_bounds[l, 0], bsz, heads, n_lat, n_ctx, reverse=False)
        o_bwd = hgrn_scan(pc, lower_bounds[l, 1], bsz, heads, n_lat, n_ctx, reverse=True)
        oc = hgrn_readout(o_fwd, o_bwd, pc, hgrn_norm[l], bsz, s, rows)

        od = neighbourhood_attention(pd, _na_bias_table(na_rpb[l]), bsz, heads, n_lat, n_ctx)
        if with_ctx:
            od_ctx = _attention_call(
                functools.partial(_sdpa_kernel, n_qk=1, scale=HEAD_DIM ** -0.5),
                [(pd, 0)], [(pd, heads, 1, False)], (pd, 2 * heads, 1), [],
                bsz=bsz, heads=heads, s=s, q_range=(n_lat, n_ctx), k_range=(n_lat, n_ctx), tq=256, lanes=256,
                name="na_ctx_attn").reshape(bsz, n_ctx, mix)
            od = jnp.concatenate([od.reshape(bsz, n_lat, mix), od_ctx], axis=1).reshape(r, mix)

        merged = merge_branches((oa, ob, oc, od), pg, w_branch[l].astype(BF16))
        mix_out = matmul(merged, w_out[l].astype(BF16), BF16, "out_proj").reshape(bsz, rows, d)
        xa, h2 = residual_norm_modulate(xa, mix_out, g1, norm_mix_post[l], norm_ffn_pre[l], sh2, sc2, n_lat)

        gv = ffn_up(h2.reshape(bsz * rows, d), ffn_up_weight(ffn_w_up, l, FFN_BN),
                    _ffn_tile_pairs(ffn_conv_w[l], FFN_BN), _ffn_tile_pairs(ffn_conv_b[l][None], FFN_BN),
                    rows, n_lat)
        ffn = matmul(gv, ffn_w_down[l].astype(BF16), BF16, "ffn_down").reshape(bsz, rows, d)
        if with_ctx:
            xa, h = residual_norm_modulate(xa, ffn, g2, norm_ffn_post[l], norm_mix_pre[l + 1],
                                           mods[l + 1][0], mods[l + 1][1], n_lat)
        else:
            xa = residual(xa, ffn, g2, norm_ffn_post[l], n_lat)
    return xa
```

```python
import functools
import math

import numpy as np
import jax
import jax.numpy as jnp
from jax import lax
from jax.experimental import pallas as pl
from jax.experimental.pallas import tpu as pltpu

F32 = jnp.float32
BF16 = jnp.bfloat16

HEAD_DIM = 128
QK_HALF = 64
MLA_NOPE = 128
MLA_ROPE = 64
MLA_V = 128
ROPE_DIM = 64
ROPE_THETA = 10000.0
GRID_W = 64
NA_ROWS = 8
NA_COLS = 16
CHUNK = 128
NORM_EPS = 1e-6
NEG_INF = -1e30
N_BRANCH = 4

LANE = 128
VMEM_LIMIT_BYTES = 56 * 1024 * 1024

NT_DIMS = (((1,), (1,)), ((), ()))
TN_DIMS = (((0,), (0,)), ((), ()))


def _params(*sem):
    return pltpu.CompilerParams(dimension_semantics=sem, vmem_limit_bytes=VMEM_LIMIT_BYTES)


def _sigmoid(x):
    return 0.5 * jnp.tanh(0.5 * x) + 0.5


def _pick(total, candidates):
    for c in candidates:
        if total % c == 0:
            return c
    raise ValueError(f"no tile for {total} in {candidates}")


def _ada_kernel(c_ref, w_ref, b_ref, o_ref):
    c = c_ref[...]
    a = (c * _sigmoid(c)).astype(BF16)
    o_ref[...] = jnp.dot(a, w_ref[...].astype(BF16), preferred_element_type=F32) + b_ref[...]


def ada_modulation(c_rows, ada_w, ada_b, layer):
    rows, d = c_rows.shape
    width = ada_w.shape[2]
    tn = _pick(width, (512, 256, 128))
    b3 = ada_b.reshape(ada_b.shape[0], 1, width)
    return pl.pallas_call(
        _ada_kernel,
        grid=(width // tn,),
        in_specs=[
            pl.BlockSpec((rows, d), lambda j: (0, 0)),
            pl.BlockSpec((None, d, tn), lambda j: (layer, 0, j)),
            pl.BlockSpec((None, 1, tn), lambda j: (layer, 0, j)),
        ],
        out_specs=pl.BlockSpec((rows, tn), lambda j: (0, j)),
        out_shape=jax.ShapeDtypeStruct((rows, width), F32),
        compiler_params=_params("parallel"),
        name="ada_modulation",
    )(c_rows, ada_w, b3)


def _rms(x, gain):
    return x * lax.rsqrt(jnp.mean(x * x, axis=-1, keepdims=True) + NORM_EPS) * gain


def _norm_mod_kernel(x_ref, g_ref, sh_ref, sc_ref, h_ref):
    h = _rms(x_ref[0], g_ref[...])
    h_ref[0] = (h * (1.0 + sc_ref[0]) + sh_ref[0]).astype(h_ref.dtype)


def _resid_kernel(x_ref, y_ref, gate_ref, gpost_ref, xo_ref):
    xo_ref[0] = x_ref[0] + gate_ref[0] * _rms(y_ref[0].astype(F32), gpost_ref[...])


def _resid_norm_mod_kernel(x_ref, y_ref, gate_ref, gpost_ref, gpre_ref, sh_ref, sc_ref, xo_ref, h_ref):
    xn = x_ref[0] + gate_ref[0] * _rms(y_ref[0].astype(F32), gpost_ref[...])
    xo_ref[0] = xn
    h = _rms(xn, gpre_ref[...])
    h_ref[0] = (h * (1.0 + sc_ref[0]) + sh_ref[0]).astype(h_ref.dtype)


def _row_specs(dims, tr, n_lat_tiles):
    s, d = dims
    xspec = pl.BlockSpec((1, tr, d), lambda b, t: (b, t, 0))
    gspec = pl.BlockSpec((1, d), lambda b, t: (0, 0))
    mspec = pl.BlockSpec((1, 1, d), lambda b, t: (2 * b + t // n_lat_tiles, 0, 0))
    return xspec, gspec, mspec


def norm_modulate(x, gain, shift, scale, n_lat, tr=256):
    bsz, s, d = x.shape
    xspec, gspec, mspec = _row_specs((s, d), tr, n_lat // tr)
    return pl.pallas_call(
        _norm_mod_kernel,
        grid=(bsz, s // tr),
        in_specs=[xspec, gspec, mspec, mspec],
        out_specs=xspec,
        out_shape=jax.ShapeDtypeStruct(x.shape, BF16),
        compiler_params=_params("parallel", "parallel"),
        name="norm_modulate",
    )(x, gain.reshape(1, d), shift, scale)


def residual(x, y, gate, gain_post, n_lat, tr=256):
    bsz, s, d = x.shape
    rows = y.shape[1]
    xspec, gspec, mspec = _row_specs((s, d), tr, n_lat // tr)
    return pl.pallas_call(
        _resid_kernel,
        grid=(bsz, rows // tr),
        in_specs=[xspec, xspec, mspec, gspec],
        out_specs=xspec,
        out_shape=jax.ShapeDtypeStruct(y.shape, F32),
        compiler_params=_params("parallel", "parallel"),
        name="residual",
    )(x, y, gate, gain_post.reshape(1, d))


def residual_norm_modulate(x, y, gate, gain_post, gain_pre, shift, scale, n_lat, tr=256):
    bsz, s, d = x.shape
    rows = y.shape[1]
    xspec, gspec, mspec = _row_specs((s, d), tr, n_lat // tr)
    return pl.pallas_call(
        _resid_norm_mod_kernel,
        grid=(bsz, rows // tr),
        in_specs=[xspec, xspec, mspec, gspec, gspec, mspec, mspec],
        out_specs=[xspec, xspec],
        out_shape=[jax.ShapeDtypeStruct(y.shape, F32), jax.ShapeDtypeStruct(y.shape, BF16)],
        compiler_params=_params("parallel", "parallel"),
        name="residual_norm_modulate",
    )(x, y, gate, gain_post.reshape(1, d), gain_pre.reshape(1, d), shift, scale)


def _mm_kernel(a_ref, b_ref, o_ref):
    o_ref[...] = jnp.dot(a_ref[...], b_ref[...], preferred_element_type=F32).astype(o_ref.dtype)


def _mm_acc_kernel(a_ref, b_ref, o_ref, acc_ref):
    k = pl.program_id(2)

    @pl.when(k == 0)
    def _():
        acc_ref[...] = jnp.zeros_like(acc_ref)

    acc_ref[...] += jnp.dot(a_ref[...], b_ref[...], preferred_element_type=F32)

    @pl.when(k == pl.num_programs(2) - 1)
    def _():
        o_ref[...] = acc_ref[...].astype(o_ref.dtype)


MAX_SINGLE_K = 4096
MATMUL_VMEM_BUDGET = VMEM_LIMIT_BYTES * 9 // 10


def matmul_leading_rows(a, b, rows, out_dtype, name):
    bsz, _, k = a.shape
    n = b.shape[1]
    assert k <= MAX_SINGLE_K
    bm = _pick(rows, (1024, 512, 256, 128))
    bn = _pick(n, (1024, 768, 640, 512, 384, 256, 128))
    tiles = rows // bm
    return pl.pallas_call(
        _mm_kernel,
        grid=(bsz, tiles, n // bn),
        in_specs=[pl.BlockSpec((None, bm, k), lambda s, i, j: (s, i, 0)),
                  pl.BlockSpec((k, bn), lambda s, i, j: (0, j))],
        out_specs=pl.BlockSpec((bm, bn), lambda s, i, j: (s * tiles + i, j)),
        out_shape=jax.ShapeDtypeStruct((bsz * rows, n), out_dtype),
        compiler_params=_params("parallel", "parallel", "parallel"),
        name=name,
    )(a, b)


def matmul(a, b, out_dtype, name):
    m, k = a.shape
    n = b.shape[1]
    out_bytes = jnp.dtype(out_dtype).itemsize
    full_k = None
    for cap in (1024, 512):
        bm = _pick(m, tuple(c for c in (1024, 512, 256, 128) if c <= cap))
        bn = _pick(n, tuple(c for c in (1024, 768, 640, 512, 384, 256, 128) if c <= cap))
        need = 2 * 2 * (bm * k + k * bn) + 2 * bm * bn * out_bytes + 4 * bm * bn
        if need <= MATMUL_VMEM_BUDGET:
            full_k = (bm, bn)
            break
    if full_k is not None:
        bm, bn = full_k
        return pl.pallas_call(
            _mm_kernel,
            grid=(m // bm, n // bn),
            in_specs=[pl.BlockSpec((bm, k), lambda i, j: (i, 0)),
                      pl.BlockSpec((k, bn), lambda i, j: (0, j))],
            out_specs=pl.BlockSpec((bm, bn), lambda i, j: (i, j)),
            out_shape=jax.ShapeDtypeStruct((m, n), out_dtype),
            compiler_params=_params("parallel", "parallel"),
            name=name,
        )(a, b)
    bm = _pick(m, (1024, 512, 256, 128))
    bn = _pick(n, (1024, 768, 640, 512, 384, 256, 128))
    nk = -(-k // MAX_SINGLE_K)
    while k % nk or (k // nk) % LANE:
        nk += 1
    bk = k // nk
    return pl.pallas_call(
        _mm_acc_kernel,
        grid=(m // bm, n // bn, nk),
        in_specs=[pl.BlockSpec((bm, bk), lambda i, j, kk: (i, kk)),
                  pl.BlockSpec((bk, bn), lambda i, j, kk: (kk, j))],
        out_specs=pl.BlockSpec((bm, bn), lambda i, j, kk: (i, j)),
        out_shape=jax.ShapeDtypeStruct((m, n), out_dtype),
        scratch_shapes=[pltpu.VMEM((bm, bn), F32)],
        compiler_params=_params("parallel", "parallel", "arbitrary"),
        name=name,
    )(a, b)


def _rope_tables(n_lat, n_ctx):
    t = jnp.arange(n_lat, dtype=jnp.int32)
    row = (t // GRID_W).astype(F32)
    col = (t % GRID_W).astype(F32)
    quarter = ROPE_DIM // 4
    inv = ROPE_THETA ** (-jnp.arange(quarter, dtype=F32) / quarter)
    ar = row[:, None] * inv
    ac = col[:, None] * inv
    ang = jnp.concatenate([ar, ar, ac, ac], axis=-1)
    cos = jnp.concatenate([jnp.cos(ang), jnp.ones((n_ctx, ROPE_DIM), F32)], axis=0)
    sin = jnp.concatenate([jnp.sin(ang), jnp.zeros((n_ctx, ROPE_DIM), F32)], axis=0)
    first = (jnp.arange(ROPE_DIM) // quarter) % 2 == 0
    sin_next = jnp.where(first, -sin, 0.0)
    sin_prev = jnp.where(first, 0.0, sin)
    reps = LANE // ROPE_DIM
    return tuple(jnp.tile(a, (1, reps)) for a in (cos, sin_next, sin_prev))


def _rope_apply(x, cos, sin_next, sin_prev):
    quarter = ROPE_DIM // 4
    return (x * cos + pltpu.roll(x, LANE - quarter, 1) * sin_next
            + pltpu.roll(x, quarter, 1) * sin_prev)


LOG2E = math.log2(math.e)
KEY_CHUNK = 256


def _attend_tiles(k_ref, q_tiles, vt_ref, s_ref, e_ref):
    m_keys, n = s_ref.shape[1:]
    kc = min(KEY_CHUNK, m_keys)
    chunks = [slice(c * kc, (c + 1) * kc) for c in range(m_keys // kc)]

    def scores(t):
        s_ref[t % 2] = lax.dot_general(k_ref[...], q_tiles[t], NT_DIMS, preferred_element_type=F32)

    def column_max(t):
        m8 = None
        for sl in chunks:
            part = jnp.max(s_ref[t % 2, sl, :].reshape(kc // 8, 8, n), axis=0)
            m8 = part if m8 is None else jnp.maximum(m8, part)
        return jnp.max(m8, axis=0, keepdims=True)

    scores(0)
    m = column_max(0)
    outs = []
    for t in range(len(q_tiles)):
        if t + 1 < len(q_tiles):
            scores(t + 1)
        l8 = jnp.zeros((8, n), F32)
        for sl in chunks:
            e = jnp.exp2(s_ref[t % 2, sl, :] - m)
            l8 = l8 + jnp.sum(e.reshape(kc // 8, 8, n), axis=0)
            e_ref[t % 2, sl, :] = e.astype(e_ref.dtype)
        o_t = jnp.dot(vt_ref[...], e_ref[t % 2], preferred_element_type=F32)
        outs.append((o_t, jnp.sum(l8, axis=0, keepdims=True)))
        if t + 1 < len(q_tiles):
            m = column_max(t + 1)
    return outs


def _load_v_transposed(v_ref, vt_ref):
    @pl.when(pl.program_id(2) == 0)
    def _():
        vt_ref[...] = v_ref[...].astype(F32).T.astype(vt_ref.dtype)


def _diff_attn_kernel(q_ref, k_ref, v_ref, lam_ref, g_ref, qcos_ref, qsn_ref, qsp_ref, kcos_ref, ksn_ref, ksp_ref,
                      o_ref, vt_ref, s_ref, e_ref, kr_ref, *, lam_init):
    _load_v_transposed(v_ref, vt_ref)

    @pl.when(pl.program_id(2) == 0)
    def _():
        kr_ref[...] = _rope_apply(k_ref[...].astype(F32), kcos_ref[...], ksn_ref[...], ksp_ref[...]
                                  ).astype(kr_ref.dtype)

    tq = s_ref.shape[2] // 2
    q = _rope_apply(q_ref[...].astype(F32), qcos_ref[...], qsn_ref[...], qsp_ref[...])
    q = (q * (QK_HALF ** -0.5 * LOG2E)).astype(kr_ref.dtype)
    lane = lax.broadcasted_iota(jnp.int32, q.shape, 1)
    zero = jnp.zeros_like(q)
    q0 = jnp.where(lane < QK_HALF, q, zero)
    q1 = jnp.where(lane >= QK_HALF, q, zero)
    tiles = [slice(t * tq, (t + 1) * tq) for t in range(q.shape[0] // tq)]
    outs = _attend_tiles(kr_ref, [jnp.concatenate([q0[sl], q1[sl]], axis=0) for sl in tiles], vt_ref, s_ref, e_ref)
    lp = lam_ref[...]
    lam = (jnp.exp(jnp.sum(lp[0:1] * lp[1:2], axis=-1, keepdims=True))
           - jnp.exp(jnp.sum(lp[2:3] * lp[3:4], axis=-1, keepdims=True)) + lam_init)
    gain = g_ref[...] * (1.0 - lam_init)
    for sl, (o_t, l) in zip(tiles, outs):
        inv = 1.0 / l
        c_t = o_t[:, :tq] * inv[:, :tq] - o_t[:, tq:] * (lam * inv[:, tq:])
        c_t = c_t * lax.rsqrt(jnp.mean(c_t * c_t, axis=0, keepdims=True) + NORM_EPS)
        o_ref[sl, :] = (c_t.T * gain).astype(o_ref.dtype)


def _sdpa_kernel(*refs, n_qk, scale, rope_q):
    q_refs, k_refs = refs[:n_qk], refs[n_qk:2 * n_qk]
    n_tab = 0 if rope_q is None else 3
    v_ref = refs[2 * n_qk]
    tables = refs[2 * n_qk + 1:2 * n_qk + 1 + n_tab]
    o_ref, vt_ref, s_ref, e_ref = refs[2 * n_qk + 1 + n_tab:2 * n_qk + 5 + n_tab]
    _load_v_transposed(v_ref, vt_ref)
    if n_qk == 1:
        kcat_ref = k_refs[0]
    else:
        kcat_ref = refs[2 * n_qk + 5 + n_tab]

        @pl.when(pl.program_id(2) == 0)
        def _():
            for i, k_ref in enumerate(k_refs):
                kcat_ref[:, i * HEAD_DIM:(i + 1) * HEAD_DIM] = k_ref[...]

    tq = s_ref.shape[2]
    parts = [q_ref[...].astype(F32) for q_ref in q_refs]
    if rope_q is not None:
        parts[rope_q] = _rope_apply(parts[rope_q], *[t[...] for t in tables])
    q = (jnp.concatenate(parts, axis=1) * (scale * LOG2E)).astype(kcat_ref.dtype)
    tiles = [slice(t * tq, (t + 1) * tq) for t in range(q.shape[0] // tq)]
    outs = _attend_tiles(kcat_ref, [q[sl] for sl in tiles], vt_ref, s_ref, e_ref)
    for sl, (o_t, l) in zip(tiles, outs):
        o_ref[sl, :] = (o_t * (1.0 / l)).T.astype(o_ref.dtype)


def _attention_call(kernel, q_list, k_list, v, extras, *, bsz, heads, s, q_range, k_range, tq, lanes, name,
                    q_row_tables=(), k_row_tables=()):
    q_start, q_len = q_range
    k_start, k_len = k_range
    nq = q_len // tq
    q_off = q_start // tq
    k_off = k_start // k_len
    assert q_start % tq == 0 and q_len % tq == 0 and k_start % k_len == 0

    def qspec(col0):
        return pl.BlockSpec((None, tq, HEAD_DIM), lambda b, h, i: (b, q_off + i, col0 + h))

    def kspec(col0, stride, shared=False):
        if shared:
            return pl.BlockSpec((None, k_len, HEAD_DIM), lambda b, h, i: (b, k_off, col0))
        return pl.BlockSpec((None, k_len, HEAD_DIM), lambda b, h, i: (b, k_off, col0 + stride * h))

    def per_sample(arr):
        return arr.reshape(bsz, s, arr.shape[-1])

    in_specs, args = [], []
    for arr, col0 in q_list:
        in_specs.append(qspec(col0)); args.append(per_sample(arr))
    for arr, col0, stride, shared in k_list:
        in_specs.append(kspec(col0, stride, shared)); args.append(per_sample(arr))
    arr, col0, stride = v
    in_specs.append(kspec(col0, stride)); args.append(per_sample(arr))
    for e in extras:
        in_specs.append(pl.BlockSpec(e.shape, lambda b, h, i: (0, 0))); args.append(e)
    for t in q_row_tables:
        in_specs.append(pl.BlockSpec((tq, LANE), lambda b, h, i: (q_off + i, 0))); args.append(t)
    for t in k_row_tables:
        in_specs.append(pl.BlockSpec((k_len, LANE), lambda b, h, i: (k_off, 0))); args.append(t)
    return pl.pallas_call(
        kernel,
        grid=(bsz, heads, nq),
        in_specs=in_specs,
        out_specs=pl.BlockSpec((tq, HEAD_DIM), lambda b, h, i: (b * nq + i, h)),
        out_shape=jax.ShapeDtypeStruct((bsz * q_len, heads * HEAD_DIM), BF16),
        scratch_shapes=[pltpu.VMEM((HEAD_DIM, k_len), BF16), pltpu.VMEM((2, k_len, lanes), F32),
                        pltpu.VMEM((2, k_len, lanes), BF16)]
        + ([pltpu.VMEM((k_len, len(k_list) * HEAD_DIM), BF16)] if len(k_list) > 1 else [])
        + ([pltpu.VMEM((k_len, HEAD_DIM), BF16)] if k_row_tables else []),
        compiler_params=_params("parallel", "parallel", "arbitrary"),
        name=name,
    )(*args)


def _attend_lat_and_ctx(call, n_lat, n_ctx, with_ctx):
    s = n_lat + n_ctx
    o_lat = call((0, n_lat), (0, s), "lat")
    if not with_ctx:
        return o_lat
    bsz = o_lat.shape[0] // n_lat
    o_ctx = call((n_lat, n_ctx), (n_lat, n_ctx), "ctx").reshape(bsz, n_ctx, -1)
    return jnp.concatenate([o_lat.reshape(bsz, n_lat, -1), o_ctx], axis=1).reshape(bsz * s, -1)


NA_STEP_ROWS = 8
NA_WIN_ROWS = 16


def _na_window_start(step, grid_rows):
    return jnp.clip(step * NA_STEP_ROWS - NA_ROWS // 2, 0, grid_rows - NA_WIN_ROWS)


def _na_bias_table(rpb, grid_rows):
    steps = grid_rows // NA_STEP_ROWS
    col = jnp.arange(GRID_W, dtype=jnp.int32)
    cs = jnp.clip(col - NA_COLS // 2, 0, GRID_W - NA_COLS)
    colmask = (col[None, :] >= cs[:, None]) & (col[None, :] < cs[:, None] + NA_COLS)
    dc = jnp.clip(col[None, :] - col[:, None] + NA_COLS - 1, 0, 2 * NA_COLS - 2)
    step = jnp.array([0, min(1, steps - 1), steps - 1], dtype=jnp.int32)
    r = step[:, None] * NA_STEP_ROWS + jnp.arange(NA_STEP_ROWS, dtype=jnp.int32)[None, :]
    key_row = (_na_window_start(step, grid_rows)[:, None]
               + jnp.arange(NA_WIN_ROWS, dtype=jnp.int32)[None, :])
    rs = jnp.clip(r - NA_ROWS // 2, 0, grid_rows - NA_ROWS)
    in_rows = (key_row[:, None, :] >= rs[:, :, None]) & (key_row[:, None, :] < rs[:, :, None] + NA_ROWS)
    dr = jnp.clip(key_row[:, None, :] - r[:, :, None] + NA_ROWS - 1, 0, 2 * NA_ROWS - 2)
    bias = rpb[:, dr][..., dc] * LOG2E
    valid = in_rows[None, :, :, :, None, None] & colmask[None, None, None, None]
    bias = jnp.where(valid, bias, NEG_INF)
    bias = jnp.transpose(bias, (0, 1, 2, 4, 3, 5))
    return bias.reshape(rpb.shape[0], 3, NA_STEP_ROWS * GRID_W, NA_WIN_ROWS * GRID_W).astype(F32)


def _na_kernel(q_ref, k_ref, v_ref, bias_ref, o_ref, *, n_lat, grid_rows):
    start = pl.multiple_of(_na_window_start(pl.program_id(2), grid_rows) * GRID_W, GRID_W)
    win = NA_WIN_ROWS * GRID_W
    q = (q_ref[...].astype(F32) * (HEAD_DIM ** -0.5 * LOG2E)).astype(k_ref.dtype)
    sl = lax.dot_general(q, k_ref[pl.ds(start, win), :], NT_DIMS, preferred_element_type=F32) + bias_ref[...]
    sc = lax.dot_general(q, k_ref[n_lat:, :], NT_DIMS, preferred_element_type=F32)
    m = jnp.maximum(jnp.max(sl, axis=-1, keepdims=True), jnp.max(sc, axis=-1, keepdims=True))
    el = jnp.exp2(sl - m)
    ec = jnp.exp2(sc - m)
    inv = 1.0 / (jnp.sum(el, axis=-1, keepdims=True) + jnp.sum(ec, axis=-1, keepdims=True))
    o = (jnp.dot(el.astype(v_ref.dtype), v_ref[pl.ds(start, win), :], preferred_element_type=F32)
         + jnp.dot(ec.astype(v_ref.dtype), v_ref[n_lat:, :], preferred_element_type=F32))
    o_ref[...] = (o * inv).astype(o_ref.dtype)


def neighbourhood_attention(pd, bias, bsz, heads, n_lat, n_ctx):
    s = n_lat + n_ctx
    grid_rows = n_lat // GRID_W
    tq = NA_STEP_ROWS * GRID_W
    win = NA_WIN_ROWS * GRID_W
    steps = grid_rows // NA_STEP_ROWS
    assert grid_rows >= NA_WIN_ROWS and grid_rows % NA_STEP_ROWS == 0 and steps >= 2
    assert NA_WIN_ROWS >= NA_STEP_ROWS + NA_ROWS - 1

    def kind(g):
        return jnp.where(g == 0, 0, jnp.where(g == steps - 1, 2, 1))

    pd3 = pd.reshape(bsz, s, pd.shape[-1])
    return pl.pallas_call(
        functools.partial(_na_kernel, n_lat=n_lat, grid_rows=grid_rows),
        grid=(bsz, heads, steps),
        in_specs=[
            pl.BlockSpec((None, tq, HEAD_DIM), lambda b, h, g: (b, g, h)),
            pl.BlockSpec((None, s, HEAD_DIM), lambda b, h, g: (b, 0, heads + h)),
            pl.BlockSpec((None, s, HEAD_DIM), lambda b, h, g: (b, 0, 2 * heads + h)),
            pl.BlockSpec((None, None, tq, win), lambda b, h, g: (h, kind(g), 0, 0)),
        ],
        out_specs=pl.BlockSpec((tq, HEAD_DIM), lambda b, h, g: (b * steps + g, h)),
        out_shape=jax.ShapeDtypeStruct((bsz * n_lat, heads * HEAD_DIM), BF16),
        compiler_params=_params("parallel", "parallel", "arbitrary"),
        name="neighbourhood_attention",
    )(pd3, pd3, pd3, bias)


def _hgrn_constants(reverse):
    c = CHUNK
    t = np.arange(c)
    tri = (t[None, :] >= t[:, None]) if reverse else (t[None, :] <= t[:, None])
    mats = [tri.astype(np.float32)]
    masks = [np.eye(c, dtype=np.float32)]
    m = c // 2
    while m >= 1:
        node = (t // (2 * m)) * (2 * m)
        ref = node + (m if reverse else m - 1)
        if m < HGRN_BROADCAST_MIN:
            mats.append(tri[ref].astype(np.float32))
        upper = (t % (2 * m)) >= m
        same = node[:, None] == node[None, :]
        pair = same & (upper[:, None] & ~upper[None, :])
        masks.append((pair.T if reverse else pair).astype(np.float32))
        m //= 2
    return np.concatenate(mats, axis=0), np.stack(masks)


HGRN_BROADCAST_MIN = 8


def _hgrn_reference_rows(cum, cs, lev, reverse):
    c = CHUNK
    m = c >> lev
    if m < HGRN_BROADCAST_MIN:
        k = lev - int(math.log2(c // HGRN_BROADCAST_MIN))
        return cs[k * c:(k + 1) * c]
    rows = []
    for node in range(0, c, 2 * m):
        ref = node + (m if reverse else m - 1)
        rows.append(jnp.broadcast_to(cum[ref:ref + 1, :], (2 * m, cum.shape[1])))
    return rows[0] if len(rows) == 1 else jnp.concatenate(rows, axis=0)


def _hgrn_kernel(cq_ref, ci_ref, z_ref, lb_ref, ms_ref, mk_ref, o_ref, st_ref, *, heads, reverse):
    @pl.when(pl.program_id(1) == 0)
    def _():
        st_ref[...] = jnp.zeros_like(st_ref)

    c = CHUNK
    n_levels = mk_ref.shape[0]
    ms = ms_ref[...].astype(BF16)
    scale = HEAD_DIM ** -0.5
    tot_row = 0 if reverse else c - 1
    for h in range(heads):
        sl = slice(h * HEAD_DIM, (h + 1) * HEAD_DIM)
        lb = lb_ref[:, sl]
        f = lb + (1.0 - lb) * _sigmoid(z_ref[:, sl].astype(F32))
        kk = 1.0 - f
        g = jnp.log(f)
        cq = cq_ref[:, sl].astype(F32)
        qq = cq * _sigmoid(cq) * scale
        v = ci_ref[:, sl]
        g_hi = g.astype(BF16)
        r1 = g - g_hi.astype(F32)
        g_mid = r1.astype(BF16)
        g_lo = (r1 - g_mid.astype(F32)).astype(BF16)
        cs3 = jnp.dot(ms, jnp.concatenate([g_hi, g_mid, g_lo], axis=1), preferred_element_type=F32)
        cs = cs3[:, :HEAD_DIM] + cs3[:, HEAD_DIM:2 * HEAD_DIM] + cs3[:, 2 * HEAD_DIM:]
        cum = cs[:c]
        tot = cs[tot_row:tot_row + 1]
        st = st_ref[h]
        o = lax.dot_general((qq * jnp.exp(cum)).astype(BF16), st.astype(BF16), NT_DIMS,
                            preferred_element_type=F32)
        a = mk_ref[0] * lax.dot_general(qq.astype(BF16), kk.astype(BF16), NT_DIMS,
                                        preferred_element_type=F32)
        for lev in range(1, n_levels):
            e = jnp.exp(-jnp.abs(cum - _hgrn_reference_rows(cum, cs, lev, reverse)))
            a = a + mk_ref[lev] * lax.dot_general((qq * e).astype(BF16), (kk * e).astype(BF16), NT_DIMS,
                                                  preferred_element_type=F32)
        o = o + jnp.dot(a.astype(BF16), v, preferred_element_type=F32)
        o_ref[:, sl] = o
        ke = (kk * jnp.exp(tot - cum)).astype(BF16)
        st_ref[h] = st * jnp.exp(tot) + lax.dot_general(v, ke, TN_DIMS, preferred_element_type=F32)


def hgrn_scan(pc, lb, bsz, heads, n_lat, n_ctx, reverse):
    s = n_lat + n_ctx
    chunks = s // CHUNK
    lat_chunks = n_lat // CHUNK
    width = heads * HEAD_DIM
    ms, mk = _hgrn_constants(reverse)

    def chunk_of(step):
        return (chunks - 1 - step) if reverse else lax.rem(step + lat_chunks, chunks)

    def spec(col):
        return pl.BlockSpec((CHUNK, width), lambda b, t: (b * chunks + chunk_of(t), col))

    return pl.pallas_call(
        functools.partial(_hgrn_kernel, heads=heads, reverse=reverse),
        grid=(bsz, chunks),
        in_specs=[spec(0), spec(1), spec(3 if reverse else 2),
                  pl.BlockSpec((1, width), lambda b, t: (0, 0)),
                  pl.BlockSpec(ms.shape, lambda b, t: (0, 0)),
                  pl.BlockSpec(mk.shape, lambda b, t: (0, 0, 0))],
        out_specs=spec(0),
        out_shape=jax.ShapeDtypeStruct((bsz * s, width), F32),
        scratch_shapes=[pltpu.VMEM((heads, HEAD_DIM, HEAD_DIM), F32)],
        compiler_params=_params("parallel", "arbitrary"),
        name="hgrn_scan_bwd" if reverse else "hgrn_scan_fwd",
    )(pc, pc, pc, lb.reshape(1, width), jnp.asarray(ms), jnp.asarray(mk))


def _hgrn_readout_kernel(of_ref, ob_ref, gate_ref, g_ref, o_ref):
    gain = g_ref[...]
    for h in range(of_ref.shape[1] // HEAD_DIM):
        sl = slice(h * HEAD_DIM, (h + 1) * HEAD_DIM)
        gate = gate_ref[:, sl].astype(F32)
        o = _rms(of_ref[:, sl] + ob_ref[:, sl], gain)
        o_ref[:, sl] = (o * (gate * _sigmoid(gate))).astype(o_ref.dtype)


def hgrn_readout(o_fwd, o_bwd, pc, gain, bsz, s, rows, tr=256):
    width = o_fwd.shape[1]
    in_tiles, out_tiles = s // tr, rows // tr
    ispec = pl.BlockSpec((tr, width), lambda b, i: (b * in_tiles + i, 0))
    return pl.pallas_call(
        _hgrn_readout_kernel,
        grid=(bsz, out_tiles),
        in_specs=[ispec, ispec, pl.BlockSpec((tr, width), lambda b, i: (b * in_tiles + i, 4)),
                  pl.BlockSpec((1, HEAD_DIM), lambda b, i: (0, 0))],
        out_specs=pl.BlockSpec((tr, width), lambda b, i: (b * out_tiles + i, 0)),
        out_shape=jax.ShapeDtypeStruct((bsz * rows, width), BF16),
        compiler_params=_params("parallel", "parallel"),
        name="hgrn_readout",
    )(o_fwd, o_bwd, pc, gain.reshape(1, HEAD_DIM))


def _mla_prep_kernel(p_ref, qg_ref, kg_ref, cos_ref, sn_ref, sp_ref, qn_ref, cn_ref, kr_ref, *, q_rank, kv_rank):
    qn_ref[...] = _rms(p_ref[:, :q_rank].astype(F32), qg_ref[...]).astype(qn_ref.dtype)
    cn_ref[...] = _rms(p_ref[:, q_rank:q_rank + kv_rank].astype(F32), kg_ref[...]).astype(cn_ref.dtype)
    kr = p_ref[:, q_rank + kv_rank:q_rank + kv_rank + LANE].astype(F32)
    kr_ref[...] = _rope_apply(kr, cos_ref[...], sn_ref[...], sp_ref[...]).astype(kr_ref.dtype)


def mla_prep(pb, q_gain, kv_gain, tables, s, tr=256):
    r, width = pb.shape
    q_rank, kv_rank = q_gain.shape[0], kv_gain.shape[0]
    assert width >= q_rank + kv_rank + LANE and q_rank % LANE == 0 and kv_rank % LANE == 0
    tiles_per_seq = s // tr
    tspec = pl.BlockSpec((tr, LANE), lambda i: (i % tiles_per_seq, 0))

    def ospec(w):
        return pl.BlockSpec((tr, w), lambda i: (i, 0))

    return pl.pallas_call(
        functools.partial(_mla_prep_kernel, q_rank=q_rank, kv_rank=kv_rank),
        grid=(r // tr,),
        in_specs=[ospec(width), pl.BlockSpec((1, q_rank), lambda i: (0, 0)),
                  pl.BlockSpec((1, kv_rank), lambda i: (0, 0)), tspec, tspec, tspec],
        out_specs=[ospec(q_rank), ospec(kv_rank), ospec(LANE)],
        out_shape=[jax.ShapeDtypeStruct((r, q_rank), BF16), jax.ShapeDtypeStruct((r, kv_rank), BF16),
                   jax.ShapeDtypeStruct((r, LANE), BF16)],
        compiler_params=_params("parallel"),
        name="mla_prep",
    )(pb, q_gain.reshape(1, q_rank), kv_gain.reshape(1, kv_rank), *tables)


def _merge_kernel(oa_ref, ob_ref, oc_ref, od_ref, ga_ref, gb_ref, gc_ref, gd_ref, w_ref, o_ref):
    acc = None
    for i, (o_ref_i, g_ref_i) in enumerate(((oa_ref, ga_ref), (ob_ref, gb_ref), (oc_ref, gc_ref), (od_ref, gd_ref))):
        y = jnp.dot(o_ref_i[...], w_ref[i], preferred_element_type=F32)
        term = _sigmoid(g_ref_i[...].astype(F32)) * y
        acc = term if acc is None else acc + term
    o_ref[...] = acc.astype(o_ref.dtype)


def merge_branches(outs, gate_logits, w_branch):
    r, c = outs[0].shape
    d = w_branch.shape[2]
    bm = _pick(r, (512, 256, 128))
    bn = _pick(d, (1024, 512, 256, 128))
    nb = d // bn
    ospec = pl.BlockSpec((bm, c), lambda i, j: (i, 0))

    def gspec(branch):
        return pl.BlockSpec((bm, bn), lambda i, j: (i, branch * nb + j))

    return pl.pallas_call(
        _merge_kernel,
        grid=(r // bm, nb),
        in_specs=[ospec] * N_BRANCH + [gspec(i) for i in range(N_BRANCH)]
        + [pl.BlockSpec((N_BRANCH, c, bn), lambda i, j: (0, 0, j))],
        out_specs=pl.BlockSpec((bm, bn), lambda i, j: (i, j)),
        out_shape=jax.ShapeDtypeStruct((r, d), BF16),
        compiler_params=_params("parallel", "parallel"),
        name="merge_branches",
    )(*outs, gate_logits, gate_logits, gate_logits, gate_logits, w_branch)


HALO = 16


FFN_BN = 640


def _ffn_tile_pairs(a, bn):
    f = a.shape[-1] // 2
    lead = a.shape[:-1]
    pairs = jnp.stack([a[..., :f].reshape(*lead, f // bn, bn), a[..., f:].reshape(*lead, f // bn, bn)], axis=-2)
    return pairs.reshape(*lead, 2 * f)


def _cast_pair_kernel(g_ref, v_ref, o_ref):
    bn = g_ref.shape[1]
    o_ref[:, :bn] = g_ref[...].astype(o_ref.dtype)
    o_ref[:, bn:] = v_ref[...].astype(o_ref.dtype)


def ffn_up_weight(w_up, layer, bn):
    _, d, two_f = w_up.shape
    nb = two_f // 2 // bn
    tk = _pick(d, (1024, 512, 256, 128))
    return pl.pallas_call(
        _cast_pair_kernel,
        grid=(d // tk, nb),
        in_specs=[pl.BlockSpec((None, tk, bn), lambda i, j: (layer, i, j)),
                  pl.BlockSpec((None, tk, bn), lambda i, j: (layer, i, nb + j))],
        out_specs=pl.BlockSpec((tk, 2 * bn), lambda i, j: (i, j)),
        out_shape=jax.ShapeDtypeStruct((d, two_f), BF16),
        compiler_params=_params("parallel", "parallel"),
        name="ffn_up_weight",
    )(w_up, w_up)


def _ffn_up_kernel(h_ref, hp_ref, hn_ref, w_ref, cw_ref, cb_ref, o_ref, hbuf_ref, *, bm, s, n_lat):
    i = pl.program_id(0)
    bn = o_ref.shape[1]

    @pl.when(pl.program_id(1) == 0)
    def _():
        hbuf_ref[0:HALO, :] = hp_ref[...]
        hbuf_ref[HALO:HALO + bm, :] = h_ref[...]
        hbuf_ref[HALO + bm:, :] = hn_ref[...]

    pos = lax.rem(i * bm + lax.broadcasted_iota(jnp.int32, (bm, 1), 0), s)
    has_prev = jnp.where((pos == 0) | (pos == n_lat), 0.0, 1.0)
    has_next = jnp.where((pos == n_lat - 1) | (pos == s - 1), 0.0, 1.0)
    rows = bm + 2 * HALO
    u = jnp.dot(hbuf_ref[...], w_ref[...], preferred_element_type=F32)
    prev = pltpu.roll(u, 1, 0)[HALO:HALO + bm]
    nxt = pltpu.roll(u, rows - 1, 0)[HALO:HALO + bm]
    cw = cw_ref[...]
    y = prev * has_prev * cw[0:1] + u[HALO:HALO + bm] * cw[1:2] + nxt * has_next * cw[2:3] + cb_ref[...]
    gate, val = y[:, :bn], y[:, bn:]
    o_ref[...] = (gate * _sigmoid(gate) * val).astype(o_ref.dtype)


def ffn_up(h, w_up, conv_w, conv_b, s, n_lat):
    r, d = h.shape
    f = w_up.shape[1] // 2
    bm = _pick(r, (512, 256, 128))
    bn = FFN_BN
    assert f % bn == 0
    halo_blocks = r // HALO
    per_tile = bm // HALO
    return pl.pallas_call(
        functools.partial(_ffn_up_kernel, bm=bm, s=s, n_lat=n_lat),
        grid=(r // bm, f // bn),
        in_specs=[
            pl.BlockSpec((bm, d), lambda i, j: (i, 0)),
            pl.BlockSpec((HALO, d), lambda i, j: (jnp.maximum(i * per_tile - 1, 0), 0)),
            pl.BlockSpec((HALO, d), lambda i, j: (jnp.minimum((i + 1) * per_tile, halo_blocks - 1), 0)),
            pl.BlockSpec((d, 2 * bn), lambda i, j: (0, j)),
            pl.BlockSpec((3, 2 * bn), lambda i, j: (0, j)),
            pl.BlockSpec((1, 2 * bn), lambda i, j: (0, j)),
        ],
        out_specs=pl.BlockSpec((bm, bn), lambda i, j: (i, j)),
        out_shape=jax.ShapeDtypeStruct((r, f), BF16),
        scratch_shapes=[pltpu.VMEM((bm + 2 * HALO, d), BF16)],
        compiler_params=_params("parallel", "arbitrary"),
        name="ffn_up",
    )(h, h, h, w_up, conv_w, conv_b)


def _split_w_in(w, d, heads):
    mix = heads * HEAD_DIM
    q_rank = 3 * d // 16
    kv_rank = d // 8
    sizes = [('a', 3 * mix), ('b_qa', q_rank), ('b_kva', kv_rank + MLA_ROPE), ('c', 5 * mix), ('d', 3 * mix),
             ('gate', N_BRANCH * d)]
    out, start = {}, 0
    for name, size in sizes:
        out[name] = (start, size)
        start += size
    assert start == w.shape[1]

    def cols(name):
        a, n = out[name]
        return w[:, a:a + n].astype(BF16)

    used = q_rank + kv_rank + LANE
    pad = -used % 512 + LANE - MLA_ROPE
    w_b = jnp.concatenate([cols('b_qa'), cols('b_kva'), jnp.zeros((d, pad), BF16)], axis=1)
    return cols('a'), w_b, cols('c'), cols('d'), cols('gate'), q_rank, kv_rank


def _mla_q_weight(w_qb, heads):
    rq = w_qb.shape[0]
    w = w_qb.reshape(rq, heads, MLA_NOPE + MLA_ROPE)
    nope = w[:, :, :MLA_NOPE].reshape(rq, heads * MLA_NOPE)
    ropep = jnp.pad(w[:, :, MLA_NOPE:], ((0, 0), (0, 0), (0, LANE - MLA_ROPE))).reshape(rq, heads * LANE)
    return jnp.concatenate([nope, ropep], axis=1).astype(BF16)


def kernel(x, c, ctx, c_ctx, ada_w, ada_b, norm_mix_pre, norm_mix_post, norm_ffn_pre, norm_ffn_post, w_in,
           diff_lambda, diff_subln, mla_q_norm, mla_w_qb, mla_kv_norm, mla_w_kvb, hgrn_lb_logits, hgrn_norm,
           na_rpb, w_branch, w_out, ffn_w_up, ffn_conv_w, ffn_conv_b, ffn_w_down):
    bsz, n_lat, d = x.shape
    n_ctx = ctx.shape[1]
    s = n_lat + n_ctx
    r = bsz * s
    depth = ada_w.shape[0]
    heads = d // (N_BRANCH * HEAD_DIM)
    mix = heads * HEAD_DIM
    assert n_lat % 256 == 0 and n_ctx % 256 == 0 and s % n_ctx == 0 and n_lat % GRID_W == 0

    tables = _rope_tables(n_lat, n_ctx)
    lb_sm = jax.nn.softmax(hgrn_lb_logits.astype(F32), axis=0)
    lower_bounds = jnp.cumsum(lb_sm, axis=0) - lb_sm[0]

    c_rows = jnp.zeros((8, d), F32).at[:bsz].set(c).at[bsz].set(c_ctx)
    xa = jnp.concatenate([x, ctx], axis=1)

    def mod_rows(mod, idx):
        part = mod[:, idx * d:(idx + 1) * d]
        both = jnp.stack([part[:bsz], jnp.broadcast_to(part[bsz], (bsz, d))], axis=1)
        return both.reshape(2 * bsz, 1, d)

    mods = []
    for l in range(depth):
        mod = ada_modulation(c_rows, ada_w, ada_b, l)
        mods.append([mod_rows(mod, i) for i in range(6)])

    h = norm_modulate(xa, norm_mix_pre[0], mods[0][0], mods[0][1], n_lat)
    for l in range(depth):
        with_ctx = l < depth - 1
        sh1, sc1, g1, sh2, sc2, g2 = mods[l]
        h2d = h.reshape(r, d)
        rows = s if with_ctx else n_lat

        w_a, w_b, w_c, w_d, w_g, q_rank, kv_rank = _split_w_in(w_in[l], d, heads)
        pa = matmul(h2d, w_a, BF16, "in_proj_a")
        pb = matmul(h2d, w_b, BF16, "in_proj_b")
        pc = matmul(h2d, w_c, BF16, "in_proj_c")
        pd = matmul(h2d, w_d, BF16, "in_proj_d")
        if with_ctx:
            pg = matmul(h2d, w_g, BF16, "in_proj_gate")
        else:
            pg = matmul_leading_rows(h, w_g, rows, BF16, "in_proj_gate")

        lam_init = 0.8 - 0.6 * math.exp(-0.3 * l)
        extras_a = [diff_lambda[l].astype(F32), diff_subln[l].reshape(1, HEAD_DIM).astype(F32)]

        def call_a(q_range, k_range, tag):
            return _attention_call(
                functools.partial(_diff_attn_kernel, lam_init=lam_init),
                [(pa, 0)], [(pa, heads, 1, False)], (pa, 2 * heads, 1), extras_a,
                bsz=bsz, heads=heads, s=s, q_range=q_range, k_range=k_range, tq=min(1024, q_range[1]), lanes=256,
                name="diff_attn_" + tag, q_row_tables=tables, k_row_tables=tables)

        oa = _attend_lat_and_ctx(call_a, n_lat, n_ctx, with_ctx)

        qn, cn, kr = mla_prep(pb, mla_q_norm[l], mla_kv_norm[l], tables, s)
        qb = matmul(qn, _mla_q_weight(mla_w_qb[l], heads), BF16, "mla_q_up")
        kvb = matmul(cn, mla_w_kvb[l].astype(BF16), BF16, "mla_kv_up")

        def call_b(q_range, k_range, tag):
            return _attention_call(
                functools.partial(_sdpa_kernel, n_qk=2, scale=(MLA_NOPE + MLA_ROPE) ** -0.5, rope_q=1),
                [(qb, 0), (qb, heads)], [(kvb, 0, 2, False), (kr, 0, 0, True)], (kvb, 1, 2), [],
                bsz=bsz, heads=heads, s=s, q_range=q_range, k_range=k_range, tq=min(2048, q_range[1]), lanes=256,
                name="mla_attn_" + tag, q_row_tables=tables)

        ob = _attend_lat_and_ctx(call_b, n_lat, n_ctx, with_ctx)

        o_fwd = hgrn_scan(pc, lower_bounds[l, 0], bsz, heads, n_lat, n_ctx, reverse=False)
        o_bwd = hgrn_scan(pc, lower_bounds[l, 1], bsz, heads, n_lat, n_ctx, reverse=True)
        oc = hgrn_readout(o_fwd, o_bwd, pc, hgrn_norm[l], bsz, s, rows)

        od = neighbourhood_attention(pd, _na_bias_table(na_rpb[l], n_lat // GRID_W), bsz, heads, n_lat, n_ctx)
        if with_ctx:
            od_ctx = _attention_call(
                functools.partial(_sdpa_kernel, n_qk=1, scale=HEAD_DIM ** -0.5, rope_q=None),
                [(pd, 0)], [(pd, heads, 1, False)], (pd, 2 * heads, 1), [],
                bsz=bsz, heads=heads, s=s, q_range=(n_lat, n_ctx), k_range=(n_lat, n_ctx), tq=256, lanes=256,
                name="na_ctx_attn").reshape(bsz, n_ctx, mix)
            od = jnp.concatenate([od.reshape(bsz, n_lat, mix), od_ctx], axis=1).reshape(r, mix)

        merged = merge_branches((oa, ob, oc, od), pg, w_branch[l].astype(BF16))
        mix_out = matmul(merged, w_out[l].astype(BF16), BF16, "out_proj").reshape(bsz, rows, d)
        xa, h2 = residual_norm_modulate(xa, mix_out, g1, norm_mix_post[l], norm_ffn_pre[l], sh2, sc2, n_lat)

        gv = ffn_up(h2.reshape(bsz * rows, d), ffn_up_weight(ffn_w_up, l, FFN_BN),
                    _ffn_tile_pairs(ffn_conv_w[l], FFN_BN), _ffn_tile_pairs(ffn_conv_b[l][None], FFN_BN),
                    rows, n_lat)
        ffn = matmul(gv, ffn_w_down[l].astype(BF16), BF16, "ffn_down").reshape(bsz, rows, d)
        if with_ctx:
            xa, h = residual_norm_modulate(xa, ffn, g2, norm_ffn_post[l], norm_mix_pre[l + 1],
                                           mods[l + 1][0], mods[l + 1][1], n_lat)
        else:
            xa = residual(xa, ffn, g2, norm_ffn_post[l], n_lat)
    return xa
```

```python
import functools
import math

import numpy as np
import jax
import jax.numpy as jnp
from jax import lax
from jax.experimental import pallas as pl
from jax.experimental.pallas import tpu as pltpu

F32 = jnp.float32
BF16 = jnp.bfloat16

HEAD_DIM = 128
QK_HALF = 64
MLA_NOPE = 128
MLA_ROPE = 64
MLA_V = 128
ROPE_DIM = 64
ROPE_THETA = 10000.0
GRID_W = 64
NA_ROWS = 8
NA_COLS = 16
CHUNK = 128
NORM_EPS = 1e-6
NEG_INF = -1e30
N_BRANCH = 4

LANE = 128
VMEM_LIMIT_BYTES = 56 * 1024 * 1024

NT_DIMS = (((1,), (1,)), ((), ()))
TN_DIMS = (((0,), (0,)), ((), ()))


def _params(*sem):
    return pltpu.CompilerParams(dimension_semantics=sem, vmem_limit_bytes=VMEM_LIMIT_BYTES)


def _sigmoid(x):
    return 0.5 * jnp.tanh(0.5 * x) + 0.5


def _pick(total, candidates):
    for c in candidates:
        if total % c == 0:
            return c
    raise ValueError(f"no tile for {total} in {candidates}")


def _ada_kernel(c_ref, w_ref, b_ref, o_ref):
    c = c_ref[...]
    a = (c * _sigmoid(c)).astype(BF16)
    o_ref[...] = jnp.dot(a, w_ref[...].astype(BF16), preferred_element_type=F32) + b_ref[...]


def ada_modulation(c_rows, ada_w, ada_b, layer):
    rows, d = c_rows.shape
    width = ada_w.shape[2]
    tn = _pick(width, (512, 256, 128))
    b3 = ada_b.reshape(ada_b.shape[0], 1, width)
    return pl.pallas_call(
        _ada_kernel,
        grid=(width // tn,),
        in_specs=[
            pl.BlockSpec((rows, d), lambda j: (0, 0)),
            pl.BlockSpec((None, d, tn), lambda j: (layer, 0, j)),
            pl.BlockSpec((None, 1, tn), lambda j: (layer, 0, j)),
        ],
        out_specs=pl.BlockSpec((rows, tn), lambda j: (0, j)),
        out_shape=jax.ShapeDtypeStruct((rows, width), F32),
        compiler_params=_params("parallel"),
        name="ada_modulation",
    )(c_rows, ada_w, b3)


def _rms(x, gain):
    return x * lax.rsqrt(jnp.mean(x * x, axis=-1, keepdims=True) + NORM_EPS) * gain


def _norm_mod_kernel(x_ref, g_ref, sh_ref, sc_ref, h_ref):
    h = _rms(x_ref[0], g_ref[...])
    h_ref[0] = (h * (1.0 + sc_ref[0]) + sh_ref[0]).astype(h_ref.dtype)


def _resid_kernel(x_ref, y_ref, gate_ref, gpost_ref, xo_ref):
    xo_ref[0] = x_ref[0] + gate_ref[0] * _rms(y_ref[0].astype(F32), gpost_ref[...])


def _resid_norm_mod_kernel(x_ref, y_ref, gate_ref, gpost_ref, gpre_ref, sh_ref, sc_ref, xo_ref, h_ref):
    xn = x_ref[0] + gate_ref[0] * _rms(y_ref[0].astype(F32), gpost_ref[...])
    xo_ref[0] = xn
    h = _rms(xn, gpre_ref[...])
    h_ref[0] = (h * (1.0 + sc_ref[0]) + sh_ref[0]).astype(h_ref.dtype)


def _row_specs(dims, tr, n_lat_tiles):
    s, d = dims
    xspec = pl.BlockSpec((1, tr, d), lambda b, t: (b, t, 0))
    gspec = pl.BlockSpec((1, d), lambda b, t: (0, 0))
    mspec = pl.BlockSpec((1, 1, d), lambda b, t: (2 * b + t // n_lat_tiles, 0, 0))
    return xspec, gspec, mspec


def norm_modulate(x, gain, shift, scale, n_lat, tr=256):
    bsz, s, d = x.shape
    xspec, gspec, mspec = _row_specs((s, d), tr, n_lat // tr)
    return pl.pallas_call(
        _norm_mod_kernel,
        grid=(bsz, s // tr),
        in_specs=[xspec, gspec, mspec, mspec],
        out_specs=xspec,
        out_shape=jax.ShapeDtypeStruct(x.shape, BF16),
        compiler_params=_params("parallel", "parallel"),
        name="norm_modulate",
    )(x, gain.reshape(1, d), shift, scale)


def residual(x, y, gate, gain_post, n_lat, tr=256):
    bsz, s, d = x.shape
    rows = y.shape[1]
    xspec, gspec, mspec = _row_specs((s, d), tr, n_lat // tr)
    return pl.pallas_call(
        _resid_kernel,
        grid=(bsz, rows // tr),
        in_specs=[xspec, xspec, mspec, gspec],
        out_specs=xspec,
        out_shape=jax.ShapeDtypeStruct(y.shape, F32),
        compiler_params=_params("parallel", "parallel"),
        name="residual",
    )(x, y, gate, gain_post.reshape(1, d))


def residual_norm_modulate(x, y, gate, gain_post, gain_pre, shift, scale, n_lat, tr=256):
    bsz, s, d = x.shape
    rows = y.shape[1]
    xspec, gspec, mspec = _row_specs((s, d), tr, n_lat // tr)
    return pl.pallas_call(
        _resid_norm_mod_kernel,
        grid=(bsz, rows // tr),
        in_specs=[xspec, xspec, mspec, gspec, gspec, mspec, mspec],
        out_specs=[xspec, xspec],
        out_shape=[jax.ShapeDtypeStruct(y.shape, F32), jax.ShapeDtypeStruct(y.shape, BF16)],
        compiler_params=_params("parallel", "parallel"),
        name="residual_norm_modulate",
    )(x, y, gate, gain_post.reshape(1, d), gain_pre.reshape(1, d), shift, scale)


def _mm_kernel(a_ref, b_ref, o_ref):
    o_ref[...] = jnp.dot(a_ref[...], b_ref[...], preferred_element_type=F32).astype(o_ref.dtype)


def _mm_acc_kernel(a_ref, b_ref, o_ref, acc_ref):
    k = pl.program_id(2)

    @pl.when(k == 0)
    def _():
        acc_ref[...] = jnp.zeros_like(acc_ref)

    acc_ref[...] += jnp.dot(a_ref[...], b_ref[...], preferred_element_type=F32)

    @pl.when(k == pl.num_programs(2) - 1)
    def _():
        o_ref[...] = acc_ref[...].astype(o_ref.dtype)


MAX_SINGLE_K = 4096
MATMUL_VMEM_BUDGET = VMEM_LIMIT_BYTES * 9 // 10


def matmul_leading_rows(a, b, rows, out_dtype, name):
    bsz, _, k = a.shape
    n = b.shape[1]
    assert k <= MAX_SINGLE_K
    bm = _pick(rows, (1024, 512, 256, 128))
    bn = _pick(n, (1024, 768, 640, 512, 384, 256, 128))
    tiles = rows // bm
    return pl.pallas_call(
        _mm_kernel,
        grid=(bsz, tiles, n // bn),
        in_specs=[pl.BlockSpec((None, bm, k), lambda s, i, j: (s, i, 0)),
                  pl.BlockSpec((k, bn), lambda s, i, j: (0, j))],
        out_specs=pl.BlockSpec((bm, bn), lambda s, i, j: (s * tiles + i, j)),
        out_shape=jax.ShapeDtypeStruct((bsz * rows, n), out_dtype),
        compiler_params=_params("parallel", "parallel", "parallel"),
        name=name,
    )(a, b)


def matmul(a, b, out_dtype, name):
    m, k = a.shape
    n = b.shape[1]
    out_bytes = jnp.dtype(out_dtype).itemsize
    full_k = None
    for cap in (1024, 512):
        bm = _pick(m, tuple(c for c in (1024, 512, 256, 128) if c <= cap))
        bn = _pick(n, tuple(c for c in (1024, 768, 640, 512, 384, 256, 128) if c <= cap))
        need = 2 * 2 * (bm * k + k * bn) + 2 * bm * bn * out_bytes + 4 * bm * bn
        if need <= MATMUL_VMEM_BUDGET:
            full_k = (bm, bn)
            break
    if full_k is not None:
        bm, bn = full_k
        return pl.pallas_call(
            _mm_kernel,
            grid=(m // bm, n // bn),
            in_specs=[pl.BlockSpec((bm, k), lambda i, j: (i, 0)),
                      pl.BlockSpec((k, bn), lambda i, j: (0, j))],
            out_specs=pl.BlockSpec((bm, bn), lambda i, j: (i, j)),
            out_shape=jax.ShapeDtypeStruct((m, n), out_dtype),
            compiler_params=_params("parallel", "parallel"),
            name=name,
        )(a, b)
    bm = _pick(m, (1024, 512, 256, 128))
    bn = _pick(n, (1024, 768, 640, 512, 384, 256, 128))
    nk = -(-k // MAX_SINGLE_K)
    while k % nk or (k // nk) % LANE:
        nk += 1
    bk = k // nk
    return pl.pallas_call(
        _mm_acc_kernel,
        grid=(m // bm, n // bn, nk),
        in_specs=[pl.BlockSpec((bm, bk), lambda i, j, kk: (i, kk)),
                  pl.BlockSpec((bk, bn), lambda i, j, kk: (kk, j))],
        out_specs=pl.BlockSpec((bm, bn), lambda i, j, kk: (i, j)),
        out_shape=jax.ShapeDtypeStruct((m, n), out_dtype),
        scratch_shapes=[pltpu.VMEM((bm, bn), F32)],
        compiler_params=_params("parallel", "parallel", "arbitrary"),
        name=name,
    )(a, b)


def _rope_tables(n_lat, n_ctx):
    t = jnp.arange(n_lat, dtype=jnp.int32)
    row = (t // GRID_W).astype(F32)
    col = (t % GRID_W).astype(F32)
    quarter = ROPE_DIM // 4
    inv = ROPE_THETA ** (-jnp.arange(quarter, dtype=F32) / quarter)
    ar = row[:, None] * inv
    ac = col[:, None] * inv
    ang = jnp.concatenate([ar, ar, ac, ac], axis=-1)
    cos = jnp.concatenate([jnp.cos(ang), jnp.ones((n_ctx, ROPE_DIM), F32)], axis=0)
    sin = jnp.concatenate([jnp.sin(ang), jnp.zeros((n_ctx, ROPE_DIM), F32)], axis=0)
    first = (jnp.arange(ROPE_DIM) // quarter) % 2 == 0
    sin_next = jnp.where(first, -sin, 0.0)
    sin_prev = jnp.where(first, 0.0, sin)
    reps = LANE // ROPE_DIM
    return tuple(jnp.tile(a, (1, reps)) for a in (cos, sin_next, sin_prev))


def _rope_apply(x, cos, sin_next, sin_prev):
    quarter = ROPE_DIM // 4
    return (x * cos + pltpu.roll(x, LANE - quarter, 1) * sin_next
            + pltpu.roll(x, quarter, 1) * sin_prev)


LOG2E = math.log2(math.e)
KEY_CHUNK = 256


def _attend_tiles(k_ref, q_tiles, vt_ref, s_ref, e_ref):
    m_keys, n = s_ref.shape[1:]
    kc = min(KEY_CHUNK, m_keys)
    chunks = [slice(c * kc, (c + 1) * kc) for c in range(m_keys // kc)]

    def scores(t):
        s_ref[t % 2] = lax.dot_general(k_ref[...], q_tiles[t], NT_DIMS, preferred_element_type=F32)

    def column_max(t):
        m8 = None
        for sl in chunks:
            part = jnp.max(s_ref[t % 2, sl, :].reshape(kc // 8, 8, n), axis=0)
            m8 = part if m8 is None else jnp.maximum(m8, part)
        return jnp.max(m8, axis=0, keepdims=True)

    scores(0)
    m = column_max(0)
    outs = []
    for t in range(len(q_tiles)):
        if t + 1 < len(q_tiles):
            scores(t + 1)
        l8 = jnp.zeros((8, n), F32)
        for sl in chunks:
            e = jnp.exp2(s_ref[t % 2, sl, :] - m)
            l8 = l8 + jnp.sum(e.reshape(kc // 8, 8, n), axis=0)
            e_ref[t % 2, sl, :] = e.astype(e_ref.dtype)
        o_t = jnp.dot(vt_ref[...], e_ref[t % 2], preferred_element_type=F32)
        outs.append((o_t, jnp.sum(l8, axis=0, keepdims=True)))
        if t + 1 < len(q_tiles):
            m = column_max(t + 1)
    return outs


def _load_v_transposed(v_ref, vt_ref):
    @pl.when(pl.program_id(2) == 0)
    def _():
        vt_ref[...] = v_ref[...].astype(F32).T.astype(vt_ref.dtype)


def _diff_attn_kernel(q_ref, k_ref, v_ref, lam_ref, g_ref, qcos_ref, qsn_ref, qsp_ref, kcos_ref, ksn_ref, ksp_ref,
                      o_ref, vt_ref, s_ref, e_ref, kr_ref, *, lam_init):
    _load_v_transposed(v_ref, vt_ref)

    @pl.when(pl.program_id(2) == 0)
    def _():
        kr_ref[...] = _rope_apply(k_ref[...].astype(F32), kcos_ref[...], ksn_ref[...], ksp_ref[...]
                                  ).astype(kr_ref.dtype)

    tq = s_ref.shape[2] // 2
    q = _rope_apply(q_ref[...].astype(F32), qcos_ref[...], qsn_ref[...], qsp_ref[...])
    q = (q * (QK_HALF ** -0.5 * LOG2E)).astype(kr_ref.dtype)
    lane = lax.broadcasted_iota(jnp.int32, q.shape, 1)
    zero = jnp.zeros_like(q)
    q0 = jnp.where(lane < QK_HALF, q, zero)
    q1 = jnp.where(lane >= QK_HALF, q, zero)
    tiles = [slice(t * tq, (t + 1) * tq) for t in range(q.shape[0] // tq)]
    outs = _attend_tiles(kr_ref, [jnp.concatenate([q0[sl], q1[sl]], axis=0) for sl in tiles], vt_ref, s_ref, e_ref)
    lp = lam_ref[...]
    lam = (jnp.exp(jnp.sum(lp[0:1] * lp[1:2], axis=-1, keepdims=True))
           - jnp.exp(jnp.sum(lp[2:3] * lp[3:4], axis=-1, keepdims=True)) + lam_init)
    gain = g_ref[...] * (1.0 - lam_init)
    for sl, (o_t, l) in zip(tiles, outs):
        inv = 1.0 / l
        c_t = o_t[:, :tq] * inv[:, :tq] - o_t[:, tq:] * (lam * inv[:, tq:])
        c_t = c_t * lax.rsqrt(jnp.mean(c_t * c_t, axis=0, keepdims=True) + NORM_EPS)
        o_ref[sl, :] = (c_t.T * gain).astype(o_ref.dtype)


def _sdpa_kernel(*refs, n_qk, scale, rope_q):
    q_refs, k_refs = refs[:n_qk], refs[n_qk:2 * n_qk]
    n_tab = 0 if rope_q is None else 3
    v_ref = refs[2 * n_qk]
    tables = refs[2 * n_qk + 1:2 * n_qk + 1 + n_tab]
    o_ref, vt_ref, s_ref, e_ref = refs[2 * n_qk + 1 + n_tab:2 * n_qk + 5 + n_tab]
    _load_v_transposed(v_ref, vt_ref)
    if n_qk == 1:
        kcat_ref = k_refs[0]
    else:
        kcat_ref = refs[2 * n_qk + 5 + n_tab]

        @pl.when(pl.program_id(2) == 0)
        def _():
            for i, k_ref in enumerate(k_refs):
                kcat_ref[:, i * HEAD_DIM:(i + 1) * HEAD_DIM] = k_ref[...]

    tq = s_ref.shape[2]
    parts = [q_ref[...].astype(F32) for q_ref in q_refs]
    if rope_q is not None:
        parts[rope_q] = _rope_apply(parts[rope_q], *[t[...] for t in tables])
    q = (jnp.concatenate(parts, axis=1) * (scale * LOG2E)).astype(kcat_ref.dtype)
    tiles = [slice(t * tq, (t + 1) * tq) for t in range(q.shape[0] // tq)]
    outs = _attend_tiles(kcat_ref, [q[sl] for sl in tiles], vt_ref, s_ref, e_ref)
    for sl, (o_t, l) in zip(tiles, outs):
        o_ref[sl, :] = (o_t * (1.0 / l)).T.astype(o_ref.dtype)


def _attention_call(kernel, q_list, k_list, v, extras, *, bsz, heads, s, q_range, k_range, tq, lanes, name,
                    q_row_tables=(), k_row_tables=()):
    q_start, q_len = q_range
    k_start, k_len = k_range
    nq = q_len // tq
    q_off = q_start // tq
    k_off = k_start // k_len
    assert q_start % tq == 0 and q_len % tq == 0 and k_start % k_len == 0

    def qspec(col0):
        return pl.BlockSpec((None, tq, HEAD_DIM), lambda b, h, i: (b, q_off + i, col0 + h))

    def kspec(col0, stride, shared=False):
        if shared:
            return pl.BlockSpec((None, k_len, HEAD_DIM), lambda b, h, i: (b, k_off, col0))
        return pl.BlockSpec((None, k_len, HEAD_DIM), lambda b, h, i: (b, k_off, col0 + stride * h))

    def per_sample(arr):
        return arr.reshape(bsz, s, arr.shape[-1])

    in_specs, args = [], []
    for arr, col0 in q_list:
        in_specs.append(qspec(col0)); args.append(per_sample(arr))
    for arr, col0, stride, shared in k_list:
        in_specs.append(kspec(col0, stride, shared)); args.append(per_sample(arr))
    arr, col0, stride = v
    in_specs.append(kspec(col0, stride)); args.append(per_sample(arr))
    for e in extras:
        in_specs.append(pl.BlockSpec(e.shape, lambda b, h, i: (0, 0))); args.append(e)
    for t in q_row_tables:
        in_specs.append(pl.BlockSpec((tq, LANE), lambda b, h, i: (q_off + i, 0))); args.append(t)
    for t in k_row_tables:
        in_specs.append(pl.BlockSpec((k_len, LANE), lambda b, h, i: (k_off, 0))); args.append(t)
    return pl.pallas_call(
        kernel,
        grid=(bsz, heads, nq),
        in_specs=in_specs,
        out_specs=pl.BlockSpec((tq, HEAD_DIM), lambda b, h, i: (b * nq + i, h)),
        out_shape=jax.ShapeDtypeStruct((bsz * q_len, heads * HEAD_DIM), BF16),
        scratch_shapes=[pltpu.VMEM((HEAD_DIM, k_len), BF16), pltpu.VMEM((2, k_len, lanes), F32),
                        pltpu.VMEM((2, k_len, lanes), BF16)]
        + ([pltpu.VMEM((k_len, len(k_list) * HEAD_DIM), BF16)] if len(k_list) > 1 else [])
        + ([pltpu.VMEM((k_len, HEAD_DIM), BF16)] if k_row_tables else []),
        compiler_params=_params("parallel", "parallel", "arbitrary"),
        name=name,
    )(*args)


def _attend_lat_and_ctx(call, n_lat, n_ctx, with_ctx):
    s = n_lat + n_ctx
    o_lat = call((0, n_lat), (0, s), "lat")
    if not with_ctx:
        return o_lat
    bsz = o_lat.shape[0] // n_lat
    o_ctx = call((n_lat, n_ctx), (n_lat, n_ctx), "ctx").reshape(bsz, n_ctx, -1)
    return jnp.concatenate([o_lat.reshape(bsz, n_lat, -1), o_ctx], axis=1).reshape(bsz * s, -1)


NA_STEP_ROWS = 8
NA_WIN_ROWS = 16


def _na_window_start(step, grid_rows):
    return jnp.clip(step * NA_STEP_ROWS - NA_ROWS // 2, 0, grid_rows - NA_WIN_ROWS)


def _na_bias_table(rpb, grid_rows):
    steps = grid_rows // NA_STEP_ROWS
    heads = rpb.shape[0]
    n_dr = 2 * NA_ROWS - 1
    col = jnp.arange(GRID_W, dtype=jnp.int32)
    cs = jnp.clip(col - NA_COLS // 2, 0, GRID_W - NA_COLS)
    colmask = (col[None, :] >= cs[:, None]) & (col[None, :] < cs[:, None] + NA_COLS)
    dc = jnp.clip(col[None, :] - col[:, None] + NA_COLS - 1, 0, 2 * NA_COLS - 2)
    pat = jnp.where(colmask[None, None], rpb[:, :, dc].astype(F32) * LOG2E, NEG_INF)
    blank = jnp.full((heads, 1, GRID_W, GRID_W), NEG_INF, F32)
    pairs = jnp.concatenate([jnp.concatenate([blank, pat], axis=1), jnp.concatenate([pat, blank], axis=1)], axis=-1)

    kind_steps = [0, min(1, steps - 1), steps - 1]
    plan = []
    for st in kind_steps:
        ws = min(max(st * NA_STEP_ROWS - NA_ROWS // 2, 0), grid_rows - NA_WIN_ROWS)
        rows = []
        for i in range(NA_STEP_ROWS):
            r = st * NA_STEP_ROWS + i
            rs = min(max(r - NA_ROWS // 2, 0), grid_rows - NA_ROWS)
            rows.append([(ws + jw - r + NA_ROWS - 1) if rs <= ws + jw < rs + NA_ROWS else None
                         for jw in range(NA_WIN_ROWS)])
        plan.append(rows)

    def expand_kernel(p_ref, o_ref):
        lane = lax.broadcasted_iota(jnp.int32, (GRID_W, 2 * GRID_W), 1)
        for kind, rows in enumerate(plan):
            @pl.when(pl.program_id(1) == kind)
            def _(rows=rows):
                for i, offs in enumerate(rows):
                    for jp in range(NA_WIN_ROWS // 2):
                        left, right = offs[2 * jp], offs[2 * jp + 1]
                        if left is None and right is None:
                            blk = jnp.full((GRID_W, 2 * GRID_W), NEG_INF, F32)
                        else:
                            blk = p_ref[right if right is not None else left + 1]
                            if left is None:
                                blk = jnp.where(lane >= GRID_W, blk, NEG_INF)
                            if right is None:
                                blk = jnp.where(lane < GRID_W, blk, NEG_INF)
                        o_ref[i * GRID_W:(i + 1) * GRID_W, 2 * jp * GRID_W:2 * (jp + 1) * GRID_W] = blk

    tq, win = NA_STEP_ROWS * GRID_W, NA_WIN_ROWS * GRID_W
    return pl.pallas_call(
        expand_kernel,
        grid=(heads, 3),
        in_specs=[pl.BlockSpec((None, n_dr + 1, GRID_W, 2 * GRID_W), lambda h, k: (h, 0, 0, 0))],
        out_specs=pl.BlockSpec((None, None, tq, win), lambda h, k: (h, k, 0, 0)),
        out_shape=jax.ShapeDtypeStruct((heads, 3, tq, win), F32),
        compiler_params=_params("parallel", "arbitrary"),
        name="na_bias_table",
    )(pairs)


def _na_kernel(q_ref, k_ref, v_ref, bias_ref, o_ref, *, n_lat, grid_rows):
    start = pl.multiple_of(_na_window_start(pl.program_id(2), grid_rows) * GRID_W, GRID_W)
    win = NA_WIN_ROWS * GRID_W
    q = (q_ref[...].astype(F32) * (HEAD_DIM ** -0.5 * LOG2E)).astype(k_ref.dtype)
    sl = lax.dot_general(q, k_ref[pl.ds(start, win), :], NT_DIMS, preferred_element_type=F32) + bias_ref[...]
    sc = lax.dot_general(q, k_ref[n_lat:, :], NT_DIMS, preferred_element_type=F32)
    m = jnp.maximum(jnp.max(sl, axis=-1, keepdims=True), jnp.max(sc, axis=-1, keepdims=True))
    el = jnp.exp2(sl - m)
    ec = jnp.exp2(sc - m)
    inv = 1.0 / (jnp.sum(el, axis=-1, keepdims=True) + jnp.sum(ec, axis=-1, keepdims=True))
    o = (jnp.dot(el.astype(v_ref.dtype), v_ref[pl.ds(start, win), :], preferred_element_type=F32)
         + jnp.dot(ec.astype(v_ref.dtype), v_ref[n_lat:, :], preferred_element_type=F32))
    o_ref[...] = (o * inv).astype(o_ref.dtype)


def neighbourhood_attention(pd, bias, bsz, heads, n_lat, n_ctx):
    s = n_lat + n_ctx
    grid_rows = n_lat // GRID_W
    tq = NA_STEP_ROWS * GRID_W
    win = NA_WIN_ROWS * GRID_W
    steps = grid_rows // NA_STEP_ROWS
    assert grid_rows >= NA_WIN_ROWS and grid_rows % NA_STEP_ROWS == 0 and steps >= 2
    assert NA_WIN_ROWS >= NA_STEP_ROWS + NA_ROWS - 1

    def kind(g):
        return jnp.where(g == 0, 0, jnp.where(g == steps - 1, 2, 1))

    pd3 = pd.reshape(bsz, s, pd.shape[-1])
    return pl.pallas_call(
        functools.partial(_na_kernel, n_lat=n_lat, grid_rows=grid_rows),
        grid=(bsz, heads, steps),
        in_specs=[
            pl.BlockSpec((None, tq, HEAD_DIM), lambda b, h, g: (b, g, h)),
            pl.BlockSpec((None, s, HEAD_DIM), lambda b, h, g: (b, 0, heads + h)),
            pl.BlockSpec((None, s, HEAD_DIM), lambda b, h, g: (b, 0, 2 * heads + h)),
            pl.BlockSpec((None, None, tq, win), lambda b, h, g: (h, kind(g), 0, 0)),
        ],
        out_specs=pl.BlockSpec((tq, HEAD_DIM), lambda b, h, g: (b * steps + g, h)),
        out_shape=jax.ShapeDtypeStruct((bsz * n_lat, heads * HEAD_DIM), BF16),
        compiler_params=_params("parallel", "parallel", "arbitrary"),
        name="neighbourhood_attention",
    )(pd3, pd3, pd3, bias)


def _hgrn_constants(reverse):
    c = CHUNK
    t = np.arange(c)
    tri = (t[None, :] >= t[:, None]) if reverse else (t[None, :] <= t[:, None])
    mats = [tri.astype(np.float32)]
    masks = [np.eye(c, dtype=np.float32)]
    m = c // 2
    while m >= 1:
        node = (t // (2 * m)) * (2 * m)
        ref = node + (m if reverse else m - 1)
        if m < HGRN_BROADCAST_MIN:
            mats.append(tri[ref].astype(np.float32))
        upper = (t % (2 * m)) >= m
        same = node[:, None] == node[None, :]
        pair = same & (upper[:, None] & ~upper[None, :])
        masks.append((pair.T if reverse else pair).astype(np.float32))
        m //= 2
    return np.concatenate(mats, axis=0), np.stack(masks)


HGRN_BROADCAST_MIN = 8


def _hgrn_reference_rows(cum, cs, lev, reverse):
    c = CHUNK
    m = c >> lev
    if m < HGRN_BROADCAST_MIN:
        k = lev - int(math.log2(c // HGRN_BROADCAST_MIN))
        return cs[k * c:(k + 1) * c]
    rows = []
    for node in range(0, c, 2 * m):
        ref = node + (m if reverse else m - 1)
        rows.append(jnp.broadcast_to(cum[ref:ref + 1, :], (2 * m, cum.shape[1])))
    return rows[0] if len(rows) == 1 else jnp.concatenate(rows, axis=0)


def _hgrn_kernel(cq_ref, ci_ref, z_ref, lb_ref, ms_ref, mk_ref, o_ref, st_ref, *, heads, reverse):
    @pl.when(pl.program_id(1) == 0)
    def _():
        st_ref[...] = jnp.zeros_like(st_ref)

    c = CHUNK
    n_levels = mk_ref.shape[0]
    ms = ms_ref[...].astype(BF16)
    scale = HEAD_DIM ** -0.5
    tot_row = 0 if reverse else c - 1
    for h in range(heads):
        sl = slice(h * HEAD_DIM, (h + 1) * HEAD_DIM)
        lb = lb_ref[:, sl]
        f = lb + (1.0 - lb) * _sigmoid(z_ref[:, sl].astype(F32))
        kk = 1.0 - f
        g = jnp.log(f)
        cq = cq_ref[:, sl].astype(F32)
        qq = cq * _sigmoid(cq) * scale
        v = ci_ref[:, sl]
        g_hi = g.astype(BF16)
        r1 = g - g_hi.astype(F32)
        g_mid = r1.astype(BF16)
        g_lo = (r1 - g_mid.astype(F32)).astype(BF16)
        cs3 = jnp.dot(ms, jnp.concatenate([g_hi, g_mid, g_lo], axis=1), preferred_element_type=F32)
        cs = cs3[:, :HEAD_DIM] + cs3[:, HEAD_DIM:2 * HEAD_DIM] + cs3[:, 2 * HEAD_DIM:]
        cum = cs[:c]
        tot = cs[tot_row:tot_row + 1]
        st = st_ref[h]
        o = lax.dot_general((qq * jnp.exp(cum)).astype(BF16), st.astype(BF16), NT_DIMS,
                            preferred_element_type=F32)
        a = mk_ref[0] * lax.dot_general(qq.astype(BF16), kk.astype(BF16), NT_DIMS,
                                        preferred_element_type=F32)
        for lev in range(1, n_levels):
            e = jnp.exp(-jnp.abs(cum - _hgrn_reference_rows(cum, cs, lev, reverse)))
            a = a + mk_ref[lev] * lax.dot_general((qq * e).astype(BF16), (kk * e).astype(BF16), NT_DIMS,
                                                  preferred_element_type=F32)
        o = o + jnp.dot(a.astype(BF16), v, preferred_element_type=F32)
        o_ref[:, sl] = o
        ke = (kk * jnp.exp(tot - cum)).astype(BF16)
        st_ref[h] = st * jnp.exp(tot) + lax.dot_general(v, ke, TN_DIMS, preferred_element_type=F32)


def hgrn_scan(pc, lb, bsz, heads, n_lat, n_ctx, reverse):
    s = n_lat + n_ctx
    chunks = s // CHUNK
    lat_chunks = n_lat // CHUNK
    width = heads * HEAD_DIM
    ms, mk = _hgrn_constants(reverse)

    def chunk_of(step):
        return (chunks - 1 - step) if reverse else lax.rem(step + lat_chunks, chunks)

    def spec(col):
        return pl.BlockSpec((CHUNK, width), lambda b, t: (b * chunks + chunk_of(t), col))

    return pl.pallas_call(
        functools.partial(_hgrn_kernel, heads=heads, reverse=reverse),
        grid=(bsz, chunks),
        in_specs=[spec(0), spec(1), spec(3 if reverse else 2),
                  pl.BlockSpec((1, width), lambda b, t: (0, 0)),
                  pl.BlockSpec(ms.shape, lambda b, t: (0, 0)),
                  pl.BlockSpec(mk.shape, lambda b, t: (0, 0, 0))],
        out_specs=spec(0),
        out_shape=jax.ShapeDtypeStruct((bsz * s, width), F32),
        scratch_shapes=[pltpu.VMEM((heads, HEAD_DIM, HEAD_DIM), F32)],
        compiler_params=_params("parallel", "arbitrary"),
        name="hgrn_scan_bwd" if reverse else "hgrn_scan_fwd",
    )(pc, pc, pc, lb.reshape(1, width), jnp.asarray(ms), jnp.asarray(mk))


def _hgrn_readout_kernel(of_ref, ob_ref, gate_ref, g_ref, o_ref):
    gain = g_ref[...]
    for h in range(of_ref.shape[1] // HEAD_DIM):
        sl = slice(h * HEAD_DIM, (h + 1) * HEAD_DIM)
        gate = gate_ref[:, sl].astype(F32)
        o = _rms(of_ref[:, sl] + ob_ref[:, sl], gain)
        o_ref[:, sl] = (o * (gate * _sigmoid(gate))).astype(o_ref.dtype)


def hgrn_readout(o_fwd, o_bwd, pc, gain, bsz, s, rows, tr=256):
    width = o_fwd.shape[1]
    in_tiles, out_tiles = s // tr, rows // tr
    ispec = pl.BlockSpec((tr, width), lambda b, i: (b * in_tiles + i, 0))
    return pl.pallas_call(
        _hgrn_readout_kernel,
        grid=(bsz, out_tiles),
        in_specs=[ispec, ispec, pl.BlockSpec((tr, width), lambda b, i: (b * in_tiles + i, 4)),
                  pl.BlockSpec((1, HEAD_DIM), lambda b, i: (0, 0))],
        out_specs=pl.BlockSpec((tr, width), lambda b, i: (b * out_tiles + i, 0)),
        out_shape=jax.ShapeDtypeStruct((bsz * rows, width), BF16),
        compiler_params=_params("parallel", "parallel"),
        name="hgrn_readout",
    )(o_fwd, o_bwd, pc, gain.reshape(1, HEAD_DIM))


def _mla_prep_kernel(p_ref, qg_ref, kg_ref, cos_ref, sn_ref, sp_ref, qn_ref, cn_ref, kr_ref, *, q_rank, kv_rank):
    qn_ref[...] = _rms(p_ref[:, :q_rank].astype(F32), qg_ref[...]).astype(qn_ref.dtype)
    cn_ref[...] = _rms(p_ref[:, q_rank:q_rank + kv_rank].astype(F32), kg_ref[...]).astype(cn_ref.dtype)
    kr = p_ref[:, q_rank + kv_rank:q_rank + kv_rank + LANE].astype(F32)
    kr_ref[...] = _rope_apply(kr, cos_ref[...], sn_ref[...], sp_ref[...]).astype(kr_ref.dtype)


def mla_prep(pb, q_gain, kv_gain, tables, s, tr=256):
    r, width = pb.shape
    q_rank, kv_rank = q_gain.shape[0], kv_gain.shape[0]
    assert width >= q_rank + kv_rank + LANE and q_rank % LANE == 0 and kv_rank % LANE == 0
    tiles_per_seq = s // tr
    tspec = pl.BlockSpec((tr, LANE), lambda i: (i % tiles_per_seq, 0))

    def ospec(w):
        return pl.BlockSpec((tr, w), lambda i: (i, 0))

    return pl.pallas_call(
        functools.partial(_mla_prep_kernel, q_rank=q_rank, kv_rank=kv_rank),
        grid=(r // tr,),
        in_specs=[ospec(width), pl.BlockSpec((1, q_rank), lambda i: (0, 0)),
                  pl.BlockSpec((1, kv_rank), lambda i: (0, 0)), tspec, tspec, tspec],
        out_specs=[ospec(q_rank), ospec(kv_rank), ospec(LANE)],
        out_shape=[jax.ShapeDtypeStruct((r, q_rank), BF16), jax.ShapeDtypeStruct((r, kv_rank), BF16),
                   jax.ShapeDtypeStruct((r, LANE), BF16)],
        compiler_params=_params("parallel"),
        name="mla_prep",
    )(pb, q_gain.reshape(1, q_rank), kv_gain.reshape(1, kv_rank), *tables)


def _merge_kernel(oa_ref, ob_ref, oc_ref, od_ref, ga_ref, gb_ref, gc_ref, gd_ref, w_ref, o_ref):
    acc = None
    for i, (o_ref_i, g_ref_i) in enumerate(((oa_ref, ga_ref), (ob_ref, gb_ref), (oc_ref, gc_ref), (od_ref, gd_ref))):
        y = jnp.dot(o_ref_i[...], w_ref[i], preferred_element_type=F32)
        term = _sigmoid(g_ref_i[...].astype(F32)) * y
        acc = term if acc is None else acc + term
    o_ref[...] = acc.astype(o_ref.dtype)


def merge_branches(outs, gate_logits, w_branch):
    r, c = outs[0].shape
    d = w_branch.shape[2]
    bm = _pick(r, (512, 256, 128))
    bn = _pick(d, (1024, 512, 256, 128))
    nb = d // bn
    ospec = pl.BlockSpec((bm, c), lambda i, j: (i, 0))

    def gspec(branch):
        return pl.BlockSpec((bm, bn), lambda i, j: (i, branch * nb + j))

    return pl.pallas_call(
        _merge_kernel,
        grid=(r // bm, nb),
        in_specs=[ospec] * N_BRANCH + [gspec(i) for i in range(N_BRANCH)]
        + [pl.BlockSpec((N_BRANCH, c, bn), lambda i, j: (0, 0, j))],
        out_specs=pl.BlockSpec((bm, bn), lambda i, j: (i, j)),
        out_shape=jax.ShapeDtypeStruct((r, d), BF16),
        compiler_params=_params("parallel", "parallel"),
        name="merge_branches",
    )(*outs, gate_logits, gate_logits, gate_logits, gate_logits, w_branch)


HALO = 16


FFN_BN = 640


def _ffn_tile_pairs(a, bn):
    f = a.shape[-1] // 2
    lead = a.shape[:-1]
    pairs = jnp.stack([a[..., :f].reshape(*lead, f // bn, bn), a[..., f:].reshape(*lead, f // bn, bn)], axis=-2)
    return pairs.reshape(*lead, 2 * f)


def _cast_pair_kernel(g_ref, v_ref, o_ref):
    bn = g_ref.shape[1]
    o_ref[:, :bn] = g_ref[...].astype(o_ref.dtype)
    o_ref[:, bn:] = v_ref[...].astype(o_ref.dtype)


def ffn_up_weight(w_up, layer, bn):
    _, d, two_f = w_up.shape
    nb = two_f // 2 // bn
    tk = _pick(d, (1024, 512, 256, 128))
    return pl.pallas_call(
        _cast_pair_kernel,
        grid=(d // tk, nb),
        in_specs=[pl.BlockSpec((None, tk, bn), lambda i, j: (layer, i, j)),
                  pl.BlockSpec((None, tk, bn), lambda i, j: (layer, i, nb + j))],
        out_specs=pl.BlockSpec((tk, 2 * bn), lambda i, j: (i, j)),
        out_shape=jax.ShapeDtypeStruct((d, two_f), BF16),
        compiler_params=_params("parallel", "parallel"),
        name="ffn_up_weight",
    )(w_up, w_up)


def _ffn_up_kernel(h_ref, hp_ref, hn_ref, w_ref, cw_ref, cb_ref, o_ref, hbuf_ref, *, bm, s, n_lat):
    i = pl.program_id(0)
    bn = o_ref.shape[1]

    @pl.when(pl.program_id(1) == 0)
    def _():
        hbuf_ref[0:HALO, :] = hp_ref[...]
        hbuf_ref[HALO:HALO + bm, :] = h_ref[...]
        hbuf_ref[HALO + bm:, :] = hn_ref[...]

    pos = lax.rem(i * bm + lax.broadcasted_iota(jnp.int32, (bm, 1), 0), s)
    has_prev = jnp.where((pos == 0) | (pos == n_lat), 0.0, 1.0)
    has_next = jnp.where((pos == n_lat - 1) | (pos == s - 1), 0.0, 1.0)
    rows = bm + 2 * HALO
    u = jnp.dot(hbuf_ref[...], w_ref[...], preferred_element_type=F32)
    prev = pltpu.roll(u, 1, 0)[HALO:HALO + bm]
    nxt = pltpu.roll(u, rows - 1, 0)[HALO:HALO + bm]
    cw = cw_ref[...]
    y = prev * has_prev * cw[0:1] + u[HALO:HALO + bm] * cw[1:2] + nxt * has_next * cw[2:3] + cb_ref[...]
    gate, val = y[:, :bn], y[:, bn:]
    o_ref[...] = (gate * _sigmoid(gate) * val).astype(o_ref.dtype)


def ffn_up(h, w_up, conv_w, conv_b, s, n_lat):
    r, d = h.shape
    f = w_up.shape[1] // 2
    bm = _pick(r, (512, 256, 128))
    bn = FFN_BN
    assert f % bn == 0
    halo_blocks = r // HALO
    per_tile = bm // HALO
    return pl.pallas_call(
        functools.partial(_ffn_up_kernel, bm=bm, s=s, n_lat=n_lat),
        grid=(r // bm, f // bn),
        in_specs=[
            pl.BlockSpec((bm, d), lambda i, j: (i, 0)),
            pl.BlockSpec((HALO, d), lambda i, j: (jnp.maximum(i * per_tile - 1, 0), 0)),
            pl.BlockSpec((HALO, d), lambda i, j: (jnp.minimum((i + 1) * per_tile, halo_blocks - 1), 0)),
            pl.BlockSpec((d, 2 * bn), lambda i, j: (0, j)),
            pl.BlockSpec((3, 2 * bn), lambda i, j: (0, j)),
            pl.BlockSpec((1, 2 * bn), lambda i, j: (0, j)),
        ],
        out_specs=pl.BlockSpec((bm, bn), lambda i, j: (i, j)),
        out_shape=jax.ShapeDtypeStruct((r, f), BF16),
        scratch_shapes=[pltpu.VMEM((bm + 2 * HALO, d), BF16)],
        compiler_params=_params("parallel", "arbitrary"),
        name="ffn_up",
    )(h, h, h, w_up, conv_w, conv_b)


def _split_w_in(w, d, heads):
    mix = heads * HEAD_DIM
    q_rank = 3 * d // 16
    kv_rank = d // 8
    sizes = [('a', 3 * mix), ('b_qa', q_rank), ('b_kva', kv_rank + MLA_ROPE), ('c', 5 * mix), ('d', 3 * mix),
             ('gate', N_BRANCH * d)]
    out, start = {}, 0
    for name, size in sizes:
        out[name] = (start, size)
        start += size
    assert start == w.shape[1]

    def cols(name):
        a, n = out[name]
        return w[:, a:a + n].astype(BF16)

    used = q_rank + kv_rank + LANE
    pad = -used % 512 + LANE - MLA_ROPE
    w_b = jnp.concatenate([cols('b_qa'), cols('b_kva'), jnp.zeros((d, pad), BF16)], axis=1)
    return cols('a'), w_b, cols('c'), cols('d'), cols('gate'), q_rank, kv_rank


def _mla_q_weight(w_qb, heads):
    rq = w_qb.shape[0]
    w = w_qb.reshape(rq, heads, MLA_NOPE + MLA_ROPE)
    nope = w[:, :, :MLA_NOPE].reshape(rq, heads * MLA_NOPE)
    ropep = jnp.pad(w[:, :, MLA_NOPE:], ((0, 0), (0, 0), (0, LANE - MLA_ROPE))).reshape(rq, heads * LANE)
    return jnp.concatenate([nope, ropep], axis=1).astype(BF16)


def kernel(x, c, ctx, c_ctx, ada_w, ada_b, norm_mix_pre, norm_mix_post, norm_ffn_pre, norm_ffn_post, w_in,
           diff_lambda, diff_subln, mla_q_norm, mla_w_qb, mla_kv_norm, mla_w_kvb, hgrn_lb_logits, hgrn_norm,
           na_rpb, w_branch, w_out, ffn_w_up, ffn_conv_w, ffn_conv_b, ffn_w_down):
    bsz, n_lat, d = x.shape
    n_ctx = ctx.shape[1]
    s = n_lat + n_ctx
    r = bsz * s
    depth = ada_w.shape[0]
    heads = d // (N_BRANCH * HEAD_DIM)
    mix = heads * HEAD_DIM
    assert n_lat % 256 == 0 and n_ctx % 256 == 0 and s % n_ctx == 0 and n_lat % GRID_W == 0

    tables = _rope_tables(n_lat, n_ctx)
    lb_sm = jax.nn.softmax(hgrn_lb_logits.astype(F32), axis=0)
    lower_bounds = jnp.cumsum(lb_sm, axis=0) - lb_sm[0]

    c_rows = jnp.zeros((8, d), F32).at[:bsz].set(c).at[bsz].set(c_ctx)
    xa = jnp.concatenate([x, ctx], axis=1)

    def mod_rows(mod, idx):
        part = mod[:, idx * d:(idx + 1) * d]
        both = jnp.stack([part[:bsz], jnp.broadcast_to(part[bsz], (bsz, d))], axis=1)
        return both.reshape(2 * bsz, 1, d)

    mods = []
    for l in range(depth):
        mod = ada_modulation(c_rows, ada_w, ada_b, l)
        mods.append([mod_rows(mod, i) for i in range(6)])

    h = norm_modulate(xa, norm_mix_pre[0], mods[0][0], mods[0][1], n_lat)
    for l in range(depth):
        with_ctx = l < depth - 1
        sh1, sc1, g1, sh2, sc2, g2 = mods[l]
        h2d = h.reshape(r, d)
        rows = s if with_ctx else n_lat

        w_a, w_b, w_c, w_d, w_g, q_rank, kv_rank = _split_w_in(w_in[l], d, heads)
        pa = matmul(h2d, w_a, BF16, "in_proj_a")
        pb = matmul(h2d, w_b, BF16, "in_proj_b")
        pc = matmul(h2d, w_c, BF16, "in_proj_c")
        pd = matmul(h2d, w_d, BF16, "in_proj_d")
        if with_ctx:
            pg = matmul(h2d, w_g, BF16, "in_proj_gate")
        else:
            pg = matmul_leading_rows(h, w_g, rows, BF16, "in_proj_gate")

        lam_init = 0.8 - 0.6 * math.exp(-0.3 * l)
        extras_a = [diff_lambda[l].astype(F32), diff_subln[l].reshape(1, HEAD_DIM).astype(F32)]

        def call_a(q_range, k_range, tag):
            return _attention_call(
                functools.partial(_diff_attn_kernel, lam_init=lam_init),
                [(pa, 0)], [(pa, heads, 1, False)], (pa, 2 * heads, 1), extras_a,
                bsz=bsz, heads=heads, s=s, q_range=q_range, k_range=k_range, tq=min(1024, q_range[1]), lanes=256,
                name="diff_attn_" + tag, q_row_tables=tables, k_row_tables=tables)

        oa = _attend_lat_and_ctx(call_a, n_lat, n_ctx, with_ctx)

        qn, cn, kr = mla_prep(pb, mla_q_norm[l], mla_kv_norm[l], tables, s)
        qb = matmul(qn, _mla_q_weight(mla_w_qb[l], heads), BF16, "mla_q_up")
        kvb = matmul(cn, mla_w_kvb[l].astype(BF16), BF16, "mla_kv_up")

        def call_b(q_range, k_range, tag):
            return _attention_call(
                functools.partial(_sdpa_kernel, n_qk=2, scale=(MLA_NOPE + MLA_ROPE) ** -0.5, rope_q=1),
                [(qb, 0), (qb, heads)], [(kvb, 0, 2, False), (kr, 0, 0, True)], (kvb, 1, 2), [],
                bsz=bsz, heads=heads, s=s, q_range=q_range, k_range=k_range, tq=min(2048, q_range[1]), lanes=256,
                name="mla_attn_" + tag, q_row_tables=tables)

        ob = _attend_lat_and_ctx(call_b, n_lat, n_ctx, with_ctx)

        o_fwd = hgrn_scan(pc, lower_bounds[l, 0], bsz, heads, n_lat, n_ctx, reverse=False)
        o_bwd = hgrn_scan(pc, lower_bounds[l, 1], bsz, heads, n_lat, n_ctx, reverse=True)
        oc = hgrn_readout(o_fwd, o_bwd, pc, hgrn_norm[l], bsz, s, rows)

        od = neighbourhood_attention(pd, _na_bias_table(na_rpb[l], n_lat // GRID_W), bsz, heads, n_lat, n_ctx)
        if with_ctx:
            od_ctx = _attention_call(
                functools.partial(_sdpa_kernel, n_qk=1, scale=HEAD_DIM ** -0.5, rope_q=None),
                [(pd, 0)], [(pd, heads, 1, False)], (pd, 2 * heads, 1), [],
                bsz=bsz, heads=heads, s=s, q_range=(n_lat, n_ctx), k_range=(n_lat, n_ctx), tq=256, lanes=256,
                name="na_ctx_attn").reshape(bsz, n_ctx, mix)
            od = jnp.concatenate([od.reshape(bsz, n_lat, mix), od_ctx], axis=1).reshape(r, mix)

        merged = merge_branches((oa, ob, oc, od), pg, w_branch[l].astype(BF16))
        mix_out = matmul(merged, w_out[l].astype(BF16), BF16, "out_proj").reshape(bsz, rows, d)
        xa, h2 = residual_norm_modulate(xa, mix_out, g1, norm_mix_post[l], norm_ffn_pre[l], sh2, sc2, n_lat)

        gv = ffn_up(h2.reshape(bsz * rows, d), ffn_up_weight(ffn_w_up, l, FFN_BN),
                    _ffn_tile_pairs(ffn_conv_w[l], FFN_BN), _ffn_tile_pairs(ffn_conv_b[l][None], FFN_BN),
                    rows, n_lat)
        ffn = matmul(gv, ffn_w_down[l].astype(BF16), BF16, "ffn_down").reshape(bsz, rows, d)
        if with_ctx:
            xa, h = residual_norm_modulate(xa, ffn, g2, norm_ffn_post[l], norm_mix_pre[l + 1],
                                           mods[l + 1][0], mods[l + 1][1], n_lat)
        else:
            xa = residual(xa, ffn, g2, norm_ffn_post[l], n_lat)
    return xa
```

```python
import functools
import math

import numpy as np
import jax
import jax.numpy as jnp
from jax import lax
from jax.experimental import pallas as pl
from jax.experimental.pallas import tpu as pltpu

F32 = jnp.float32
BF16 = jnp.bfloat16

HEAD_DIM = 128
QK_HALF = 64
MLA_NOPE = 128
MLA_ROPE = 64
MLA_V = 128
ROPE_DIM = 64
ROPE_THETA = 10000.0
GRID_W = 64
NA_ROWS = 8
NA_COLS = 16
CHUNK = 128
NORM_EPS = 1e-6
NEG_INF = -1e30
N_BRANCH = 4

LANE = 128
VMEM_LIMIT_BYTES = 56 * 1024 * 1024

NT_DIMS = (((1,), (1,)), ((), ()))
TN_DIMS = (((0,), (0,)), ((), ()))


def _params(*sem):
    return pltpu.CompilerParams(dimension_semantics=sem, vmem_limit_bytes=VMEM_LIMIT_BYTES)


def _sigmoid(x):
    return 0.5 * jnp.tanh(0.5 * x) + 0.5


def _pick(total, candidates):
    for c in candidates:
        if total % c == 0:
            return c
    raise ValueError(f"no tile for {total} in {candidates}")


def _ada_kernel(c_ref, w_ref, b_ref, o_ref):
    c = c_ref[...]
    a = (c * _sigmoid(c)).astype(BF16)
    o_ref[...] = jnp.dot(a, w_ref[...].astype(BF16), preferred_element_type=F32) + b_ref[...]


def ada_modulation(c_rows, ada_w, ada_b, layer):
    rows, d = c_rows.shape
    width = ada_w.shape[2]
    tn = _pick(width, (512, 256, 128))
    b3 = ada_b.reshape(ada_b.shape[0], 1, width)
    return pl.pallas_call(
        _ada_kernel,
        grid=(width // tn,),
        in_specs=[
            pl.BlockSpec((rows, d), lambda j: (0, 0)),
            pl.BlockSpec((None, d, tn), lambda j: (layer, 0, j)),
            pl.BlockSpec((None, 1, tn), lambda j: (layer, 0, j)),
        ],
        out_specs=pl.BlockSpec((rows, tn), lambda j: (0, j)),
        out_shape=jax.ShapeDtypeStruct((rows, width), F32),
        compiler_params=_params("parallel"),
        name="ada_modulation",
    )(c_rows, ada_w, b3)


def _rms(x, gain):
    return x * lax.rsqrt(jnp.mean(x * x, axis=-1, keepdims=True) + NORM_EPS) * gain


def _norm_mod_kernel(x_ref, g_ref, sh_ref, sc_ref, h_ref):
    h = _rms(x_ref[0], g_ref[...])
    h_ref[0] = (h * (1.0 + sc_ref[0]) + sh_ref[0]).astype(h_ref.dtype)


def _resid_kernel(x_ref, y_ref, gate_ref, gpost_ref, xo_ref):
    xo_ref[0] = x_ref[0] + gate_ref[0] * _rms(y_ref[0].astype(F32), gpost_ref[...])


def _resid_norm_mod_kernel(x_ref, y_ref, gate_ref, gpost_ref, gpre_ref, sh_ref, sc_ref, xo_ref, h_ref):
    xn = x_ref[0] + gate_ref[0] * _rms(y_ref[0].astype(F32), gpost_ref[...])
    xo_ref[0] = xn
    h = _rms(xn, gpre_ref[...])
    h_ref[0] = (h * (1.0 + sc_ref[0]) + sh_ref[0]).astype(h_ref.dtype)


def _row_specs(dims, tr, n_lat_tiles):
    s, d = dims
    xspec = pl.BlockSpec((1, tr, d), lambda b, t: (b, t, 0))
    gspec = pl.BlockSpec((1, d), lambda b, t: (0, 0))
    mspec = pl.BlockSpec((1, 1, d), lambda b, t: (2 * b + t // n_lat_tiles, 0, 0))
    return xspec, gspec, mspec


def norm_modulate(x, gain, shift, scale, n_lat, tr=256):
    bsz, s, d = x.shape
    xspec, gspec, mspec = _row_specs((s, d), tr, n_lat // tr)
    return pl.pallas_call(
        _norm_mod_kernel,
        grid=(bsz, s // tr),
        in_specs=[xspec, gspec, mspec, mspec],
        out_specs=xspec,
        out_shape=jax.ShapeDtypeStruct(x.shape, BF16),
        compiler_params=_params("parallel", "parallel"),
        name="norm_modulate",
    )(x, gain.reshape(1, d), shift, scale)


def residual(x, y, gate, gain_post, n_lat, tr=256):
    bsz, s, d = x.shape
    rows = y.shape[1]
    xspec, gspec, mspec = _row_specs((s, d), tr, n_lat // tr)
    return pl.pallas_call(
        _resid_kernel,
        grid=(bsz, rows // tr),
        in_specs=[xspec, xspec, mspec, gspec],
        out_specs=xspec,
        out_shape=jax.ShapeDtypeStruct(y.shape, F32),
        compiler_params=_params("parallel", "parallel"),
        name="residual",
    )(x, y, gate, gain_post.reshape(1, d))


def residual_norm_modulate(x, y, gate, gain_post, gain_pre, shift, scale, n_lat, tr=256):
    bsz, s, d = x.shape
    rows = y.shape[1]
    xspec, gspec, mspec = _row_specs((s, d), tr, n_lat // tr)
    return pl.pallas_call(
        _resid_norm_mod_kernel,
        grid=(bsz, rows // tr),
        in_specs=[xspec, xspec, mspec, gspec, gspec, mspec, mspec],
        out_specs=[xspec, xspec],
        out_shape=[jax.ShapeDtypeStruct(y.shape, F32), jax.ShapeDtypeStruct(y.shape, BF16)],
        compiler_params=_params("parallel", "parallel"),
        name="residual_norm_modulate",
    )(x, y, gate, gain_post.reshape(1, d), gain_pre.reshape(1, d), shift, scale)


def _mm_kernel(a_ref, b_ref, o_ref):
    o_ref[...] = jnp.dot(a_ref[...], b_ref[...], preferred_element_type=F32).astype(o_ref.dtype)


def _mm_acc_kernel(a_ref, b_ref, o_ref, acc_ref):
    k = pl.program_id(2)

    @pl.when(k == 0)
    def _():
        acc_ref[...] = jnp.zeros_like(acc_ref)

    acc_ref[...] += jnp.dot(a_ref[...], b_ref[...], preferred_element_type=F32)

    @pl.when(k == pl.num_programs(2) - 1)
    def _():
        o_ref[...] = acc_ref[...].astype(o_ref.dtype)


MAX_SINGLE_K = 4096
MATMUL_VMEM_BUDGET = VMEM_LIMIT_BYTES * 9 // 10


def matmul_leading_rows(a, b, rows, out_dtype, name):
    bsz, _, k = a.shape
    n = b.shape[1]
    assert k <= MAX_SINGLE_K
    bm = _pick(rows, (1024, 512, 256, 128))
    bn = _pick(n, (1024, 768, 640, 512, 384, 256, 128))
    tiles = rows // bm
    return pl.pallas_call(
        _mm_kernel,
        grid=(bsz, tiles, n // bn),
        in_specs=[pl.BlockSpec((None, bm, k), lambda s, i, j: (s, i, 0)),
                  pl.BlockSpec((k, bn), lambda s, i, j: (0, j))],
        out_specs=pl.BlockSpec((bm, bn), lambda s, i, j: (s * tiles + i, j)),
        out_shape=jax.ShapeDtypeStruct((bsz * rows, n), out_dtype),
        compiler_params=_params("parallel", "parallel", "parallel"),
        name=name,
    )(a, b)


def matmul(a, b, out_dtype, name):
    m, k = a.shape
    n = b.shape[1]
    out_bytes = jnp.dtype(out_dtype).itemsize
    full_k = None
    for cap in (1024, 512):
        bm = _pick(m, tuple(c for c in (1024, 512, 256, 128) if c <= cap))
        bn = _pick(n, tuple(c for c in (1024, 768, 640, 512, 384, 256, 128) if c <= cap))
        need = 2 * 2 * (bm * k + k * bn) + 2 * bm * bn * out_bytes + 4 * bm * bn
        if need <= MATMUL_VMEM_BUDGET:
            full_k = (bm, bn)
            break
    if full_k is not None:
        bm, bn = full_k
        return pl.pallas_call(
            _mm_kernel,
            grid=(m // bm, n // bn),
            in_specs=[pl.BlockSpec((bm, k), lambda i, j: (i, 0)),
                      pl.BlockSpec((k, bn), lambda i, j: (0, j))],
            out_specs=pl.BlockSpec((bm, bn), lambda i, j: (i, j)),
            out_shape=jax.ShapeDtypeStruct((m, n), out_dtype),
            compiler_params=_params("parallel", "parallel"),
            name=name,
        )(a, b)
    bm = _pick(m, (1024, 512, 256, 128))
    bn = _pick(n, (1024, 768, 640, 512, 384, 256, 128))
    nk = -(-k // MAX_SINGLE_K)
    while k % nk or (k // nk) % LANE:
        nk += 1
    bk = k // nk
    return pl.pallas_call(
        _mm_acc_kernel,
        grid=(m // bm, n // bn, nk),
        in_specs=[pl.BlockSpec((bm, bk), lambda i, j, kk: (i, kk)),
                  pl.BlockSpec((bk, bn), lambda i, j, kk: (kk, j))],
        out_specs=pl.BlockSpec((bm, bn), lambda i, j, kk: (i, j)),
        out_shape=jax.ShapeDtypeStruct((m, n), out_dtype),
        scratch_shapes=[pltpu.VMEM((bm, bn), F32)],
        compiler_params=_params("parallel", "parallel", "arbitrary"),
        name=name,
    )(a, b)


def _rope_tables(n_lat, n_ctx):
    t = jnp.arange(n_lat, dtype=jnp.int32)
    row = (t // GRID_W).astype(F32)
    col = (t % GRID_W).astype(F32)
    quarter = ROPE_DIM // 4
    inv = ROPE_THETA ** (-jnp.arange(quarter, dtype=F32) / quarter)
    ar = row[:, None] * inv
    ac = col[:, None] * inv
    ang = jnp.concatenate([ar, ar, ac, ac], axis=-1)
    cos = jnp.concatenate([jnp.cos(ang), jnp.ones((n_ctx, ROPE_DIM), F32)], axis=0)
    sin = jnp.concatenate([jnp.sin(ang), jnp.zeros((n_ctx, ROPE_DIM), F32)], axis=0)
    first = (jnp.arange(ROPE_DIM) // quarter) % 2 == 0
    sin_next = jnp.where(first, -sin, 0.0)
    sin_prev = jnp.where(first, 0.0, sin)
    reps = LANE // ROPE_DIM
    return tuple(jnp.tile(a, (1, reps)) for a in (cos, sin_next, sin_prev))


def _rope_apply(x, cos, sin_next, sin_prev):
    quarter = ROPE_DIM // 4
    return (x * cos + pltpu.roll(x, LANE - quarter, 1) * sin_next
            + pltpu.roll(x, quarter, 1) * sin_prev)


LOG2E = math.log2(math.e)
KEY_CHUNK = 256


def _attend_tiles(k_ref, q_tiles, vt_ref, s_ref, e_ref):
    m_keys, n = s_ref.shape[1:]
    kc = min(KEY_CHUNK, m_keys)
    chunks = [slice(c * kc, (c + 1) * kc) for c in range(m_keys // kc)]

    def scores(t):
        s_ref[t % 2] = lax.dot_general(k_ref[...], q_tiles[t], NT_DIMS, preferred_element_type=F32)

    def column_max(t):
        m8 = None
        for sl in chunks:
            part = jnp.max(s_ref[t % 2, sl, :].reshape(kc // 8, 8, n), axis=0)
            m8 = part if m8 is None else jnp.maximum(m8, part)
        return jnp.max(m8, axis=0, keepdims=True)

    scores(0)
    m = column_max(0)
    outs = []
    for t in range(len(q_tiles)):
        if t + 1 < len(q_tiles):
            scores(t + 1)
        l8 = jnp.zeros((8, n), F32)
        for sl in chunks:
            e = jnp.exp2(s_ref[t % 2, sl, :] - m)
            l8 = l8 + jnp.sum(e.reshape(kc // 8, 8, n), axis=0)
            e_ref[t % 2, sl, :] = e.astype(e_ref.dtype)
        o_t = jnp.dot(vt_ref[...], e_ref[t % 2], preferred_element_type=F32)
        outs.append((o_t, jnp.sum(l8, axis=0, keepdims=True)))
        if t + 1 < len(q_tiles):
            m = column_max(t + 1)
    return outs


def _load_v_transposed(v_ref, vt_ref):
    @pl.when(pl.program_id(2) == 0)
    def _():
        vt_ref[...] = v_ref[...].astype(F32).T.astype(vt_ref.dtype)


def _diff_attn_kernel(q_ref, k_ref, v_ref, lam_ref, g_ref, qcos_ref, qsn_ref, qsp_ref, kcos_ref, ksn_ref, ksp_ref,
                      o_ref, vt_ref, s_ref, e_ref, kr_ref, *, lam_init):
    _load_v_transposed(v_ref, vt_ref)

    @pl.when(pl.program_id(2) == 0)
    def _():
        kr_ref[...] = _rope_apply(k_ref[...].astype(F32), kcos_ref[...], ksn_ref[...], ksp_ref[...]
                                  ).astype(kr_ref.dtype)

    tq = s_ref.shape[2] // 2
    q = _rope_apply(q_ref[...].astype(F32), qcos_ref[...], qsn_ref[...], qsp_ref[...])
    q = (q * (QK_HALF ** -0.5 * LOG2E)).astype(kr_ref.dtype)
    lane = lax.broadcasted_iota(jnp.int32, q.shape, 1)
    zero = jnp.zeros_like(q)
    q0 = jnp.where(lane < QK_HALF, q, zero)
    q1 = jnp.where(lane >= QK_HALF, q, zero)
    tiles = [slice(t * tq, (t + 1) * tq) for t in range(q.shape[0] // tq)]
    outs = _attend_tiles(kr_ref, [jnp.concatenate([q0[sl], q1[sl]], axis=0) for sl in tiles], vt_ref, s_ref, e_ref)
    lp = lam_ref[...]
    lam = (jnp.exp(jnp.sum(lp[0:1] * lp[1:2], axis=-1, keepdims=True))
           - jnp.exp(jnp.sum(lp[2:3] * lp[3:4], axis=-1, keepdims=True)) + lam_init)
    gain = g_ref[...] * (1.0 - lam_init)
    for sl, (o_t, l) in zip(tiles, outs):
        inv = 1.0 / l
        c_t = o_t[:, :tq] * inv[:, :tq] - o_t[:, tq:] * (lam * inv[:, tq:])
        c_t = c_t * lax.rsqrt(jnp.mean(c_t * c_t, axis=0, keepdims=True) + NORM_EPS)
        o_ref[sl, :] = (c_t.T * gain).astype(o_ref.dtype)


def _sdpa_kernel(*refs, n_qk, scale, rope_q):
    q_refs, k_refs = refs[:n_qk], refs[n_qk:2 * n_qk]
    n_tab = 0 if rope_q is None else 3
    v_ref = refs[2 * n_qk]
    tables = refs[2 * n_qk + 1:2 * n_qk + 1 + n_tab]
    o_ref, vt_ref, s_ref, e_ref = refs[2 * n_qk + 1 + n_tab:2 * n_qk + 5 + n_tab]
    _load_v_transposed(v_ref, vt_ref)
    if n_qk == 1:
        kcat_ref = k_refs[0]
    else:
        kcat_ref = refs[2 * n_qk + 5 + n_tab]

        @pl.when(pl.program_id(2) == 0)
        def _():
            for i, k_ref in enumerate(k_refs):
                kcat_ref[:, i * HEAD_DIM:(i + 1) * HEAD_DIM] = k_ref[...]

    tq = s_ref.shape[2]
    parts = [q_ref[...].astype(F32) for q_ref in q_refs]
    if rope_q is not None:
        parts[rope_q] = _rope_apply(parts[rope_q], *[t[...] for t in tables])
    q = (jnp.concatenate(parts, axis=1) * (scale * LOG2E)).astype(kcat_ref.dtype)
    tiles = [slice(t * tq, (t + 1) * tq) for t in range(q.shape[0] // tq)]
    outs = _attend_tiles(kcat_ref, [q[sl] for sl in tiles], vt_ref, s_ref, e_ref)
    for sl, (o_t, l) in zip(tiles, outs):
        o_ref[sl, :] = (o_t * (1.0 / l)).T.astype(o_ref.dtype)


def _attention_call(kernel, q_list, k_list, v, extras, *, bsz, heads, s, q_range, k_range, tq, lanes, name,
                    q_row_tables=(), k_row_tables=()):
    q_start, q_len = q_range
    k_start, k_len = k_range
    nq = q_len // tq
    q_off = q_start // tq
    k_off = k_start // k_len
    assert q_start % tq == 0 and q_len % tq == 0 and k_start % k_len == 0

    def qspec(col0):
        return pl.BlockSpec((None, tq, HEAD_DIM), lambda b, h, i: (b, q_off + i, col0 + h))

    def kspec(col0, stride, shared=False):
        if shared:
            return pl.BlockSpec((None, k_len, HEAD_DIM), lambda b, h, i: (b, k_off, col0))
        return pl.BlockSpec((None, k_len, HEAD_DIM), lambda b, h, i: (b, k_off, col0 + stride * h))

    def per_sample(arr):
        return arr.reshape(bsz, s, arr.shape[-1])

    in_specs, args = [], []
    for arr, col0 in q_list:
        in_specs.append(qspec(col0)); args.append(per_sample(arr))
    for arr, col0, stride, shared in k_list:
        in_specs.append(kspec(col0, stride, shared)); args.append(per_sample(arr))
    arr, col0, stride = v
    in_specs.append(kspec(col0, stride)); args.append(per_sample(arr))
    for e in extras:
        in_specs.append(pl.BlockSpec(e.shape, lambda b, h, i: (0, 0))); args.append(e)
    for t in q_row_tables:
        in_specs.append(pl.BlockSpec((tq, LANE), lambda b, h, i: (q_off + i, 0))); args.append(t)
    for t in k_row_tables:
        in_specs.append(pl.BlockSpec((k_len, LANE), lambda b, h, i: (k_off, 0))); args.append(t)
    return pl.pallas_call(
        kernel,
        grid=(bsz, heads, nq),
        in_specs=in_specs,
        out_specs=pl.BlockSpec((tq, HEAD_DIM), lambda b, h, i: (b * nq + i, h)),
        out_shape=jax.ShapeDtypeStruct((bsz * q_len, heads * HEAD_DIM), BF16),
        scratch_shapes=[pltpu.VMEM((HEAD_DIM, k_len), BF16), pltpu.VMEM((2, k_len, lanes), F32),
                        pltpu.VMEM((2, k_len, lanes), BF16)]
        + ([pltpu.VMEM((k_len, len(k_list) * HEAD_DIM), BF16)] if len(k_list) > 1 else [])
        + ([pltpu.VMEM((k_len, HEAD_DIM), BF16)] if k_row_tables else []),
        compiler_params=_params("parallel", "parallel", "arbitrary"),
        name=name,
    )(*args)


def _attend_lat_and_ctx(call, n_lat, n_ctx, with_ctx):
    s = n_lat + n_ctx
    o_lat = call((0, n_lat), (0, s), "lat")
    if not with_ctx:
        return o_lat
    bsz = o_lat.shape[0] // n_lat
    o_ctx = call((n_lat, n_ctx), (n_lat, n_ctx), "ctx").reshape(bsz, n_ctx, -1)
    return jnp.concatenate([o_lat.reshape(bsz, n_lat, -1), o_ctx], axis=1).reshape(bsz * s, -1)


NA_STEP_ROWS = 8
NA_WIN_ROWS = 16


def _na_window_start(step, grid_rows):
    return jnp.clip(step * NA_STEP_ROWS - NA_ROWS // 2, 0, grid_rows - NA_WIN_ROWS)


def _na_bias_table(rpb, grid_rows):
    steps = grid_rows // NA_STEP_ROWS
    heads = rpb.shape[0]
    n_dr = 2 * NA_ROWS - 1
    col = jnp.arange(GRID_W, dtype=jnp.int32)
    cs = jnp.clip(col - NA_COLS // 2, 0, GRID_W - NA_COLS)
    colmask = (col[None, :] >= cs[:, None]) & (col[None, :] < cs[:, None] + NA_COLS)
    dc = jnp.clip(col[None, :] - col[:, None] + NA_COLS - 1, 0, 2 * NA_COLS - 2)
    pat = jnp.where(colmask[None, None], rpb[:, :, dc].astype(F32) * LOG2E, NEG_INF)
    blank = jnp.full((heads, 1, GRID_W, GRID_W), NEG_INF, F32)
    pairs = jnp.concatenate([jnp.concatenate([blank, pat], axis=1), jnp.concatenate([pat, blank], axis=1)], axis=-1)

    kind_steps = [0, min(1, steps - 1), steps - 1]
    plan = []
    for st in kind_steps:
        ws = min(max(st * NA_STEP_ROWS - NA_ROWS // 2, 0), grid_rows - NA_WIN_ROWS)
        rows = []
        for i in range(NA_STEP_ROWS):
            r = st * NA_STEP_ROWS + i
            rs = min(max(r - NA_ROWS // 2, 0), grid_rows - NA_ROWS)
            rows.append([(ws + jw - r + NA_ROWS - 1) if rs <= ws + jw < rs + NA_ROWS else None
                         for jw in range(NA_WIN_ROWS)])
        plan.append(rows)

    def expand_kernel(p_ref, o_ref):
        lane = lax.broadcasted_iota(jnp.int32, (GRID_W, 2 * GRID_W), 1)
        for kind, rows in enumerate(plan):
            @pl.when(pl.program_id(1) == kind)
            def _(rows=rows):
                for i, offs in enumerate(rows):
                    for jp in range(NA_WIN_ROWS // 2):
                        left, right = offs[2 * jp], offs[2 * jp + 1]
                        if left is None and right is None:
                            blk = jnp.full((GRID_W, 2 * GRID_W), NEG_INF, F32)
                        else:
                            blk = p_ref[right if right is not None else left + 1]
                            if left is None:
                                blk = jnp.where(lane >= GRID_W, blk, NEG_INF)
                            if right is None:
                                blk = jnp.where(lane < GRID_W, blk, NEG_INF)
                        o_ref[i * GRID_W:(i + 1) * GRID_W, 2 * jp * GRID_W:2 * (jp + 1) * GRID_W] = blk

    tq, win = NA_STEP_ROWS * GRID_W, NA_WIN_ROWS * GRID_W
    return pl.pallas_call(
        expand_kernel,
        grid=(heads, 3),
        in_specs=[pl.BlockSpec((None, n_dr + 1, GRID_W, 2 * GRID_W), lambda h, k: (h, 0, 0, 0))],
        out_specs=pl.BlockSpec((None, None, tq, win), lambda h, k: (h, k, 0, 0)),
        out_shape=jax.ShapeDtypeStruct((heads, 3, tq, win), F32),
        compiler_params=_params("parallel", "arbitrary"),
        name="na_bias_table",
    )(pairs)


def _na_kernel(q_ref, k_ref, v_ref, bias_ref, o_ref, *, n_lat, grid_rows):
    start = pl.multiple_of(_na_window_start(pl.program_id(2), grid_rows) * GRID_W, GRID_W)
    win = NA_WIN_ROWS * GRID_W
    q = (q_ref[...].astype(F32) * (HEAD_DIM ** -0.5 * LOG2E)).astype(k_ref.dtype)
    sl = lax.dot_general(q, k_ref[pl.ds(start, win), :], NT_DIMS, preferred_element_type=F32) + bias_ref[...]
    sc = lax.dot_general(q, k_ref[n_lat:, :], NT_DIMS, preferred_element_type=F32)
    m = jnp.maximum(jnp.max(sl, axis=-1, keepdims=True), jnp.max(sc, axis=-1, keepdims=True))
    el = jnp.exp2(sl - m)
    ec = jnp.exp2(sc - m)
    inv = 1.0 / (jnp.sum(el, axis=-1, keepdims=True) + jnp.sum(ec, axis=-1, keepdims=True))
    o = (jnp.dot(el.astype(v_ref.dtype), v_ref[pl.ds(start, win), :], preferred_element_type=F32)
         + jnp.dot(ec.astype(v_ref.dtype), v_ref[n_lat:, :], preferred_element_type=F32))
    o_ref[...] = (o * inv).astype(o_ref.dtype)


def neighbourhood_attention(pd, bias, bsz, heads, n_lat, n_ctx):
    s = n_lat + n_ctx
    grid_rows = n_lat // GRID_W
    tq = NA_STEP_ROWS * GRID_W
    win = NA_WIN_ROWS * GRID_W
    steps = grid_rows // NA_STEP_ROWS
    assert grid_rows >= NA_WIN_ROWS and grid_rows % NA_STEP_ROWS == 0 and steps >= 2
    assert NA_WIN_ROWS >= NA_STEP_ROWS + NA_ROWS - 1

    def kind(g):
        return jnp.where(g == 0, 0, jnp.where(g == steps - 1, 2, 1))

    pd3 = pd.reshape(bsz, s, pd.shape[-1])
    return pl.pallas_call(
        functools.partial(_na_kernel, n_lat=n_lat, grid_rows=grid_rows),
        grid=(bsz, heads, steps),
        in_specs=[
            pl.BlockSpec((None, tq, HEAD_DIM), lambda b, h, g: (b, g, h)),
            pl.BlockSpec((None, s, HEAD_DIM), lambda b, h, g: (b, 0, heads + h)),
            pl.BlockSpec((None, s, HEAD_DIM), lambda b, h, g: (b, 0, 2 * heads + h)),
            pl.BlockSpec((None, None, tq, win), lambda b, h, g: (h, kind(g), 0, 0)),
        ],
        out_specs=pl.BlockSpec((tq, HEAD_DIM), lambda b, h, g: (b * steps + g, h)),
        out_shape=jax.ShapeDtypeStruct((bsz * n_lat, heads * HEAD_DIM), BF16),
        compiler_params=_params("parallel", "parallel", "arbitrary"),
        name="neighbourhood_attention",
    )(pd3, pd3, pd3, bias)


def _hgrn_constants(reverse):
    c = CHUNK
    t = np.arange(c)
    tri = (t[None, :] >= t[:, None]) if reverse else (t[None, :] <= t[:, None])
    mats = [tri.astype(np.float32)]
    masks = [np.eye(c, dtype=np.float32)]
    m = c // 2
    while m >= 1:
        node = (t // (2 * m)) * (2 * m)
        ref = node + (m if reverse else m - 1)
        if m < HGRN_BROADCAST_MIN:
            mats.append(tri[ref].astype(np.float32))
        upper = (t % (2 * m)) >= m
        same = node[:, None] == node[None, :]
        pair = same & (upper[:, None] & ~upper[None, :])
        masks.append((pair.T if reverse else pair).astype(np.float32))
        m //= 2
    return np.concatenate(mats, axis=0), np.stack(masks)


HGRN_BROADCAST_MIN = 8


def _hgrn_reference_rows(cum, cs, lev, reverse):
    c = CHUNK
    m = c >> lev
    if m < HGRN_BROADCAST_MIN:
        k = lev - int(math.log2(c // HGRN_BROADCAST_MIN))
        return cs[k * c:(k + 1) * c]
    rows = []
    for node in range(0, c, 2 * m):
        ref = node + (m if reverse else m - 1)
        rows.append(jnp.broadcast_to(cum[ref:ref + 1, :], (2 * m, cum.shape[1])))
    return rows[0] if len(rows) == 1 else jnp.concatenate(rows, axis=0)


def _hgrn_kernel(cq_ref, ci_ref, z_ref, lb_ref, ms_ref, mk_ref, o_ref, st_ref, *, heads, reverse):
    @pl.when(pl.program_id(1) == 0)
    def _():
        st_ref[...] = jnp.zeros_like(st_ref)

    c = CHUNK
    n_levels = mk_ref.shape[0]
    ms = ms_ref[...].astype(BF16)
    scale = HEAD_DIM ** -0.5
    tot_row = 0 if reverse else c - 1
    for h in range(heads):
        sl = slice(h * HEAD_DIM, (h + 1) * HEAD_DIM)
        lb = lb_ref[:, sl]
        f = lb + (1.0 - lb) * _sigmoid(z_ref[:, sl].astype(F32))
        kk = 1.0 - f
        g = jnp.log(f)
        cq = cq_ref[:, sl].astype(F32)
        qq = cq * _sigmoid(cq) * scale
        v = ci_ref[:, sl]
        g_hi = g.astype(BF16)
        r1 = g - g_hi.astype(F32)
        g_mid = r1.astype(BF16)
        g_lo = (r1 - g_mid.astype(F32)).astype(BF16)
        cs3 = jnp.dot(ms, jnp.concatenate([g_hi, g_mid, g_lo], axis=1), preferred_element_type=F32)
        cs = cs3[:, :HEAD_DIM] + cs3[:, HEAD_DIM:2 * HEAD_DIM] + cs3[:, 2 * HEAD_DIM:]
        cum = cs[:c]
        tot = cs[tot_row:tot_row + 1]
        st = st_ref[h]
        o = lax.dot_general((qq * jnp.exp(cum)).astype(BF16), st.astype(BF16), NT_DIMS,
                            preferred_element_type=F32)
        a = mk_ref[0] * lax.dot_general(qq.astype(BF16), kk.astype(BF16), NT_DIMS,
                                        preferred_element_type=F32)
        for lev in range(1, n_levels):
            e = jnp.exp(-jnp.abs(cum - _hgrn_reference_rows(cum, cs, lev, reverse)))
            a = a + mk_ref[lev] * lax.dot_general((qq * e).astype(BF16), (kk * e).astype(BF16), NT_DIMS,
                                                  preferred_element_type=F32)
        o = o + jnp.dot(a.astype(BF16), v, preferred_element_type=F32)
        o_ref[:, sl] = o
        ke = (kk * jnp.exp(tot - cum)).astype(BF16)
        st_ref[h] = st * jnp.exp(tot) + lax.dot_general(v, ke, TN_DIMS, preferred_element_type=F32)


def hgrn_scan(pc, lb, bsz, heads, n_lat, n_ctx, reverse):
    s = n_lat + n_ctx
    chunks = s // CHUNK
    lat_chunks = n_lat // CHUNK
    width = heads * HEAD_DIM
    ms, mk = _hgrn_constants(reverse)

    def chunk_of(step):
        return (chunks - 1 - step) if reverse else lax.rem(step + lat_chunks, chunks)

    def spec(col):
        return pl.BlockSpec((CHUNK, width), lambda b, t: (b * chunks + chunk_of(t), col))

    return pl.pallas_call(
        functools.partial(_hgrn_kernel, heads=heads, reverse=reverse),
        grid=(bsz, chunks),
        in_specs=[spec(0), spec(1), spec(3 if reverse else 2),
                  pl.BlockSpec((1, width), lambda b, t: (0, 0)),
                  pl.BlockSpec(ms.shape, lambda b, t: (0, 0)),
                  pl.BlockSpec(mk.shape, lambda b, t: (0, 0, 0))],
        out_specs=spec(0),
        out_shape=jax.ShapeDtypeStruct((bsz * s, width), F32),
        scratch_shapes=[pltpu.VMEM((heads, HEAD_DIM, HEAD_DIM), F32)],
        compiler_params=_params("parallel", "arbitrary"),
        name="hgrn_scan_bwd" if reverse else "hgrn_scan_fwd",
    )(pc, pc, pc, lb.reshape(1, width), jnp.asarray(ms), jnp.asarray(mk))


def _hgrn_readout_kernel(of_ref, ob_ref, gate_ref, g_ref, o_ref):
    gain = g_ref[...]
    for h in range(of_ref.shape[1] // HEAD_DIM):
        sl = slice(h * HEAD_DIM, (h + 1) * HEAD_DIM)
        gate = gate_ref[:, sl].astype(F32)
        o = _rms(of_ref[:, sl] + ob_ref[:, sl], gain)
        o_ref[:, sl] = (o * (gate * _sigmoid(gate))).astype(o_ref.dtype)


def hgrn_readout(o_fwd, o_bwd, pc, gain, bsz, s, rows, tr=256):
    width = o_fwd.shape[1]
    in_tiles, out_tiles = s // tr, rows // tr
    ispec = pl.BlockSpec((tr, width), lambda b, i: (b * in_tiles + i, 0))
    return pl.pallas_call(
        _hgrn_readout_kernel,
        grid=(bsz, out_tiles),
        in_specs=[ispec, ispec, pl.BlockSpec((tr, width), lambda b, i: (b * in_tiles + i, 4)),
                  pl.BlockSpec((1, HEAD_DIM), lambda b, i: (0, 0))],
        out_specs=pl.BlockSpec((tr, width), lambda b, i: (b * out_tiles + i, 0)),
        out_shape=jax.ShapeDtypeStruct((bsz * rows, width), BF16),
        compiler_params=_params("parallel", "parallel"),
        name="hgrn_readout",
    )(o_fwd, o_bwd, pc, gain.reshape(1, HEAD_DIM))


def _mla_prep_kernel(p_ref, qg_ref, kg_ref, cos_ref, sn_ref, sp_ref, qn_ref, cn_ref, kr_ref, *, q_rank, kv_rank):
    qn_ref[...] = _rms(p_ref[:, :q_rank].astype(F32), qg_ref[...]).astype(qn_ref.dtype)
    cn_ref[...] = _rms(p_ref[:, q_rank:q_rank + kv_rank].astype(F32), kg_ref[...]).astype(cn_ref.dtype)
    kr = p_ref[:, q_rank + kv_rank:q_rank + kv_rank + LANE].astype(F32)
    kr_ref[...] = _rope_apply(kr, cos_ref[...], sn_ref[...], sp_ref[...]).astype(kr_ref.dtype)


def mla_prep(pb, q_gain, kv_gain, tables, s, tr=256):
    r, width = pb.shape
    q_rank, kv_rank = q_gain.shape[0], kv_gain.shape[0]
    assert width >= q_rank + kv_rank + LANE and q_rank % LANE == 0 and kv_rank % LANE == 0
    tiles_per_seq = s // tr
    tspec = pl.BlockSpec((tr, LANE), lambda i: (i % tiles_per_seq, 0))

    def ospec(w):
        return pl.BlockSpec((tr, w), lambda i: (i, 0))

    return pl.pallas_call(
        functools.partial(_mla_prep_kernel, q_rank=q_rank, kv_rank=kv_rank),
        grid=(r // tr,),
        in_specs=[ospec(width), pl.BlockSpec((1, q_rank), lambda i: (0, 0)),
                  pl.BlockSpec((1, kv_rank), lambda i: (0, 0)), tspec, tspec, tspec],
        out_specs=[ospec(q_rank), ospec(kv_rank), ospec(LANE)],
        out_shape=[jax.ShapeDtypeStruct((r, q_rank), BF16), jax.ShapeDtypeStruct((r, kv_rank), BF16),
                   jax.ShapeDtypeStruct((r, LANE), BF16)],
        compiler_params=_params("parallel"),
        name="mla_prep",
    )(pb, q_gain.reshape(1, q_rank), kv_gain.reshape(1, kv_rank), *tables)


def _merge_kernel(oa_ref, ob_ref, oc_ref, od_ref, ga_ref, gb_ref, gc_ref, gd_ref, w_ref, o_ref):
    acc = None
    for i, (o_ref_i, g_ref_i) in enumerate(((oa_ref, ga_ref), (ob_ref, gb_ref), (oc_ref, gc_ref), (od_ref, gd_ref))):
        y = jnp.dot(o_ref_i[...], w_ref[i], preferred_element_type=F32)
        term = _sigmoid(g_ref_i[...].astype(F32)) * y
        acc = term if acc is None else acc + term
    o_ref[...] = acc.astype(o_ref.dtype)


def merge_branches(outs, gate_logits, w_branch):
    r, c = outs[0].shape
    d = w_branch.shape[2]
    bm = _pick(r, (512, 256, 128))
    bn = _pick(d, (1024, 512, 256, 128))
    nb = d // bn
    ospec = pl.BlockSpec((bm, c), lambda i, j: (i, 0))

    def gspec(branch):
        return pl.BlockSpec((bm, bn), lambda i, j: (i, branch * nb + j))

    return pl.pallas_call(
        _merge_kernel,
        grid=(r // bm, nb),
        in_specs=[ospec] * N_BRANCH + [gspec(i) for i in range(N_BRANCH)]
        + [pl.BlockSpec((N_BRANCH, c, bn), lambda i, j: (0, 0, j))],
        out_specs=pl.BlockSpec((bm, bn), lambda i, j: (i, j)),
        out_shape=jax.ShapeDtypeStruct((r, d), BF16),
        compiler_params=_params("parallel", "parallel"),
        name="merge_branches",
    )(*outs, gate_logits, gate_logits, gate_logits, gate_logits, w_branch)


HALO = 16


FFN_BN = 640


def _ffn_tile_pairs(a, bn):
    f = a.shape[-1] // 2
    lead = a.shape[:-1]
    pairs = jnp.stack([a[..., :f].reshape(*lead, f // bn, bn), a[..., f:].reshape(*lead, f // bn, bn)], axis=-2)
    return pairs.reshape(*lead, 2 * f)


def _cast_pair_kernel(g_ref, v_ref, o_ref):
    bn = g_ref.shape[1]
    o_ref[:, :bn] = g_ref[...].astype(o_ref.dtype)
    o_ref[:, bn:] = v_ref[...].astype(o_ref.dtype)


def ffn_up_weight(w_up, layer, bn):
    _, d, two_f = w_up.shape
    nb = two_f // 2 // bn
    tk = _pick(d, (1024, 512, 256, 128))
    return pl.pallas_call(
        _cast_pair_kernel,
        grid=(d // tk, nb),
        in_specs=[pl.BlockSpec((None, tk, bn), lambda i, j: (layer, i, j)),
                  pl.BlockSpec((None, tk, bn), lambda i, j: (layer, i, nb + j))],
        out_specs=pl.BlockSpec((tk, 2 * bn), lambda i, j: (i, j)),
        out_shape=jax.ShapeDtypeStruct((d, two_f), BF16),
        compiler_params=_params("parallel", "parallel"),
        name="ffn_up_weight",
    )(w_up, w_up)


FFN_EPILOGUE_ROWS = 256


def _ffn_up_kernel(h_ref, hp_ref, hn_ref, w_ref, cw_ref, cb_ref, o_ref, hbuf_ref, u_ref, *, bm, s, n_lat):
    i = pl.program_id(0)
    bn = o_ref.shape[1]

    @pl.when(pl.program_id(1) == 0)
    def _():
        hbuf_ref[0:HALO, :] = hp_ref[...]
        hbuf_ref[HALO:HALO + bm, :] = h_ref[...]
        hbuf_ref[HALO + bm:, :] = hn_ref[...]

    u_ref[...] = jnp.dot(hbuf_ref[...], w_ref[...], preferred_element_type=F32)
    cw = cw_ref[...]
    cb = cb_ref[...]
    er = min(FFN_EPILOGUE_ROWS, bm)
    for c in range(bm // er):
        r0 = HALO + c * er
        pos = lax.rem(i * bm + c * er + lax.broadcasted_iota(jnp.int32, (er, 1), 0), s)
        has_prev = jnp.where((pos == 0) | (pos == n_lat), 0.0, 1.0)
        has_next = jnp.where((pos == n_lat - 1) | (pos == s - 1), 0.0, 1.0)
        win = u_ref[r0 - 8:r0 + er + 8, :]
        prev = pltpu.roll(win, 1, 0)[8:8 + er]
        nxt = pltpu.roll(win, er + 16 - 1, 0)[8:8 + er]
        y = prev * has_prev * cw[0:1] + win[8:8 + er] * cw[1:2] + nxt * has_next * cw[2:3] + cb
        gate, val = y[:, :bn], y[:, bn:]
        o_ref[c * er:(c + 1) * er, :] = (gate * _sigmoid(gate) * val).astype(o_ref.dtype)


def ffn_up(h, w_up, conv_w, conv_b, s, n_lat):
    r, d = h.shape
    f = w_up.shape[1] // 2
    bm = _pick(r, (1024, 512, 256, 128))
    bn = FFN_BN
    assert f % bn == 0
    halo_blocks = r // HALO
    per_tile = bm // HALO
    return pl.pallas_call(
        functools.partial(_ffn_up_kernel, bm=bm, s=s, n_lat=n_lat),
        grid=(r // bm, f // bn),
        in_specs=[
            pl.BlockSpec((bm, d), lambda i, j: (i, 0), pipeline_mode=pl.Buffered(1)),
            pl.BlockSpec((HALO, d), lambda i, j: (jnp.maximum(i * per_tile - 1, 0), 0)),
            pl.BlockSpec((HALO, d), lambda i, j: (jnp.minimum((i + 1) * per_tile, halo_blocks - 1), 0)),
            pl.BlockSpec((d, 2 * bn), lambda i, j: (0, j)),
            pl.BlockSpec((3, 2 * bn), lambda i, j: (0, j)),
            pl.BlockSpec((1, 2 * bn), lambda i, j: (0, j)),
        ],
        out_specs=pl.BlockSpec((bm, bn), lambda i, j: (i, j)),
        out_shape=jax.ShapeDtypeStruct((r, f), BF16),
        scratch_shapes=[pltpu.VMEM((bm + 2 * HALO, d), BF16), pltpu.VMEM((bm + 2 * HALO, 2 * bn), F32)],
        compiler_params=_params("parallel", "arbitrary"),
        name="ffn_up",
    )(h, h, h, w_up, conv_w, conv_b)


def _split_w_in(w, d, heads):
    mix = heads * HEAD_DIM
    q_rank = 3 * d // 16
    kv_rank = d // 8
    sizes = [('a', 3 * mix), ('b_qa', q_rank), ('b_kva', kv_rank + MLA_ROPE), ('c', 5 * mix), ('d', 3 * mix),
             ('gate', N_BRANCH * d)]
    out, start = {}, 0
    for name, size in sizes:
        out[name] = (start, size)
        start += size
    assert start == w.shape[1]

    def cols(name):
        a, n = out[name]
        return w[:, a:a + n].astype(BF16)

    used = q_rank + kv_rank + LANE
    pad = -used % 512 + LANE - MLA_ROPE
    w_b = jnp.concatenate([cols('b_qa'), cols('b_kva'), jnp.zeros((d, pad), BF16)], axis=1)
    return cols('a'), w_b, cols('c'), cols('d'), cols('gate'), q_rank, kv_rank


def _mla_q_weight(w_qb, heads):
    rq = w_qb.shape[0]
    w = w_qb.reshape(rq, heads, MLA_NOPE + MLA_ROPE)
    nope = w[:, :, :MLA_NOPE].reshape(rq, heads * MLA_NOPE)
    ropep = jnp.pad(w[:, :, MLA_NOPE:], ((0, 0), (0, 0), (0, LANE - MLA_ROPE))).reshape(rq, heads * LANE)
    return jnp.concatenate([nope, ropep], axis=1).astype(BF16)


def kernel(x, c, ctx, c_ctx, ada_w, ada_b, norm_mix_pre, norm_mix_post, norm_ffn_pre, norm_ffn_post, w_in,
           diff_lambda, diff_subln, mla_q_norm, mla_w_qb, mla_kv_norm, mla_w_kvb, hgrn_lb_logits, hgrn_norm,
           na_rpb, w_branch, w_out, ffn_w_up, ffn_conv_w, ffn_conv_b, ffn_w_down):
    bsz, n_lat, d = x.shape
    n_ctx = ctx.shape[1]
    s = n_lat + n_ctx
    r = bsz * s
    depth = ada_w.shape[0]
    heads = d // (N_BRANCH * HEAD_DIM)
    mix = heads * HEAD_DIM
    assert n_lat % 256 == 0 and n_ctx % 256 == 0 and s % n_ctx == 0 and n_lat % GRID_W == 0

    tables = _rope_tables(n_lat, n_ctx)
    lb_sm = jax.nn.softmax(hgrn_lb_logits.astype(F32), axis=0)
    lower_bounds = jnp.cumsum(lb_sm, axis=0) - lb_sm[0]

    c_rows = jnp.zeros((8, d), F32).at[:bsz].set(c).at[bsz].set(c_ctx)
    xa = jnp.concatenate([x, ctx], axis=1)

    def mod_rows(mod, idx):
        part = mod[:, idx * d:(idx + 1) * d]
        both = jnp.stack([part[:bsz], jnp.broadcast_to(part[bsz], (bsz, d))], axis=1)
        return both.reshape(2 * bsz, 1, d)

    mods = []
    for l in range(depth):
        mod = ada_modulation(c_rows, ada_w, ada_b, l)
        mods.append([mod_rows(mod, i) for i in range(6)])

    h = norm_modulate(xa, norm_mix_pre[0], mods[0][0], mods[0][1], n_lat)
    for l in range(depth):
        with_ctx = l < depth - 1
        sh1, sc1, g1, sh2, sc2, g2 = mods[l]
        h2d = h.reshape(r, d)
        rows = s if with_ctx else n_lat

        w_a, w_b, w_c, w_d, w_g, q_rank, kv_rank = _split_w_in(w_in[l], d, heads)
        pa = matmul(h2d, w_a, BF16, "in_proj_a")
        pb = matmul(h2d, w_b, BF16, "in_proj_b")
        pc = matmul(h2d, w_c, BF16, "in_proj_c")
        pd = matmul(h2d, w_d, BF16, "in_proj_d")
        if with_ctx:
            pg = matmul(h2d, w_g, BF16, "in_proj_gate")
        else:
            pg = matmul_leading_rows(h, w_g, rows, BF16, "in_proj_gate")

        lam_init = 0.8 - 0.6 * math.exp(-0.3 * l)
        extras_a = [diff_lambda[l].astype(F32), diff_subln[l].reshape(1, HEAD_DIM).astype(F32)]

        def call_a(q_range, k_range, tag):
            return _attention_call(
                functools.partial(_diff_attn_kernel, lam_init=lam_init),
                [(pa, 0)], [(pa, heads, 1, False)], (pa, 2 * heads, 1), extras_a,
                bsz=bsz, heads=heads, s=s, q_range=q_range, k_range=k_range, tq=min(1024, q_range[1]), lanes=256,
                name="diff_attn_" + tag, q_row_tables=tables, k_row_tables=tables)

        oa = _attend_lat_and_ctx(call_a, n_lat, n_ctx, with_ctx)

        qn, cn, kr = mla_prep(pb, mla_q_norm[l], mla_kv_norm[l], tables, s)
        qb = matmul(qn, _mla_q_weight(mla_w_qb[l], heads), BF16, "mla_q_up")
        kvb = matmul(cn, mla_w_kvb[l].astype(BF16), BF16, "mla_kv_up")

        def call_b(q_range, k_range, tag):
            return _attention_call(
                functools.partial(_sdpa_kernel, n_qk=2, scale=(MLA_NOPE + MLA_ROPE) ** -0.5, rope_q=1),
                [(qb, 0), (qb, heads)], [(kvb, 0, 2, False), (kr, 0, 0, True)], (kvb, 1, 2), [],
                bsz=bsz, heads=heads, s=s, q_range=q_range, k_range=k_range, tq=min(2048, q_range[1]), lanes=256,
                name="mla_attn_" + tag, q_row_tables=tables)

        ob = _attend_lat_and_ctx(call_b, n_lat, n_ctx, with_ctx)

        o_fwd = hgrn_scan(pc, lower_bounds[l, 0], bsz, heads, n_lat, n_ctx, reverse=False)
        o_bwd = hgrn_scan(pc, lower_bounds[l, 1], bsz, heads, n_lat, n_ctx, reverse=True)
        oc = hgrn_readout(o_fwd, o_bwd, pc, hgrn_norm[l], bsz, s, rows)

        od = neighbourhood_attention(pd, _na_bias_table(na_rpb[l], n_lat // GRID_W), bsz, heads, n_lat, n_ctx)
        if with_ctx:
            od_ctx = _attention_call(
                functools.partial(_sdpa_kernel, n_qk=1, scale=HEAD_DIM ** -0.5, rope_q=None),
                [(pd, 0)], [(pd, heads, 1, False)], (pd, 2 * heads, 1), [],
                bsz=bsz, heads=heads, s=s, q_range=(n_lat, n_ctx), k_range=(n_lat, n_ctx), tq=256, lanes=256,
                name="na_ctx_attn").reshape(bsz, n_ctx, mix)
            od = jnp.concatenate([od.reshape(bsz, n_lat, mix), od_ctx], axis=1).reshape(r, mix)

        merged = merge_branches((oa, ob, oc, od), pg, w_branch[l].astype(BF16))
        mix_out = matmul(merged, w_out[l].astype(BF16), BF16, "out_proj").reshape(bsz, rows, d)
        xa, h2 = residual_norm_modulate(xa, mix_out, g1, norm_mix_post[l], norm_ffn_pre[l], sh2, sc2, n_lat)

        gv = ffn_up(h2.reshape(bsz * rows, d), ffn_up_weight(ffn_w_up, l, FFN_BN),
                    _ffn_tile_pairs(ffn_conv_w[l], FFN_BN), _ffn_tile_pairs(ffn_conv_b[l][None], FFN_BN),
                    rows, n_lat)
        ffn = matmul(gv, ffn_w_down[l].astype(BF16), BF16, "ffn_down").reshape(bsz, rows, d)
        if with_ctx:
            xa, h = residual_norm_modulate(xa, ffn, g2, norm_ffn_post[l], norm_mix_pre[l + 1],
                                           mods[l + 1][0], mods[l + 1][1], n_lat)
        else:
            xa = residual(xa, ffn, g2, norm_ffn_post[l], n_lat)
    return xa
```

```python
import functools
import math

import numpy as np
import jax
import jax.numpy as jnp
from jax import lax
from jax.experimental import pallas as pl
from jax.experimental.pallas import tpu as pltpu

F32 = jnp.float32
BF16 = jnp.bfloat16

HEAD_DIM = 128
QK_HALF = 64
MLA_NOPE = 128
MLA_ROPE = 64
MLA_V = 128
ROPE_DIM = 64
ROPE_THETA = 10000.0
GRID_W = 64
NA_ROWS = 8
NA_COLS = 16
CHUNK = 128
NORM_EPS = 1e-6
NEG_INF = -1e30
N_BRANCH = 4

LANE = 128
VMEM_LIMIT_BYTES = 56 * 1024 * 1024

NT_DIMS = (((1,), (1,)), ((), ()))
TN_DIMS = (((0,), (0,)), ((), ()))


def _params(*sem):
    return pltpu.CompilerParams(dimension_semantics=sem, vmem_limit_bytes=VMEM_LIMIT_BYTES)


def _sigmoid(x):
    return 0.5 * jnp.tanh(0.5 * x) + 0.5


def _pick(total, candidates):
    for c in candidates:
        if total % c == 0:
            return c
    raise ValueError(f"no tile for {total} in {candidates}")


def _ada_kernel(c_ref, w_ref, b_ref, o_ref):
    c = c_ref[...]
    a = (c * _sigmoid(c)).astype(BF16)
    o_ref[...] = jnp.dot(a, w_ref[...].astype(BF16), preferred_element_type=F32) + b_ref[...]


def ada_modulation(c_rows, ada_w, ada_b, layer):
    rows, d = c_rows.shape
    width = ada_w.shape[2]
    tn = _pick(width, (512, 256, 128))
    b3 = ada_b.reshape(ada_b.shape[0], 1, width)
    return pl.pallas_call(
        _ada_kernel,
        grid=(width // tn,),
        in_specs=[
            pl.BlockSpec((rows, d), lambda j: (0, 0)),
            pl.BlockSpec((None, d, tn), lambda j: (layer, 0, j)),
            pl.BlockSpec((None, 1, tn), lambda j: (layer, 0, j)),
        ],
        out_specs=pl.BlockSpec((rows, tn), lambda j: (0, j)),
        out_shape=jax.ShapeDtypeStruct((rows, width), F32),
        compiler_params=_params("parallel"),
        name="ada_modulation",
    )(c_rows, ada_w, b3)


def _rms(x, gain):
    return x * lax.rsqrt(jnp.mean(x * x, axis=-1, keepdims=True) + NORM_EPS) * gain


def _norm_mod_kernel(x_ref, g_ref, sh_ref, sc_ref, h_ref):
    h = _rms(x_ref[0], g_ref[...])
    h_ref[0] = (h * (1.0 + sc_ref[0]) + sh_ref[0]).astype(h_ref.dtype)


def _resid_kernel(x_ref, y_ref, gate_ref, gpost_ref, xo_ref):
    xo_ref[0] = x_ref[0] + gate_ref[0] * _rms(y_ref[0].astype(F32), gpost_ref[...])


def _resid_norm_mod_kernel(x_ref, y_ref, gate_ref, gpost_ref, gpre_ref, sh_ref, sc_ref, xo_ref, h_ref):
    xn = x_ref[0] + gate_ref[0] * _rms(y_ref[0].astype(F32), gpost_ref[...])
    xo_ref[0] = xn
    h = _rms(xn, gpre_ref[...])
    h_ref[0] = (h * (1.0 + sc_ref[0]) + sh_ref[0]).astype(h_ref.dtype)


def _row_specs(dims, tr, n_lat_tiles):
    s, d = dims
    xspec = pl.BlockSpec((1, tr, d), lambda b, t: (b, t, 0))
    gspec = pl.BlockSpec((1, d), lambda b, t: (0, 0))
    mspec = pl.BlockSpec((1, 1, d), lambda b, t: (2 * b + t // n_lat_tiles, 0, 0))
    return xspec, gspec, mspec


def norm_modulate(x, gain, shift, scale, n_lat, tr=256):
    bsz, s, d = x.shape
    xspec, gspec, mspec = _row_specs((s, d), tr, n_lat // tr)
    return pl.pallas_call(
        _norm_mod_kernel,
        grid=(bsz, s // tr),
        in_specs=[xspec, gspec, mspec, mspec],
        out_specs=xspec,
        out_shape=jax.ShapeDtypeStruct(x.shape, BF16),
        compiler_params=_params("parallel", "parallel"),
        name="norm_modulate",
    )(x, gain.reshape(1, d), shift, scale)


def residual(x, y, gate, gain_post, n_lat, tr=256):
    bsz, s, d = x.shape
    rows = y.shape[1]
    xspec, gspec, mspec = _row_specs((s, d), tr, n_lat // tr)
    return pl.pallas_call(
        _resid_kernel,
        grid=(bsz, rows // tr),
        in_specs=[xspec, xspec, mspec, gspec],
        out_specs=xspec,
        out_shape=jax.ShapeDtypeStruct(y.shape, F32),
        compiler_params=_params("parallel", "parallel"),
        name="residual",
    )(x, y, gate, gain_post.reshape(1, d))


def residual_norm_modulate(x, y, gate, gain_post, gain_pre, shift, scale, n_lat, tr=256):
    bsz, s, d = x.shape
    rows = y.shape[1]
    xspec, gspec, mspec = _row_specs((s, d), tr, n_lat // tr)
    return pl.pallas_call(
        _resid_norm_mod_kernel,
        grid=(bsz, rows // tr),
        in_specs=[xspec, xspec, mspec, gspec, gspec, mspec, mspec],
        out_specs=[xspec, xspec],
        out_shape=[jax.ShapeDtypeStruct(y.shape, F32), jax.ShapeDtypeStruct(y.shape, BF16)],
        compiler_params=_params("parallel", "parallel"),
        name="residual_norm_modulate",
    )(x, y, gate, gain_post.reshape(1, d), gain_pre.reshape(1, d), shift, scale)


def _mm_kernel(a_ref, b_ref, o_ref):
    o_ref[...] = jnp.dot(a_ref[...], b_ref[...], preferred_element_type=F32).astype(o_ref.dtype)


def _mm_acc_kernel(a_ref, b_ref, o_ref, acc_ref):
    k = pl.program_id(2)

    @pl.when(k == 0)
    def _():
        acc_ref[...] = jnp.zeros_like(acc_ref)

    acc_ref[...] += jnp.dot(a_ref[...], b_ref[...], preferred_element_type=F32)

    @pl.when(k == pl.num_programs(2) - 1)
    def _():
        o_ref[...] = acc_ref[...].astype(o_ref.dtype)


MAX_SINGLE_K = 4096
MATMUL_VMEM_BUDGET = VMEM_LIMIT_BYTES * 9 // 10


def matmul_leading_rows(a, b, rows, out_dtype, name):
    bsz, _, k = a.shape
    n = b.shape[1]
    assert k <= MAX_SINGLE_K
    bm = _pick(rows, (1024, 512, 256, 128))
    bn = _pick(n, (1024, 768, 640, 512, 384, 256, 128))
    tiles = rows // bm
    return pl.pallas_call(
        _mm_kernel,
        grid=(bsz, tiles, n // bn),
        in_specs=[pl.BlockSpec((None, bm, k), lambda s, i, j: (s, i, 0)),
                  pl.BlockSpec((k, bn), lambda s, i, j: (0, j))],
        out_specs=pl.BlockSpec((bm, bn), lambda s, i, j: (s * tiles + i, j)),
        out_shape=jax.ShapeDtypeStruct((bsz * rows, n), out_dtype),
        compiler_params=_params("parallel", "parallel", "parallel"),
        name=name,
    )(a, b)


def matmul(a, b, out_dtype, name):
    m, k = a.shape
    n = b.shape[1]
    out_bytes = jnp.dtype(out_dtype).itemsize
    full_k = None
    for cap in (1024, 512):
        bm = _pick(m, tuple(c for c in (1024, 512, 256, 128) if c <= cap))
        bn = _pick(n, tuple(c for c in (1024, 768, 640, 512, 384, 256, 128) if c <= cap))
        need = 2 * 2 * (bm * k + k * bn) + 2 * bm * bn * out_bytes + 4 * bm * bn
        if need <= MATMUL_VMEM_BUDGET:
            full_k = (bm, bn)
            break
    if full_k is not None:
        bm, bn = full_k
        return pl.pallas_call(
            _mm_kernel,
            grid=(m // bm, n // bn),
            in_specs=[pl.BlockSpec((bm, k), lambda i, j: (i, 0)),
                      pl.BlockSpec((k, bn), lambda i, j: (0, j))],
            out_specs=pl.BlockSpec((bm, bn), lambda i, j: (i, j)),
            out_shape=jax.ShapeDtypeStruct((m, n), out_dtype),
            compiler_params=_params("parallel", "parallel"),
            name=name,
        )(a, b)
    bm = _pick(m, (1024, 512, 256, 128))
    bn = _pick(n, (1024, 768, 640, 512, 384, 256, 128))
    nk = -(-k // MAX_SINGLE_K)
    while k % nk or (k // nk) % LANE:
        nk += 1
    bk = k // nk
    return pl.pallas_call(
        _mm_acc_kernel,
        grid=(m // bm, n // bn, nk),
        in_specs=[pl.BlockSpec((bm, bk), lambda i, j, kk: (i, kk)),
                  pl.BlockSpec((bk, bn), lambda i, j, kk: (kk, j))],
        out_specs=pl.BlockSpec((bm, bn), lambda i, j, kk: (i, j)),
        out_shape=jax.ShapeDtypeStruct((m, n), out_dtype),
        scratch_shapes=[pltpu.VMEM((bm, bn), F32)],
        compiler_params=_params("parallel", "parallel", "arbitrary"),
        name=name,
    )(a, b)


def _rope_tables(n_lat, n_ctx):
    t = jnp.arange(n_lat, dtype=jnp.int32)
    row = (t // GRID_W).astype(F32)
    col = (t % GRID_W).astype(F32)
    quarter = ROPE_DIM // 4
    inv = ROPE_THETA ** (-jnp.arange(quarter, dtype=F32) / quarter)
    ar = row[:, None] * inv
    ac = col[:, None] * inv
    ang = jnp.concatenate([ar, ar, ac, ac], axis=-1)
    cos = jnp.concatenate([jnp.cos(ang), jnp.ones((n_ctx, ROPE_DIM), F32)], axis=0)
    sin = jnp.concatenate([jnp.sin(ang), jnp.zeros((n_ctx, ROPE_DIM), F32)], axis=0)
    first = (jnp.arange(ROPE_DIM) // quarter) % 2 == 0
    sin_next = jnp.where(first, -sin, 0.0)
    sin_prev = jnp.where(first, 0.0, sin)
    reps = LANE // ROPE_DIM
    return tuple(jnp.tile(a, (1, reps)) for a in (cos, sin_next, sin_prev))


def _rope_apply(x, cos, sin_next, sin_prev):
    quarter = ROPE_DIM // 4
    return (x * cos + pltpu.roll(x, LANE - quarter, 1) * sin_next
            + pltpu.roll(x, quarter, 1) * sin_prev)


LOG2E = math.log2(math.e)
KEY_CHUNK = 256


def _attend_tiles(k_ref, q_tiles, vt_ref, s_ref, e_ref):
    m_keys, n = s_ref.shape[1:]
    kc = min(KEY_CHUNK, m_keys)
    chunks = [slice(c * kc, (c + 1) * kc) for c in range(m_keys // kc)]

    def scores(t):
        s_ref[t % 2] = lax.dot_general(k_ref[...], q_tiles[t], NT_DIMS, preferred_element_type=F32)

    def column_max(t):
        m8 = None
        for sl in chunks:
            part = jnp.max(s_ref[t % 2, sl, :].reshape(kc // 8, 8, n), axis=0)
            m8 = part if m8 is None else jnp.maximum(m8, part)
        return jnp.max(m8, axis=0, keepdims=True)

    scores(0)
    m = column_max(0)
    outs = []
    for t in range(len(q_tiles)):
        if t + 1 < len(q_tiles):
            scores(t + 1)
        l8 = jnp.zeros((8, n), F32)
        for sl in chunks:
            e = jnp.exp2(s_ref[t % 2, sl, :] - m)
            l8 = l8 + jnp.sum(e.reshape(kc // 8, 8, n), axis=0)
            e_ref[t % 2, sl, :] = e.astype(e_ref.dtype)
        o_t = jnp.dot(vt_ref[...], e_ref[t % 2], preferred_element_type=F32)
        outs.append((o_t, jnp.sum(l8, axis=0, keepdims=True)))
        if t + 1 < len(q_tiles):
            m = column_max(t + 1)
    return outs


def _load_v_transposed(v_ref, vt_ref):
    @pl.when(pl.program_id(2) == 0)
    def _():
        vt_ref[...] = v_ref[...].astype(F32).T.astype(vt_ref.dtype)


def _diff_attn_kernel(q_ref, k_ref, v_ref, lam_ref, g_ref, qcos_ref, qsn_ref, qsp_ref, kcos_ref, ksn_ref, ksp_ref,
                      o_ref, vt_ref, s_ref, e_ref, kr_ref, *, lam_init):
    _load_v_transposed(v_ref, vt_ref)

    @pl.when(pl.program_id(2) == 0)
    def _():
        kr_ref[...] = _rope_apply(k_ref[...].astype(F32), kcos_ref[...], ksn_ref[...], ksp_ref[...]
                                  ).astype(kr_ref.dtype)

    tq = s_ref.shape[2] // 2
    q = _rope_apply(q_ref[...].astype(F32), qcos_ref[...], qsn_ref[...], qsp_ref[...])
    q = (q * (QK_HALF ** -0.5 * LOG2E)).astype(kr_ref.dtype)
    lane = lax.broadcasted_iota(jnp.int32, q.shape, 1)
    zero = jnp.zeros_like(q)
    q0 = jnp.where(lane < QK_HALF, q, zero)
    q1 = jnp.where(lane >= QK_HALF, q, zero)
    tiles = [slice(t * tq, (t + 1) * tq) for t in range(q.shape[0] // tq)]
    outs = _attend_tiles(kr_ref, [jnp.concatenate([q0[sl], q1[sl]], axis=0) for sl in tiles], vt_ref, s_ref, e_ref)
    lp = lam_ref[...]
    lam = (jnp.exp(jnp.sum(lp[0:1] * lp[1:2], axis=-1, keepdims=True))
           - jnp.exp(jnp.sum(lp[2:3] * lp[3:4], axis=-1, keepdims=True)) + lam_init)
    gain = g_ref[...] * (1.0 - lam_init)
    for sl, (o_t, l) in zip(tiles, outs):
        inv = 1.0 / l
        c_t = o_t[:, :tq] * inv[:, :tq] - o_t[:, tq:] * (lam * inv[:, tq:])
        c_t = c_t * lax.rsqrt(jnp.mean(c_t * c_t, axis=0, keepdims=True) + NORM_EPS)
        o_ref[sl, :] = (c_t.T * gain).astype(o_ref.dtype)


def _sdpa_kernel(*refs, n_qk, scale, rope_q):
    q_refs, k_refs = refs[:n_qk], refs[n_qk:2 * n_qk]
    n_tab = 0 if rope_q is None else 3
    v_ref = refs[2 * n_qk]
    tables = refs[2 * n_qk + 1:2 * n_qk + 1 + n_tab]
    o_ref, vt_ref, s_ref, e_ref = refs[2 * n_qk + 1 + n_tab:2 * n_qk + 5 + n_tab]
    _load_v_transposed(v_ref, vt_ref)
    if n_qk == 1:
        kcat_ref = k_refs[0]
    else:
        kcat_ref = refs[2 * n_qk + 5 + n_tab]

        @pl.when(pl.program_id(2) == 0)
        def _():
            for i, k_ref in enumerate(k_refs):
                kcat_ref[:, i * HEAD_DIM:(i + 1) * HEAD_DIM] = k_ref[...]

    tq = s_ref.shape[2]
    parts = [q_ref[...].astype(F32) for q_ref in q_refs]
    if rope_q is not None:
        parts[rope_q] = _rope_apply(parts[rope_q], *[t[...] for t in tables])
    q = (jnp.concatenate(parts, axis=1) * (scale * LOG2E)).astype(kcat_ref.dtype)
    tiles = [slice(t * tq, (t + 1) * tq) for t in range(q.shape[0] // tq)]
    outs = _attend_tiles(kcat_ref, [q[sl] for sl in tiles], vt_ref, s_ref, e_ref)
    for sl, (o_t, l) in zip(tiles, outs):
        o_ref[sl, :] = (o_t * (1.0 / l)).T.astype(o_ref.dtype)


def _attention_call(kernel, q_list, k_list, v, extras, *, bsz, heads, s, q_range, k_range, tq, lanes, name,
                    q_row_tables=(), k_row_tables=()):
    q_start, q_len = q_range
    k_start, k_len = k_range
    nq = q_len // tq
    q_off = q_start // tq
    k_off = k_start // k_len
    assert q_start % tq == 0 and q_len % tq == 0 and k_start % k_len == 0

    def qspec(col0):
        return pl.BlockSpec((None, tq, HEAD_DIM), lambda b, h, i: (b, q_off + i, col0 + h))

    def kspec(col0, stride, shared=False):
        if shared:
            return pl.BlockSpec((None, k_len, HEAD_DIM), lambda b, h, i: (b, k_off, col0))
        return pl.BlockSpec((None, k_len, HEAD_DIM), lambda b, h, i: (b, k_off, col0 + stride * h))

    def per_sample(arr):
        return arr.reshape(bsz, s, arr.shape[-1])

    in_specs, args = [], []
    for arr, col0 in q_list:
        in_specs.append(qspec(col0)); args.append(per_sample(arr))
    for arr, col0, stride, shared in k_list:
        in_specs.append(kspec(col0, stride, shared)); args.append(per_sample(arr))
    arr, col0, stride = v
    in_specs.append(kspec(col0, stride)); args.append(per_sample(arr))
    for e in extras:
        in_specs.append(pl.BlockSpec(e.shape, lambda b, h, i: (0, 0))); args.append(e)
    for t in q_row_tables:
        in_specs.append(pl.BlockSpec((tq, LANE), lambda b, h, i: (q_off + i, 0))); args.append(t)
    for t in k_row_tables:
        in_specs.append(pl.BlockSpec((k_len, LANE), lambda b, h, i: (k_off, 0))); args.append(t)
    return pl.pallas_call(
        kernel,
        grid=(bsz, heads, nq),
        in_specs=in_specs,
        out_specs=pl.BlockSpec((tq, HEAD_DIM), lambda b, h, i: (b * nq + i, h)),
        out_shape=jax.ShapeDtypeStruct((bsz * q_len, heads * HEAD_DIM), BF16),
        scratch_shapes=[pltpu.VMEM((HEAD_DIM, k_len), BF16), pltpu.VMEM((2, k_len, lanes), F32),
                        pltpu.VMEM((2, k_len, lanes), BF16)]
        + ([pltpu.VMEM((k_len, len(k_list) * HEAD_DIM), BF16)] if len(k_list) > 1 else [])
        + ([pltpu.VMEM((k_len, HEAD_DIM), BF16)] if k_row_tables else []),
        compiler_params=_params("parallel", "parallel", "arbitrary"),
        name=name,
    )(*args)


def _attend_lat_and_ctx(call, n_lat, n_ctx, with_ctx):
    s = n_lat + n_ctx
    o_lat = call((0, n_lat), (0, s), "lat")
    if not with_ctx:
        return o_lat
    bsz = o_lat.shape[0] // n_lat
    o_ctx = call((n_lat, n_ctx), (n_lat, n_ctx), "ctx").reshape(bsz, n_ctx, -1)
    return jnp.concatenate([o_lat.reshape(bsz, n_lat, -1), o_ctx], axis=1).reshape(bsz * s, -1)


NA_STEP_ROWS = 8
NA_WIN_ROWS = 16


def _na_window_start(step, grid_rows):
    return jnp.clip(step * NA_STEP_ROWS - NA_ROWS // 2, 0, grid_rows - NA_WIN_ROWS)


def _na_bias_table(rpb, grid_rows):
    steps = grid_rows // NA_STEP_ROWS
    heads = rpb.shape[0]
    n_dr = 2 * NA_ROWS - 1
    col = jnp.arange(GRID_W, dtype=jnp.int32)
    cs = jnp.clip(col - NA_COLS // 2, 0, GRID_W - NA_COLS)
    colmask = (col[None, :] >= cs[:, None]) & (col[None, :] < cs[:, None] + NA_COLS)
    dc = jnp.clip(col[None, :] - col[:, None] + NA_COLS - 1, 0, 2 * NA_COLS - 2)
    pat = jnp.where(colmask[None, None], rpb[:, :, dc].astype(F32) * LOG2E, NEG_INF)
    blank = jnp.full((heads, 1, GRID_W, GRID_W), NEG_INF, F32)
    pairs = jnp.concatenate([jnp.concatenate([blank, pat], axis=1), jnp.concatenate([pat, blank], axis=1)], axis=-1)

    kind_steps = [0, min(1, steps - 1), steps - 1]
    plan = []
    for st in kind_steps:
        ws = min(max(st * NA_STEP_ROWS - NA_ROWS // 2, 0), grid_rows - NA_WIN_ROWS)
        rows = []
        for i in range(NA_STEP_ROWS):
            r = st * NA_STEP_ROWS + i
            rs = min(max(r - NA_ROWS // 2, 0), grid_rows - NA_ROWS)
            rows.append([(ws + jw - r + NA_ROWS - 1) if rs <= ws + jw < rs + NA_ROWS else None
                         for jw in range(NA_WIN_ROWS)])
        plan.append(rows)

    def expand_kernel(p_ref, o_ref):
        lane = lax.broadcasted_iota(jnp.int32, (GRID_W, 2 * GRID_W), 1)
        for kind, rows in enumerate(plan):
            @pl.when(pl.program_id(1) == kind)
            def _(rows=rows):
                for i, offs in enumerate(rows):
                    for jp in range(NA_WIN_ROWS // 2):
                        left, right = offs[2 * jp], offs[2 * jp + 1]
                        if left is None and right is None:
                            blk = jnp.full((GRID_W, 2 * GRID_W), NEG_INF, F32)
                        else:
                            blk = p_ref[right if right is not None else left + 1]
                            if left is None:
                                blk = jnp.where(lane >= GRID_W, blk, NEG_INF)
                            if right is None:
                                blk = jnp.where(lane < GRID_W, blk, NEG_INF)
                        o_ref[i * GRID_W:(i + 1) * GRID_W, 2 * jp * GRID_W:2 * (jp + 1) * GRID_W] = blk

    tq, win = NA_STEP_ROWS * GRID_W, NA_WIN_ROWS * GRID_W
    return pl.pallas_call(
        expand_kernel,
        grid=(heads, 3),
        in_specs=[pl.BlockSpec((None, n_dr + 1, GRID_W, 2 * GRID_W), lambda h, k: (h, 0, 0, 0))],
        out_specs=pl.BlockSpec((None, None, tq, win), lambda h, k: (h, k, 0, 0)),
        out_shape=jax.ShapeDtypeStruct((heads, 3, tq, win), F32),
        compiler_params=_params("parallel", "arbitrary"),
        name="na_bias_table",
    )(pairs)


def _na_kernel(q_ref, k_ref, v_ref, bias_ref, o_ref, *, n_lat, grid_rows):
    start = pl.multiple_of(_na_window_start(pl.program_id(2), grid_rows) * GRID_W, GRID_W)
    win = NA_WIN_ROWS * GRID_W
    q = (q_ref[...].astype(F32) * (HEAD_DIM ** -0.5 * LOG2E)).astype(k_ref.dtype)
    sl = lax.dot_general(q, k_ref[pl.ds(start, win), :], NT_DIMS, preferred_element_type=F32) + bias_ref[...]
    sc = lax.dot_general(q, k_ref[n_lat:, :], NT_DIMS, preferred_element_type=F32)
    m = jnp.maximum(jnp.max(sl, axis=-1, keepdims=True), jnp.max(sc, axis=-1, keepdims=True))
    el = jnp.exp2(sl - m)
    ec = jnp.exp2(sc - m)
    inv = 1.0 / (jnp.sum(el, axis=-1, keepdims=True) + jnp.sum(ec, axis=-1, keepdims=True))
    o = (jnp.dot(el.astype(v_ref.dtype), v_ref[pl.ds(start, win), :], preferred_element_type=F32)
         + jnp.dot(ec.astype(v_ref.dtype), v_ref[n_lat:, :], preferred_element_type=F32))
    o_ref[...] = (o * inv).astype(o_ref.dtype)


def neighbourhood_attention(pd, bias, bsz, heads, n_lat, n_ctx):
    s = n_lat + n_ctx
    grid_rows = n_lat // GRID_W
    tq = NA_STEP_ROWS * GRID_W
    win = NA_WIN_ROWS * GRID_W
    steps = grid_rows // NA_STEP_ROWS
    assert grid_rows >= NA_WIN_ROWS and grid_rows % NA_STEP_ROWS == 0 and steps >= 2
    assert NA_WIN_ROWS >= NA_STEP_ROWS + NA_ROWS - 1

    def kind(g):
        return jnp.where(g == 0, 0, jnp.where(g == steps - 1, 2, 1))

    pd3 = pd.reshape(bsz, s, pd.shape[-1])
    return pl.pallas_call(
        functools.partial(_na_kernel, n_lat=n_lat, grid_rows=grid_rows),
        grid=(bsz, heads, steps),
        in_specs=[
            pl.BlockSpec((None, tq, HEAD_DIM), lambda b, h, g: (b, g, h)),
            pl.BlockSpec((None, s, HEAD_DIM), lambda b, h, g: (b, 0, heads + h)),
            pl.BlockSpec((None, s, HEAD_DIM), lambda b, h, g: (b, 0, 2 * heads + h)),
            pl.BlockSpec((None, None, tq, win), lambda b, h, g: (h, kind(g), 0, 0)),
        ],
        out_specs=pl.BlockSpec((tq, HEAD_DIM), lambda b, h, g: (b * steps + g, h)),
        out_shape=jax.ShapeDtypeStruct((bsz * n_lat, heads * HEAD_DIM), BF16),
        compiler_params=_params("parallel", "parallel", "arbitrary"),
        name="neighbourhood_attention",
    )(pd3, pd3, pd3, bias)


def _hgrn_constants(reverse):
    c = CHUNK
    t = np.arange(c)
    tri = (t[None, :] >= t[:, None]) if reverse else (t[None, :] <= t[:, None])
    mats = [tri.astype(np.float32)]
    masks = [np.eye(c, dtype=np.float32)]
    m = c // 2
    while m >= 1:
        node = (t // (2 * m)) * (2 * m)
        ref = node + (m if reverse else m - 1)
        if m < HGRN_BROADCAST_MIN:
            mats.append(tri[ref].astype(np.float32))
        upper = (t % (2 * m)) >= m
        same = node[:, None] == node[None, :]
        pair = same & (upper[:, None] & ~upper[None, :])
        masks.append((pair.T if reverse else pair).astype(np.float32))
        m //= 2
    return np.concatenate(mats, axis=0), np.stack(masks)


HGRN_BROADCAST_MIN = 8


def _hgrn_reference_rows(cum, cs, lev, reverse):
    c = CHUNK
    m = c >> lev
    if m < HGRN_BROADCAST_MIN:
        k = lev - int(math.log2(c // HGRN_BROADCAST_MIN))
        return cs[k * c:(k + 1) * c]
    rows = []
    for node in range(0, c, 2 * m):
        ref = node + (m if reverse else m - 1)
        rows.append(jnp.broadcast_to(cum[ref:ref + 1, :], (2 * m, cum.shape[1])))
    return rows[0] if len(rows) == 1 else jnp.concatenate(rows, axis=0)


def _hgrn_kernel(cq_ref, ci_ref, z_ref, lb_ref, ms_ref, mk_ref, o_ref, st_ref, *, heads, reverse):
    @pl.when(pl.program_id(1) == 0)
    def _():
        st_ref[...] = jnp.zeros_like(st_ref)

    c = CHUNK
    n_levels = mk_ref.shape[0]
    ms = ms_ref[...].astype(BF16)
    scale = HEAD_DIM ** -0.5
    tot_row = 0 if reverse else c - 1
    for h in range(heads):
        sl = slice(h * HEAD_DIM, (h + 1) * HEAD_DIM)
        lb = lb_ref[:, sl]
        f = lb + (1.0 - lb) * _sigmoid(z_ref[:, sl].astype(F32))
        kk = 1.0 - f
        g = jnp.log(f)
        cq = cq_ref[:, sl].astype(F32)
        qq = cq * _sigmoid(cq) * scale
        v = ci_ref[:, sl]
        g_hi = g.astype(BF16)
        r1 = g - g_hi.astype(F32)
        g_mid = r1.astype(BF16)
        g_lo = (r1 - g_mid.astype(F32)).astype(BF16)
        cs3 = jnp.dot(ms, jnp.concatenate([g_hi, g_mid, g_lo], axis=1), preferred_element_type=F32)
        cs = cs3[:, :HEAD_DIM] + cs3[:, HEAD_DIM:2 * HEAD_DIM] + cs3[:, 2 * HEAD_DIM:]
        cum = cs[:c]
        tot = cs[tot_row:tot_row + 1]
        st = st_ref[h]
        o = lax.dot_general((qq * jnp.exp(cum)).astype(BF16), st.astype(BF16), NT_DIMS,
                            preferred_element_type=F32)
        a = mk_ref[0] * lax.dot_general(qq.astype(BF16), kk.astype(BF16), NT_DIMS,
                                        preferred_element_type=F32)
        for lev in range(1, n_levels):
            e = jnp.exp(-jnp.abs(cum - _hgrn_reference_rows(cum, cs, lev, reverse)))
            a = a + mk_ref[lev] * lax.dot_general((qq * e).astype(BF16), (kk * e).astype(BF16), NT_DIMS,
                                                  preferred_element_type=F32)
        o = o + jnp.dot(a.astype(BF16), v, preferred_element_type=F32)
        o_ref[:, sl] = o
        ke = (kk * jnp.exp(tot - cum)).astype(BF16)
        st_ref[h] = st * jnp.exp(tot) + lax.dot_general(v, ke, TN_DIMS, preferred_element_type=F32)


def hgrn_scan(pc, lb, bsz, heads, n_lat, n_ctx, reverse):
    s = n_lat + n_ctx
    chunks = s // CHUNK
    lat_chunks = n_lat // CHUNK
    width = heads * HEAD_DIM
    ms, mk = _hgrn_constants(reverse)

    def chunk_of(step):
        return (chunks - 1 - step) if reverse else lax.rem(step + lat_chunks, chunks)

    def spec(col):
        return pl.BlockSpec((CHUNK, width), lambda b, t: (b * chunks + chunk_of(t), col))

    return pl.pallas_call(
        functools.partial(_hgrn_kernel, heads=heads, reverse=reverse),
        grid=(bsz, chunks),
        in_specs=[spec(0), spec(1), spec(3 if reverse else 2),
                  pl.BlockSpec((1, width), lambda b, t: (0, 0)),
                  pl.BlockSpec(ms.shape, lambda b, t: (0, 0)),
                  pl.BlockSpec(mk.shape, lambda b, t: (0, 0, 0))],
        out_specs=spec(0),
        out_shape=jax.ShapeDtypeStruct((bsz * s, width), F32),
        scratch_shapes=[pltpu.VMEM((heads, HEAD_DIM, HEAD_DIM), F32)],
        compiler_params=_params("parallel", "arbitrary"),
        name="hgrn_scan_bwd" if reverse else "hgrn_scan_fwd",
    )(pc, pc, pc, lb.reshape(1, width), jnp.asarray(ms), jnp.asarray(mk))


def _hgrn_readout_kernel(of_ref, ob_ref, gate_ref, g_ref, o_ref):
    gain = g_ref[...]
    for h in range(of_ref.shape[1] // HEAD_DIM):
        sl = slice(h * HEAD_DIM, (h + 1) * HEAD_DIM)
        gate = gate_ref[:, sl].astype(F32)
        o = _rms(of_ref[:, sl] + ob_ref[:, sl], gain)
        o_ref[:, sl] = (o * (gate * _sigmoid(gate))).astype(o_ref.dtype)


def hgrn_readout(o_fwd, o_bwd, pc, gain, bsz, s, rows, tr=256):
    width = o_fwd.shape[1]
    in_tiles, out_tiles = s // tr, rows // tr
    ispec = pl.BlockSpec((tr, width), lambda b, i: (b * in_tiles + i, 0))
    return pl.pallas_call(
        _hgrn_readout_kernel,
        grid=(bsz, out_tiles),
        in_specs=[ispec, ispec, pl.BlockSpec((tr, width), lambda b, i: (b * in_tiles + i, 4)),
                  pl.BlockSpec((1, HEAD_DIM), lambda b, i: (0, 0))],
        out_specs=pl.BlockSpec((tr, width), lambda b, i: (b * out_tiles + i, 0)),
        out_shape=jax.ShapeDtypeStruct((bsz * rows, width), BF16),
        compiler_params=_params("parallel", "parallel"),
        name="hgrn_readout",
    )(o_fwd, o_bwd, pc, gain.reshape(1, HEAD_DIM))


def _mla_prep_kernel(p_ref, qg_ref, kg_ref, cos_ref, sn_ref, sp_ref, qn_ref, cn_ref, kr_ref, *, q_rank, kv_rank):
    qn_ref[...] = _rms(p_ref[:, :q_rank].astype(F32), qg_ref[...]).astype(qn_ref.dtype)
    cn_ref[...] = _rms(p_ref[:, q_rank:q_rank + kv_rank].astype(F32), kg_ref[...]).astype(cn_ref.dtype)
    kr = p_ref[:, q_rank + kv_rank:q_rank + kv_rank + LANE].astype(F32)
    kr_ref[...] = _rope_apply(kr, cos_ref[...], sn_ref[...], sp_ref[...]).astype(kr_ref.dtype)


def mla_prep(pb, q_gain, kv_gain, tables, s, tr=256):
    r, width = pb.shape
    q_rank, kv_rank = q_gain.shape[0], kv_gain.shape[0]
    assert width >= q_rank + kv_rank + LANE and q_rank % LANE == 0 and kv_rank % LANE == 0
    tiles_per_seq = s // tr
    tspec = pl.BlockSpec((tr, LANE), lambda i: (i % tiles_per_seq, 0))

    def ospec(w):
        return pl.BlockSpec((tr, w), lambda i: (i, 0))

    return pl.pallas_call(
        functools.partial(_mla_prep_kernel, q_rank=q_rank, kv_rank=kv_rank),
        grid=(r // tr,),
        in_specs=[ospec(width), pl.BlockSpec((1, q_rank), lambda i: (0, 0)),
                  pl.BlockSpec((1, kv_rank), lambda i: (0, 0)), tspec, tspec, tspec],
        out_specs=[ospec(q_rank), ospec(kv_rank), ospec(LANE)],
        out_shape=[jax.ShapeDtypeStruct((r, q_rank), BF16), jax.ShapeDtypeStruct((r, kv_rank), BF16),
                   jax.ShapeDtypeStruct((r, LANE), BF16)],
        compiler_params=_params("parallel"),
        name="mla_prep",
    )(pb, q_gain.reshape(1, q_rank), kv_gain.reshape(1, kv_rank), *tables)


def _merge_kernel(oa_ref, ob_ref, oc_ref, od_ref, ga_ref, gb_ref, gc_ref, gd_ref, w_ref, o_ref):
    acc = None
    for i, (o_ref_i, g_ref_i) in enumerate(((oa_ref, ga_ref), (ob_ref, gb_ref), (oc_ref, gc_ref), (od_ref, gd_ref))):
        y = jnp.dot(o_ref_i[...], w_ref[i], preferred_element_type=F32)
        term = _sigmoid(g_ref_i[...].astype(F32)) * y
        acc = term if acc is None else acc + term
    o_ref[...] = acc.astype(o_ref.dtype)


def merge_branches(outs, gate_logits, w_branch):
    r, c = outs[0].shape
    d = w_branch.shape[2]
    bm = _pick(r, (512, 256, 128))
    bn = _pick(d, (1024, 512, 256, 128))
    nb = d // bn
    ospec = pl.BlockSpec((bm, c), lambda i, j: (i, 0))

    def gspec(branch):
        return pl.BlockSpec((bm, bn), lambda i, j: (i, branch * nb + j))

    return pl.pallas_call(
        _merge_kernel,
        grid=(r // bm, nb),
        in_specs=[ospec] * N_BRANCH + [gspec(i) for i in range(N_BRANCH)]
        + [pl.BlockSpec((N_BRANCH, c, bn), lambda i, j: (0, 0, j))],
        out_specs=pl.BlockSpec((bm, bn), lambda i, j: (i, j)),
        out_shape=jax.ShapeDtypeStruct((r, d), BF16),
        compiler_params=_params("parallel", "parallel"),
        name="merge_branches",
    )(*outs, gate_logits, gate_logits, gate_logits, gate_logits, w_branch)


HALO = 16


FFN_BN = 640


def _ffn_tile_pairs(a, bn):
    f = a.shape[-1] // 2
    lead = a.shape[:-1]
    pairs = jnp.stack([a[..., :f].reshape(*lead, f // bn, bn), a[..., f:].reshape(*lead, f // bn, bn)], axis=-2)
    return pairs.reshape(*lead, 2 * f)


def _cast_pair_kernel(g_ref, v_ref, o_ref):
    bn = g_ref.shape[1]
    o_ref[:, :bn] = g_ref[...].astype(o_ref.dtype)
    o_ref[:, bn:] = v_ref[...].astype(o_ref.dtype)


def ffn_up_weight(w_up, layer, bn):
    _, d, two_f = w_up.shape
    nb = two_f // 2 // bn
    tk = _pick(d, (1024, 512, 256, 128))
    return pl.pallas_call(
        _cast_pair_kernel,
        grid=(d // tk, nb),
        in_specs=[pl.BlockSpec((None, tk, bn), lambda i, j: (layer, i, j)),
                  pl.BlockSpec((None, tk, bn), lambda i, j: (layer, i, nb + j))],
        out_specs=pl.BlockSpec((tk, 2 * bn), lambda i, j: (i, j)),
        out_shape=jax.ShapeDtypeStruct((d, two_f), BF16),
        compiler_params=_params("parallel", "parallel"),
        name="ffn_up_weight",
    )(w_up, w_up)


FFN_EPILOGUE_ROWS = 256


def _ffn_up_kernel(h_ref, hp_ref, hn_ref, w_ref, cw_ref, cb_ref, o_ref, hbuf_ref, u_ref, *, bm, s, n_lat):
    i = pl.program_id(0)
    bn = o_ref.shape[1]

    @pl.when(pl.program_id(1) == 0)
    def _():
        hbuf_ref[0:HALO, :] = hp_ref[...]
        hbuf_ref[HALO:HALO + bm, :] = h_ref[...]
        hbuf_ref[HALO + bm:, :] = hn_ref[...]

    u_ref[...] = jnp.dot(hbuf_ref[...], w_ref[...], preferred_element_type=F32)
    cw = cw_ref[...]
    cb = cb_ref[...]
    er = min(FFN_EPILOGUE_ROWS, bm)
    for c in range(bm // er):
        r0 = HALO + c * er
        pos = lax.rem(i * bm + c * er + lax.broadcasted_iota(jnp.int32, (er, 1), 0), s)
        has_prev = jnp.where((pos == 0) | (pos == n_lat), 0.0, 1.0)
        has_next = jnp.where((pos == n_lat - 1) | (pos == s - 1), 0.0, 1.0)
        win = u_ref[r0 - 8:r0 + er + 8, :]
        prev = pltpu.roll(win, 1, 0)[8:8 + er]
        nxt = pltpu.roll(win, er + 16 - 1, 0)[8:8 + er]
        y = prev * has_prev * cw[0:1] + win[8:8 + er] * cw[1:2] + nxt * has_next * cw[2:3] + cb
        gate, val = y[:, :bn], y[:, bn:]
        o_ref[c * er:(c + 1) * er, :] = (gate * _sigmoid(gate) * val).astype(o_ref.dtype)


def ffn_up(h, w_up, conv_w, conv_b, s, n_lat):
    r, d = h.shape
    f = w_up.shape[1] // 2
    bm = _pick(r, (1024, 512, 256, 128))
    bn = FFN_BN
    assert f % bn == 0
    halo_blocks = r // HALO
    per_tile = bm // HALO
    return pl.pallas_call(
        functools.partial(_ffn_up_kernel, bm=bm, s=s, n_lat=n_lat),
        grid=(r // bm, f // bn),
        in_specs=[
            pl.BlockSpec((bm, d), lambda i, j: (i, 0), pipeline_mode=pl.Buffered(1)),
            pl.BlockSpec((HALO, d), lambda i, j: (jnp.maximum(i * per_tile - 1, 0), 0)),
            pl.BlockSpec((HALO, d), lambda i, j: (jnp.minimum((i + 1) * per_tile, halo_blocks - 1), 0)),
            pl.BlockSpec((d, 2 * bn), lambda i, j: (0, j)),
            pl.BlockSpec((3, 2 * bn), lambda i, j: (0, j)),
            pl.BlockSpec((1, 2 * bn), lambda i, j: (0, j)),
        ],
        out_specs=pl.BlockSpec((bm, bn), lambda i, j: (i, j)),
        out_shape=jax.ShapeDtypeStruct((r, f), BF16),
        scratch_shapes=[pltpu.VMEM((bm + 2 * HALO, d), BF16), pltpu.VMEM((bm + 2 * HALO, 2 * bn), F32)],
        compiler_params=_params("parallel", "arbitrary"),
        name="ffn_up",
    )(h, h, h, w_up, conv_w, conv_b)


W_IN_SPLIT_ROWS = 128


def _w_in_split_kernel(w_ref, *out_refs, starts):
    total = w_ref.shape[1]
    for o_ref, start in zip(out_refs, starts):
        width = o_ref.shape[1]
        shift = start % LANE
        if shift == 0:
            o_ref[...] = w_ref[:, start:start + width].astype(o_ref.dtype)
            continue
        for t in range(width // LANE):
            base = start - shift + t * LANE
            x = w_ref[:, base:min(base + 2 * LANE, total)]
            o_ref[:, t * LANE:(t + 1) * LANE] = x[:, shift:shift + LANE].astype(o_ref.dtype)


def _split_w_in(w_in, layer, d, heads):
    mix = heads * HEAD_DIM
    q_rank = 3 * d // 16
    kv_rank = d // 8
    b_used = q_rank + kv_rank + MLA_ROPE
    b_width = -(-(q_rank + kv_rank + LANE) // 512) * 512
    sizes = [3 * mix, b_used, 5 * mix, 3 * mix, N_BRANCH * d]
    starts = [sum(sizes[:i]) for i in range(len(sizes))]
    total = w_in.shape[2]
    assert sum(sizes) == total and starts[1] + b_width <= total
    widths = [sizes[0], b_width, sizes[2], sizes[3], sizes[4]]
    tk = W_IN_SPLIT_ROWS
    outs = pl.pallas_call(
        functools.partial(_w_in_split_kernel, starts=tuple(starts)),
        grid=(d // tk,),
        in_specs=[pl.BlockSpec((None, tk, total), lambda i: (layer, i, 0))],
        out_specs=[pl.BlockSpec((tk, w), lambda i: (i, 0)) for w in widths],
        out_shape=[jax.ShapeDtypeStruct((d, w), BF16) for w in widths],
        compiler_params=_params("parallel"),
        name="w_in_split",
    )(w_in)
    return (*outs, q_rank, kv_rank)


def _mla_q_weight(w_qb, heads):
    rq = w_qb.shape[0]
    w = w_qb.reshape(rq, heads, MLA_NOPE + MLA_ROPE)
    nope = w[:, :, :MLA_NOPE].reshape(rq, heads * MLA_NOPE)
    ropep = jnp.pad(w[:, :, MLA_NOPE:], ((0, 0), (0, 0), (0, LANE - MLA_ROPE))).reshape(rq, heads * LANE)
    return jnp.concatenate([nope, ropep], axis=1).astype(BF16)


def kernel(x, c, ctx, c_ctx, ada_w, ada_b, norm_mix_pre, norm_mix_post, norm_ffn_pre, norm_ffn_post, w_in,
           diff_lambda, diff_subln, mla_q_norm, mla_w_qb, mla_kv_norm, mla_w_kvb, hgrn_lb_logits, hgrn_norm,
           na_rpb, w_branch, w_out, ffn_w_up, ffn_conv_w, ffn_conv_b, ffn_w_down):
    bsz, n_lat, d = x.shape
    n_ctx = ctx.shape[1]
    s = n_lat + n_ctx
    r = bsz * s
    depth = ada_w.shape[0]
    heads = d // (N_BRANCH * HEAD_DIM)
    mix = heads * HEAD_DIM
    assert n_lat % 256 == 0 and n_ctx % 256 == 0 and s % n_ctx == 0 and n_lat % GRID_W == 0

    tables = _rope_tables(n_lat, n_ctx)
    lb_sm = jax.nn.softmax(hgrn_lb_logits.astype(F32), axis=0)
    lower_bounds = jnp.cumsum(lb_sm, axis=0) - lb_sm[0]

    c_rows = jnp.zeros((8, d), F32).at[:bsz].set(c).at[bsz].set(c_ctx)
    xa = jnp.concatenate([x, ctx], axis=1)

    def mod_rows(mod, idx):
        part = mod[:, idx * d:(idx + 1) * d]
        both = jnp.stack([part[:bsz], jnp.broadcast_to(part[bsz], (bsz, d))], axis=1)
        return both.reshape(2 * bsz, 1, d)

    mods = []
    for l in range(depth):
        mod = ada_modulation(c_rows, ada_w, ada_b, l)
        mods.append([mod_rows(mod, i) for i in range(6)])

    h = norm_modulate(xa, norm_mix_pre[0], mods[0][0], mods[0][1], n_lat)
    for l in range(depth):
        with_ctx = l < depth - 1
        sh1, sc1, g1, sh2, sc2, g2 = mods[l]
        h2d = h.reshape(r, d)
        rows = s if with_ctx else n_lat

        w_a, w_b, w_c, w_d, w_g, q_rank, kv_rank = _split_w_in(w_in, l, d, heads)
        pa = matmul(h2d, w_a, BF16, "in_proj_a")
        pb = matmul(h2d, w_b, BF16, "in_proj_b")
        pc = matmul(h2d, w_c, BF16, "in_proj_c")
        pd = matmul(h2d, w_d, BF16, "in_proj_d")
        if with_ctx:
            pg = matmul(h2d, w_g, BF16, "in_proj_gate")
        else:
            pg = matmul_leading_rows(h, w_g, rows, BF16, "in_proj_gate")

        lam_init = 0.8 - 0.6 * math.exp(-0.3 * l)
        extras_a = [diff_lambda[l].astype(F32), diff_subln[l].reshape(1, HEAD_DIM).astype(F32)]

        def call_a(q_range, k_range, tag):
            return _attention_call(
                functools.partial(_diff_attn_kernel, lam_init=lam_init),
                [(pa, 0)], [(pa, heads, 1, False)], (pa, 2 * heads, 1), extras_a,
                bsz=bsz, heads=heads, s=s, q_range=q_range, k_range=k_range, tq=min(1024, q_range[1]), lanes=256,
                name="diff_attn_" + tag, q_row_tables=tables, k_row_tables=tables)

        oa = _attend_lat_and_ctx(call_a, n_lat, n_ctx, with_ctx)

        qn, cn, kr = mla_prep(pb, mla_q_norm[l], mla_kv_norm[l], tables, s)
        qb = matmul(qn, _mla_q_weight(mla_w_qb[l], heads), BF16, "mla_q_up")
        kvb = matmul(cn, mla_w_kvb[l].astype(BF16), BF16, "mla_kv_up")

        def call_b(q_range, k_range, tag):
            return _attention_call(
                functools.partial(_sdpa_kernel, n_qk=2, scale=(MLA_NOPE + MLA_ROPE) ** -0.5, rope_q=1),
                [(qb, 0), (qb, heads)], [(kvb, 0, 2, False), (kr, 0, 0, True)], (kvb, 1, 2), [],
                bsz=bsz, heads=heads, s=s, q_range=q_range, k_range=k_range, tq=min(2048, q_range[1]), lanes=256,
                name="mla_attn_" + tag, q_row_tables=tables)

        ob = _attend_lat_and_ctx(call_b, n_lat, n_ctx, with_ctx)

        o_fwd = hgrn_scan(pc, lower_bounds[l, 0], bsz, heads, n_lat, n_ctx, reverse=False)
        o_bwd = hgrn_scan(pc, lower_bounds[l, 1], bsz, heads, n_lat, n_ctx, reverse=True)
        oc = hgrn_readout(o_fwd, o_bwd, pc, hgrn_norm[l], bsz, s, rows)

        od = neighbourhood_attention(pd, _na_bias_table(na_rpb[l], n_lat // GRID_W), bsz, heads, n_lat, n_ctx)
        if with_ctx:
            od_ctx = _attention_call(
                functools.partial(_sdpa_kernel, n_qk=1, scale=HEAD_DIM ** -0.5, rope_q=None),
                [(pd, 0)], [(pd, heads, 1, False)], (pd, 2 * heads, 1), [],
                bsz=bsz, heads=heads, s=s, q_range=(n_lat, n_ctx), k_range=(n_lat, n_ctx), tq=256, lanes=256,
                name="na_ctx_attn").reshape(bsz, n_ctx, mix)
            od = jnp.concatenate([od.reshape(bsz, n_lat, mix), od_ctx], axis=1).reshape(r, mix)

        merged = merge_branches((oa, ob, oc, od), pg, w_branch[l].astype(BF16))
        mix_out = matmul(merged, w_out[l].astype(BF16), BF16, "out_proj").reshape(bsz, rows, d)
        xa, h2 = residual_norm_modulate(xa, mix_out, g1, norm_mix_post[l], norm_ffn_pre[l], sh2, sc2, n_lat)

        gv = ffn_up(h2.reshape(bsz * rows, d), ffn_up_weight(ffn_w_up, l, FFN_BN),
                    _ffn_tile_pairs(ffn_conv_w[l], FFN_BN), _ffn_tile_pairs(ffn_conv_b[l][None], FFN_BN),
                    rows, n_lat)
        ffn = matmul(gv, ffn_w_down[l].astype(BF16), BF16, "ffn_down").reshape(bsz, rows, d)
        if with_ctx:
            xa, h = residual_norm_modulate(xa, ffn, g2, norm_ffn_post[l], norm_mix_pre[l + 1],
                                           mods[l + 1][0], mods[l + 1][1], n_lat)
        else:
            xa = residual(xa, ffn, g2, norm_ffn_post[l], n_lat)
    return xa
```

```python
import functools
import math

import numpy as np
import jax
import jax.numpy as jnp
from jax import lax
from jax.experimental import pallas as pl
from jax.experimental.pallas import tpu as pltpu

F32 = jnp.float32
BF16 = jnp.bfloat16

HEAD_DIM = 128
QK_HALF = 64
MLA_NOPE = 128
MLA_ROPE = 64
MLA_V = 128
ROPE_DIM = 64
ROPE_THETA = 10000.0
GRID_W = 64
NA_ROWS = 8
NA_COLS = 16
CHUNK = 128
NORM_EPS = 1e-6
NEG_INF = -1e30
N_BRANCH = 4

LANE = 128
VMEM_LIMIT_BYTES = 56 * 1024 * 1024

NT_DIMS = (((1,), (1,)), ((), ()))
TN_DIMS = (((0,), (0,)), ((), ()))


def _params(*sem):
    return pltpu.CompilerParams(dimension_semantics=sem, vmem_limit_bytes=VMEM_LIMIT_BYTES)


def _sigmoid(x):
    return 0.5 * jnp.tanh(0.5 * x) + 0.5


def _pick(total, candidates):
    for c in candidates:
        if total % c == 0:
            return c
    raise ValueError(f"no tile for {total} in {candidates}")


def _ada_kernel(c_ref, w_ref, b_ref, o_ref):
    c = c_ref[...]
    a = (c * _sigmoid(c)).astype(BF16)
    o_ref[...] = jnp.dot(a, w_ref[...].astype(BF16), preferred_element_type=F32) + b_ref[...]


def ada_modulation(c_rows, ada_w, ada_b, layer):
    rows, d = c_rows.shape
    width = ada_w.shape[2]
    tn = _pick(width, (512, 256, 128))
    b3 = ada_b.reshape(ada_b.shape[0], 1, width)
    return pl.pallas_call(
        _ada_kernel,
        grid=(width // tn,),
        in_specs=[
            pl.BlockSpec((rows, d), lambda j: (0, 0)),
            pl.BlockSpec((None, d, tn), lambda j: (layer, 0, j)),
            pl.BlockSpec((None, 1, tn), lambda j: (layer, 0, j)),
        ],
        out_specs=pl.BlockSpec((rows, tn), lambda j: (0, j)),
        out_shape=jax.ShapeDtypeStruct((rows, width), F32),
        compiler_params=_params("parallel"),
        name="ada_modulation",
    )(c_rows, ada_w, b3)


def _rms(x, gain):
    return x * lax.rsqrt(jnp.mean(x * x, axis=-1, keepdims=True) + NORM_EPS) * gain


def _norm_mod_kernel(x_ref, g_ref, sh_ref, sc_ref, h_ref):
    h = _rms(x_ref[0], g_ref[...])
    h_ref[0] = (h * (1.0 + sc_ref[0]) + sh_ref[0]).astype(h_ref.dtype)


def _resid_kernel(x_ref, y_ref, gate_ref, gpost_ref, xo_ref):
    xo_ref[0] = x_ref[0] + gate_ref[0] * _rms(y_ref[0].astype(F32), gpost_ref[...])


def _resid_norm_mod_kernel(x_ref, y_ref, gate_ref, gpost_ref, gpre_ref, sh_ref, sc_ref, xo_ref, h_ref):
    xn = x_ref[0] + gate_ref[0] * _rms(y_ref[0].astype(F32), gpost_ref[...])
    xo_ref[0] = xn
    h = _rms(xn, gpre_ref[...])
    h_ref[0] = (h * (1.0 + sc_ref[0]) + sh_ref[0]).astype(h_ref.dtype)


def _row_specs(dims, tr, n_lat_tiles):
    s, d = dims
    xspec = pl.BlockSpec((1, tr, d), lambda b, t: (b, t, 0))
    gspec = pl.BlockSpec((1, d), lambda b, t: (0, 0))
    mspec = pl.BlockSpec((1, 1, d), lambda b, t: (2 * b + t // n_lat_tiles, 0, 0))
    return xspec, gspec, mspec


def norm_modulate(x, gain, shift, scale, n_lat, tr=256):
    bsz, s, d = x.shape
    xspec, gspec, mspec = _row_specs((s, d), tr, n_lat // tr)
    return pl.pallas_call(
        _norm_mod_kernel,
        grid=(bsz, s // tr),
        in_specs=[xspec, gspec, mspec, mspec],
        out_specs=xspec,
        out_shape=jax.ShapeDtypeStruct(x.shape, BF16),
        compiler_params=_params("parallel", "parallel"),
        name="norm_modulate",
    )(x, gain.reshape(1, d), shift, scale)


def residual(x, y, gate, gain_post, n_lat, tr=256):
    bsz, s, d = x.shape
    rows = y.shape[1]
    xspec, gspec, mspec = _row_specs((s, d), tr, n_lat // tr)
    return pl.pallas_call(
        _resid_kernel,
        grid=(bsz, rows // tr),
        in_specs=[xspec, xspec, mspec, gspec],
        out_specs=xspec,
        out_shape=jax.ShapeDtypeStruct(y.shape, F32),
        compiler_params=_params("parallel", "parallel"),
        name="residual",
    )(x, y, gate, gain_post.reshape(1, d))


def residual_norm_modulate(x, y, gate, gain_post, gain_pre, shift, scale, n_lat, tr=256):
    bsz, s, d = x.shape
    rows = y.shape[1]
    xspec, gspec, mspec = _row_specs((s, d), tr, n_lat // tr)
    return pl.pallas_call(
        _resid_norm_mod_kernel,
        grid=(bsz, rows // tr),
        in_specs=[xspec, xspec, mspec, gspec, gspec, mspec, mspec],
        out_specs=[xspec, xspec],
        out_shape=[jax.ShapeDtypeStruct(y.shape, F32), jax.ShapeDtypeStruct(y.shape, BF16)],
        compiler_params=_params("parallel", "parallel"),
        name="residual_norm_modulate",
    )(x, y, gate, gain_post.reshape(1, d), gain_pre.reshape(1, d), shift, scale)


def _mm_kernel(a_ref, b_ref, o_ref):
    o_ref[...] = jnp.dot(a_ref[...], b_ref[...], preferred_element_type=F32).astype(o_ref.dtype)


def _mm_acc_kernel(a_ref, b_ref, o_ref, acc_ref):
    k = pl.program_id(2)

    @pl.when(k == 0)
    def _():
        acc_ref[...] = jnp.zeros_like(acc_ref)

    acc_ref[...] += jnp.dot(a_ref[...], b_ref[...], preferred_element_type=F32)

    @pl.when(k == pl.num_programs(2) - 1)
    def _():
        o_ref[...] = acc_ref[...].astype(o_ref.dtype)


MAX_SINGLE_K = 4096
MATMUL_VMEM_BUDGET = VMEM_LIMIT_BYTES * 9 // 10


def matmul_leading_rows(a, b, rows, out_dtype, name):
    bsz, _, k = a.shape
    n = b.shape[1]
    assert k <= MAX_SINGLE_K
    bm = _pick(rows, (1024, 512, 256, 128))
    bn = _pick(n, (1024, 768, 640, 512, 384, 256, 128))
    tiles = rows // bm
    return pl.pallas_call(
        _mm_kernel,
        grid=(bsz, tiles, n // bn),
        in_specs=[pl.BlockSpec((None, bm, k), lambda s, i, j: (s, i, 0)),
                  pl.BlockSpec((k, bn), lambda s, i, j: (0, j))],
        out_specs=pl.BlockSpec((bm, bn), lambda s, i, j: (s * tiles + i, j)),
        out_shape=jax.ShapeDtypeStruct((bsz * rows, n), out_dtype),
        compiler_params=_params("parallel", "parallel", "parallel"),
        name=name,
    )(a, b)


def matmul(a, b, out_dtype, name):
    m, k = a.shape
    n = b.shape[1]
    out_bytes = jnp.dtype(out_dtype).itemsize
    full_k = None
    for cap in (1024, 512):
        bm = _pick(m, tuple(c for c in (1024, 512, 256, 128) if c <= cap))
        bn = _pick(n, tuple(c for c in (1024, 768, 640, 512, 384, 256, 128) if c <= cap))
        need = 2 * 2 * (bm * k + k * bn) + 2 * bm * bn * out_bytes + 4 * bm * bn
        if need <= MATMUL_VMEM_BUDGET:
            full_k = (bm, bn)
            break
    if full_k is not None:
        bm, bn = full_k
        return pl.pallas_call(
            _mm_kernel,
            grid=(m // bm, n // bn),
            in_specs=[pl.BlockSpec((bm, k), lambda i, j: (i, 0)),
                      pl.BlockSpec((k, bn), lambda i, j: (0, j))],
            out_specs=pl.BlockSpec((bm, bn), lambda i, j: (i, j)),
            out_shape=jax.ShapeDtypeStruct((m, n), out_dtype),
            compiler_params=_params("parallel", "parallel"),
            name=name,
        )(a, b)
    bm = _pick(m, (1024, 512, 256, 128))
    bn = _pick(n, (1024, 768, 640, 512, 384, 256, 128))
    nk = -(-k // MAX_SINGLE_K)
    while k % nk or (k // nk) % LANE:
        nk += 1
    bk = k // nk
    return pl.pallas_call(
        _mm_acc_kernel,
        grid=(m // bm, n // bn, nk),
        in_specs=[pl.BlockSpec((bm, bk), lambda i, j, kk: (i, kk)),
                  pl.BlockSpec((bk, bn), lambda i, j, kk: (kk, j))],
        out_specs=pl.BlockSpec((bm, bn), lambda i, j, kk: (i, j)),
        out_shape=jax.ShapeDtypeStruct((m, n), out_dtype),
        scratch_shapes=[pltpu.VMEM((bm, bn), F32)],
        compiler_params=_params("parallel", "parallel", "arbitrary"),
        name=name,
    )(a, b)


def _rope_tables(n_lat, n_ctx):
    t = jnp.arange(n_lat, dtype=jnp.int32)
    row = (t // GRID_W).astype(F32)
    col = (t % GRID_W).astype(F32)
    quarter = ROPE_DIM // 4
    inv = ROPE_THETA ** (-jnp.arange(quarter, dtype=F32) / quarter)
    ar = row[:, None] * inv
    ac = col[:, None] * inv
    ang = jnp.concatenate([ar, ar, ac, ac], axis=-1)
    cos = jnp.concatenate([jnp.cos(ang), jnp.ones((n_ctx, ROPE_DIM), F32)], axis=0)
    sin = jnp.concatenate([jnp.sin(ang), jnp.zeros((n_ctx, ROPE_DIM), F32)], axis=0)
    first = (jnp.arange(ROPE_DIM) // quarter) % 2 == 0
    sin_next = jnp.where(first, -sin, 0.0)
    sin_prev = jnp.where(first, 0.0, sin)
    reps = LANE // ROPE_DIM
    return tuple(jnp.tile(a, (1, reps)) for a in (cos, sin_next, sin_prev))


def _rope_apply(x, cos, sin_next, sin_prev):
    quarter = ROPE_DIM // 4
    return (x * cos + pltpu.roll(x, LANE - quarter, 1) * sin_next
            + pltpu.roll(x, quarter, 1) * sin_prev)


LOG2E = math.log2(math.e)
KEY_CHUNK = 256


def _attend_tiles(k_ref, q_tiles, vt_ref, s_ref, e_ref):
    m_keys, n = s_ref.shape[1:]
    kc = min(KEY_CHUNK, m_keys)
    chunks = [slice(c * kc, (c + 1) * kc) for c in range(m_keys // kc)]

    def scores(t):
        s_ref[t % 2] = lax.dot_general(k_ref[...], q_tiles[t], NT_DIMS, preferred_element_type=F32)

    def column_max(t):
        m8 = None
        for sl in chunks:
            part = jnp.max(s_ref[t % 2, sl, :].reshape(kc // 8, 8, n), axis=0)
            m8 = part if m8 is None else jnp.maximum(m8, part)
        return jnp.max(m8, axis=0, keepdims=True)

    scores(0)
    m = column_max(0)
    outs = []
    for t in range(len(q_tiles)):
        if t + 1 < len(q_tiles):
            scores(t + 1)
        l8 = jnp.zeros((8, n), F32)
        for sl in chunks:
            e = jnp.exp2(s_ref[t % 2, sl, :] - m)
            l8 = l8 + jnp.sum(e.reshape(kc // 8, 8, n), axis=0)
            e_ref[t % 2, sl, :] = e.astype(e_ref.dtype)
        o_t = jnp.dot(vt_ref[...], e_ref[t % 2], preferred_element_type=F32)
        outs.append((o_t, jnp.sum(l8, axis=0, keepdims=True)))
        if t + 1 < len(q_tiles):
            m = column_max(t + 1)
    return outs


def _load_v_transposed(v_ref, vt_ref):
    @pl.when(pl.program_id(2) == 0)
    def _():
        vt_ref[...] = v_ref[...].astype(F32).T.astype(vt_ref.dtype)


def _diff_attn_kernel(q_ref, k_ref, v_ref, lam_ref, g_ref, qcos_ref, qsn_ref, qsp_ref, kcos_ref, ksn_ref, ksp_ref,
                      o_ref, vt_ref, s_ref, e_ref, kr_ref, *, lam_init):
    _load_v_transposed(v_ref, vt_ref)

    @pl.when(pl.program_id(2) == 0)
    def _():
        kr_ref[...] = _rope_apply(k_ref[...].astype(F32), kcos_ref[...], ksn_ref[...], ksp_ref[...]
                                  ).astype(kr_ref.dtype)

    tq = s_ref.shape[2] // 2
    q = _rope_apply(q_ref[...].astype(F32), qcos_ref[...], qsn_ref[...], qsp_ref[...])
    q = (q * (QK_HALF ** -0.5 * LOG2E)).astype(kr_ref.dtype)
    lane = lax.broadcasted_iota(jnp.int32, q.shape, 1)
    zero = jnp.zeros_like(q)
    q0 = jnp.where(lane < QK_HALF, q, zero)
    q1 = jnp.where(lane >= QK_HALF, q, zero)
    tiles = [slice(t * tq, (t + 1) * tq) for t in range(q.shape[0] // tq)]
    outs = _attend_tiles(kr_ref, [jnp.concatenate([q0[sl], q1[sl]], axis=0) for sl in tiles], vt_ref, s_ref, e_ref)
    lp = lam_ref[...]
    lam = (jnp.exp(jnp.sum(lp[0:1] * lp[1:2], axis=-1, keepdims=True))
           - jnp.exp(jnp.sum(lp[2:3] * lp[3:4], axis=-1, keepdims=True)) + lam_init)
    gain = g_ref[...] * (1.0 - lam_init)
    for sl, (o_t, l) in zip(tiles, outs):
        inv = 1.0 / l
        c_t = o_t[:, :tq] * inv[:, :tq] - o_t[:, tq:] * (lam * inv[:, tq:])
        c_t = c_t * lax.rsqrt(jnp.mean(c_t * c_t, axis=0, keepdims=True) + NORM_EPS)
        o_ref[sl, :] = (c_t.T * gain).astype(o_ref.dtype)


def _sdpa_kernel(*refs, n_qk, scale, rope_q):
    q_refs, k_refs = refs[:n_qk], refs[n_qk:2 * n_qk]
    n_tab = 0 if rope_q is None else 3
    v_ref = refs[2 * n_qk]
    tables = refs[2 * n_qk + 1:2 * n_qk + 1 + n_tab]
    o_ref, vt_ref, s_ref, e_ref = refs[2 * n_qk + 1 + n_tab:2 * n_qk + 5 + n_tab]
    _load_v_transposed(v_ref, vt_ref)
    if n_qk == 1:
        kcat_ref = k_refs[0]
    else:
        kcat_ref = refs[2 * n_qk + 5 + n_tab]

        @pl.when(pl.program_id(2) == 0)
        def _():
            for i, k_ref in enumerate(k_refs):
                kcat_ref[:, i * HEAD_DIM:(i + 1) * HEAD_DIM] = k_ref[...]

    tq = s_ref.shape[2]
    parts = [q_ref[...].astype(F32) for q_ref in q_refs]
    if rope_q is not None:
        parts[rope_q] = _rope_apply(parts[rope_q], *[t[...] for t in tables])
    q = (jnp.concatenate(parts, axis=1) * (scale * LOG2E)).astype(kcat_ref.dtype)
    tiles = [slice(t * tq, (t + 1) * tq) for t in range(q.shape[0] // tq)]
    outs = _attend_tiles(kcat_ref, [q[sl] for sl in tiles], vt_ref, s_ref, e_ref)
    for sl, (o_t, l) in zip(tiles, outs):
        o_ref[sl, :] = (o_t * (1.0 / l)).T.astype(o_ref.dtype)


def _attention_call(kernel, q_list, k_list, v, extras, *, bsz, heads, s, q_range, k_range, tq, lanes, name,
                    q_row_tables=(), k_row_tables=()):
    q_start, q_len = q_range
    k_start, k_len = k_range
    nq = q_len // tq
    q_off = q_start // tq
    k_off = k_start // k_len
    assert q_start % tq == 0 and q_len % tq == 0 and k_start % k_len == 0

    def qspec(col0):
        return pl.BlockSpec((None, tq, HEAD_DIM), lambda b, h, i: (b, q_off + i, col0 + h))

    def kspec(col0, stride, shared=False):
        if shared:
            return pl.BlockSpec((None, k_len, HEAD_DIM), lambda b, h, i: (b, k_off, col0))
        return pl.BlockSpec((None, k_len, HEAD_DIM), lambda b, h, i: (b, k_off, col0 + stride * h))

    def per_sample(arr):
        return arr.reshape(bsz, s, arr.shape[-1])

    in_specs, args = [], []
    for arr, col0 in q_list:
        in_specs.append(qspec(col0)); args.append(per_sample(arr))
    for arr, col0, stride, shared in k_list:
        in_specs.append(kspec(col0, stride, shared)); args.append(per_sample(arr))
    arr, col0, stride = v
    in_specs.append(kspec(col0, stride)); args.append(per_sample(arr))
    for e in extras:
        in_specs.append(pl.BlockSpec(e.shape, lambda b, h, i: (0, 0))); args.append(e)
    for t in q_row_tables:
        in_specs.append(pl.BlockSpec((tq, LANE), lambda b, h, i: (q_off + i, 0))); args.append(t)
    for t in k_row_tables:
        in_specs.append(pl.BlockSpec((k_len, LANE), lambda b, h, i: (k_off, 0))); args.append(t)
    return pl.pallas_call(
        kernel,
        grid=(bsz, heads, nq),
        in_specs=in_specs,
        out_specs=pl.BlockSpec((tq, HEAD_DIM), lambda b, h, i: (b * nq + i, h)),
        out_shape=jax.ShapeDtypeStruct((bsz * q_len, heads * HEAD_DIM), BF16),
        scratch_shapes=[pltpu.VMEM((HEAD_DIM, k_len), BF16), pltpu.VMEM((2, k_len, lanes), F32),
                        pltpu.VMEM((2, k_len, lanes), BF16)]
        + ([pltpu.VMEM((k_len, len(k_list) * HEAD_DIM), BF16)] if len(k_list) > 1 else [])
        + ([pltpu.VMEM((k_len, HEAD_DIM), BF16)] if k_row_tables else []),
        compiler_params=_params("parallel", "parallel", "arbitrary"),
        name=name,
    )(*args)


def _attend_lat_and_ctx(call, n_lat, n_ctx, with_ctx):
    s = n_lat + n_ctx
    o_lat = call((0, n_lat), (0, s), "lat")
    if not with_ctx:
        return o_lat
    bsz = o_lat.shape[0] // n_lat
    o_ctx = call((n_lat, n_ctx), (n_lat, n_ctx), "ctx").reshape(bsz, n_ctx, -1)
    return jnp.concatenate([o_lat.reshape(bsz, n_lat, -1), o_ctx], axis=1).reshape(bsz * s, -1)


NA_STEP_ROWS = 8
NA_WIN_ROWS = 16


def _na_window_start(step, grid_rows):
    return jnp.clip(step * NA_STEP_ROWS - NA_ROWS // 2, 0, grid_rows - NA_WIN_ROWS)


def _na_bias_table(rpb, grid_rows):
    steps = grid_rows // NA_STEP_ROWS
    heads = rpb.shape[0]
    n_dr = 2 * NA_ROWS - 1
    col = jnp.arange(GRID_W, dtype=jnp.int32)
    cs = jnp.clip(col - NA_COLS // 2, 0, GRID_W - NA_COLS)
    colmask = (col[None, :] >= cs[:, None]) & (col[None, :] < cs[:, None] + NA_COLS)
    dc = jnp.clip(col[None, :] - col[:, None] + NA_COLS - 1, 0, 2 * NA_COLS - 2)
    pat = jnp.where(colmask[None, None], rpb[:, :, dc].astype(F32) * LOG2E, NEG_INF)
    blank = jnp.full((heads, 1, GRID_W, GRID_W), NEG_INF, F32)
    pairs = jnp.concatenate([jnp.concatenate([blank, pat], axis=1), jnp.concatenate([pat, blank], axis=1)], axis=-1)

    kind_steps = [0, min(1, steps - 1), steps - 1]
    plan = []
    for st in kind_steps:
        ws = min(max(st * NA_STEP_ROWS - NA_ROWS // 2, 0), grid_rows - NA_WIN_ROWS)
        rows = []
        for i in range(NA_STEP_ROWS):
            r = st * NA_STEP_ROWS + i
            rs = min(max(r - NA_ROWS // 2, 0), grid_rows - NA_ROWS)
            rows.append([(ws + jw - r + NA_ROWS - 1) if rs <= ws + jw < rs + NA_ROWS else None
                         for jw in range(NA_WIN_ROWS)])
        plan.append(rows)

    def expand_kernel(p_ref, o_ref):
        lane = lax.broadcasted_iota(jnp.int32, (GRID_W, 2 * GRID_W), 1)
        for kind, rows in enumerate(plan):
            @pl.when(pl.program_id(1) == kind)
            def _(rows=rows):
                for i, offs in enumerate(rows):
                    for jp in range(NA_WIN_ROWS // 2):
                        left, right = offs[2 * jp], offs[2 * jp + 1]
                        if left is None and right is None:
                            blk = jnp.full((GRID_W, 2 * GRID_W), NEG_INF, F32)
                        else:
                            blk = p_ref[right if right is not None else left + 1]
                            if left is None:
                                blk = jnp.where(lane >= GRID_W, blk, NEG_INF)
                            if right is None:
                                blk = jnp.where(lane < GRID_W, blk, NEG_INF)
                        o_ref[i * GRID_W:(i + 1) * GRID_W, 2 * jp * GRID_W:2 * (jp + 1) * GRID_W] = blk

    tq, win = NA_STEP_ROWS * GRID_W, NA_WIN_ROWS * GRID_W
    return pl.pallas_call(
        expand_kernel,
        grid=(heads, 3),
        in_specs=[pl.BlockSpec((None, n_dr + 1, GRID_W, 2 * GRID_W), lambda h, k: (h, 0, 0, 0))],
        out_specs=pl.BlockSpec((None, None, tq, win), lambda h, k: (h, k, 0, 0)),
        out_shape=jax.ShapeDtypeStruct((heads, 3, tq, win), F32),
        compiler_params=_params("parallel", "arbitrary"),
        name="na_bias_table",
    )(pairs)


def _na_kernel(q_ref, k_ref, v_ref, bias_ref, o_ref, *, n_lat, grid_rows):
    start = pl.multiple_of(_na_window_start(pl.program_id(2), grid_rows) * GRID_W, GRID_W)
    win = NA_WIN_ROWS * GRID_W
    q = (q_ref[...].astype(F32) * (HEAD_DIM ** -0.5 * LOG2E)).astype(k_ref.dtype)
    sl = lax.dot_general(q, k_ref[pl.ds(start, win), :], NT_DIMS, preferred_element_type=F32) + bias_ref[...]
    sc = lax.dot_general(q, k_ref[n_lat:, :], NT_DIMS, preferred_element_type=F32)
    m = jnp.maximum(jnp.max(sl, axis=-1, keepdims=True), jnp.max(sc, axis=-1, keepdims=True))
    el = jnp.exp2(sl - m)
    ec = jnp.exp2(sc - m)
    inv = 1.0 / (jnp.sum(el, axis=-1, keepdims=True) + jnp.sum(ec, axis=-1, keepdims=True))
    o = (jnp.dot(el.astype(v_ref.dtype), v_ref[pl.ds(start, win), :], preferred_element_type=F32)
         + jnp.dot(ec.astype(v_ref.dtype), v_ref[n_lat:, :], preferred_element_type=F32))
    o_ref[...] = (o * inv).astype(o_ref.dtype)


def neighbourhood_attention(pd, bias, bsz, heads, n_lat, n_ctx):
    s = n_lat + n_ctx
    grid_rows = n_lat // GRID_W
    tq = NA_STEP_ROWS * GRID_W
    win = NA_WIN_ROWS * GRID_W
    steps = grid_rows // NA_STEP_ROWS
    assert grid_rows >= NA_WIN_ROWS and grid_rows % NA_STEP_ROWS == 0 and steps >= 2
    assert NA_WIN_ROWS >= NA_STEP_ROWS + NA_ROWS - 1

    def kind(g):
        return jnp.where(g == 0, 0, jnp.where(g == steps - 1, 2, 1))

    pd3 = pd.reshape(bsz, s, pd.shape[-1])
    return pl.pallas_call(
        functools.partial(_na_kernel, n_lat=n_lat, grid_rows=grid_rows),
        grid=(bsz, heads, steps),
        in_specs=[
            pl.BlockSpec((None, tq, HEAD_DIM), lambda b, h, g: (b, g, h)),
            pl.BlockSpec((None, s, HEAD_DIM), lambda b, h, g: (b, 0, heads + h)),
            pl.BlockSpec((None, s, HEAD_DIM), lambda b, h, g: (b, 0, 2 * heads + h)),
            pl.BlockSpec((None, None, tq, win), lambda b, h, g: (h, kind(g), 0, 0)),
        ],
        out_specs=pl.BlockSpec((tq, HEAD_DIM), lambda b, h, g: (b * steps + g, h)),
        out_shape=jax.ShapeDtypeStruct((bsz * n_lat, heads * HEAD_DIM), BF16),
        compiler_params=_params("parallel", "parallel", "arbitrary"),
        name="neighbourhood_attention",
    )(pd3, pd3, pd3, bias)


def _hgrn_constants(reverse):
    c = CHUNK
    t = np.arange(c)
    tri = (t[None, :] >= t[:, None]) if reverse else (t[None, :] <= t[:, None])
    mats = [tri.astype(np.float32)]
    masks = [np.eye(c, dtype=np.float32)]
    m = c // 2
    while m >= 1:
        node = (t // (2 * m)) * (2 * m)
        ref = node + (m if reverse else m - 1)
        if m < HGRN_BROADCAST_MIN:
            mats.append(tri[ref].astype(np.float32))
        upper = (t % (2 * m)) >= m
        same = node[:, None] == node[None, :]
        pair = same & (upper[:, None] & ~upper[None, :])
        masks.append((pair.T if reverse else pair).astype(np.float32))
        m //= 2
    return np.concatenate(mats, axis=0), np.stack(masks)


HGRN_BROADCAST_MIN = 8


def _hgrn_reference_rows(cum, cs, lev, reverse):
    c = CHUNK
    m = c >> lev
    if m < HGRN_BROADCAST_MIN:
        k = lev - int(math.log2(c // HGRN_BROADCAST_MIN))
        return cs[k * c:(k + 1) * c]
    rows = []
    for node in range(0, c, 2 * m):
        ref = node + (m if reverse else m - 1)
        rows.append(jnp.broadcast_to(cum[ref:ref + 1, :], (2 * m, cum.shape[1])))
    return rows[0] if len(rows) == 1 else jnp.concatenate(rows, axis=0)


def _hgrn_kernel(cq_ref, ci_ref, z_ref, lb_ref, ms_ref, mk_ref, o_ref, st_ref, *, heads, reverse):
    @pl.when(pl.program_id(1) == 0)
    def _():
        st_ref[...] = jnp.zeros_like(st_ref)

    c = CHUNK
    n_levels = mk_ref.shape[0]
    ms = ms_ref[...].astype(BF16)
    scale = HEAD_DIM ** -0.5
    tot_row = 0 if reverse else c - 1
    for h in range(heads):
        sl = slice(h * HEAD_DIM, (h + 1) * HEAD_DIM)
        lb = lb_ref[:, sl]
        f = lb + (1.0 - lb) * _sigmoid(z_ref[:, sl].astype(F32))
        kk = 1.0 - f
        g = jnp.log(f)
        cq = cq_ref[:, sl].astype(F32)
        qq = cq * _sigmoid(cq) * scale
        v = ci_ref[:, sl]
        g_hi = g.astype(BF16)
        r1 = g - g_hi.astype(F32)
        g_mid = r1.astype(BF16)
        g_lo = (r1 - g_mid.astype(F32)).astype(BF16)
        cs3 = jnp.dot(ms, jnp.concatenate([g_hi, g_mid, g_lo], axis=1), preferred_element_type=F32)
        cs = cs3[:, :HEAD_DIM] + cs3[:, HEAD_DIM:2 * HEAD_DIM] + cs3[:, 2 * HEAD_DIM:]
        cum = cs[:c]
        tot = cs[tot_row:tot_row + 1]
        st = st_ref[h]
        o = lax.dot_general((qq * jnp.exp(cum)).astype(BF16), st.astype(BF16), NT_DIMS,
                            preferred_element_type=F32)
        a = mk_ref[0] * lax.dot_general(qq.astype(BF16), kk.astype(BF16), NT_DIMS,
                                        preferred_element_type=F32)
        for lev in range(1, n_levels):
            e = jnp.exp(-jnp.abs(cum - _hgrn_reference_rows(cum, cs, lev, reverse)))
            a = a + mk_ref[lev] * lax.dot_general((qq * e).astype(BF16), (kk * e).astype(BF16), NT_DIMS,
                                                  preferred_element_type=F32)
        o = o + jnp.dot(a.astype(BF16), v, preferred_element_type=F32)
        o_ref[:, sl] = o
        ke = (kk * jnp.exp(tot - cum)).astype(BF16)
        st_ref[h] = st * jnp.exp(tot) + lax.dot_general(v, ke, TN_DIMS, preferred_element_type=F32)


def hgrn_scan(pc, lb, bsz, heads, n_lat, n_ctx, reverse):
    s = n_lat + n_ctx
    chunks = s // CHUNK
    lat_chunks = n_lat // CHUNK
    width = heads * HEAD_DIM
    ms, mk = _hgrn_constants(reverse)

    def chunk_of(step):
        return (chunks - 1 - step) if reverse else lax.rem(step + lat_chunks, chunks)

    def spec(col):
        return pl.BlockSpec((CHUNK, width), lambda b, t: (b * chunks + chunk_of(t), col))

    return pl.pallas_call(
        functools.partial(_hgrn_kernel, heads=heads, reverse=reverse),
        grid=(bsz, chunks),
        in_specs=[spec(0), spec(1), spec(3 if reverse else 2),
                  pl.BlockSpec((1, width), lambda b, t: (0, 0)),
                  pl.BlockSpec(ms.shape, lambda b, t: (0, 0)),
                  pl.BlockSpec(mk.shape, lambda b, t: (0, 0, 0))],
        out_specs=spec(0),
        out_shape=jax.ShapeDtypeStruct((bsz * s, width), F32),
        scratch_shapes=[pltpu.VMEM((heads, HEAD_DIM, HEAD_DIM), F32)],
        compiler_params=_params("parallel", "arbitrary"),
        name="hgrn_scan_bwd" if reverse else "hgrn_scan_fwd",
    )(pc, pc, pc, lb.reshape(1, width), jnp.asarray(ms), jnp.asarray(mk))


def _hgrn_readout_kernel(of_ref, ob_ref, gate_ref, g_ref, o_ref):
    gain = g_ref[...]
    for h in range(of_ref.shape[1] // HEAD_DIM):
        sl = slice(h * HEAD_DIM, (h + 1) * HEAD_DIM)
        gate = gate_ref[:, sl].astype(F32)
        o = _rms(of_ref[:, sl] + ob_ref[:, sl], gain)
        o_ref[:, sl] = (o * (gate * _sigmoid(gate))).astype(o_ref.dtype)


def hgrn_readout(o_fwd, o_bwd, pc, gain, bsz, s, rows, tr=256):
    width = o_fwd.shape[1]
    in_tiles, out_tiles = s // tr, rows // tr
    ispec = pl.BlockSpec((tr, width), lambda b, i: (b * in_tiles + i, 0))
    return pl.pallas_call(
        _hgrn_readout_kernel,
        grid=(bsz, out_tiles),
        in_specs=[ispec, ispec, pl.BlockSpec((tr, width), lambda b, i: (b * in_tiles + i, 4)),
                  pl.BlockSpec((1, HEAD_DIM), lambda b, i: (0, 0))],
        out_specs=pl.BlockSpec((tr, width), lambda b, i: (b * out_tiles + i, 0)),
        out_shape=jax.ShapeDtypeStruct((bsz * rows, width), BF16),
        compiler_params=_params("parallel", "parallel"),
        name="hgrn_readout",
    )(o_fwd, o_bwd, pc, gain.reshape(1, HEAD_DIM))


def _mla_prep_kernel(p_ref, qg_ref, kg_ref, cos_ref, sn_ref, sp_ref, qn_ref, cn_ref, kr_ref, *, q_rank, kv_rank):
    qn_ref[...] = _rms(p_ref[:, :q_rank].astype(F32), qg_ref[...]).astype(qn_ref.dtype)
    cn_ref[...] = _rms(p_ref[:, q_rank:q_rank + kv_rank].astype(F32), kg_ref[...]).astype(cn_ref.dtype)
    kr = p_ref[:, q_rank + kv_rank:q_rank + kv_rank + LANE].astype(F32)
    kr_ref[...] = _rope_apply(kr, cos_ref[...], sn_ref[...], sp_ref[...]).astype(kr_ref.dtype)


def mla_prep(pb, q_gain, kv_gain, tables, s, tr=256):
    r, width = pb.shape
    q_rank, kv_rank = q_gain.shape[0], kv_gain.shape[0]
    assert width >= q_rank + kv_rank + LANE and q_rank % LANE == 0 and kv_rank % LANE == 0
    tiles_per_seq = s // tr
    tspec = pl.BlockSpec((tr, LANE), lambda i: (i % tiles_per_seq, 0))

    def ospec(w):
        return pl.BlockSpec((tr, w), lambda i: (i, 0))

    return pl.pallas_call(
        functools.partial(_mla_prep_kernel, q_rank=q_rank, kv_rank=kv_rank),
        grid=(r // tr,),
        in_specs=[ospec(width), pl.BlockSpec((1, q_rank), lambda i: (0, 0)),
                  pl.BlockSpec((1, kv_rank), lambda i: (0, 0)), tspec, tspec, tspec],
        out_specs=[ospec(q_rank), ospec(kv_rank), ospec(LANE)],
        out_shape=[jax.ShapeDtypeStruct((r, q_rank), BF16), jax.ShapeDtypeStruct((r, kv_rank), BF16),
                   jax.ShapeDtypeStruct((r, LANE), BF16)],
        compiler_params=_params("parallel"),
        name="mla_prep",
    )(pb, q_gain.reshape(1, q_rank), kv_gain.reshape(1, kv_rank), *tables)


def _merge_kernel(oa_ref, ob_ref, oc_ref, od_ref, ga_ref, gb_ref, gc_ref, gd_ref, w_ref, o_ref):
    acc = None
    for i, (o_ref_i, g_ref_i) in enumerate(((oa_ref, ga_ref), (ob_ref, gb_ref), (oc_ref, gc_ref), (od_ref, gd_ref))):
        y = jnp.dot(o_ref_i[...], w_ref[i], preferred_element_type=F32)
        term = _sigmoid(g_ref_i[...].astype(F32)) * y
        acc = term if acc is None else acc + term
    o_ref[...] = acc.astype(o_ref.dtype)


def merge_branches(outs, gate_logits, w_branch):
    r, c = outs[0].shape
    d = w_branch.shape[2]
    bm = _pick(r, (512, 256, 128))
    bn = _pick(d, (1024, 512, 256, 128))
    nb = d // bn
    ospec = pl.BlockSpec((bm, c), lambda i, j: (i, 0))

    def gspec(branch):
        return pl.BlockSpec((bm, bn), lambda i, j: (i, branch * nb + j))

    return pl.pallas_call(
        _merge_kernel,
        grid=(r // bm, nb),
        in_specs=[ospec] * N_BRANCH + [gspec(i) for i in range(N_BRANCH)]
        + [pl.BlockSpec((N_BRANCH, c, bn), lambda i, j: (0, 0, j))],
        out_specs=pl.BlockSpec((bm, bn), lambda i, j: (i, j)),
        out_shape=jax.ShapeDtypeStruct((r, d), BF16),
        compiler_params=_params("parallel", "parallel"),
        name="merge_branches",
    )(*outs, gate_logits, gate_logits, gate_logits, gate_logits, w_branch)


HALO = 16


FFN_BN = 640


def _ffn_tile_pairs(a, bn):
    f = a.shape[-1] // 2
    lead = a.shape[:-1]
    pairs = jnp.stack([a[..., :f].reshape(*lead, f // bn, bn), a[..., f:].reshape(*lead, f // bn, bn)], axis=-2)
    return pairs.reshape(*lead, 2 * f)


def _cast_pair_kernel(g_ref, v_ref, o_ref):
    bn = g_ref.shape[1]
    o_ref[:, :bn] = g_ref[...].astype(o_ref.dtype)
    o_ref[:, bn:] = v_ref[...].astype(o_ref.dtype)


def ffn_up_weight(w_up, layer, bn):
    _, d, two_f = w_up.shape
    nb = two_f // 2 // bn
    tk = _pick(d, (1024, 512, 256, 128))
    return pl.pallas_call(
        _cast_pair_kernel,
        grid=(d // tk, nb),
        in_specs=[pl.BlockSpec((None, tk, bn), lambda i, j: (layer, i, j)),
                  pl.BlockSpec((None, tk, bn), lambda i, j: (layer, i, nb + j))],
        out_specs=pl.BlockSpec((tk, 2 * bn), lambda i, j: (i, j)),
        out_shape=jax.ShapeDtypeStruct((d, two_f), BF16),
        compiler_params=_params("parallel", "parallel"),
        name="ffn_up_weight",
    )(w_up, w_up)


FFN_EPILOGUE_ROWS = 256


def _ffn_up_kernel(h_ref, hp_ref, hn_ref, w_ref, cw_ref, cb_ref, o_ref, hbuf_ref, u_ref, *, bm, s, n_lat):
    i = pl.program_id(0)
    bn = o_ref.shape[1]

    @pl.when(pl.program_id(1) == 0)
    def _():
        hbuf_ref[0:HALO, :] = hp_ref[...]
        hbuf_ref[HALO:HALO + bm, :] = h_ref[...]
        hbuf_ref[HALO + bm:, :] = hn_ref[...]

    u_ref[...] = jnp.dot(hbuf_ref[...], w_ref[...], preferred_element_type=F32)
    cw = cw_ref[...]
    cb = cb_ref[...]
    er = min(FFN_EPILOGUE_ROWS, bm)
    for c in range(bm // er):
        r0 = HALO + c * er
        pos = lax.rem(i * bm + c * er + lax.broadcasted_iota(jnp.int32, (er, 1), 0), s)
        has_prev = jnp.where((pos == 0) | (pos == n_lat), 0.0, 1.0)
        has_next = jnp.where((pos == n_lat - 1) | (pos == s - 1), 0.0, 1.0)
        win = u_ref[r0 - 8:r0 + er + 8, :]
        prev = pltpu.roll(win, 1, 0)[8:8 + er]
        nxt = pltpu.roll(win, er + 16 - 1, 0)[8:8 + er]
        y = prev * has_prev * cw[0:1] + win[8:8 + er] * cw[1:2] + nxt * has_next * cw[2:3] + cb
        gate, val = y[:, :bn], y[:, bn:]
        o_ref[c * er:(c + 1) * er, :] = (gate * _sigmoid(gate) * val).astype(o_ref.dtype)


def ffn_up(h, w_up, conv_w, conv_b, s, n_lat):
    r, d = h.shape
    f = w_up.shape[1] // 2
    bm = _pick(r, (1024, 512, 256, 128))
    bn = FFN_BN
    assert f % bn == 0
    halo_blocks = r // HALO
    per_tile = bm // HALO
    return pl.pallas_call(
        functools.partial(_ffn_up_kernel, bm=bm, s=s, n_lat=n_lat),
        grid=(r // bm, f // bn),
        in_specs=[
            pl.BlockSpec((bm, d), lambda i, j: (i, 0), pipeline_mode=pl.Buffered(1)),
            pl.BlockSpec((HALO, d), lambda i, j: (jnp.maximum(i * per_tile - 1, 0), 0)),
            pl.BlockSpec((HALO, d), lambda i, j: (jnp.minimum((i + 1) * per_tile, halo_blocks - 1), 0)),
            pl.BlockSpec((d, 2 * bn), lambda i, j: (0, j)),
            pl.BlockSpec((3, 2 * bn), lambda i, j: (0, j)),
            pl.BlockSpec((1, 2 * bn), lambda i, j: (0, j)),
        ],
        out_specs=pl.BlockSpec((bm, bn), lambda i, j: (i, j)),
        out_shape=jax.ShapeDtypeStruct((r, f), BF16),
        scratch_shapes=[pltpu.VMEM((bm + 2 * HALO, d), BF16), pltpu.VMEM((bm + 2 * HALO, 2 * bn), F32)],
        compiler_params=_params("parallel", "arbitrary"),
        name="ffn_up",
    )(h, h, h, w_up, conv_w, conv_b)


W_IN_SPLIT_ROWS = 128


def _w_in_split_kernel(wt_ref, *out_refs, starts):
    for o_ref, start in zip(out_refs, starts):
        width = o_ref.shape[1]
        o_ref[...] = wt_ref[start:start + width, :].T.astype(o_ref.dtype)


def _split_w_in(w_in, layer, d, heads):
    mix = heads * HEAD_DIM
    q_rank = 3 * d // 16
    kv_rank = d // 8
    b_used = q_rank + kv_rank + MLA_ROPE
    b_width = -(-(q_rank + kv_rank + LANE) // 512) * 512
    sizes = [3 * mix, b_used, 5 * mix, 3 * mix, N_BRANCH * d]
    starts = [sum(sizes[:i]) for i in range(len(sizes))]
    total = w_in.shape[2]
    assert sum(sizes) == total and starts[1] + b_width <= total
    widths = [sizes[0], b_width, sizes[2], sizes[3], sizes[4]]
    assert all(s % 16 == 0 for s in starts) and all(w % LANE == 0 for w in widths)
    tk = W_IN_SPLIT_ROWS
    outs = pl.pallas_call(
        functools.partial(_w_in_split_kernel, starts=tuple(starts)),
        grid=(d // tk,),
        in_specs=[pl.BlockSpec((None, total, tk), lambda i: (layer, 0, i))],
        out_specs=[pl.BlockSpec((tk, w), lambda i: (i, 0)) for w in widths],
        out_shape=[jax.ShapeDtypeStruct((d, w), BF16) for w in widths],
        compiler_params=_params("parallel"),
        name="w_in_split",
    )(jnp.swapaxes(w_in, 1, 2))
    return (*outs, q_rank, kv_rank)


def _mla_q_weight(w_qb, heads):
    rq = w_qb.shape[0]
    w = w_qb.reshape(rq, heads, MLA_NOPE + MLA_ROPE)
    nope = w[:, :, :MLA_NOPE].reshape(rq, heads * MLA_NOPE)
    ropep = jnp.pad(w[:, :, MLA_NOPE:], ((0, 0), (0, 0), (0, LANE - MLA_ROPE))).reshape(rq, heads * LANE)
    return jnp.concatenate([nope, ropep], axis=1).astype(BF16)


def kernel(x, c, ctx, c_ctx, ada_w, ada_b, norm_mix_pre, norm_mix_post, norm_ffn_pre, norm_ffn_post, w_in,
           diff_lambda, diff_subln, mla_q_norm, mla_w_qb, mla_kv_norm, mla_w_kvb, hgrn_lb_logits, hgrn_norm,
           na_rpb, w_branch, w_out, ffn_w_up, ffn_conv_w, ffn_conv_b, ffn_w_down):
    bsz, n_lat, d = x.shape
    n_ctx = ctx.shape[1]
    s = n_lat + n_ctx
    r = bsz * s
    depth = ada_w.shape[0]
    heads = d // (N_BRANCH * HEAD_DIM)
    mix = heads * HEAD_DIM
    assert n_lat % 256 == 0 and n_ctx % 256 == 0 and s % n_ctx == 0 and n_lat % GRID_W == 0

    tables = _rope_tables(n_lat, n_ctx)
    lb_sm = jax.nn.softmax(hgrn_lb_logits.astype(F32), axis=0)
    lower_bounds = jnp.cumsum(lb_sm, axis=0) - lb_sm[0]

    c_rows = jnp.zeros((8, d), F32).at[:bsz].set(c).at[bsz].set(c_ctx)
    xa = jnp.concatenate([x, ctx], axis=1)

    def mod_rows(mod, idx):
        part = mod[:, idx * d:(idx + 1) * d]
        both = jnp.stack([part[:bsz], jnp.broadcast_to(part[bsz], (bsz, d))], axis=1)
        return both.reshape(2 * bsz, 1, d)

    mods = []
    for l in range(depth):
        mod = ada_modulation(c_rows, ada_w, ada_b, l)
        mods.append([mod_rows(mod, i) for i in range(6)])

    h = norm_modulate(xa, norm_mix_pre[0], mods[0][0], mods[0][1], n_lat)
    for l in range(depth):
        with_ctx = l < depth - 1
        sh1, sc1, g1, sh2, sc2, g2 = mods[l]
        h2d = h.reshape(r, d)
        rows = s if with_ctx else n_lat

        w_a, w_b, w_c, w_d, w_g, q_rank, kv_rank = _split_w_in(w_in, l, d, heads)
        pa = matmul(h2d, w_a, BF16, "in_proj_a")
        pb = matmul(h2d, w_b, BF16, "in_proj_b")
        pc = matmul(h2d, w_c, BF16, "in_proj_c")
        pd = matmul(h2d, w_d, BF16, "in_proj_d")
        if with_ctx:
            pg = matmul(h2d, w_g, BF16, "in_proj_gate")
        else:
            pg = matmul_leading_rows(h, w_g, rows, BF16, "in_proj_gate")

        lam_init = 0.8 - 0.6 * math.exp(-0.3 * l)
        extras_a = [diff_lambda[l].astype(F32), diff_subln[l].reshape(1, HEAD_DIM).astype(F32)]

        def call_a(q_range, k_range, tag):
            return _attention_call(
                functools.partial(_diff_attn_kernel, lam_init=lam_init),
                [(pa, 0)], [(pa, heads, 1, False)], (pa, 2 * heads, 1), extras_a,
                bsz=bsz, heads=heads, s=s, q_range=q_range, k_range=k_range, tq=min(1024, q_range[1]), lanes=256,
                name="diff_attn_" + tag, q_row_tables=tables, k_row_tables=tables)

        oa = _attend_lat_and_ctx(call_a, n_lat, n_ctx, with_ctx)

        qn, cn, kr = mla_prep(pb, mla_q_norm[l], mla_kv_norm[l], tables, s)
        qb = matmul(qn, _mla_q_weight(mla_w_qb[l], heads), BF16, "mla_q_up")
        kvb = matmul(cn, mla_w_kvb[l].astype(BF16), BF16, "mla_kv_up")

        def call_b(q_range, k_range, tag):
            return _attention_call(
                functools.partial(_sdpa_kernel, n_qk=2, scale=(MLA_NOPE + MLA_ROPE) ** -0.5, rope_q=1),
                [(qb, 0), (qb, heads)], [(kvb, 0, 2, False), (kr, 0, 0, True)], (kvb, 1, 2), [],
                bsz=bsz, heads=heads, s=s, q_range=q_range, k_range=k_range, tq=min(2048, q_range[1]), lanes=256,
                name="mla_attn_" + tag, q_row_tables=tables)

        ob = _attend_lat_and_ctx(call_b, n_lat, n_ctx, with_ctx)

        o_fwd = hgrn_scan(pc, lower_bounds[l, 0], bsz, heads, n_lat, n_ctx, reverse=False)
        o_bwd = hgrn_scan(pc, lower_bounds[l, 1], bsz, heads, n_lat, n_ctx, reverse=True)
        oc = hgrn_readout(o_fwd, o_bwd, pc, hgrn_norm[l], bsz, s, rows)

        od = neighbourhood_attention(pd, _na_bias_table(na_rpb[l], n_lat // GRID_W), bsz, heads, n_lat, n_ctx)
        if with_ctx:
            od_ctx = _attention_call(
                functools.partial(_sdpa_kernel, n_qk=1, scale=HEAD_DIM ** -0.5, rope_q=None),
                [(pd, 0)], [(pd, heads, 1, False)], (pd, 2 * heads, 1), [],
                bsz=bsz, heads=heads, s=s, q_range=(n_lat, n_ctx), k_range=(n_lat, n_ctx), tq=256, lanes=256,
                name="na_ctx_attn").reshape(bsz, n_ctx, mix)
            od = jnp.concatenate([od.reshape(bsz, n_lat, mix), od_ctx], axis=1).reshape(r, mix)

        merged = merge_branches((oa, ob, oc, od), pg, w_branch[l].astype(BF16))
        mix_out = matmul(merged, w_out[l].astype(BF16), BF16, "out_proj").reshape(bsz, rows, d)
        xa, h2 = residual_norm_modulate(xa, mix_out, g1, norm_mix_post[l], norm_ffn_pre[l], sh2, sc2, n_lat)

        gv = ffn_up(h2.reshape(bsz * rows, d), ffn_up_weight(ffn_w_up, l, FFN_BN),
                    _ffn_tile_pairs(ffn_conv_w[l], FFN_BN), _ffn_tile_pairs(ffn_conv_b[l][None], FFN_BN),
                    rows, n_lat)
        ffn = matmul(gv, ffn_w_down[l].astype(BF16), BF16, "ffn_down").reshape(bsz, rows, d)
        if with_ctx:
            xa, h = residual_norm_modulate(xa, ffn, g2, norm_ffn_post[l], norm_mix_pre[l + 1],
                                           mods[l + 1][0], mods[l + 1][1], n_lat)
        else:
            xa = residual(xa, ffn, g2, norm_ffn_post[l], n_lat)
    return xa
```

```python
import functools
import math

import numpy as np
import jax
import jax.numpy as jnp
from jax import lax
from jax.experimental import pallas as pl
from jax.experimental.pallas import tpu as pltpu

F32 = jnp.float32
BF16 = jnp.bfloat16

HEAD_DIM = 128
QK_HALF = 64
MLA_NOPE = 128
MLA_ROPE = 64
MLA_V = 128
ROPE_DIM = 64
ROPE_THETA = 10000.0
GRID_W = 64
NA_ROWS = 8
NA_COLS = 16
CHUNK = 128
NORM_EPS = 1e-6
NEG_INF = -1e30
N_BRANCH = 4

LANE = 128
VMEM_LIMIT_BYTES = 56 * 1024 * 1024

NT_DIMS = (((1,), (1,)), ((), ()))
TN_DIMS = (((0,), (0,)), ((), ()))


def _params(*sem):
    return pltpu.CompilerParams(dimension_semantics=sem, vmem_limit_bytes=VMEM_LIMIT_BYTES)


def _sigmoid(x):
    return 0.5 * jnp.tanh(0.5 * x) + 0.5


def _pick(total, candidates):
    for c in candidates:
        if total % c == 0:
            return c
    raise ValueError(f"no tile for {total} in {candidates}")


def _ada_kernel(c_ref, w_ref, b_ref, o_ref):
    c = c_ref[...]
    a = (c * _sigmoid(c)).astype(BF16)
    o_ref[...] = jnp.dot(a, w_ref[...].astype(BF16), preferred_element_type=F32) + b_ref[...]


def ada_modulation(c_rows, ada_w, ada_b, layer):
    rows, d = c_rows.shape
    width = ada_w.shape[2]
    tn = _pick(width, (512, 256, 128))
    b3 = ada_b.reshape(ada_b.shape[0], 1, width)
    return pl.pallas_call(
        _ada_kernel,
        grid=(width // tn,),
        in_specs=[
            pl.BlockSpec((rows, d), lambda j: (0, 0)),
            pl.BlockSpec((None, d, tn), lambda j: (layer, 0, j)),
            pl.BlockSpec((None, 1, tn), lambda j: (layer, 0, j)),
        ],
        out_specs=pl.BlockSpec((rows, tn), lambda j: (0, j)),
        out_shape=jax.ShapeDtypeStruct((rows, width), F32),
        compiler_params=_params("parallel"),
        name="ada_modulation",
    )(c_rows, ada_w, b3)


def _rms(x, gain):
    return x * lax.rsqrt(jnp.mean(x * x, axis=-1, keepdims=True) + NORM_EPS) * gain


def _norm_mod_kernel(x_ref, g_ref, sh_ref, sc_ref, h_ref):
    h = _rms(x_ref[0], g_ref[...])
    h_ref[0] = (h * (1.0 + sc_ref[0]) + sh_ref[0]).astype(h_ref.dtype)


def _resid_kernel(x_ref, y_ref, gate_ref, gpost_ref, xo_ref):
    xo_ref[0] = x_ref[0] + gate_ref[0] * _rms(y_ref[0].astype(F32), gpost_ref[...])


def _resid_norm_mod_kernel(x_ref, y_ref, gate_ref, gpost_ref, gpre_ref, sh_ref, sc_ref, xo_ref, h_ref):
    xn = x_ref[0] + gate_ref[0] * _rms(y_ref[0].astype(F32), gpost_ref[...])
    xo_ref[0] = xn
    h = _rms(xn, gpre_ref[...])
    h_ref[0] = (h * (1.0 + sc_ref[0]) + sh_ref[0]).astype(h_ref.dtype)


def _row_specs(dims, tr, n_lat_tiles):
    s, d = dims
    xspec = pl.BlockSpec((1, tr, d), lambda b, t: (b, t, 0))
    gspec = pl.BlockSpec((1, d), lambda b, t: (0, 0))
    mspec = pl.BlockSpec((1, 1, d), lambda b, t: (2 * b + t // n_lat_tiles, 0, 0))
    return xspec, gspec, mspec


def norm_modulate(x, gain, shift, scale, n_lat, tr=256):
    bsz, s, d = x.shape
    xspec, gspec, mspec = _row_specs((s, d), tr, n_lat // tr)
    return pl.pallas_call(
        _norm_mod_kernel,
        grid=(bsz, s // tr),
        in_specs=[xspec, gspec, mspec, mspec],
        out_specs=xspec,
        out_shape=jax.ShapeDtypeStruct(x.shape, BF16),
        compiler_params=_params("parallel", "parallel"),
        name="norm_modulate",
    )(x, gain.reshape(1, d), shift, scale)


def residual(x, y, gate, gain_post, n_lat, tr=256):
    bsz, s, d = x.shape
    rows = y.shape[1]
    xspec, gspec, mspec = _row_specs((s, d), tr, n_lat // tr)
    return pl.pallas_call(
        _resid_kernel,
        grid=(bsz, rows // tr),
        in_specs=[xspec, xspec, mspec, gspec],
        out_specs=xspec,
        out_shape=jax.ShapeDtypeStruct(y.shape, F32),
        compiler_params=_params("parallel", "parallel"),
        name="residual",
    )(x, y, gate, gain_post.reshape(1, d))


def residual_norm_modulate(x, y, gate, gain_post, gain_pre, shift, scale, n_lat, tr=256):
    bsz, s, d = x.shape
    rows = y.shape[1]
    xspec, gspec, mspec = _row_specs((s, d), tr, n_lat // tr)
    return pl.pallas_call(
        _resid_norm_mod_kernel,
        grid=(bsz, rows // tr),
        in_specs=[xspec, xspec, mspec, gspec, gspec, mspec, mspec],
        out_specs=[xspec, xspec],
        out_shape=[jax.ShapeDtypeStruct(y.shape, F32), jax.ShapeDtypeStruct(y.shape, BF16)],
        compiler_params=_params("parallel", "parallel"),
        name="residual_norm_modulate",
    )(x, y, gate, gain_post.reshape(1, d), gain_pre.reshape(1, d), shift, scale)


def _mm_kernel(a_ref, b_ref, o_ref):
    o_ref[...] = jnp.dot(a_ref[...], b_ref[...], preferred_element_type=F32).astype(o_ref.dtype)


def _mm_acc_kernel(a_ref, b_ref, o_ref, acc_ref):
    k = pl.program_id(2)

    @pl.when(k == 0)
    def _():
        acc_ref[...] = jnp.zeros_like(acc_ref)

    acc_ref[...] += jnp.dot(a_ref[...], b_ref[...], preferred_element_type=F32)

    @pl.when(k == pl.num_programs(2) - 1)
    def _():
        o_ref[...] = acc_ref[...].astype(o_ref.dtype)


MAX_SINGLE_K = 4096
MATMUL_VMEM_BUDGET = VMEM_LIMIT_BYTES * 9 // 10


def matmul_leading_rows(a, b, rows, out_dtype, name):
    bsz, _, k = a.shape
    n = b.shape[1]
    assert k <= MAX_SINGLE_K
    bm = _pick(rows, (1024, 512, 256, 128))
    bn = _pick(n, (1024, 768, 640, 512, 384, 256, 128))
    tiles = rows // bm
    return pl.pallas_call(
        _mm_kernel,
        grid=(bsz, tiles, n // bn),
        in_specs=[pl.BlockSpec((None, bm, k), lambda s, i, j: (s, i, 0)),
                  pl.BlockSpec((k, bn), lambda s, i, j: (0, j))],
        out_specs=pl.BlockSpec((bm, bn), lambda s, i, j: (s * tiles + i, j)),
        out_shape=jax.ShapeDtypeStruct((bsz * rows, n), out_dtype),
        compiler_params=_params("parallel", "parallel", "parallel"),
        name=name,
    )(a, b)


def matmul(a, b, out_dtype, name, layer=None):
    m, k = a.shape
    n = b.shape[-1]
    if layer is None:
        def bspec(rows, cols, index):
            return pl.BlockSpec((rows, cols), index)
    else:
        def bspec(rows, cols, index):
            return pl.BlockSpec((None, rows, cols), lambda *g: (layer, *index(*g)))
    out_bytes = jnp.dtype(out_dtype).itemsize
    full_k = None
    for cap in (1024, 512):
        bm = _pick(m, tuple(c for c in (1024, 512, 256, 128) if c <= cap))
        bn = _pick(n, tuple(c for c in (1024, 768, 640, 512, 384, 256, 128) if c <= cap))
        need = 2 * 2 * (bm * k + k * bn) + 2 * bm * bn * out_bytes + 4 * bm * bn
        if need <= MATMUL_VMEM_BUDGET:
            full_k = (bm, bn)
            break
    if full_k is not None:
        bm, bn = full_k
        return pl.pallas_call(
            _mm_kernel,
            grid=(m // bm, n // bn),
            in_specs=[pl.BlockSpec((bm, k), lambda i, j: (i, 0)),
                      bspec(k, bn, lambda i, j: (0, j))],
            out_specs=pl.BlockSpec((bm, bn), lambda i, j: (i, j)),
            out_shape=jax.ShapeDtypeStruct((m, n), out_dtype),
            compiler_params=_params("parallel", "parallel"),
            name=name,
        )(a, b)
    bm = _pick(m, (1024, 512, 256, 128))
    bn = _pick(n, (1024, 768, 640, 512, 384, 256, 128))
    nk = -(-k // MAX_SINGLE_K)
    while k % nk or (k // nk) % LANE:
        nk += 1
    bk = k // nk
    return pl.pallas_call(
        _mm_acc_kernel,
        grid=(m // bm, n // bn, nk),
        in_specs=[pl.BlockSpec((bm, bk), lambda i, j, kk: (i, kk)),
                  bspec(bk, bn, lambda i, j, kk: (kk, j))],
        out_specs=pl.BlockSpec((bm, bn), lambda i, j, kk: (i, j)),
        out_shape=jax.ShapeDtypeStruct((m, n), out_dtype),
        scratch_shapes=[pltpu.VMEM((bm, bn), F32)],
        compiler_params=_params("parallel", "parallel", "arbitrary"),
        name=name,
    )(a, b)


def _rope_tables(n_lat, n_ctx):
    t = jnp.arange(n_lat, dtype=jnp.int32)
    row = (t // GRID_W).astype(F32)
    col = (t % GRID_W).astype(F32)
    quarter = ROPE_DIM // 4
    inv = ROPE_THETA ** (-jnp.arange(quarter, dtype=F32) / quarter)
    ar = row[:, None] * inv
    ac = col[:, None] * inv
    ang = jnp.concatenate([ar, ar, ac, ac], axis=-1)
    cos = jnp.concatenate([jnp.cos(ang), jnp.ones((n_ctx, ROPE_DIM), F32)], axis=0)
    sin = jnp.concatenate([jnp.sin(ang), jnp.zeros((n_ctx, ROPE_DIM), F32)], axis=0)
    first = (jnp.arange(ROPE_DIM) // quarter) % 2 == 0
    sin_next = jnp.where(first, -sin, 0.0)
    sin_prev = jnp.where(first, 0.0, sin)
    reps = LANE // ROPE_DIM
    return tuple(jnp.tile(a, (1, reps)) for a in (cos, sin_next, sin_prev))


def _rope_apply(x, cos, sin_next, sin_prev):
    quarter = ROPE_DIM // 4
    return (x * cos + pltpu.roll(x, LANE - quarter, 1) * sin_next
            + pltpu.roll(x, quarter, 1) * sin_prev)


LOG2E = math.log2(math.e)
KEY_CHUNK = 256


def _attend_tiles(k_ref, q_tiles, vt_ref, s_ref, e_ref):
    m_keys, n = s_ref.shape[1:]
    kc = min(KEY_CHUNK, m_keys)
    chunks = [slice(c * kc, (c + 1) * kc) for c in range(m_keys // kc)]

    def scores(t):
        s_ref[t % 2] = lax.dot_general(k_ref[...], q_tiles[t], NT_DIMS, preferred_element_type=F32)

    def column_max(t):
        m8 = None
        for sl in chunks:
            part = jnp.max(s_ref[t % 2, sl, :].reshape(kc // 8, 8, n), axis=0)
            m8 = part if m8 is None else jnp.maximum(m8, part)
        return jnp.max(m8, axis=0, keepdims=True)

    scores(0)
    m = column_max(0)
    outs = []
    for t in range(len(q_tiles)):
        if t + 1 < len(q_tiles):
            scores(t + 1)
        l8 = jnp.zeros((8, n), F32)
        for sl in chunks:
            e = jnp.exp2(s_ref[t % 2, sl, :] - m)
            l8 = l8 + jnp.sum(e.reshape(kc // 8, 8, n), axis=0)
            e_ref[t % 2, sl, :] = e.astype(e_ref.dtype)
        o_t = jnp.dot(vt_ref[...], e_ref[t % 2], preferred_element_type=F32)
        outs.append((o_t, jnp.sum(l8, axis=0, keepdims=True)))
        if t + 1 < len(q_tiles):
            m = column_max(t + 1)
    return outs


def _load_v_transposed(v_ref, vt_ref):
    @pl.when(pl.program_id(2) == 0)
    def _():
        vt_ref[...] = v_ref[...].astype(F32).T.astype(vt_ref.dtype)


def _diff_attn_kernel(q_ref, k_ref, v_ref, lam_ref, g_ref, qcos_ref, qsn_ref, qsp_ref, kcos_ref, ksn_ref, ksp_ref,
                      o_ref, vt_ref, s_ref, e_ref, kr_ref, *, lam_init):
    _load_v_transposed(v_ref, vt_ref)

    @pl.when(pl.program_id(2) == 0)
    def _():
        kr_ref[...] = _rope_apply(k_ref[...].astype(F32), kcos_ref[...], ksn_ref[...], ksp_ref[...]
                                  ).astype(kr_ref.dtype)

    tq = s_ref.shape[2] // 2
    q = _rope_apply(q_ref[...].astype(F32), qcos_ref[...], qsn_ref[...], qsp_ref[...])
    q = (q * (QK_HALF ** -0.5 * LOG2E)).astype(kr_ref.dtype)
    lane = lax.broadcasted_iota(jnp.int32, q.shape, 1)
    zero = jnp.zeros_like(q)
    q0 = jnp.where(lane < QK_HALF, q, zero)
    q1 = jnp.where(lane >= QK_HALF, q, zero)
    tiles = [slice(t * tq, (t + 1) * tq) for t in range(q.shape[0] // tq)]
    outs = _attend_tiles(kr_ref, [jnp.concatenate([q0[sl], q1[sl]], axis=0) for sl in tiles], vt_ref, s_ref, e_ref)
    lp = lam_ref[...]
    lam = (jnp.exp(jnp.sum(lp[0:1] * lp[1:2], axis=-1, keepdims=True))
           - jnp.exp(jnp.sum(lp[2:3] * lp[3:4], axis=-1, keepdims=True)) + lam_init)
    gain = g_ref[...] * (1.0 - lam_init)
    for sl, (o_t, l) in zip(tiles, outs):
        inv = 1.0 / l
        c_t = o_t[:, :tq] * inv[:, :tq] - o_t[:, tq:] * (lam * inv[:, tq:])
        c_t = c_t * lax.rsqrt(jnp.mean(c_t * c_t, axis=0, keepdims=True) + NORM_EPS)
        o_ref[sl, :] = (c_t.T * gain).astype(o_ref.dtype)


def _sdpa_kernel(*refs, n_qk, scale, rope_q):
    q_refs, k_refs = refs[:n_qk], refs[n_qk:2 * n_qk]
    n_tab = 0 if rope_q is None else 3
    v_ref = refs[2 * n_qk]
    tables = refs[2 * n_qk + 1:2 * n_qk + 1 + n_tab]
    o_ref, vt_ref, s_ref, e_ref = refs[2 * n_qk + 1 + n_tab:2 * n_qk + 5 + n_tab]
    _load_v_transposed(v_ref, vt_ref)
    if n_qk == 1:
        kcat_ref = k_refs[0]
    else:
        kcat_ref = refs[2 * n_qk + 5 + n_tab]

        @pl.when(pl.program_id(2) == 0)
        def _():
            for i, k_ref in enumerate(k_refs):
                kcat_ref[:, i * HEAD_DIM:(i + 1) * HEAD_DIM] = k_ref[...]

    tq = s_ref.shape[2]
    parts = [q_ref[...].astype(F32) for q_ref in q_refs]
    if rope_q is not None:
        parts[rope_q] = _rope_apply(parts[rope_q], *[t[...] for t in tables])
    q = (jnp.concatenate(parts, axis=1) * (scale * LOG2E)).astype(kcat_ref.dtype)
    tiles = [slice(t * tq, (t + 1) * tq) for t in range(q.shape[0] // tq)]
    outs = _attend_tiles(kcat_ref, [q[sl] for sl in tiles], vt_ref, s_ref, e_ref)
    for sl, (o_t, l) in zip(tiles, outs):
        o_ref[sl, :] = (o_t * (1.0 / l)).T.astype(o_ref.dtype)


def _attention_call(kernel, q_list, k_list, v, extras, *, bsz, heads, s, q_range, k_range, tq, lanes, name,
                    q_row_tables=(), k_row_tables=()):
    q_start, q_len = q_range
    k_start, k_len = k_range
    nq = q_len // tq
    q_off = q_start // tq
    k_off = k_start // k_len
    assert q_start % tq == 0 and q_len % tq == 0 and k_start % k_len == 0

    def qspec(col0):
        return pl.BlockSpec((None, tq, HEAD_DIM), lambda b, h, i: (b, q_off + i, col0 + h))

    def kspec(col0, stride, shared=False):
        if shared:
            return pl.BlockSpec((None, k_len, HEAD_DIM), lambda b, h, i: (b, k_off, col0))
        return pl.BlockSpec((None, k_len, HEAD_DIM), lambda b, h, i: (b, k_off, col0 + stride * h))

    def per_sample(arr):
        return arr.reshape(bsz, s, arr.shape[-1])

    in_specs, args = [], []
    for arr, col0 in q_list:
        in_specs.append(qspec(col0)); args.append(per_sample(arr))
    for arr, col0, stride, shared in k_list:
        in_specs.append(kspec(col0, stride, shared)); args.append(per_sample(arr))
    arr, col0, stride = v
    in_specs.append(kspec(col0, stride)); args.append(per_sample(arr))
    for e in extras:
        in_specs.append(pl.BlockSpec(e.shape, lambda b, h, i: (0, 0))); args.append(e)
    for t in q_row_tables:
        in_specs.append(pl.BlockSpec((tq, LANE), lambda b, h, i: (q_off + i, 0))); args.append(t)
    for t in k_row_tables:
        in_specs.append(pl.BlockSpec((k_len, LANE), lambda b, h, i: (k_off, 0))); args.append(t)
    return pl.pallas_call(
        kernel,
        grid=(bsz, heads, nq),
        in_specs=in_specs,
        out_specs=pl.BlockSpec((tq, HEAD_DIM), lambda b, h, i: (b * nq + i, h)),
        out_shape=jax.ShapeDtypeStruct((bsz * q_len, heads * HEAD_DIM), BF16),
        scratch_shapes=[pltpu.VMEM((HEAD_DIM, k_len), BF16), pltpu.VMEM((2, k_len, lanes), F32),
                        pltpu.VMEM((2, k_len, lanes), BF16)]
        + ([pltpu.VMEM((k_len, len(k_list) * HEAD_DIM), BF16)] if len(k_list) > 1 else [])
        + ([pltpu.VMEM((k_len, HEAD_DIM), BF16)] if k_row_tables else []),
        compiler_params=_params("parallel", "parallel", "arbitrary"),
        name=name,
    )(*args)


def _attend_lat_and_ctx(call, n_lat, n_ctx, with_ctx):
    s = n_lat + n_ctx
    o_lat = call((0, n_lat), (0, s), "lat")
    if not with_ctx:
        return o_lat
    bsz = o_lat.shape[0] // n_lat
    o_ctx = call((n_lat, n_ctx), (n_lat, n_ctx), "ctx").reshape(bsz, n_ctx, -1)
    return jnp.concatenate([o_lat.reshape(bsz, n_lat, -1), o_ctx], axis=1).reshape(bsz * s, -1)


NA_STEP_ROWS = 8
NA_WIN_ROWS = 16


def _na_window_start(step, grid_rows):
    return jnp.clip(step * NA_STEP_ROWS - NA_ROWS // 2, 0, grid_rows - NA_WIN_ROWS)


def _na_bias_table(rpb, grid_rows):
    steps = grid_rows // NA_STEP_ROWS
    heads = rpb.shape[0]
    n_dr = 2 * NA_ROWS - 1
    col = jnp.arange(GRID_W, dtype=jnp.int32)
    cs = jnp.clip(col - NA_COLS // 2, 0, GRID_W - NA_COLS)
    colmask = (col[None, :] >= cs[:, None]) & (col[None, :] < cs[:, None] + NA_COLS)
    dc = jnp.clip(col[None, :] - col[:, None] + NA_COLS - 1, 0, 2 * NA_COLS - 2)
    pat = jnp.where(colmask[None, None], rpb[:, :, dc].astype(F32) * LOG2E, NEG_INF)
    blank = jnp.full((heads, 1, GRID_W, GRID_W), NEG_INF, F32)
    pairs = jnp.concatenate([jnp.concatenate([blank, pat], axis=1), jnp.concatenate([pat, blank], axis=1)], axis=-1)

    kind_steps = [0, min(1, steps - 1), steps - 1]
    plan = []
    for st in kind_steps:
        ws = min(max(st * NA_STEP_ROWS - NA_ROWS // 2, 0), grid_rows - NA_WIN_ROWS)
        rows = []
        for i in range(NA_STEP_ROWS):
            r = st * NA_STEP_ROWS + i
            rs = min(max(r - NA_ROWS // 2, 0), grid_rows - NA_ROWS)
            rows.append([(ws + jw - r + NA_ROWS - 1) if rs <= ws + jw < rs + NA_ROWS else None
                         for jw in range(NA_WIN_ROWS)])
        plan.append(rows)

    def expand_kernel(p_ref, o_ref):
        lane = lax.broadcasted_iota(jnp.int32, (GRID_W, 2 * GRID_W), 1)
        for kind, rows in enumerate(plan):
            @pl.when(pl.program_id(1) == kind)
            def _(rows=rows):
                for i, offs in enumerate(rows):
                    for jp in range(NA_WIN_ROWS // 2):
                        left, right = offs[2 * jp], offs[2 * jp + 1]
                        if left is None and right is None:
                            blk = jnp.full((GRID_W, 2 * GRID_W), NEG_INF, F32)
                        else:
                            blk = p_ref[right if right is not None else left + 1]
                            if left is None:
                                blk = jnp.where(lane >= GRID_W, blk, NEG_INF)
                            if right is None:
                                blk = jnp.where(lane < GRID_W, blk, NEG_INF)
                        o_ref[i * GRID_W:(i + 1) * GRID_W, 2 * jp * GRID_W:2 * (jp + 1) * GRID_W] = blk

    tq, win = NA_STEP_ROWS * GRID_W, NA_WIN_ROWS * GRID_W
    return pl.pallas_call(
        expand_kernel,
        grid=(heads, 3),
        in_specs=[pl.BlockSpec((None, n_dr + 1, GRID_W, 2 * GRID_W), lambda h, k: (h, 0, 0, 0))],
        out_specs=pl.BlockSpec((None, None, tq, win), lambda h, k: (h, k, 0, 0)),
        out_shape=jax.ShapeDtypeStruct((heads, 3, tq, win), F32),
        compiler_params=_params("parallel", "arbitrary"),
        name="na_bias_table",
    )(pairs)


def _na_kernel(q_ref, k_ref, v_ref, bias_ref, o_ref, *, n_lat, grid_rows):
    start = pl.multiple_of(_na_window_start(pl.program_id(2), grid_rows) * GRID_W, GRID_W)
    win = NA_WIN_ROWS * GRID_W
    q = (q_ref[...].astype(F32) * (HEAD_DIM ** -0.5 * LOG2E)).astype(k_ref.dtype)
    sl = lax.dot_general(q, k_ref[pl.ds(start, win), :], NT_DIMS, preferred_element_type=F32) + bias_ref[...]
    sc = lax.dot_general(q, k_ref[n_lat:, :], NT_DIMS, preferred_element_type=F32)
    m = jnp.maximum(jnp.max(sl, axis=-1, keepdims=True), jnp.max(sc, axis=-1, keepdims=True))
    el = jnp.exp2(sl - m)
    ec = jnp.exp2(sc - m)
    inv = 1.0 / (jnp.sum(el, axis=-1, keepdims=True) + jnp.sum(ec, axis=-1, keepdims=True))
    o = (jnp.dot(el.astype(v_ref.dtype), v_ref[pl.ds(start, win), :], preferred_element_type=F32)
         + jnp.dot(ec.astype(v_ref.dtype), v_ref[n_lat:, :], preferred_element_type=F32))
    o_ref[...] = (o * inv).astype(o_ref.dtype)


def neighbourhood_attention(pd, bias, bsz, heads, n_lat, n_ctx):
    s = n_lat + n_ctx
    grid_rows = n_lat // GRID_W
    tq = NA_STEP_ROWS * GRID_W
    win = NA_WIN_ROWS * GRID_W
    steps = grid_rows // NA_STEP_ROWS
    assert grid_rows >= NA_WIN_ROWS and grid_rows % NA_STEP_ROWS == 0 and steps >= 2
    assert NA_WIN_ROWS >= NA_STEP_ROWS + NA_ROWS - 1

    def kind(g):
        return jnp.where(g == 0, 0, jnp.where(g == steps - 1, 2, 1))

    pd3 = pd.reshape(bsz, s, pd.shape[-1])
    return pl.pallas_call(
        functools.partial(_na_kernel, n_lat=n_lat, grid_rows=grid_rows),
        grid=(bsz, heads, steps),
        in_specs=[
            pl.BlockSpec((None, tq, HEAD_DIM), lambda b, h, g: (b, g, h)),
            pl.BlockSpec((None, s, HEAD_DIM), lambda b, h, g: (b, 0, heads + h)),
            pl.BlockSpec((None, s, HEAD_DIM), lambda b, h, g: (b, 0, 2 * heads + h)),
            pl.BlockSpec((None, None, tq, win), lambda b, h, g: (h, kind(g), 0, 0)),
        ],
        out_specs=pl.BlockSpec((tq, HEAD_DIM), lambda b, h, g: (b * steps + g, h)),
        out_shape=jax.ShapeDtypeStruct((bsz * n_lat, heads * HEAD_DIM), BF16),
        compiler_params=_params("parallel", "parallel", "arbitrary"),
        name="neighbourhood_attention",
    )(pd3, pd3, pd3, bias)


def _hgrn_constants(reverse):
    c = CHUNK
    t = np.arange(c)
    tri = (t[None, :] >= t[:, None]) if reverse else (t[None, :] <= t[:, None])
    mats = [tri.astype(np.float32)]
    masks = [np.eye(c, dtype=np.float32)]
    m = c // 2
    while m >= 1:
        node = (t // (2 * m)) * (2 * m)
        ref = node + (m if reverse else m - 1)
        if m < HGRN_BROADCAST_MIN:
            mats.append(tri[ref].astype(np.float32))
        upper = (t % (2 * m)) >= m
        same = node[:, None] == node[None, :]
        pair = same & (upper[:, None] & ~upper[None, :])
        masks.append((pair.T if reverse else pair).astype(np.float32))
        m //= 2
    return np.concatenate(mats, axis=0), np.stack(masks)


HGRN_BROADCAST_MIN = 8


def _hgrn_reference_rows(cum, cs, lev, reverse):
    c = CHUNK
    m = c >> lev
    if m < HGRN_BROADCAST_MIN:
        k = lev - int(math.log2(c // HGRN_BROADCAST_MIN))
        return cs[k * c:(k + 1) * c]
    rows = []
    for node in range(0, c, 2 * m):
        ref = node + (m if reverse else m - 1)
        rows.append(jnp.broadcast_to(cum[ref:ref + 1, :], (2 * m, cum.shape[1])))
    return rows[0] if len(rows) == 1 else jnp.concatenate(rows, axis=0)


def _hgrn_kernel(cq_ref, ci_ref, z_ref, lb_ref, ms_ref, mk_ref, o_ref, st_ref, *, heads, reverse):
    @pl.when(pl.program_id(1) == 0)
    def _():
        st_ref[...] = jnp.zeros_like(st_ref)

    c = CHUNK
    n_levels = mk_ref.shape[0]
    ms = ms_ref[...].astype(BF16)
    scale = HEAD_DIM ** -0.5
    tot_row = 0 if reverse else c - 1
    for h in range(heads):
        sl = slice(h * HEAD_DIM, (h + 1) * HEAD_DIM)
        lb = lb_ref[:, sl]
        f = lb + (1.0 - lb) * _sigmoid(z_ref[:, sl].astype(F32))
        kk = 1.0 - f
        g = jnp.log(f)
        cq = cq_ref[:, sl].astype(F32)
        qq = cq * _sigmoid(cq) * scale
        v = ci_ref[:, sl]
        g_hi = g.astype(BF16)
        r1 = g - g_hi.astype(F32)
        g_mid = r1.astype(BF16)
        g_lo = (r1 - g_mid.astype(F32)).astype(BF16)
        cs3 = jnp.dot(ms, jnp.concatenate([g_hi, g_mid, g_lo], axis=1), preferred_element_type=F32)
        cs = cs3[:, :HEAD_DIM] + cs3[:, HEAD_DIM:2 * HEAD_DIM] + cs3[:, 2 * HEAD_DIM:]
        cum = cs[:c]
        tot = cs[tot_row:tot_row + 1]
        st = st_ref[h]
        o = lax.dot_general((qq * jnp.exp(cum)).astype(BF16), st.astype(BF16), NT_DIMS,
                            preferred_element_type=F32)
        a = mk_ref[0] * lax.dot_general(qq.astype(BF16), kk.astype(BF16), NT_DIMS,
                                        preferred_element_type=F32)
        for lev in range(1, n_levels):
            e = jnp.exp(-jnp.abs(cum - _hgrn_reference_rows(cum, cs, lev, reverse)))
            a = a + mk_ref[lev] * lax.dot_general((qq * e).astype(BF16), (kk * e).astype(BF16), NT_DIMS,
                                                  preferred_element_type=F32)
        o = o + jnp.dot(a.astype(BF16), v, preferred_element_type=F32)
        o_ref[:, sl] = o
        ke = (kk * jnp.exp(tot - cum)).astype(BF16)
        st_ref[h] = st * jnp.exp(tot) + lax.dot_general(v, ke, TN_DIMS, preferred_element_type=F32)


def hgrn_scan(pc, lb, bsz, heads, n_lat, n_ctx, reverse):
    s = n_lat + n_ctx
    chunks = s // CHUNK
    lat_chunks = n_lat // CHUNK
    width = heads * HEAD_DIM
    ms, mk = _hgrn_constants(reverse)

    def chunk_of(step):
        return (chunks - 1 - step) if reverse else lax.rem(step + lat_chunks, chunks)

    def spec(col):
        return pl.BlockSpec((CHUNK, width), lambda b, t: (b * chunks + chunk_of(t), col))

    return pl.pallas_call(
        functools.partial(_hgrn_kernel, heads=heads, reverse=reverse),
        grid=(bsz, chunks),
        in_specs=[spec(0), spec(1), spec(3 if reverse else 2),
                  pl.BlockSpec((1, width), lambda b, t: (0, 0)),
                  pl.BlockSpec(ms.shape, lambda b, t: (0, 0)),
                  pl.BlockSpec(mk.shape, lambda b, t: (0, 0, 0))],
        out_specs=spec(0),
        out_shape=jax.ShapeDtypeStruct((bsz * s, width), F32),
        scratch_shapes=[pltpu.VMEM((heads, HEAD_DIM, HEAD_DIM), F32)],
        compiler_params=_params("parallel", "arbitrary"),
        name="hgrn_scan_bwd" if reverse else "hgrn_scan_fwd",
    )(pc, pc, pc, lb.reshape(1, width), jnp.asarray(ms), jnp.asarray(mk))


def _hgrn_readout_kernel(of_ref, ob_ref, gate_ref, g_ref, o_ref):
    gain = g_ref[...]
    for h in range(of_ref.shape[1] // HEAD_DIM):
        sl = slice(h * HEAD_DIM, (h + 1) * HEAD_DIM)
        gate = gate_ref[:, sl].astype(F32)
        o = _rms(of_ref[:, sl] + ob_ref[:, sl], gain)
        o_ref[:, sl] = (o * (gate * _sigmoid(gate))).astype(o_ref.dtype)


def hgrn_readout(o_fwd, o_bwd, pc, gain, bsz, s, rows, tr=256):
    width = o_fwd.shape[1]
    in_tiles, out_tiles = s // tr, rows // tr
    ispec = pl.BlockSpec((tr, width), lambda b, i: (b * in_tiles + i, 0))
    return pl.pallas_call(
        _hgrn_readout_kernel,
        grid=(bsz, out_tiles),
        in_specs=[ispec, ispec, pl.BlockSpec((tr, width), lambda b, i: (b * in_tiles + i, 4)),
                  pl.BlockSpec((1, HEAD_DIM), lambda b, i: (0, 0))],
        out_specs=pl.BlockSpec((tr, width), lambda b, i: (b * out_tiles + i, 0)),
        out_shape=jax.ShapeDtypeStruct((bsz * rows, width), BF16),
        compiler_params=_params("parallel", "parallel"),
        name="hgrn_readout",
    )(o_fwd, o_bwd, pc, gain.reshape(1, HEAD_DIM))


def _mla_prep_kernel(p_ref, qg_ref, kg_ref, cos_ref, sn_ref, sp_ref, qn_ref, cn_ref, kr_ref, *, q_rank, kv_rank):
    qn_ref[...] = _rms(p_ref[:, :q_rank].astype(F32), qg_ref[...]).astype(qn_ref.dtype)
    cn_ref[...] = _rms(p_ref[:, q_rank:q_rank + kv_rank].astype(F32), kg_ref[...]).astype(cn_ref.dtype)
    kr = p_ref[:, q_rank + kv_rank:q_rank + kv_rank + LANE].astype(F32)
    kr_ref[...] = _rope_apply(kr, cos_ref[...], sn_ref[...], sp_ref[...]).astype(kr_ref.dtype)


def mla_prep(pb, q_gain, kv_gain, tables, s, tr=256):
    r, width = pb.shape
    q_rank, kv_rank = q_gain.shape[0], kv_gain.shape[0]
    assert width >= q_rank + kv_rank + LANE and q_rank % LANE == 0 and kv_rank % LANE == 0
    tiles_per_seq = s // tr
    tspec = pl.BlockSpec((tr, LANE), lambda i: (i % tiles_per_seq, 0))

    def ospec(w):
        return pl.BlockSpec((tr, w), lambda i: (i, 0))

    return pl.pallas_call(
        functools.partial(_mla_prep_kernel, q_rank=q_rank, kv_rank=kv_rank),
        grid=(r // tr,),
        in_specs=[ospec(width), pl.BlockSpec((1, q_rank), lambda i: (0, 0)),
                  pl.BlockSpec((1, kv_rank), lambda i: (0, 0)), tspec, tspec, tspec],
        out_specs=[ospec(q_rank), ospec(kv_rank), ospec(LANE)],
        out_shape=[jax.ShapeDtypeStruct((r, q_rank), BF16), jax.ShapeDtypeStruct((r, kv_rank), BF16),
                   jax.ShapeDtypeStruct((r, LANE), BF16)],
        compiler_params=_params("parallel"),
        name="mla_prep",
    )(pb, q_gain.reshape(1, q_rank), kv_gain.reshape(1, kv_rank), *tables)


def _merge_kernel(oa_ref, ob_ref, oc_ref, od_ref, ga_ref, gb_ref, gc_ref, gd_ref, w_ref, o_ref):
    acc = None
    for i, (o_ref_i, g_ref_i) in enumerate(((oa_ref, ga_ref), (ob_ref, gb_ref), (oc_ref, gc_ref), (od_ref, gd_ref))):
        y = jnp.dot(o_ref_i[...], w_ref[i], preferred_element_type=F32)
        term = _sigmoid(g_ref_i[...].astype(F32)) * y
        acc = term if acc is None else acc + term
    o_ref[...] = acc.astype(o_ref.dtype)


def merge_branches(outs, gate_logits, w_branch, layer):
    r, c = outs[0].shape
    d = w_branch.shape[3]
    bm = _pick(r, (512, 256, 128))
    bn = _pick(d, (1024, 512, 256, 128))
    nb = d // bn
    ospec = pl.BlockSpec((bm, c), lambda i, j: (i, 0))

    def gspec(branch):
        return pl.BlockSpec((bm, bn), lambda i, j: (i, branch * nb + j))

    return pl.pallas_call(
        _merge_kernel,
        grid=(r // bm, nb),
        in_specs=[ospec] * N_BRANCH + [gspec(i) for i in range(N_BRANCH)]
        + [pl.BlockSpec((None, N_BRANCH, c, bn), lambda i, j: (layer, 0, 0, j))],
        out_specs=pl.BlockSpec((bm, bn), lambda i, j: (i, j)),
        out_shape=jax.ShapeDtypeStruct((r, d), BF16),
        compiler_params=_params("parallel", "parallel"),
        name="merge_branches",
    )(*outs, gate_logits, gate_logits, gate_logits, gate_logits, w_branch)


HALO = 16


FFN_BN = 640


def _ffn_tile_pairs(a, bn):
    f = a.shape[-1] // 2
    lead = a.shape[:-1]
    pairs = jnp.stack([a[..., :f].reshape(*lead, f // bn, bn), a[..., f:].reshape(*lead, f // bn, bn)], axis=-2)
    return pairs.reshape(*lead, 2 * f)


def _cast_pair_kernel(g_ref, v_ref, o_ref):
    bn = g_ref.shape[1]
    o_ref[:, :bn] = g_ref[...].astype(o_ref.dtype)
    o_ref[:, bn:] = v_ref[...].astype(o_ref.dtype)


def ffn_up_weight(w_up, layer, bn):
    _, d, two_f = w_up.shape
    nb = two_f // 2 // bn
    tk = _pick(d, (1024, 512, 256, 128))
    return pl.pallas_call(
        _cast_pair_kernel,
        grid=(d // tk, nb),
        in_specs=[pl.BlockSpec((None, tk, bn), lambda i, j: (layer, i, j)),
                  pl.BlockSpec((None, tk, bn), lambda i, j: (layer, i, nb + j))],
        out_specs=pl.BlockSpec((tk, 2 * bn), lambda i, j: (i, j)),
        out_shape=jax.ShapeDtypeStruct((d, two_f), BF16),
        compiler_params=_params("parallel", "parallel"),
        name="ffn_up_weight",
    )(w_up, w_up)


FFN_EPILOGUE_ROWS = 256


def _ffn_up_kernel(h_ref, hp_ref, hn_ref, w_ref, cw_ref, cb_ref, o_ref, hbuf_ref, u_ref, *, bm, s, n_lat):
    i = pl.program_id(0)
    bn = o_ref.shape[1]

    @pl.when(pl.program_id(1) == 0)
    def _():
        hbuf_ref[0:HALO, :] = hp_ref[...]
        hbuf_ref[HALO:HALO + bm, :] = h_ref[...]
        hbuf_ref[HALO + bm:, :] = hn_ref[...]

    u_ref[...] = jnp.dot(hbuf_ref[...], w_ref[...], preferred_element_type=F32)
    cw = cw_ref[...]
    cb = cb_ref[...]
    er = min(FFN_EPILOGUE_ROWS, bm)
    for c in range(bm // er):
        r0 = HALO + c * er
        pos = lax.rem(i * bm + c * er + lax.broadcasted_iota(jnp.int32, (er, 1), 0), s)
        has_prev = jnp.where((pos == 0) | (pos == n_lat), 0.0, 1.0)
        has_next = jnp.where((pos == n_lat - 1) | (pos == s - 1), 0.0, 1.0)
        win = u_ref[r0 - 8:r0 + er + 8, :]
        prev = pltpu.roll(win, 1, 0)[8:8 + er]
        nxt = pltpu.roll(win, er + 16 - 1, 0)[8:8 + er]
        y = prev * has_prev * cw[0:1] + win[8:8 + er] * cw[1:2] + nxt * has_next * cw[2:3] + cb
        gate, val = y[:, :bn], y[:, bn:]
        o_ref[c * er:(c + 1) * er, :] = (gate * _sigmoid(gate) * val).astype(o_ref.dtype)


def ffn_up(h, w_up, conv_w, conv_b, s, n_lat):
    r, d = h.shape
    f = w_up.shape[1] // 2
    bm = _pick(r, (1024, 512, 256, 128))
    bn = FFN_BN
    assert f % bn == 0
    halo_blocks = r // HALO
    per_tile = bm // HALO
    return pl.pallas_call(
        functools.partial(_ffn_up_kernel, bm=bm, s=s, n_lat=n_lat),
        grid=(r // bm, f // bn),
        in_specs=[
            pl.BlockSpec((bm, d), lambda i, j: (i, 0), pipeline_mode=pl.Buffered(1)),
            pl.BlockSpec((HALO, d), lambda i, j: (jnp.maximum(i * per_tile - 1, 0), 0)),
            pl.BlockSpec((HALO, d), lambda i, j: (jnp.minimum((i + 1) * per_tile, halo_blocks - 1), 0)),
            pl.BlockSpec((d, 2 * bn), lambda i, j: (0, j)),
            pl.BlockSpec((3, 2 * bn), lambda i, j: (0, j)),
            pl.BlockSpec((1, 2 * bn), lambda i, j: (0, j)),
        ],
        out_specs=pl.BlockSpec((bm, bn), lambda i, j: (i, j)),
        out_shape=jax.ShapeDtypeStruct((r, f), BF16),
        scratch_shapes=[pltpu.VMEM((bm + 2 * HALO, d), BF16), pltpu.VMEM((bm + 2 * HALO, 2 * bn), F32)],
        compiler_params=_params("parallel", "arbitrary"),
        name="ffn_up",
    )(h, h, h, w_up, conv_w, conv_b)


W_IN_SPLIT_ROWS = 128


def _w_in_split_kernel(wt_ref, *out_refs, starts):
    for o_ref, start in zip(out_refs, starts):
        width = o_ref.shape[1]
        o_ref[...] = wt_ref[start:start + width, :].T.astype(o_ref.dtype)


def _split_w_in(w_in, layer, d, heads):
    mix = heads * HEAD_DIM
    q_rank = 3 * d // 16
    kv_rank = d // 8
    b_used = q_rank + kv_rank + MLA_ROPE
    b_width = -(-(q_rank + kv_rank + LANE) // 512) * 512
    sizes = [3 * mix, b_used, 5 * mix, 3 * mix, N_BRANCH * d]
    starts = [sum(sizes[:i]) for i in range(len(sizes))]
    total = w_in.shape[2]
    assert sum(sizes) == total and starts[1] + b_width <= total
    widths = [sizes[0], b_width, sizes[2], sizes[3], sizes[4]]
    assert all(s % 16 == 0 for s in starts) and all(w % LANE == 0 for w in widths)
    tk = W_IN_SPLIT_ROWS
    outs = pl.pallas_call(
        functools.partial(_w_in_split_kernel, starts=tuple(starts)),
        grid=(d // tk,),
        in_specs=[pl.BlockSpec((None, total, tk), lambda i: (layer, 0, i))],
        out_specs=[pl.BlockSpec((tk, w), lambda i: (i, 0)) for w in widths],
        out_shape=[jax.ShapeDtypeStruct((d, w), BF16) for w in widths],
        compiler_params=_params("parallel"),
        name="w_in_split",
    )(jnp.swapaxes(w_in, 1, 2))
    return (*outs, q_rank, kv_rank)


def _mla_q_weight(w_qb, heads):
    rq = w_qb.shape[0]
    w = w_qb.reshape(rq, heads, MLA_NOPE + MLA_ROPE)
    nope = w[:, :, :MLA_NOPE].reshape(rq, heads * MLA_NOPE)
    ropep = jnp.pad(w[:, :, MLA_NOPE:], ((0, 0), (0, 0), (0, LANE - MLA_ROPE))).reshape(rq, heads * LANE)
    return jnp.concatenate([nope, ropep], axis=1).astype(BF16)


def kernel(x, c, ctx, c_ctx, ada_w, ada_b, norm_mix_pre, norm_mix_post, norm_ffn_pre, norm_ffn_post, w_in,
           diff_lambda, diff_subln, mla_q_norm, mla_w_qb, mla_kv_norm, mla_w_kvb, hgrn_lb_logits, hgrn_norm,
           na_rpb, w_branch, w_out, ffn_w_up, ffn_conv_w, ffn_conv_b, ffn_w_down):
    bsz, n_lat, d = x.shape
    n_ctx = ctx.shape[1]
    s = n_lat + n_ctx
    r = bsz * s
    depth = ada_w.shape[0]
    heads = d // (N_BRANCH * HEAD_DIM)
    mix = heads * HEAD_DIM
    assert n_lat % 256 == 0 and n_ctx % 256 == 0 and s % n_ctx == 0 and n_lat % GRID_W == 0

    tables = _rope_tables(n_lat, n_ctx)
    lb_sm = jax.nn.softmax(hgrn_lb_logits.astype(F32), axis=0)
    lower_bounds = jnp.cumsum(lb_sm, axis=0) - lb_sm[0]

    c_rows = jnp.zeros((8, d), F32).at[:bsz].set(c).at[bsz].set(c_ctx)
    xa = jnp.concatenate([x, ctx], axis=1)

    def mod_rows(mod, idx):
        part = mod[:, idx * d:(idx + 1) * d]
        both = jnp.stack([part[:bsz], jnp.broadcast_to(part[bsz], (bsz, d))], axis=1)
        return both.reshape(2 * bsz, 1, d)

    mods = []
    for l in range(depth):
        mod = ada_modulation(c_rows, ada_w, ada_b, l)
        mods.append([mod_rows(mod, i) for i in range(6)])

    w_branch_bf, w_out_bf, w_down_bf = (w.astype(BF16) for w in (w_branch, w_out, ffn_w_down))

    h = norm_modulate(xa, norm_mix_pre[0], mods[0][0], mods[0][1], n_lat)
    for l in range(depth):
        with_ctx = l < depth - 1
        sh1, sc1, g1, sh2, sc2, g2 = mods[l]
        h2d = h.reshape(r, d)
        rows = s if with_ctx else n_lat

        w_a, w_b, w_c, w_d, w_g, q_rank, kv_rank = _split_w_in(w_in, l, d, heads)
        pa = matmul(h2d, w_a, BF16, "in_proj_a")
        pb = matmul(h2d, w_b, BF16, "in_proj_b")
        pc = matmul(h2d, w_c, BF16, "in_proj_c")
        pd = matmul(h2d, w_d, BF16, "in_proj_d")
        if with_ctx:
            pg = matmul(h2d, w_g, BF16, "in_proj_gate")
        else:
            pg = matmul_leading_rows(h, w_g, rows, BF16, "in_proj_gate")

        lam_init = 0.8 - 0.6 * math.exp(-0.3 * l)
        extras_a = [diff_lambda[l].astype(F32), diff_subln[l].reshape(1, HEAD_DIM).astype(F32)]

        def call_a(q_range, k_range, tag):
            return _attention_call(
                functools.partial(_diff_attn_kernel, lam_init=lam_init),
                [(pa, 0)], [(pa, heads, 1, False)], (pa, 2 * heads, 1), extras_a,
                bsz=bsz, heads=heads, s=s, q_range=q_range, k_range=k_range, tq=min(1024, q_range[1]), lanes=256,
                name="diff_attn_" + tag, q_row_tables=tables, k_row_tables=tables)

        oa = _attend_lat_and_ctx(call_a, n_lat, n_ctx, with_ctx)

        qn, cn, kr = mla_prep(pb, mla_q_norm[l], mla_kv_norm[l], tables, s)
        qb = matmul(qn, _mla_q_weight(mla_w_qb[l], heads), BF16, "mla_q_up")
        kvb = matmul(cn, mla_w_kvb[l].astype(BF16), BF16, "mla_kv_up")

        def call_b(q_range, k_range, tag):
            return _attention_call(
                functools.partial(_sdpa_kernel, n_qk=2, scale=(MLA_NOPE + MLA_ROPE) ** -0.5, rope_q=1),
                [(qb, 0), (qb, heads)], [(kvb, 0, 2, False), (kr, 0, 0, True)], (kvb, 1, 2), [],
                bsz=bsz, heads=heads, s=s, q_range=q_range, k_range=k_range, tq=min(2048, q_range[1]), lanes=256,
                name="mla_attn_" + tag, q_row_tables=tables)

        ob = _attend_lat_and_ctx(call_b, n_lat, n_ctx, with_ctx)

        o_fwd = hgrn_scan(pc, lower_bounds[l, 0], bsz, heads, n_lat, n_ctx, reverse=False)
        o_bwd = hgrn_scan(pc, lower_bounds[l, 1], bsz, heads, n_lat, n_ctx, reverse=True)
        oc = hgrn_readout(o_fwd, o_bwd, pc, hgrn_norm[l], bsz, s, rows)

        od = neighbourhood_attention(pd, _na_bias_table(na_rpb[l], n_lat // GRID_W), bsz, heads, n_lat, n_ctx)
        if with_ctx:
            od_ctx = _attention_call(
                functools.partial(_sdpa_kernel, n_qk=1, scale=HEAD_DIM ** -0.5, rope_q=None),
                [(pd, 0)], [(pd, heads, 1, False)], (pd, 2 * heads, 1), [],
                bsz=bsz, heads=heads, s=s, q_range=(n_lat, n_ctx), k_range=(n_lat, n_ctx), tq=256, lanes=256,
                name="na_ctx_attn").reshape(bsz, n_ctx, mix)
            od = jnp.concatenate([od.reshape(bsz, n_lat, mix), od_ctx], axis=1).reshape(r, mix)

        merged = merge_branches((oa, ob, oc, od), pg, w_branch_bf, l)
        mix_out = matmul(merged, w_out_bf, BF16, "out_proj", layer=l).reshape(bsz, rows, d)
        xa, h2 = residual_norm_modulate(xa, mix_out, g1, norm_mix_post[l], norm_ffn_pre[l], sh2, sc2, n_lat)

        gv = ffn_up(h2.reshape(bsz * rows, d), ffn_up_weight(ffn_w_up, l, FFN_BN),
                    _ffn_tile_pairs(ffn_conv_w[l], FFN_BN), _ffn_tile_pairs(ffn_conv_b[l][None], FFN_BN),
                    rows, n_lat)
        ffn = matmul(gv, w_down_bf, BF16, "ffn_down", layer=l).reshape(bsz, rows, d)
        if with_ctx:
            xa, h = residual_norm_modulate(xa, ffn, g2, norm_ffn_post[l], norm_mix_pre[l + 1],
                                           mods[l + 1][0], mods[l + 1][1], n_lat)
        else:
            xa = residual(xa, ffn, g2, norm_ffn_post[l], n_lat)
    return xa
```

```python
import functools
import math

import numpy as np
import jax
import jax.numpy as jnp
from jax import lax
from jax.experimental import pallas as pl
from jax.experimental.pallas import tpu as pltpu

F32 = jnp.float32
BF16 = jnp.bfloat16

HEAD_DIM = 128
QK_HALF = 64
MLA_NOPE = 128
MLA_ROPE = 64
MLA_V = 128
ROPE_DIM = 64
ROPE_THETA = 10000.0
GRID_W = 64
NA_ROWS = 8
NA_COLS = 16
CHUNK = 128
NORM_EPS = 1e-6
NEG_INF = -1e30
N_BRANCH = 4

LANE = 128
VMEM_LIMIT_BYTES = 56 * 1024 * 1024

NT_DIMS = (((1,), (1,)), ((), ()))
TN_DIMS = (((0,), (0,)), ((), ()))


def _params(*sem):
    return pltpu.CompilerParams(dimension_semantics=sem, vmem_limit_bytes=VMEM_LIMIT_BYTES)


def _sigmoid(x):
    return 0.5 * jnp.tanh(0.5 * x) + 0.5


def _pick(total, candidates):
    for c in candidates:
        if total % c == 0:
            return c
    raise ValueError(f"no tile for {total} in {candidates}")


def _ada_kernel(c_ref, w_ref, b_ref, o_ref):
    c = c_ref[...]
    a = (c * _sigmoid(c)).astype(BF16)
    o_ref[...] = jnp.dot(a, w_ref[...].astype(BF16), preferred_element_type=F32) + b_ref[...]


def ada_modulation(c_rows, ada_w, ada_b, layer):
    rows, d = c_rows.shape
    width = ada_w.shape[2]
    tn = _pick(width, (512, 256, 128))
    b3 = ada_b.reshape(ada_b.shape[0], 1, width)
    return pl.pallas_call(
        _ada_kernel,
        grid=(width // tn,),
        in_specs=[
            pl.BlockSpec((rows, d), lambda j: (0, 0)),
            pl.BlockSpec((None, d, tn), lambda j: (layer, 0, j)),
            pl.BlockSpec((None, 1, tn), lambda j: (layer, 0, j)),
        ],
        out_specs=pl.BlockSpec((rows, tn), lambda j: (0, j)),
        out_shape=jax.ShapeDtypeStruct((rows, width), F32),
        compiler_params=_params("parallel"),
        name="ada_modulation",
    )(c_rows, ada_w, b3)


def _rms(x, gain):
    return x * lax.rsqrt(jnp.mean(x * x, axis=-1, keepdims=True) + NORM_EPS) * gain


def _norm_mod_kernel(x_ref, g_ref, sh_ref, sc_ref, h_ref):
    h = _rms(x_ref[0], g_ref[...])
    h_ref[0] = (h * (1.0 + sc_ref[0]) + sh_ref[0]).astype(h_ref.dtype)


def _resid_kernel(x_ref, y_ref, gate_ref, gpost_ref, xo_ref):
    xo_ref[0] = x_ref[0] + gate_ref[0] * _rms(y_ref[0].astype(F32), gpost_ref[...])


def _resid_norm_mod_kernel(x_ref, y_ref, gate_ref, gpost_ref, gpre_ref, sh_ref, sc_ref, xo_ref, h_ref):
    xn = x_ref[0] + gate_ref[0] * _rms(y_ref[0].astype(F32), gpost_ref[...])
    xo_ref[0] = xn
    h = _rms(xn, gpre_ref[...])
    h_ref[0] = (h * (1.0 + sc_ref[0]) + sh_ref[0]).astype(h_ref.dtype)


def _row_specs(dims, tr, n_lat_tiles):
    s, d = dims
    xspec = pl.BlockSpec((1, tr, d), lambda b, t: (b, t, 0))
    gspec = pl.BlockSpec((1, d), lambda b, t: (0, 0))
    mspec = pl.BlockSpec((1, 1, d), lambda b, t: (2 * b + t // n_lat_tiles, 0, 0))
    return xspec, gspec, mspec


def norm_modulate(x, gain, shift, scale, n_lat, tr=256):
    bsz, s, d = x.shape
    xspec, gspec, mspec = _row_specs((s, d), tr, n_lat // tr)
    return pl.pallas_call(
        _norm_mod_kernel,
        grid=(bsz, s // tr),
        in_specs=[xspec, gspec, mspec, mspec],
        out_specs=xspec,
        out_shape=jax.ShapeDtypeStruct(x.shape, BF16),
        compiler_params=_params("parallel", "parallel"),
        name="norm_modulate",
    )(x, gain.reshape(1, d), shift, scale)


def residual(x, y, gate, gain_post, n_lat, tr=256):
    bsz, s, d = x.shape
    rows = y.shape[1]
    xspec, gspec, mspec = _row_specs((s, d), tr, n_lat // tr)
    return pl.pallas_call(
        _resid_kernel,
        grid=(bsz, rows // tr),
        in_specs=[xspec, xspec, mspec, gspec],
        out_specs=xspec,
        out_shape=jax.ShapeDtypeStruct(y.shape, F32),
        compiler_params=_params("parallel", "parallel"),
        name="residual",
    )(x, y, gate, gain_post.reshape(1, d))


def residual_norm_modulate(x, y, gate, gain_post, gain_pre, shift, scale, n_lat, tr=256):
    bsz, s, d = x.shape
    rows = y.shape[1]
    xspec, gspec, mspec = _row_specs((s, d), tr, n_lat // tr)
    return pl.pallas_call(
        _resid_norm_mod_kernel,
        grid=(bsz, rows // tr),
        in_specs=[xspec, xspec, mspec, gspec, gspec, mspec, mspec],
        out_specs=[xspec, xspec],
        out_shape=[jax.ShapeDtypeStruct(y.shape, F32), jax.ShapeDtypeStruct(y.shape, BF16)],
        compiler_params=_params("parallel", "parallel"),
        name="residual_norm_modulate",
    )(x, y, gate, gain_post.reshape(1, d), gain_pre.reshape(1, d), shift, scale)


def _mm_kernel(a_ref, b_ref, o_ref):
    o_ref[...] = jnp.dot(a_ref[...], b_ref[...], preferred_element_type=F32).astype(o_ref.dtype)


def _mm_acc_kernel(a_ref, b_ref, o_ref, acc_ref):
    k = pl.program_id(2)

    @pl.when(k == 0)
    def _():
        acc_ref[...] = jnp.zeros_like(acc_ref)

    acc_ref[...] += jnp.dot(a_ref[...], b_ref[...], preferred_element_type=F32)

    @pl.when(k == pl.num_programs(2) - 1)
    def _():
        o_ref[...] = acc_ref[...].astype(o_ref.dtype)


MAX_SINGLE_K = 4096
MATMUL_VMEM_BUDGET = VMEM_LIMIT_BYTES * 9 // 10


def matmul_leading_rows(a, b, rows, out_dtype, name):
    bsz, _, k = a.shape
    n = b.shape[1]
    assert k <= MAX_SINGLE_K
    bm = _pick(rows, (1024, 512, 256, 128))
    bn = _pick(n, (1024, 768, 640, 512, 384, 256, 128))
    tiles = rows // bm
    return pl.pallas_call(
        _mm_kernel,
        grid=(bsz, tiles, n // bn),
        in_specs=[pl.BlockSpec((None, bm, k), lambda s, i, j: (s, i, 0)),
                  pl.BlockSpec((k, bn), lambda s, i, j: (0, j))],
        out_specs=pl.BlockSpec((bm, bn), lambda s, i, j: (s * tiles + i, j)),
        out_shape=jax.ShapeDtypeStruct((bsz * rows, n), out_dtype),
        compiler_params=_params("parallel", "parallel", "parallel"),
        name=name,
    )(a, b)


def matmul(a, b, out_dtype, name, layer=None):
    m, k = a.shape
    n = b.shape[-1]
    if layer is None:
        def bspec(rows, cols, index):
            return pl.BlockSpec((rows, cols), index)
    else:
        def bspec(rows, cols, index):
            return pl.BlockSpec((None, rows, cols), lambda *g: (layer, *index(*g)))
    out_bytes = jnp.dtype(out_dtype).itemsize
    full_k = None
    for cap in (1024, 512):
        bm = _pick(m, tuple(c for c in (1024, 512, 256, 128) if c <= cap))
        bn = _pick(n, tuple(c for c in (1024, 768, 640, 512, 384, 256, 128) if c <= cap))
        need = 2 * 2 * (bm * k + k * bn) + 2 * bm * bn * out_bytes + 4 * bm * bn
        if need <= MATMUL_VMEM_BUDGET:
            full_k = (bm, bn)
            break
    if full_k is not None:
        bm, bn = full_k
        return pl.pallas_call(
            _mm_kernel,
            grid=(m // bm, n // bn),
            in_specs=[pl.BlockSpec((bm, k), lambda i, j: (i, 0)),
                      bspec(k, bn, lambda i, j: (0, j))],
            out_specs=pl.BlockSpec((bm, bn), lambda i, j: (i, j)),
            out_shape=jax.ShapeDtypeStruct((m, n), out_dtype),
            compiler_params=_params("parallel", "parallel"),
            name=name,
        )(a, b)
    bm = _pick(m, (1024, 512, 256, 128))
    bn = _pick(n, (1024, 768, 640, 512, 384, 256, 128))
    nk = -(-k // MAX_SINGLE_K)
    while k % nk or (k // nk) % LANE:
        nk += 1
    bk = k // nk
    return pl.pallas_call(
        _mm_acc_kernel,
        grid=(m // bm, n // bn, nk),
        in_specs=[pl.BlockSpec((bm, bk), lambda i, j, kk: (i, kk)),
                  bspec(bk, bn, lambda i, j, kk: (kk, j))],
        out_specs=pl.BlockSpec((bm, bn), lambda i, j, kk: (i, j)),
        out_shape=jax.ShapeDtypeStruct((m, n), out_dtype),
        scratch_shapes=[pltpu.VMEM((bm, bn), F32)],
        compiler_params=_params("parallel", "parallel", "arbitrary"),
        name=name,
    )(a, b)


def _rope_tables(n_lat, n_ctx):
    t = jnp.arange(n_lat, dtype=jnp.int32)
    row = (t // GRID_W).astype(F32)
    col = (t % GRID_W).astype(F32)
    quarter = ROPE_DIM // 4
    inv = ROPE_THETA ** (-jnp.arange(quarter, dtype=F32) / quarter)
    ar = row[:, None] * inv
    ac = col[:, None] * inv
    ang = jnp.concatenate([ar, ar, ac, ac], axis=-1)
    cos = jnp.concatenate([jnp.cos(ang), jnp.ones((n_ctx, ROPE_DIM), F32)], axis=0)
    sin = jnp.concatenate([jnp.sin(ang), jnp.zeros((n_ctx, ROPE_DIM), F32)], axis=0)
    first = (jnp.arange(ROPE_DIM) // quarter) % 2 == 0
    sin_next = jnp.where(first, -sin, 0.0)
    sin_prev = jnp.where(first, 0.0, sin)
    reps = LANE // ROPE_DIM
    return tuple(jnp.tile(a, (1, reps)) for a in (cos, sin_next, sin_prev))


def _rope_apply(x, cos, sin_next, sin_prev):
    quarter = ROPE_DIM // 4
    return (x * cos + pltpu.roll(x, LANE - quarter, 1) * sin_next
            + pltpu.roll(x, quarter, 1) * sin_prev)


LOG2E = math.log2(math.e)
KEY_CHUNK = 256


def _attend_tiles(k_ref, q_tiles, vt_ref, s_ref, e_ref):
    m_keys, n = s_ref.shape[1:]
    kc = min(KEY_CHUNK, m_keys)
    chunks = [slice(c * kc, (c + 1) * kc) for c in range(m_keys // kc)]

    def scores(t):
        s_ref[t % 2] = lax.dot_general(k_ref[...], q_tiles[t], NT_DIMS, preferred_element_type=F32)

    def column_max(t):
        m8 = None
        for sl in chunks:
            part = jnp.max(s_ref[t % 2, sl, :].reshape(kc // 8, 8, n), axis=0)
            m8 = part if m8 is None else jnp.maximum(m8, part)
        return jnp.max(m8, axis=0, keepdims=True)

    scores(0)
    m = column_max(0)
    outs = []
    for t in range(len(q_tiles)):
        if t + 1 < len(q_tiles):
            scores(t + 1)
        l8 = jnp.zeros((8, n), F32)
        for sl in chunks:
            e = jnp.exp2(s_ref[t % 2, sl, :] - m)
            l8 = l8 + jnp.sum(e.reshape(kc // 8, 8, n), axis=0)
            e_ref[t % 2, sl, :] = e.astype(e_ref.dtype)
        o_t = jnp.dot(vt_ref[...], e_ref[t % 2], preferred_element_type=F32)
        outs.append((o_t, jnp.sum(l8, axis=0, keepdims=True)))
        if t + 1 < len(q_tiles):
            m = column_max(t + 1)
    return outs


def _load_v_transposed(v_ref, vt_ref):
    @pl.when(pl.program_id(2) == 0)
    def _():
        vt_ref[...] = v_ref[...].astype(F32).T.astype(vt_ref.dtype)


def _diff_attn_kernel(q_ref, k_ref, v_ref, lam_ref, g_ref, qcos_ref, qsn_ref, qsp_ref, kcos_ref, ksn_ref, ksp_ref,
                      o_ref, vt_ref, s_ref, e_ref, kr_ref, *, lam_init):
    _load_v_transposed(v_ref, vt_ref)

    @pl.when(pl.program_id(2) == 0)
    def _():
        kr_ref[...] = _rope_apply(k_ref[...].astype(F32), kcos_ref[...], ksn_ref[...], ksp_ref[...]
                                  ).astype(kr_ref.dtype)

    tq = s_ref.shape[2] // 2
    q = _rope_apply(q_ref[...].astype(F32), qcos_ref[...], qsn_ref[...], qsp_ref[...])
    q = (q * (QK_HALF ** -0.5 * LOG2E)).astype(kr_ref.dtype)
    lane = lax.broadcasted_iota(jnp.int32, q.shape, 1)
    zero = jnp.zeros_like(q)
    q0 = jnp.where(lane < QK_HALF, q, zero)
    q1 = jnp.where(lane >= QK_HALF, q, zero)
    tiles = [slice(t * tq, (t + 1) * tq) for t in range(q.shape[0] // tq)]
    outs = _attend_tiles(kr_ref, [jnp.concatenate([q0[sl], q1[sl]], axis=0) for sl in tiles], vt_ref, s_ref, e_ref)
    lp = lam_ref[...]
    lam = (jnp.exp(jnp.sum(lp[0:1] * lp[1:2], axis=-1, keepdims=True))
           - jnp.exp(jnp.sum(lp[2:3] * lp[3:4], axis=-1, keepdims=True)) + lam_init)
    gain = g_ref[...] * (1.0 - lam_init)
    for sl, (o_t, l) in zip(tiles, outs):
        inv = 1.0 / l
        c_t = o_t[:, :tq] * inv[:, :tq] - o_t[:, tq:] * (lam * inv[:, tq:])
        c_t = c_t * lax.rsqrt(jnp.mean(c_t * c_t, axis=0, keepdims=True) + NORM_EPS)
        o_ref[sl, :] = (c_t.T * gain).astype(o_ref.dtype)


def _sdpa_kernel(*refs, n_qk, scale, rope_q):
    q_refs, k_refs = refs[:n_qk], refs[n_qk:2 * n_qk]
    n_tab = 0 if rope_q is None else 3
    v_ref = refs[2 * n_qk]
    tables = refs[2 * n_qk + 1:2 * n_qk + 1 + n_tab]
    o_ref, vt_ref, s_ref, e_ref = refs[2 * n_qk + 1 + n_tab:2 * n_qk + 5 + n_tab]
    _load_v_transposed(v_ref, vt_ref)
    if n_qk == 1:
        kcat_ref = k_refs[0]
    else:
        kcat_ref = refs[2 * n_qk + 5 + n_tab]

        @pl.when(pl.program_id(2) == 0)
        def _():
            for i, k_ref in enumerate(k_refs):
                kcat_ref[:, i * HEAD_DIM:(i + 1) * HEAD_DIM] = k_ref[...]

    tq = s_ref.shape[2]
    parts = [q_ref[...].astype(F32) for q_ref in q_refs]
    if rope_q is not None:
        parts[rope_q] = _rope_apply(parts[rope_q], *[t[...] for t in tables])
    q = (jnp.concatenate(parts, axis=1) * (scale * LOG2E)).astype(kcat_ref.dtype)
    tiles = [slice(t * tq, (t + 1) * tq) for t in range(q.shape[0] // tq)]
    outs = _attend_tiles(kcat_ref, [q[sl] for sl in tiles], vt_ref, s_ref, e_ref)
    for sl, (o_t, l) in zip(tiles, outs):
        o_ref[sl, :] = (o_t * (1.0 / l)).T.astype(o_ref.dtype)


def _attention_call(kernel, q_list, k_list, v, extras, *, bsz, heads, s, q_range, k_range, tq, lanes, name,
                    q_row_tables=(), k_row_tables=()):
    q_start, q_len = q_range
    k_start, k_len = k_range
    nq = q_len // tq
    q_off = q_start // tq
    k_off = k_start // k_len
    assert q_start % tq == 0 and q_len % tq == 0 and k_start % k_len == 0

    def qspec(col0):
        return pl.BlockSpec((None, tq, HEAD_DIM), lambda b, h, i: (b, q_off + i, col0 + h))

    def kspec(col0, stride, shared=False):
        if shared:
            return pl.BlockSpec((None, k_len, HEAD_DIM), lambda b, h, i: (b, k_off, col0))
        return pl.BlockSpec((None, k_len, HEAD_DIM), lambda b, h, i: (b, k_off, col0 + stride * h))

    def per_sample(arr):
        return arr.reshape(bsz, s, arr.shape[-1])

    in_specs, args = [], []
    for arr, col0 in q_list:
        in_specs.append(qspec(col0)); args.append(per_sample(arr))
    for arr, col0, stride, shared in k_list:
        in_specs.append(kspec(col0, stride, shared)); args.append(per_sample(arr))
    arr, col0, stride = v
    in_specs.append(kspec(col0, stride)); args.append(per_sample(arr))
    for e in extras:
        in_specs.append(pl.BlockSpec(e.shape, lambda b, h, i: (0, 0))); args.append(e)
    for t in q_row_tables:
        in_specs.append(pl.BlockSpec((tq, LANE), lambda b, h, i: (q_off + i, 0))); args.append(t)
    for t in k_row_tables:
        in_specs.append(pl.BlockSpec((k_len, LANE), lambda b, h, i: (k_off, 0))); args.append(t)
    return pl.pallas_call(
        kernel,
        grid=(bsz, heads, nq),
        in_specs=in_specs,
        out_specs=pl.BlockSpec((tq, HEAD_DIM), lambda b, h, i: (b * nq + i, h)),
        out_shape=jax.ShapeDtypeStruct((bsz * q_len, heads * HEAD_DIM), BF16),
        scratch_shapes=[pltpu.VMEM((HEAD_DIM, k_len), BF16), pltpu.VMEM((2, k_len, lanes), F32),
                        pltpu.VMEM((2, k_len, lanes), BF16)]
        + ([pltpu.VMEM((k_len, len(k_list) * HEAD_DIM), BF16)] if len(k_list) > 1 else [])
        + ([pltpu.VMEM((k_len, HEAD_DIM), BF16)] if k_row_tables else []),
        compiler_params=_params("parallel", "parallel", "arbitrary"),
        name=name,
    )(*args)


def _attend_lat_and_ctx(call, n_lat, n_ctx, with_ctx):
    s = n_lat + n_ctx
    o_lat = call((0, n_lat), (0, s), "lat")
    if not with_ctx:
        return o_lat
    bsz = o_lat.shape[0] // n_lat
    o_ctx = call((n_lat, n_ctx), (n_lat, n_ctx), "ctx").reshape(bsz, n_ctx, -1)
    return jnp.concatenate([o_lat.reshape(bsz, n_lat, -1), o_ctx], axis=1).reshape(bsz * s, -1)


NA_STEP_ROWS = 8
NA_WIN_ROWS = 16


def _na_window_start(step, grid_rows):
    return jnp.clip(step * NA_STEP_ROWS - NA_ROWS // 2, 0, grid_rows - NA_WIN_ROWS)


def _na_bias_table(rpb, grid_rows):
    steps = grid_rows // NA_STEP_ROWS
    heads = rpb.shape[0]
    n_dr = 2 * NA_ROWS - 1
    col = jnp.arange(GRID_W, dtype=jnp.int32)
    cs = jnp.clip(col - NA_COLS // 2, 0, GRID_W - NA_COLS)
    colmask = (col[None, :] >= cs[:, None]) & (col[None, :] < cs[:, None] + NA_COLS)
    dc = jnp.clip(col[None, :] - col[:, None] + NA_COLS - 1, 0, 2 * NA_COLS - 2)
    pat = jnp.where(colmask[None, None], rpb[:, :, dc].astype(F32) * LOG2E, NEG_INF)
    blank = jnp.full((heads, 1, GRID_W, GRID_W), NEG_INF, F32)
    pairs = jnp.concatenate([jnp.concatenate([blank, pat], axis=1), jnp.concatenate([pat, blank], axis=1)], axis=-1)

    kind_steps = [0, min(1, steps - 1), steps - 1]
    plan = []
    for st in kind_steps:
        ws = min(max(st * NA_STEP_ROWS - NA_ROWS // 2, 0), grid_rows - NA_WIN_ROWS)
        rows = []
        for i in range(NA_STEP_ROWS):
            r = st * NA_STEP_ROWS + i
            rs = min(max(r - NA_ROWS // 2, 0), grid_rows - NA_ROWS)
            rows.append([(ws + jw - r + NA_ROWS - 1) if rs <= ws + jw < rs + NA_ROWS else None
                         for jw in range(NA_WIN_ROWS)])
        plan.append(rows)

    def expand_kernel(p_ref, o_ref):
        lane = lax.broadcasted_iota(jnp.int32, (GRID_W, 2 * GRID_W), 1)
        for kind, rows in enumerate(plan):
            @pl.when(pl.program_id(1) == kind)
            def _(rows=rows):
                for i, offs in enumerate(rows):
                    for jp in range(NA_WIN_ROWS // 2):
                        left, right = offs[2 * jp], offs[2 * jp + 1]
                        if left is None and right is None:
                            blk = jnp.full((GRID_W, 2 * GRID_W), NEG_INF, F32)
                        else:
                            blk = p_ref[right if right is not None else left + 1]
                            if left is None:
                                blk = jnp.where(lane >= GRID_W, blk, NEG_INF)
                            if right is None:
                                blk = jnp.where(lane < GRID_W, blk, NEG_INF)
                        o_ref[i * GRID_W:(i + 1) * GRID_W, 2 * jp * GRID_W:2 * (jp + 1) * GRID_W] = blk

    tq, win = NA_STEP_ROWS * GRID_W, NA_WIN_ROWS * GRID_W
    return pl.pallas_call(
        expand_kernel,
        grid=(heads, 3),
        in_specs=[pl.BlockSpec((None, n_dr + 1, GRID_W, 2 * GRID_W), lambda h, k: (h, 0, 0, 0))],
        out_specs=pl.BlockSpec((None, None, tq, win), lambda h, k: (h, k, 0, 0)),
        out_shape=jax.ShapeDtypeStruct((heads, 3, tq, win), F32),
        compiler_params=_params("parallel", "arbitrary"),
        name="na_bias_table",
    )(pairs)


def _na_kernel(q_ref, k_ref, v_ref, bias_ref, o_ref, *, n_lat, grid_rows):
    start = pl.multiple_of(_na_window_start(pl.program_id(2), grid_rows) * GRID_W, GRID_W)
    win = NA_WIN_ROWS * GRID_W
    q = (q_ref[...].astype(F32) * (HEAD_DIM ** -0.5 * LOG2E)).astype(k_ref.dtype)
    sl = lax.dot_general(q, k_ref[pl.ds(start, win), :], NT_DIMS, preferred_element_type=F32) + bias_ref[...]
    sc = lax.dot_general(q, k_ref[n_lat:, :], NT_DIMS, preferred_element_type=F32)
    m = jnp.maximum(jnp.max(sl, axis=-1, keepdims=True), jnp.max(sc, axis=-1, keepdims=True))
    el = jnp.exp2(sl - m)
    ec = jnp.exp2(sc - m)
    inv = 1.0 / (jnp.sum(el, axis=-1, keepdims=True) + jnp.sum(ec, axis=-1, keepdims=True))
    o = (jnp.dot(el.astype(v_ref.dtype), v_ref[pl.ds(start, win), :], preferred_element_type=F32)
         + jnp.dot(ec.astype(v_ref.dtype), v_ref[n_lat:, :], preferred_element_type=F32))
    o_ref[...] = (o * inv).astype(o_ref.dtype)


def neighbourhood_attention(pd, bias, bsz, heads, n_lat, n_ctx):
    s = n_lat + n_ctx
    grid_rows = n_lat // GRID_W
    tq = NA_STEP_ROWS * GRID_W
    win = NA_WIN_ROWS * GRID_W
    steps = grid_rows // NA_STEP_ROWS
    assert grid_rows >= NA_WIN_ROWS and grid_rows % NA_STEP_ROWS == 0 and steps >= 2
    assert NA_WIN_ROWS >= NA_STEP_ROWS + NA_ROWS - 1

    def kind(g):
        return jnp.where(g == 0, 0, jnp.where(g == steps - 1, 2, 1))

    pd3 = pd.reshape(bsz, s, pd.shape[-1])
    return pl.pallas_call(
        functools.partial(_na_kernel, n_lat=n_lat, grid_rows=grid_rows),
        grid=(bsz, heads, steps),
        in_specs=[
            pl.BlockSpec((None, tq, HEAD_DIM), lambda b, h, g: (b, g, h)),
            pl.BlockSpec((None, s, HEAD_DIM), lambda b, h, g: (b, 0, heads + h)),
            pl.BlockSpec((None, s, HEAD_DIM), lambda b, h, g: (b, 0, 2 * heads + h)),
            pl.BlockSpec((None, None, tq, win), lambda b, h, g: (h, kind(g), 0, 0)),
        ],
        out_specs=pl.BlockSpec((tq, HEAD_DIM), lambda b, h, g: (b * steps + g, h)),
        out_shape=jax.ShapeDtypeStruct((bsz * n_lat, heads * HEAD_DIM), BF16),
        compiler_params=_params("parallel", "parallel", "arbitrary"),
        name="neighbourhood_attention",
    )(pd3, pd3, pd3, bias)


def _hgrn_constants(reverse):
    c = CHUNK
    t = np.arange(c)
    tri = (t[None, :] >= t[:, None]) if reverse else (t[None, :] <= t[:, None])
    mats = [tri.astype(np.float32)]
    masks = [np.eye(c, dtype=np.float32)]
    m = c // 2
    while m >= 1:
        node = (t // (2 * m)) * (2 * m)
        ref = node + (m if reverse else m - 1)
        if m < HGRN_BROADCAST_MIN:
            mats.append(tri[ref].astype(np.float32))
        upper = (t % (2 * m)) >= m
        same = node[:, None] == node[None, :]
        pair = same & (upper[:, None] & ~upper[None, :])
        masks.append((pair.T if reverse else pair).astype(np.float32))
        m //= 2
    return np.concatenate(mats, axis=0), np.stack(masks)


HGRN_BROADCAST_MIN = 8


def _hgrn_reference_rows(cum, cs, lev, reverse):
    c = CHUNK
    m = c >> lev
    if m < HGRN_BROADCAST_MIN:
        k = lev - int(math.log2(c // HGRN_BROADCAST_MIN))
        return cs[k * c:(k + 1) * c]
    rows = []
    for node in range(0, c, 2 * m):
        ref = node + (m if reverse else m - 1)
        rows.append(jnp.broadcast_to(cum[ref:ref + 1, :], (2 * m, cum.shape[1])))
    return rows[0] if len(rows) == 1 else jnp.concatenate(rows, axis=0)


def _hgrn_kernel(cq_ref, ci_ref, z_ref, lb_ref, ms_ref, mk_ref, o_ref, st_ref, *, heads, reverse):
    @pl.when(pl.program_id(1) == 0)
    def _():
        st_ref[...] = jnp.zeros_like(st_ref)

    c = CHUNK
    n_levels = mk_ref.shape[0]
    ms = ms_ref[...].astype(BF16)
    scale = HEAD_DIM ** -0.5
    tot_row = 0 if reverse else c - 1
    for h in range(heads):
        sl = slice(h * HEAD_DIM, (h + 1) * HEAD_DIM)
        lb = lb_ref[:, sl]
        f = lb + (1.0 - lb) * _sigmoid(z_ref[:, sl].astype(F32))
        kk = 1.0 - f
        g = jnp.log(f)
        cq = cq_ref[:, sl].astype(F32)
        qq = cq * _sigmoid(cq) * scale
        v = ci_ref[:, sl]
        g_hi = g.astype(BF16)
        r1 = g - g_hi.astype(F32)
        g_mid = r1.astype(BF16)
        g_lo = (r1 - g_mid.astype(F32)).astype(BF16)
        cs3 = jnp.dot(ms, jnp.concatenate([g_hi, g_mid, g_lo], axis=1), preferred_element_type=F32)
        cs = cs3[:, :HEAD_DIM] + cs3[:, HEAD_DIM:2 * HEAD_DIM] + cs3[:, 2 * HEAD_DIM:]
        cum = cs[:c]
        tot = cs[tot_row:tot_row + 1]
        st = st_ref[h]
        o = lax.dot_general((qq * jnp.exp(cum)).astype(BF16), st.astype(BF16), NT_DIMS,
                            preferred_element_type=F32)
        a = mk_ref[0] * lax.dot_general(qq.astype(BF16), kk.astype(BF16), NT_DIMS,
                                        preferred_element_type=F32)
        for lev in range(1, n_levels):
            e = jnp.exp(-jnp.abs(cum - _hgrn_reference_rows(cum, cs, lev, reverse)))
            a = a + mk_ref[lev] * lax.dot_general((qq * e).astype(BF16), (kk * e).astype(BF16), NT_DIMS,
                                                  preferred_element_type=F32)
        o = o + jnp.dot(a.astype(BF16), v, preferred_element_type=F32)
        o_ref[:, sl] = o
        ke = (kk * jnp.exp(tot - cum)).astype(BF16)
        st_ref[h] = st * jnp.exp(tot) + lax.dot_general(v, ke, TN_DIMS, preferred_element_type=F32)


def hgrn_scan(pc, lb, bsz, heads, n_lat, n_ctx, reverse):
    s = n_lat + n_ctx
    chunks = s // CHUNK
    lat_chunks = n_lat // CHUNK
    width = heads * HEAD_DIM
    ms, mk = _hgrn_constants(reverse)

    def chunk_of(step):
        return (chunks - 1 - step) if reverse else lax.rem(step + lat_chunks, chunks)

    def spec(col):
        return pl.BlockSpec((CHUNK, width), lambda b, t: (b * chunks + chunk_of(t), col))

    return pl.pallas_call(
        functools.partial(_hgrn_kernel, heads=heads, reverse=reverse),
        grid=(bsz, chunks),
        in_specs=[spec(0), spec(1), spec(3 if reverse else 2),
                  pl.BlockSpec((1, width), lambda b, t: (0, 0)),
                  pl.BlockSpec(ms.shape, lambda b, t: (0, 0)),
                  pl.BlockSpec(mk.shape, lambda b, t: (0, 0, 0))],
        out_specs=spec(0),
        out_shape=jax.ShapeDtypeStruct((bsz * s, width), F32),
        scratch_shapes=[pltpu.VMEM((heads, HEAD_DIM, HEAD_DIM), F32)],
        compiler_params=_params("parallel", "arbitrary"),
        name="hgrn_scan_bwd" if reverse else "hgrn_scan_fwd",
    )(pc, pc, pc, lb.reshape(1, width), jnp.asarray(ms), jnp.asarray(mk))


def _hgrn_readout_kernel(of_ref, ob_ref, gate_ref, g_ref, o_ref):
    gain = g_ref[...]
    for h in range(of_ref.shape[1] // HEAD_DIM):
        sl = slice(h * HEAD_DIM, (h + 1) * HEAD_DIM)
        gate = gate_ref[:, sl].astype(F32)
        o = _rms(of_ref[:, sl] + ob_ref[:, sl], gain)
        o_ref[:, sl] = (o * (gate * _sigmoid(gate))).astype(o_ref.dtype)


def hgrn_readout(o_fwd, o_bwd, pc, gain, bsz, s, rows, tr=256):
    width = o_fwd.shape[1]
    in_tiles, out_tiles = s // tr, rows // tr
    ispec = pl.BlockSpec((tr, width), lambda b, i: (b * in_tiles + i, 0))
    return pl.pallas_call(
        _hgrn_readout_kernel,
        grid=(bsz, out_tiles),
        in_specs=[ispec, ispec, pl.BlockSpec((tr, width), lambda b, i: (b * in_tiles + i, 4)),
                  pl.BlockSpec((1, HEAD_DIM), lambda b, i: (0, 0))],
        out_specs=pl.BlockSpec((tr, width), lambda b, i: (b * out_tiles + i, 0)),
        out_shape=jax.ShapeDtypeStruct((bsz * rows, width), BF16),
        compiler_params=_params("parallel", "parallel"),
        name="hgrn_readout",
    )(o_fwd, o_bwd, pc, gain.reshape(1, HEAD_DIM))


def _mla_prep_kernel(p_ref, qg_ref, kg_ref, cos_ref, sn_ref, sp_ref, qn_ref, cn_ref, kr_ref, *, q_rank, kv_rank):
    qn_ref[...] = _rms(p_ref[:, :q_rank].astype(F32), qg_ref[...]).astype(qn_ref.dtype)
    cn_ref[...] = _rms(p_ref[:, q_rank:q_rank + kv_rank].astype(F32), kg_ref[...]).astype(cn_ref.dtype)
    kr = p_ref[:, q_rank + kv_rank:q_rank + kv_rank + LANE].astype(F32)
    kr_ref[...] = _rope_apply(kr, cos_ref[...], sn_ref[...], sp_ref[...]).astype(kr_ref.dtype)


def mla_prep(pb, q_gain, kv_gain, tables, s, tr=256):
    r, width = pb.shape
    q_rank, kv_rank = q_gain.shape[0], kv_gain.shape[0]
    assert width >= q_rank + kv_rank + LANE and q_rank % LANE == 0 and kv_rank % LANE == 0
    tiles_per_seq = s // tr
    tspec = pl.BlockSpec((tr, LANE), lambda i: (i % tiles_per_seq, 0))

    def ospec(w):
        return pl.BlockSpec((tr, w), lambda i: (i, 0))

    return pl.pallas_call(
        functools.partial(_mla_prep_kernel, q_rank=q_rank, kv_rank=kv_rank),
        grid=(r // tr,),
        in_specs=[ospec(width), pl.BlockSpec((1, q_rank), lambda i: (0, 0)),
                  pl.BlockSpec((1, kv_rank), lambda i: (0, 0)), tspec, tspec, tspec],
        out_specs=[ospec(q_rank), ospec(kv_rank), ospec(LANE)],
        out_shape=[jax.ShapeDtypeStruct((r, q_rank), BF16), jax.ShapeDtypeStruct((r, kv_rank), BF16),
                   jax.ShapeDtypeStruct((r, LANE), BF16)],
        compiler_params=_params("parallel"),
        name="mla_prep",
    )(pb, q_gain.reshape(1, q_rank), kv_gain.reshape(1, kv_rank), *tables)


def _merge_kernel(oa_ref, ob_ref, oc_ref, od_ref, ga_ref, gb_ref, gc_ref, gd_ref, w_ref, o_ref):
    acc = None
    for i, (o_ref_i, g_ref_i) in enumerate(((oa_ref, ga_ref), (ob_ref, gb_ref), (oc_ref, gc_ref), (od_ref, gd_ref))):
        y = jnp.dot(o_ref_i[...], w_ref[i], preferred_element_type=F32)
        term = _sigmoid(g_ref_i[...].astype(F32)) * y
        acc = term if acc is None else acc + term
    o_ref[...] = acc.astype(o_ref.dtype)


def merge_branches(outs, gate_logits, w_branch, layer):
    r, c = outs[0].shape
    d = w_branch.shape[3]
    bm = _pick(r, (512, 256, 128))
    bn = _pick(d, (1024, 512, 256, 128))
    nb = d // bn
    ospec = pl.BlockSpec((bm, c), lambda i, j: (i, 0))

    def gspec(branch):
        return pl.BlockSpec((bm, bn), lambda i, j: (i, branch * nb + j))

    return pl.pallas_call(
        _merge_kernel,
        grid=(r // bm, nb),
        in_specs=[ospec] * N_BRANCH + [gspec(i) for i in range(N_BRANCH)]
        + [pl.BlockSpec((None, N_BRANCH, c, bn), lambda i, j: (layer, 0, 0, j))],
        out_specs=pl.BlockSpec((bm, bn), lambda i, j: (i, j)),
        out_shape=jax.ShapeDtypeStruct((r, d), BF16),
        compiler_params=_params("parallel", "parallel"),
        name="merge_branches",
    )(*outs, gate_logits, gate_logits, gate_logits, gate_logits, w_branch)


HALO = 16


FFN_BN = 640


def _ffn_tile_pairs(a, bn):
    f = a.shape[-1] // 2
    lead = a.shape[:-1]
    pairs = jnp.stack([a[..., :f].reshape(*lead, f // bn, bn), a[..., f:].reshape(*lead, f // bn, bn)], axis=-2)
    return pairs.reshape(*lead, 2 * f)


def _cast_pair_kernel(g_ref, v_ref, o_ref):
    bn = g_ref.shape[1]
    o_ref[:, :bn] = g_ref[...].astype(o_ref.dtype)
    o_ref[:, bn:] = v_ref[...].astype(o_ref.dtype)


def ffn_up_weight(w_up, layer, bn):
    _, d, two_f = w_up.shape
    nb = two_f // 2 // bn
    tk = _pick(d, (1024, 512, 256, 128))
    return pl.pallas_call(
        _cast_pair_kernel,
        grid=(d // tk, nb),
        in_specs=[pl.BlockSpec((None, tk, bn), lambda i, j: (layer, i, j)),
                  pl.BlockSpec((None, tk, bn), lambda i, j: (layer, i, nb + j))],
        out_specs=pl.BlockSpec((tk, 2 * bn), lambda i, j: (i, j)),
        out_shape=jax.ShapeDtypeStruct((d, two_f), BF16),
        compiler_params=_params("parallel", "parallel"),
        name="ffn_up_weight",
    )(w_up, w_up)


FFN_EPILOGUE_ROWS = 256


def _ffn_up_kernel(h_ref, hp_ref, hn_ref, w_ref, cw_ref, cb_ref, o_ref, hbuf_ref, u_ref, *, bm, s, n_lat):
    i = pl.program_id(0)
    bn = o_ref.shape[1]

    @pl.when(pl.program_id(1) == 0)
    def _():
        hbuf_ref[0:HALO, :] = hp_ref[...]
        hbuf_ref[HALO:HALO + bm, :] = h_ref[...]
        hbuf_ref[HALO + bm:, :] = hn_ref[...]

    u_ref[...] = jnp.dot(hbuf_ref[...], w_ref[...], preferred_element_type=F32)
    cw = cw_ref[...]
    cb = cb_ref[...]
    er = min(FFN_EPILOGUE_ROWS, bm)
    for c in range(bm // er):
        r0 = HALO + c * er
        pos = lax.rem(i * bm + c * er + lax.broadcasted_iota(jnp.int32, (er, 1), 0), s)
        has_prev = jnp.where((pos == 0) | (pos == n_lat), 0.0, 1.0)
        has_next = jnp.where((pos == n_lat - 1) | (pos == s - 1), 0.0, 1.0)
        win = u_ref[r0 - 8:r0 + er + 8, :]
        prev = pltpu.roll(win, 1, 0)[8:8 + er]
        nxt = pltpu.roll(win, er + 16 - 1, 0)[8:8 + er]
        y = prev * has_prev * cw[0:1] + win[8:8 + er] * cw[1:2] + nxt * has_next * cw[2:3] + cb
        gate, val = y[:, :bn], y[:, bn:]
        o_ref[c * er:(c + 1) * er, :] = (gate * _sigmoid(gate) * val).astype(o_ref.dtype)


def ffn_up(h, w_up, conv_w, conv_b, s, n_lat):
    r, d = h.shape
    f = w_up.shape[1] // 2
    bm = _pick(r, (1024, 512, 256, 128))
    bn = FFN_BN
    assert f % bn == 0
    halo_blocks = r // HALO
    per_tile = bm // HALO
    return pl.pallas_call(
        functools.partial(_ffn_up_kernel, bm=bm, s=s, n_lat=n_lat),
        grid=(r // bm, f // bn),
        in_specs=[
            pl.BlockSpec((bm, d), lambda i, j: (i, 0), pipeline_mode=pl.Buffered(1)),
            pl.BlockSpec((HALO, d), lambda i, j: (jnp.maximum(i * per_tile - 1, 0), 0)),
            pl.BlockSpec((HALO, d), lambda i, j: (jnp.minimum((i + 1) * per_tile, halo_blocks - 1), 0)),
            pl.BlockSpec((d, 2 * bn), lambda i, j: (0, j)),
            pl.BlockSpec((3, 2 * bn), lambda i, j: (0, j)),
            pl.BlockSpec((1, 2 * bn), lambda i, j: (0, j)),
        ],
        out_specs=pl.BlockSpec((bm, bn), lambda i, j: (i, j)),
        out_shape=jax.ShapeDtypeStruct((r, f), BF16),
        scratch_shapes=[pltpu.VMEM((bm + 2 * HALO, d), BF16), pltpu.VMEM((bm + 2 * HALO, 2 * bn), F32)],
        compiler_params=_params("parallel", "arbitrary"),
        name="ffn_up",
    )(h, h, h, w_up, conv_w, conv_b)


W_IN_SPLIT_ROWS = 128


def _w_in_split_kernel(wt_ref, *out_refs, starts):
    for o_ref, start in zip(out_refs, starts):
        width = o_ref.shape[1]
        o_ref[...] = wt_ref[start:start + width, :].T.astype(o_ref.dtype)


def _split_w_in(w_in, layer, d, heads):
    mix = heads * HEAD_DIM
    q_rank = 3 * d // 16
    kv_rank = d // 8
    b_used = q_rank + kv_rank + MLA_ROPE
    b_width = -(-(q_rank + kv_rank + LANE) // 512) * 512
    sizes = [3 * mix, b_used, 5 * mix, 3 * mix, N_BRANCH * d]
    starts = [sum(sizes[:i]) for i in range(len(sizes))]
    total = w_in.shape[2]
    assert sum(sizes) == total and starts[1] + b_width <= total
    widths = [sizes[0], b_width, sizes[2], sizes[3], sizes[4]]
    assert all(s % 16 == 0 for s in starts) and all(w % LANE == 0 for w in widths)
    tk = W_IN_SPLIT_ROWS
    outs = pl.pallas_call(
        functools.partial(_w_in_split_kernel, starts=tuple(starts)),
        grid=(d // tk,),
        in_specs=[pl.BlockSpec((None, total, tk), lambda i: (layer, 0, i))],
        out_specs=[pl.BlockSpec((tk, w), lambda i: (i, 0)) for w in widths],
        out_shape=[jax.ShapeDtypeStruct((d, w), BF16) for w in widths],
        compiler_params=_params("parallel"),
        name="w_in_split",
    )(jnp.swapaxes(w_in, 1, 2))
    return (*outs, q_rank, kv_rank)


def _mla_q_weight(w_qb, heads):
    rq = w_qb.shape[0]
    w = w_qb.reshape(rq, heads, MLA_NOPE + MLA_ROPE)
    nope = w[:, :, :MLA_NOPE].reshape(rq, heads * MLA_NOPE)
    ropep = jnp.pad(w[:, :, MLA_NOPE:], ((0, 0), (0, 0), (0, LANE - MLA_ROPE))).reshape(rq, heads * LANE)
    return jnp.concatenate([nope, ropep], axis=1).astype(BF16)


def kernel(x, c, ctx, c_ctx, ada_w, ada_b, norm_mix_pre, norm_mix_post, norm_ffn_pre, norm_ffn_post, w_in,
           diff_lambda, diff_subln, mla_q_norm, mla_w_qb, mla_kv_norm, mla_w_kvb, hgrn_lb_logits, hgrn_norm,
           na_rpb, w_branch, w_out, ffn_w_up, ffn_conv_w, ffn_conv_b, ffn_w_down):
    bsz, n_lat, d = x.shape
    n_ctx = ctx.shape[1]
    s = n_lat + n_ctx
    r = bsz * s
    depth = ada_w.shape[0]
    heads = d // (N_BRANCH * HEAD_DIM)
    mix = heads * HEAD_DIM
    assert n_lat % 256 == 0 and n_ctx % 256 == 0 and s % n_ctx == 0 and n_lat % GRID_W == 0

    tables = _rope_tables(n_lat, n_ctx)
    lb_sm = jax.nn.softmax(hgrn_lb_logits.astype(F32), axis=0)
    lower_bounds = jnp.cumsum(lb_sm, axis=0) - lb_sm[0]

    c_rows = jnp.zeros((8, d), F32).at[:bsz].set(c).at[bsz].set(c_ctx)
    xa = jnp.concatenate([x, ctx], axis=1)

    def mod_rows(mod, idx):
        part = mod[:, idx * d:(idx + 1) * d]
        both = jnp.stack([part[:bsz], jnp.broadcast_to(part[bsz], (bsz, d))], axis=1)
        return both.reshape(2 * bsz, 1, d)

    mods = []
    for l in range(depth):
        mod = ada_modulation(c_rows, ada_w, ada_b, l)
        mods.append([mod_rows(mod, i) for i in range(6)])

    w_branch_bf, w_out_bf, w_down_bf = (w.astype(BF16) for w in (w_branch, w_out, ffn_w_down))

    h = norm_modulate(xa, norm_mix_pre[0], mods[0][0], mods[0][1], n_lat)
    for l in range(depth):
        with_ctx = l < depth - 1
        sh1, sc1, g1, sh2, sc2, g2 = mods[l]
        h2d = h.reshape(r, d)
        rows = s if with_ctx else n_lat

        w_a, w_b, w_c, w_d, w_g, q_rank, kv_rank = _split_w_in(w_in, l, d, heads)
        pa = matmul(h2d, w_a, BF16, "in_proj_a")
        pb = matmul(h2d, w_b, BF16, "in_proj_b")
        pc = matmul(h2d, w_c, BF16, "in_proj_c")
        pd = matmul(h2d, w_d, BF16, "in_proj_d")
        if with_ctx:
            pg = matmul(h2d, w_g, BF16, "in_proj_gate")
        else:
            pg = matmul_leading_rows(h, w_g, rows, BF16, "in_proj_gate")

        lam_init = 0.8 - 0.6 * math.exp(-0.3 * l)
        extras_a = [diff_lambda[l].astype(F32), diff_subln[l].reshape(1, HEAD_DIM).astype(F32)]

        def call_a(q_range, k_range, tag):
            return _attention_call(
                functools.partial(_diff_attn_kernel, lam_init=lam_init),
                [(pa, 0)], [(pa, heads, 1, False)], (pa, 2 * heads, 1), extras_a,
                bsz=bsz, heads=heads, s=s, q_range=q_range, k_range=k_range, tq=min(2048, q_range[1]), lanes=256,
                name="diff_attn_" + tag, q_row_tables=tables, k_row_tables=tables)

        oa = _attend_lat_and_ctx(call_a, n_lat, n_ctx, with_ctx)

        qn, cn, kr = mla_prep(pb, mla_q_norm[l], mla_kv_norm[l], tables, s)
        qb = matmul(qn, _mla_q_weight(mla_w_qb[l], heads), BF16, "mla_q_up")
        kvb = matmul(cn, mla_w_kvb[l].astype(BF16), BF16, "mla_kv_up")

        def call_b(q_range, k_range, tag):
            return _attention_call(
                functools.partial(_sdpa_kernel, n_qk=2, scale=(MLA_NOPE + MLA_ROPE) ** -0.5, rope_q=1),
                [(qb, 0), (qb, heads)], [(kvb, 0, 2, False), (kr, 0, 0, True)], (kvb, 1, 2), [],
                bsz=bsz, heads=heads, s=s, q_range=q_range, k_range=k_range, tq=min(2048, q_range[1]), lanes=256,
                name="mla_attn_" + tag, q_row_tables=tables)

        ob = _attend_lat_and_ctx(call_b, n_lat, n_ctx, with_ctx)

        o_fwd = hgrn_scan(pc, lower_bounds[l, 0], bsz, heads, n_lat, n_ctx, reverse=False)
        o_bwd = hgrn_scan(pc, lower_bounds[l, 1], bsz, heads, n_lat, n_ctx, reverse=True)
        oc = hgrn_readout(o_fwd, o_bwd, pc, hgrn_norm[l], bsz, s, rows)

        od = neighbourhood_attention(pd, _na_bias_table(na_rpb[l], n_lat // GRID_W), bsz, heads, n_lat, n_ctx)
        if with_ctx:
            od_ctx = _attention_call(
                functools.partial(_sdpa_kernel, n_qk=1, scale=HEAD_DIM ** -0.5, rope_q=None),
                [(pd, 0)], [(pd, heads, 1, False)], (pd, 2 * heads, 1), [],
                bsz=bsz, heads=heads, s=s, q_range=(n_lat, n_ctx), k_range=(n_lat, n_ctx), tq=256, lanes=256,
                name="na_ctx_attn").reshape(bsz, n_ctx, mix)
            od = jnp.concatenate([od.reshape(bsz, n_lat, mix), od_ctx], axis=1).reshape(r, mix)

        merged = merge_branches((oa, ob, oc, od), pg, w_branch_bf, l)
        mix_out = matmul(merged, w_out_bf, BF16, "out_proj", layer=l).reshape(bsz, rows, d)
        xa, h2 = residual_norm_modulate(xa, mix_out, g1, norm_mix_post[l], norm_ffn_pre[l], sh2, sc2, n_lat)

        gv = ffn_up(h2.reshape(bsz * rows, d), ffn_up_weight(ffn_w_up, l, FFN_BN),
                    _ffn_tile_pairs(ffn_conv_w[l], FFN_BN), _ffn_tile_pairs(ffn_conv_b[l][None], FFN_BN),
                    rows, n_lat)
        ffn = matmul(gv, w_down_bf, BF16, "ffn_down", layer=l).reshape(bsz, rows, d)
        if with_ctx:
            xa, h = residual_norm_modulate(xa, ffn, g2, norm_ffn_post[l], norm_mix_pre[l + 1],
                                           mods[l + 1][0], mods[l + 1][1], n_lat)
        else:
            xa = residual(xa, ffn, g2, norm_ffn_post[l], n_lat)
    return xa
```

```python
import functools
import math

import numpy as np
import jax
import jax.numpy as jnp
from jax import lax
from jax.experimental import pallas as pl
from jax.experimental.pallas import tpu as pltpu

F32 = jnp.float32
BF16 = jnp.bfloat16

HEAD_DIM = 128
QK_HALF = 64
MLA_NOPE = 128
MLA_ROPE = 64
MLA_V = 128
ROPE_DIM = 64
ROPE_THETA = 10000.0
GRID_W = 64
NA_ROWS = 8
NA_COLS = 16
CHUNK = 128
NORM_EPS = 1e-6
NEG_INF = -1e30
N_BRANCH = 4

LANE = 128
VMEM_LIMIT_BYTES = 56 * 1024 * 1024

NT_DIMS = (((1,), (1,)), ((), ()))
TN_DIMS = (((0,), (0,)), ((), ()))


def _params(*sem):
    return pltpu.CompilerParams(dimension_semantics=sem, vmem_limit_bytes=VMEM_LIMIT_BYTES)


def _sigmoid(x):
    return 0.5 * jnp.tanh(0.5 * x) + 0.5


def _pick(total, candidates):
    for c in candidates:
        if total % c == 0:
            return c
    raise ValueError(f"no tile for {total} in {candidates}")


def _ada_kernel(c_ref, w_ref, b_ref, o_ref):
    c = c_ref[...]
    a = (c * _sigmoid(c)).astype(BF16)
    o_ref[...] = jnp.dot(a, w_ref[...].astype(BF16), preferred_element_type=F32) + b_ref[...]


def ada_modulation(c_rows, ada_w, ada_b, layer):
    rows, d = c_rows.shape
    width = ada_w.shape[2]
    tn = _pick(width, (512, 256, 128))
    b3 = ada_b.reshape(ada_b.shape[0], 1, width)
    return pl.pallas_call(
        _ada_kernel,
        grid=(width // tn,),
        in_specs=[
            pl.BlockSpec((rows, d), lambda j: (0, 0)),
            pl.BlockSpec((None, d, tn), lambda j: (layer, 0, j)),
            pl.BlockSpec((None, 1, tn), lambda j: (layer, 0, j)),
        ],
        out_specs=pl.BlockSpec((rows, tn), lambda j: (0, j)),
        out_shape=jax.ShapeDtypeStruct((rows, width), F32),
        compiler_params=_params("parallel"),
        name="ada_modulation",
    )(c_rows, ada_w, b3)


def _rms(x, gain):
    return x * lax.rsqrt(jnp.mean(x * x, axis=-1, keepdims=True) + NORM_EPS) * gain


def _norm_mod_kernel(x_ref, g_ref, sh_ref, sc_ref, h_ref):
    h = _rms(x_ref[0], g_ref[...])
    h_ref[0] = (h * (1.0 + sc_ref[0]) + sh_ref[0]).astype(h_ref.dtype)


def _resid_kernel(x_ref, y_ref, gate_ref, gpost_ref, xo_ref):
    xo_ref[0] = x_ref[0] + gate_ref[0] * _rms(y_ref[0].astype(F32), gpost_ref[...])


def _resid_norm_mod_kernel(x_ref, y_ref, gate_ref, gpost_ref, gpre_ref, sh_ref, sc_ref, xo_ref, h_ref):
    xn = x_ref[0] + gate_ref[0] * _rms(y_ref[0].astype(F32), gpost_ref[...])
    xo_ref[0] = xn
    h = _rms(xn, gpre_ref[...])
    h_ref[0] = (h * (1.0 + sc_ref[0]) + sh_ref[0]).astype(h_ref.dtype)


def _row_specs(dims, tr, n_lat_tiles):
    s, d = dims
    xspec = pl.BlockSpec((1, tr, d), lambda b, t: (b, t, 0))
    gspec = pl.BlockSpec((1, d), lambda b, t: (0, 0))
    mspec = pl.BlockSpec((1, 1, d), lambda b, t: (2 * b + t // n_lat_tiles, 0, 0))
    return xspec, gspec, mspec


def norm_modulate(x, gain, shift, scale, n_lat, tr=256):
    bsz, s, d = x.shape
    xspec, gspec, mspec = _row_specs((s, d), tr, n_lat // tr)
    return pl.pallas_call(
        _norm_mod_kernel,
        grid=(bsz, s // tr),
        in_specs=[xspec, gspec, mspec, mspec],
        out_specs=xspec,
        out_shape=jax.ShapeDtypeStruct(x.shape, BF16),
        compiler_params=_params("parallel", "parallel"),
        name="norm_modulate",
    )(x, gain.reshape(1, d), shift, scale)


def residual(x, y, gate, gain_post, n_lat, tr=256):
    bsz, s, d = x.shape
    rows = y.shape[1]
    xspec, gspec, mspec = _row_specs((s, d), tr, n_lat // tr)
    return pl.pallas_call(
        _resid_kernel,
        grid=(bsz, rows // tr),
        in_specs=[xspec, xspec, mspec, gspec],
        out_specs=xspec,
        out_shape=jax.ShapeDtypeStruct(y.shape, F32),
        compiler_params=_params("parallel", "parallel"),
        name="residual",
    )(x, y, gate, gain_post.reshape(1, d))


def residual_norm_modulate(x, y, gate, gain_post, gain_pre, shift, scale, n_lat, tr=256):
    bsz, s, d = x.shape
    rows = y.shape[1]
    xspec, gspec, mspec = _row_specs((s, d), tr, n_lat // tr)
    return pl.pallas_call(
        _resid_norm_mod_kernel,
        grid=(bsz, rows // tr),
        in_specs=[xspec, xspec, mspec, gspec, gspec, mspec, mspec],
        out_specs=[xspec, xspec],
        out_shape=[jax.ShapeDtypeStruct(y.shape, F32), jax.ShapeDtypeStruct(y.shape, BF16)],
        compiler_params=_params("parallel", "parallel"),
        name="residual_norm_modulate",
    )(x, y, gate, gain_post.reshape(1, d), gain_pre.reshape(1, d), shift, scale)


def _mm_kernel(a_ref, b_ref, o_ref):
    o_ref[...] = jnp.dot(a_ref[...], b_ref[...], preferred_element_type=F32).astype(o_ref.dtype)


def _mm_acc_kernel(a_ref, b_ref, o_ref, acc_ref):
    k = pl.program_id(2)

    @pl.when(k == 0)
    def _():
        acc_ref[...] = jnp.zeros_like(acc_ref)

    acc_ref[...] += jnp.dot(a_ref[...], b_ref[...], preferred_element_type=F32)

    @pl.when(k == pl.num_programs(2) - 1)
    def _():
        o_ref[...] = acc_ref[...].astype(o_ref.dtype)


MAX_SINGLE_K = 4096
MATMUL_VMEM_BUDGET = VMEM_LIMIT_BYTES * 9 // 10


def matmul_leading_rows(a, b, rows, out_dtype, name):
    bsz, _, k = a.shape
    n = b.shape[1]
    assert k <= MAX_SINGLE_K
    bm = _pick(rows, (1024, 512, 256, 128))
    bn = _pick(n, (1024, 768, 640, 512, 384, 256, 128))
    tiles = rows // bm
    return pl.pallas_call(
        _mm_kernel,
        grid=(bsz, tiles, n // bn),
        in_specs=[pl.BlockSpec((None, bm, k), lambda s, i, j: (s, i, 0)),
                  pl.BlockSpec((k, bn), lambda s, i, j: (0, j))],
        out_specs=pl.BlockSpec((bm, bn), lambda s, i, j: (s * tiles + i, j)),
        out_shape=jax.ShapeDtypeStruct((bsz * rows, n), out_dtype),
        compiler_params=_params("parallel", "parallel", "parallel"),
        name=name,
    )(a, b)


def matmul(a, b, out_dtype, name, layer=None, weights_outer=False):
    m, k = a.shape
    n = b.shape[-1]
    if layer is None:
        def bspec(rows, cols, index):
            return pl.BlockSpec((rows, cols), index)
    else:
        def bspec(rows, cols, index):
            return pl.BlockSpec((None, rows, cols), lambda *g: (layer, *index(*g)))
    out_bytes = jnp.dtype(out_dtype).itemsize
    full_k = None
    for cap in (1024, 512):
        bm = _pick(m, tuple(c for c in (1024, 512, 256, 128) if c <= cap))
        bn = _pick(n, tuple(c for c in (1024, 768, 640, 512, 384, 256, 128) if c <= cap))
        need = 2 * 2 * (bm * k + k * bn) + 2 * bm * bn * out_bytes + 4 * bm * bn
        if need <= MATMUL_VMEM_BUDGET:
            full_k = (bm, bn)
            break
    if full_k is not None and weights_outer:
        bm, bn = full_k
        return pl.pallas_call(
            _mm_kernel,
            grid=(n // bn, m // bm),
            in_specs=[pl.BlockSpec((bm, k), lambda j, i: (i, 0)),
                      bspec(k, bn, lambda j, i: (0, j))],
            out_specs=pl.BlockSpec((bm, bn), lambda j, i: (i, j)),
            out_shape=jax.ShapeDtypeStruct((m, n), out_dtype),
            compiler_params=_params("parallel", "parallel"),
            name=name,
        )(a, b)
    if full_k is not None:
        bm, bn = full_k
        return pl.pallas_call(
            _mm_kernel,
            grid=(m // bm, n // bn),
            in_specs=[pl.BlockSpec((bm, k), lambda i, j: (i, 0)),
                      bspec(k, bn, lambda i, j: (0, j))],
            out_specs=pl.BlockSpec((bm, bn), lambda i, j: (i, j)),
            out_shape=jax.ShapeDtypeStruct((m, n), out_dtype),
            compiler_params=_params("parallel", "parallel"),
            name=name,
        )(a, b)
    bm = _pick(m, (1024, 512, 256, 128))
    bn = _pick(n, (1024, 768, 640, 512, 384, 256, 128))
    nk = -(-k // MAX_SINGLE_K)
    while k % nk or (k // nk) % LANE:
        nk += 1
    bk = k // nk
    return pl.pallas_call(
        _mm_acc_kernel,
        grid=(m // bm, n // bn, nk),
        in_specs=[pl.BlockSpec((bm, bk), lambda i, j, kk: (i, kk)),
                  bspec(bk, bn, lambda i, j, kk: (kk, j))],
        out_specs=pl.BlockSpec((bm, bn), lambda i, j, kk: (i, j)),
        out_shape=jax.ShapeDtypeStruct((m, n), out_dtype),
        scratch_shapes=[pltpu.VMEM((bm, bn), F32)],
        compiler_params=_params("parallel", "parallel", "arbitrary"),
        name=name,
    )(a, b)


def _rope_tables(n_lat, n_ctx):
    t = jnp.arange(n_lat, dtype=jnp.int32)
    row = (t // GRID_W).astype(F32)
    col = (t % GRID_W).astype(F32)
    quarter = ROPE_DIM // 4
    inv = ROPE_THETA ** (-jnp.arange(quarter, dtype=F32) / quarter)
    ar = row[:, None] * inv
    ac = col[:, None] * inv
    ang = jnp.concatenate([ar, ar, ac, ac], axis=-1)
    cos = jnp.concatenate([jnp.cos(ang), jnp.ones((n_ctx, ROPE_DIM), F32)], axis=0)
    sin = jnp.concatenate([jnp.sin(ang), jnp.zeros((n_ctx, ROPE_DIM), F32)], axis=0)
    first = (jnp.arange(ROPE_DIM) // quarter) % 2 == 0
    sin_next = jnp.where(first, -sin, 0.0)
    sin_prev = jnp.where(first, 0.0, sin)
    reps = LANE // ROPE_DIM
    return tuple(jnp.tile(a, (1, reps)) for a in (cos, sin_next, sin_prev))


def _rope_apply(x, cos, sin_next, sin_prev):
    quarter = ROPE_DIM // 4
    return (x * cos + pltpu.roll(x, LANE - quarter, 1) * sin_next
            + pltpu.roll(x, quarter, 1) * sin_prev)


LOG2E = math.log2(math.e)
KEY_CHUNK = 256


def _attend_tiles(k_ref, q_tiles, vt_ref, s_ref, e_ref):
    m_keys, n = s_ref.shape[1:]
    kc = min(KEY_CHUNK, m_keys)
    chunks = [slice(c * kc, (c + 1) * kc) for c in range(m_keys // kc)]

    def scores(t):
        s_ref[t % 2] = lax.dot_general(k_ref[...], q_tiles[t], NT_DIMS, preferred_element_type=F32)

    def column_max(t):
        m8 = None
        for sl in chunks:
            part = jnp.max(s_ref[t % 2, sl, :].reshape(kc // 8, 8, n), axis=0)
            m8 = part if m8 is None else jnp.maximum(m8, part)
        return jnp.max(m8, axis=0, keepdims=True)

    scores(0)
    m = column_max(0)
    outs = []
    for t in range(len(q_tiles)):
        if t + 1 < len(q_tiles):
            scores(t + 1)
        l8 = jnp.zeros((8, n), F32)
        for sl in chunks:
            e = jnp.exp2(s_ref[t % 2, sl, :] - m)
            l8 = l8 + jnp.sum(e.reshape(kc // 8, 8, n), axis=0)
            e_ref[t % 2, sl, :] = e.astype(e_ref.dtype)
        o_t = jnp.dot(vt_ref[...], e_ref[t % 2], preferred_element_type=F32)
        outs.append((o_t, jnp.sum(l8, axis=0, keepdims=True)))
        if t + 1 < len(q_tiles):
            m = column_max(t + 1)
    return outs


def _load_v_transposed(v_ref, vt_ref):
    @pl.when(pl.program_id(2) == 0)
    def _():
        vt_ref[...] = v_ref[...].astype(F32).T.astype(vt_ref.dtype)


def _diff_attn_kernel(q_ref, k_ref, v_ref, lam_ref, g_ref, qcos_ref, qsn_ref, qsp_ref, kcos_ref, ksn_ref, ksp_ref,
                      o_ref, vt_ref, s_ref, e_ref, kr_ref, *, lam_init):
    _load_v_transposed(v_ref, vt_ref)

    @pl.when(pl.program_id(2) == 0)
    def _():
        kr_ref[...] = _rope_apply(k_ref[...].astype(F32), kcos_ref[...], ksn_ref[...], ksp_ref[...]
                                  ).astype(kr_ref.dtype)

    tq = s_ref.shape[2] // 2
    q = _rope_apply(q_ref[...].astype(F32), qcos_ref[...], qsn_ref[...], qsp_ref[...])
    q = (q * (QK_HALF ** -0.5 * LOG2E)).astype(kr_ref.dtype)
    lane = lax.broadcasted_iota(jnp.int32, q.shape, 1)
    zero = jnp.zeros_like(q)
    q0 = jnp.where(lane < QK_HALF, q, zero)
    q1 = jnp.where(lane >= QK_HALF, q, zero)
    tiles = [slice(t * tq, (t + 1) * tq) for t in range(q.shape[0] // tq)]
    outs = _attend_tiles(kr_ref, [jnp.concatenate([q0[sl], q1[sl]], axis=0) for sl in tiles], vt_ref, s_ref, e_ref)
    lp = lam_ref[...]
    lam = (jnp.exp(jnp.sum(lp[0:1] * lp[1:2], axis=-1, keepdims=True))
           - jnp.exp(jnp.sum(lp[2:3] * lp[3:4], axis=-1, keepdims=True)) + lam_init)
    gain = g_ref[...] * (1.0 - lam_init)
    for sl, (o_t, l) in zip(tiles, outs):
        inv = 1.0 / l
        c_t = o_t[:, :tq] * inv[:, :tq] - o_t[:, tq:] * (lam * inv[:, tq:])
        c_t = c_t * lax.rsqrt(jnp.mean(c_t * c_t, axis=0, keepdims=True) + NORM_EPS)
        o_ref[sl, :] = (c_t.T * gain).astype(o_ref.dtype)


def _sdpa_kernel(*refs, n_qk, scale, rope_q):
    q_refs, k_refs = refs[:n_qk], refs[n_qk:2 * n_qk]
    n_tab = 0 if rope_q is None else 3
    v_ref = refs[2 * n_qk]
    tables = refs[2 * n_qk + 1:2 * n_qk + 1 + n_tab]
    o_ref, vt_ref, s_ref, e_ref = refs[2 * n_qk + 1 + n_tab:2 * n_qk + 5 + n_tab]
    _load_v_transposed(v_ref, vt_ref)
    if n_qk == 1:
        kcat_ref = k_refs[0]
    else:
        kcat_ref = refs[2 * n_qk + 5 + n_tab]

        @pl.when(pl.program_id(2) == 0)
        def _():
            for i, k_ref in enumerate(k_refs):
                kcat_ref[:, i * HEAD_DIM:(i + 1) * HEAD_DIM] = k_ref[...]

    tq = s_ref.shape[2]
    parts = [q_ref[...].astype(F32) for q_ref in q_refs]
    if rope_q is not None:
        parts[rope_q] = _rope_apply(parts[rope_q], *[t[...] for t in tables])
    q = (jnp.concatenate(parts, axis=1) * (scale * LOG2E)).astype(kcat_ref.dtype)
    tiles = [slice(t * tq, (t + 1) * tq) for t in range(q.shape[0] // tq)]
    outs = _attend_tiles(kcat_ref, [q[sl] for sl in tiles], vt_ref, s_ref, e_ref)
    for sl, (o_t, l) in zip(tiles, outs):
        o_ref[sl, :] = (o_t * (1.0 / l)).T.astype(o_ref.dtype)


def _attention_call(kernel, q_list, k_list, v, extras, *, bsz, heads, s, q_range, k_range, tq, lanes, name,
                    q_row_tables=(), k_row_tables=()):
    q_start, q_len = q_range
    k_start, k_len = k_range
    nq = q_len // tq
    q_off = q_start // tq
    k_off = k_start // k_len
    assert q_start % tq == 0 and q_len % tq == 0 and k_start % k_len == 0

    def qspec(col0):
        return pl.BlockSpec((None, tq, HEAD_DIM), lambda b, h, i: (b, q_off + i, col0 + h))

    def kspec(col0, stride, shared=False):
        if shared:
            return pl.BlockSpec((None, k_len, HEAD_DIM), lambda b, h, i: (b, k_off, col0))
        return pl.BlockSpec((None, k_len, HEAD_DIM), lambda b, h, i: (b, k_off, col0 + stride * h))

    def per_sample(arr):
        return arr.reshape(bsz, s, arr.shape[-1])

    in_specs, args = [], []
    for arr, col0 in q_list:
        in_specs.append(qspec(col0)); args.append(per_sample(arr))
    for arr, col0, stride, shared in k_list:
        in_specs.append(kspec(col0, stride, shared)); args.append(per_sample(arr))
    arr, col0, stride = v
    in_specs.append(kspec(col0, stride)); args.append(per_sample(arr))
    for e in extras:
        in_specs.append(pl.BlockSpec(e.shape, lambda b, h, i: (0, 0))); args.append(e)
    for t in q_row_tables:
        in_specs.append(pl.BlockSpec((tq, LANE), lambda b, h, i: (q_off + i, 0))); args.append(t)
    for t in k_row_tables:
        in_specs.append(pl.BlockSpec((k_len, LANE), lambda b, h, i: (k_off, 0))); args.append(t)
    return pl.pallas_call(
        kernel,
        grid=(bsz, heads, nq),
        in_specs=in_specs,
        out_specs=pl.BlockSpec((tq, HEAD_DIM), lambda b, h, i: (b * nq + i, h)),
        out_shape=jax.ShapeDtypeStruct((bsz * q_len, heads * HEAD_DIM), BF16),
        scratch_shapes=[pltpu.VMEM((HEAD_DIM, k_len), BF16), pltpu.VMEM((2, k_len, lanes), F32),
                        pltpu.VMEM((2, k_len, lanes), BF16)]
        + ([pltpu.VMEM((k_len, len(k_list) * HEAD_DIM), BF16)] if len(k_list) > 1 else [])
        + ([pltpu.VMEM((k_len, HEAD_DIM), BF16)] if k_row_tables else []),
        compiler_params=_params("parallel", "parallel", "arbitrary"),
        name=name,
    )(*args)


def _attend_lat_and_ctx(call, n_lat, n_ctx, with_ctx):
    s = n_lat + n_ctx
    o_lat = call((0, n_lat), (0, s), "lat")
    if not with_ctx:
        return o_lat
    bsz = o_lat.shape[0] // n_lat
    o_ctx = call((n_lat, n_ctx), (n_lat, n_ctx), "ctx").reshape(bsz, n_ctx, -1)
    return jnp.concatenate([o_lat.reshape(bsz, n_lat, -1), o_ctx], axis=1).reshape(bsz * s, -1)


NA_STEP_ROWS = 8
NA_WIN_ROWS = 16


def _na_window_start(step, grid_rows):
    return jnp.clip(step * NA_STEP_ROWS - NA_ROWS // 2, 0, grid_rows - NA_WIN_ROWS)


def _na_bias_table(rpb, grid_rows):
    steps = grid_rows // NA_STEP_ROWS
    heads = rpb.shape[0]
    n_dr = 2 * NA_ROWS - 1
    col = jnp.arange(GRID_W, dtype=jnp.int32)
    cs = jnp.clip(col - NA_COLS // 2, 0, GRID_W - NA_COLS)
    colmask = (col[None, :] >= cs[:, None]) & (col[None, :] < cs[:, None] + NA_COLS)
    dc = jnp.clip(col[None, :] - col[:, None] + NA_COLS - 1, 0, 2 * NA_COLS - 2)
    pat = jnp.where(colmask[None, None], rpb[:, :, dc].astype(F32) * LOG2E, NEG_INF)
    blank = jnp.full((heads, 1, GRID_W, GRID_W), NEG_INF, F32)
    pairs = jnp.concatenate([jnp.concatenate([blank, pat], axis=1), jnp.concatenate([pat, blank], axis=1)], axis=-1)

    kind_steps = [0, min(1, steps - 1), steps - 1]
    plan = []
    for st in kind_steps:
        ws = min(max(st * NA_STEP_ROWS - NA_ROWS // 2, 0), grid_rows - NA_WIN_ROWS)
        rows = []
        for i in range(NA_STEP_ROWS):
            r = st * NA_STEP_ROWS + i
            rs = min(max(r - NA_ROWS // 2, 0), grid_rows - NA_ROWS)
            rows.append([(ws + jw - r + NA_ROWS - 1) if rs <= ws + jw < rs + NA_ROWS else None
                         for jw in range(NA_WIN_ROWS)])
        plan.append(rows)

    def expand_kernel(p_ref, o_ref):
        lane = lax.broadcasted_iota(jnp.int32, (GRID_W, 2 * GRID_W), 1)
        for kind, rows in enumerate(plan):
            @pl.when(pl.program_id(1) == kind)
            def _(rows=rows):
                for i, offs in enumerate(rows):
                    for jp in range(NA_WIN_ROWS // 2):
                        left, right = offs[2 * jp], offs[2 * jp + 1]
                        if left is None and right is None:
                            blk = jnp.full((GRID_W, 2 * GRID_W), NEG_INF, F32)
                        else:
                            blk = p_ref[right if right is not None else left + 1]
                            if left is None:
                                blk = jnp.where(lane >= GRID_W, blk, NEG_INF)
                            if right is None:
                                blk = jnp.where(lane < GRID_W, blk, NEG_INF)
                        o_ref[i * GRID_W:(i + 1) * GRID_W, 2 * jp * GRID_W:2 * (jp + 1) * GRID_W] = blk

    tq, win = NA_STEP_ROWS * GRID_W, NA_WIN_ROWS * GRID_W
    return pl.pallas_call(
        expand_kernel,
        grid=(heads, 3),
        in_specs=[pl.BlockSpec((None, n_dr + 1, GRID_W, 2 * GRID_W), lambda h, k: (h, 0, 0, 0))],
        out_specs=pl.BlockSpec((None, None, tq, win), lambda h, k: (h, k, 0, 0)),
        out_shape=jax.ShapeDtypeStruct((heads, 3, tq, win), F32),
        compiler_params=_params("parallel", "arbitrary"),
        name="na_bias_table",
    )(pairs)


def _na_kernel(q_ref, k_ref, v_ref, bias_ref, o_ref, *, n_lat, grid_rows):
    start = pl.multiple_of(_na_window_start(pl.program_id(2), grid_rows) * GRID_W, GRID_W)
    win = NA_WIN_ROWS * GRID_W
    q = (q_ref[...].astype(F32) * (HEAD_DIM ** -0.5 * LOG2E)).astype(k_ref.dtype)
    sl = lax.dot_general(q, k_ref[pl.ds(start, win), :], NT_DIMS, preferred_element_type=F32) + bias_ref[...]
    sc = lax.dot_general(q, k_ref[n_lat:, :], NT_DIMS, preferred_element_type=F32)
    m = jnp.maximum(jnp.max(sl, axis=-1, keepdims=True), jnp.max(sc, axis=-1, keepdims=True))
    el = jnp.exp2(sl - m)
    ec = jnp.exp2(sc - m)
    inv = 1.0 / (jnp.sum(el, axis=-1, keepdims=True) + jnp.sum(ec, axis=-1, keepdims=True))
    o = (jnp.dot(el.astype(v_ref.dtype), v_ref[pl.ds(start, win), :], preferred_element_type=F32)
         + jnp.dot(ec.astype(v_ref.dtype), v_ref[n_lat:, :], preferred_element_type=F32))
    o_ref[...] = (o * inv).astype(o_ref.dtype)


def neighbourhood_attention(pd, bias, bsz, heads, n_lat, n_ctx):
    s = n_lat + n_ctx
    grid_rows = n_lat // GRID_W
    tq = NA_STEP_ROWS * GRID_W
    win = NA_WIN_ROWS * GRID_W
    steps = grid_rows // NA_STEP_ROWS
    assert grid_rows >= NA_WIN_ROWS and grid_rows % NA_STEP_ROWS == 0 and steps >= 2
    assert NA_WIN_ROWS >= NA_STEP_ROWS + NA_ROWS - 1

    def kind(g):
        return jnp.where(g == 0, 0, jnp.where(g == steps - 1, 2, 1))

    pd3 = pd.reshape(bsz, s, pd.shape[-1])
    return pl.pallas_call(
        functools.partial(_na_kernel, n_lat=n_lat, grid_rows=grid_rows),
        grid=(bsz, heads, steps),
        in_specs=[
            pl.BlockSpec((None, tq, HEAD_DIM), lambda b, h, g: (b, g, h)),
            pl.BlockSpec((None, s, HEAD_DIM), lambda b, h, g: (b, 0, heads + h)),
            pl.BlockSpec((None, s, HEAD_DIM), lambda b, h, g: (b, 0, 2 * heads + h)),
            pl.BlockSpec((None, None, tq, win), lambda b, h, g: (h, kind(g), 0, 0)),
        ],
        out_specs=pl.BlockSpec((tq, HEAD_DIM), lambda b, h, g: (b * steps + g, h)),
        out_shape=jax.ShapeDtypeStruct((bsz * n_lat, heads * HEAD_DIM), BF16),
        compiler_params=_params("parallel", "parallel", "arbitrary"),
        name="neighbourhood_attention",
    )(pd3, pd3, pd3, bias)


def _hgrn_constants(reverse):
    c = CHUNK
    t = np.arange(c)
    tri = (t[None, :] >= t[:, None]) if reverse else (t[None, :] <= t[:, None])
    mats = [tri.astype(np.float32)]
    masks = [np.eye(c, dtype=np.float32)]
    m = c // 2
    while m >= 1:
        node = (t // (2 * m)) * (2 * m)
        ref = node + (m if reverse else m - 1)
        if m < HGRN_BROADCAST_MIN:
            mats.append(tri[ref].astype(np.float32))
        upper = (t % (2 * m)) >= m
        same = node[:, None] == node[None, :]
        pair = same & (upper[:, None] & ~upper[None, :])
        masks.append((pair.T if reverse else pair).astype(np.float32))
        m //= 2
    return np.concatenate(mats, axis=0), np.stack(masks)


HGRN_BROADCAST_MIN = 8


def _hgrn_reference_rows(cum, cs, lev, reverse):
    c = CHUNK
    m = c >> lev
    if m < HGRN_BROADCAST_MIN:
        k = lev - int(math.log2(c // HGRN_BROADCAST_MIN))
        return cs[k * c:(k + 1) * c]
    rows = []
    for node in range(0, c, 2 * m):
        ref = node + (m if reverse else m - 1)
        rows.append(jnp.broadcast_to(cum[ref:ref + 1, :], (2 * m, cum.shape[1])))
    return rows[0] if len(rows) == 1 else jnp.concatenate(rows, axis=0)


def _hgrn_kernel(cq_ref, ci_ref, z_ref, lb_ref, ms_ref, mk_ref, o_ref, st_ref, *, heads, reverse):
    @pl.when(pl.program_id(1) == 0)
    def _():
        st_ref[...] = jnp.zeros_like(st_ref)

    c = CHUNK
    n_levels = mk_ref.shape[0]
    ms = ms_ref[...].astype(BF16)
    scale = HEAD_DIM ** -0.5
    tot_row = 0 if reverse else c - 1
    for h in range(heads):
        sl = slice(h * HEAD_DIM, (h + 1) * HEAD_DIM)
        lb = lb_ref[:, sl]
        f = lb + (1.0 - lb) * _sigmoid(z_ref[:, sl].astype(F32))
        kk = 1.0 - f
        g = jnp.log(f)
        cq = cq_ref[:, sl].astype(F32)
        qq = cq * _sigmoid(cq) * scale
        v = ci_ref[:, sl]
        g_hi = g.astype(BF16)
        r1 = g - g_hi.astype(F32)
        g_mid = r1.astype(BF16)
        g_lo = (r1 - g_mid.astype(F32)).astype(BF16)
        cs3 = jnp.dot(ms, jnp.concatenate([g_hi, g_mid, g_lo], axis=1), preferred_element_type=F32)
        cs = cs3[:, :HEAD_DIM] + cs3[:, HEAD_DIM:2 * HEAD_DIM] + cs3[:, 2 * HEAD_DIM:]
        cum = cs[:c]
        tot = cs[tot_row:tot_row + 1]
        st = st_ref[h]
        o = lax.dot_general((qq * jnp.exp(cum)).astype(BF16), st.astype(BF16), NT_DIMS,
                            preferred_element_type=F32)
        a = mk_ref[0] * lax.dot_general(qq.astype(BF16), kk.astype(BF16), NT_DIMS,
                                        preferred_element_type=F32)
        for lev in range(1, n_levels):
            e = jnp.exp(-jnp.abs(cum - _hgrn_reference_rows(cum, cs, lev, reverse)))
            a = a + mk_ref[lev] * lax.dot_general((qq * e).astype(BF16), (kk * e).astype(BF16), NT_DIMS,
                                                  preferred_element_type=F32)
        o = o + jnp.dot(a.astype(BF16), v, preferred_element_type=F32)
        o_ref[:, sl] = o
        ke = (kk * jnp.exp(tot - cum)).astype(BF16)
        st_ref[h] = st * jnp.exp(tot) + lax.dot_general(v, ke, TN_DIMS, preferred_element_type=F32)


def hgrn_scan(pc, lb, bsz, heads, n_lat, n_ctx, reverse):
    s = n_lat + n_ctx
    chunks = s // CHUNK
    lat_chunks = n_lat // CHUNK
    width = heads * HEAD_DIM
    ms, mk = _hgrn_constants(reverse)

    def chunk_of(step):
        return (chunks - 1 - step) if reverse else lax.rem(step + lat_chunks, chunks)

    def spec(col):
        return pl.BlockSpec((CHUNK, width), lambda b, t: (b * chunks + chunk_of(t), col))

    return pl.pallas_call(
        functools.partial(_hgrn_kernel, heads=heads, reverse=reverse),
        grid=(bsz, chunks),
        in_specs=[spec(0), spec(1), spec(3 if reverse else 2),
                  pl.BlockSpec((1, width), lambda b, t: (0, 0)),
                  pl.BlockSpec(ms.shape, lambda b, t: (0, 0)),
                  pl.BlockSpec(mk.shape, lambda b, t: (0, 0, 0))],
        out_specs=spec(0),
        out_shape=jax.ShapeDtypeStruct((bsz * s, width), F32),
        scratch_shapes=[pltpu.VMEM((heads, HEAD_DIM, HEAD_DIM), F32)],
        compiler_params=_params("parallel", "arbitrary"),
        name="hgrn_scan_bwd" if reverse else "hgrn_scan_fwd",
    )(pc, pc, pc, lb.reshape(1, width), jnp.asarray(ms), jnp.asarray(mk))


def _hgrn_readout_kernel(of_ref, ob_ref, gate_ref, g_ref, o_ref):
    gain = g_ref[...]
    for h in range(of_ref.shape[1] // HEAD_DIM):
        sl = slice(h * HEAD_DIM, (h + 1) * HEAD_DIM)
        gate = gate_ref[:, sl].astype(F32)
        o = _rms(of_ref[:, sl] + ob_ref[:, sl], gain)
        o_ref[:, sl] = (o * (gate * _sigmoid(gate))).astype(o_ref.dtype)


def hgrn_readout(o_fwd, o_bwd, pc, gain, bsz, s, rows, tr=256):
    width = o_fwd.shape[1]
    in_tiles, out_tiles = s // tr, rows // tr
    ispec = pl.BlockSpec((tr, width), lambda b, i: (b * in_tiles + i, 0))
    return pl.pallas_call(
        _hgrn_readout_kernel,
        grid=(bsz, out_tiles),
        in_specs=[ispec, ispec, pl.BlockSpec((tr, width), lambda b, i: (b * in_tiles + i, 4)),
                  pl.BlockSpec((1, HEAD_DIM), lambda b, i: (0, 0))],
        out_specs=pl.BlockSpec((tr, width), lambda b, i: (b * out_tiles + i, 0)),
        out_shape=jax.ShapeDtypeStruct((bsz * rows, width), BF16),
        compiler_params=_params("parallel", "parallel"),
        name="hgrn_readout",
    )(o_fwd, o_bwd, pc, gain.reshape(1, HEAD_DIM))


def _mla_prep_kernel(p_ref, qg_ref, kg_ref, cos_ref, sn_ref, sp_ref, qn_ref, cn_ref, kr_ref, *, q_rank, kv_rank):
    qn_ref[...] = _rms(p_ref[:, :q_rank].astype(F32), qg_ref[...]).astype(qn_ref.dtype)
    cn_ref[...] = _rms(p_ref[:, q_rank:q_rank + kv_rank].astype(F32), kg_ref[...]).astype(cn_ref.dtype)
    kr = p_ref[:, q_rank + kv_rank:q_rank + kv_rank + LANE].astype(F32)
    kr_ref[...] = _rope_apply(kr, cos_ref[...], sn_ref[...], sp_ref[...]).astype(kr_ref.dtype)


def mla_prep(pb, q_gain, kv_gain, tables, s, tr=256):
    r, width = pb.shape
    q_rank, kv_rank = q_gain.shape[0], kv_gain.shape[0]
    assert width >= q_rank + kv_rank + LANE and q_rank % LANE == 0 and kv_rank % LANE == 0
    tiles_per_seq = s // tr
    tspec = pl.BlockSpec((tr, LANE), lambda i: (i % tiles_per_seq, 0))

    def ospec(w):
        return pl.BlockSpec((tr, w), lambda i: (i, 0))

    return pl.pallas_call(
        functools.partial(_mla_prep_kernel, q_rank=q_rank, kv_rank=kv_rank),
        grid=(r // tr,),
        in_specs=[ospec(width), pl.BlockSpec((1, q_rank), lambda i: (0, 0)),
                  pl.BlockSpec((1, kv_rank), lambda i: (0, 0)), tspec, tspec, tspec],
        out_specs=[ospec(q_rank), ospec(kv_rank), ospec(LANE)],
        out_shape=[jax.ShapeDtypeStruct((r, q_rank), BF16), jax.ShapeDtypeStruct((r, kv_rank), BF16),
                   jax.ShapeDtypeStruct((r, LANE), BF16)],
        compiler_params=_params("parallel"),
        name="mla_prep",
    )(pb, q_gain.reshape(1, q_rank), kv_gain.reshape(1, kv_rank), *tables)


def _merge_kernel(oa_ref, ob_ref, oc_ref, od_ref, ga_ref, gb_ref, gc_ref, gd_ref, w_ref, o_ref):
    acc = None
    for i, (o_ref_i, g_ref_i) in enumerate(((oa_ref, ga_ref), (ob_ref, gb_ref), (oc_ref, gc_ref), (od_ref, gd_ref))):
        y = jnp.dot(o_ref_i[...], w_ref[i], preferred_element_type=F32)
        term = _sigmoid(g_ref_i[...].astype(F32)) * y
        acc = term if acc is None else acc + term
    o_ref[...] = acc.astype(o_ref.dtype)


def merge_branches(outs, gate_logits, w_branch, layer):
    r, c = outs[0].shape
    d = w_branch.shape[3]
    bm = _pick(r, (512, 256, 128))
    bn = _pick(d, (1024, 512, 256, 128))
    nb = d // bn
    ospec = pl.BlockSpec((bm, c), lambda i, j: (i, 0))

    def gspec(branch):
        return pl.BlockSpec((bm, bn), lambda i, j: (i, branch * nb + j))

    return pl.pallas_call(
        _merge_kernel,
        grid=(r // bm, nb),
        in_specs=[ospec] * N_BRANCH + [gspec(i) for i in range(N_BRANCH)]
        + [pl.BlockSpec((None, N_BRANCH, c, bn), lambda i, j: (layer, 0, 0, j))],
        out_specs=pl.BlockSpec((bm, bn), lambda i, j: (i, j)),
        out_shape=jax.ShapeDtypeStruct((r, d), BF16),
        compiler_params=_params("parallel", "parallel"),
        name="merge_branches",
    )(*outs, gate_logits, gate_logits, gate_logits, gate_logits, w_branch)


HALO = 16


FFN_BN = 640


def _ffn_tile_pairs(a, bn):
    f = a.shape[-1] // 2
    lead = a.shape[:-1]
    pairs = jnp.stack([a[..., :f].reshape(*lead, f // bn, bn), a[..., f:].reshape(*lead, f // bn, bn)], axis=-2)
    return pairs.reshape(*lead, 2 * f)


def _cast_pair_kernel(g_ref, v_ref, o_ref):
    bn = g_ref.shape[1]
    o_ref[:, :bn] = g_ref[...].astype(o_ref.dtype)
    o_ref[:, bn:] = v_ref[...].astype(o_ref.dtype)


def ffn_up_weight(w_up, layer, bn):
    _, d, two_f = w_up.shape
    nb = two_f // 2 // bn
    tk = _pick(d, (1024, 512, 256, 128))
    return pl.pallas_call(
        _cast_pair_kernel,
        grid=(d // tk, nb),
        in_specs=[pl.BlockSpec((None, tk, bn), lambda i, j: (layer, i, j)),
                  pl.BlockSpec((None, tk, bn), lambda i, j: (layer, i, nb + j))],
        out_specs=pl.BlockSpec((tk, 2 * bn), lambda i, j: (i, j)),
        out_shape=jax.ShapeDtypeStruct((d, two_f), BF16),
        compiler_params=_params("parallel", "parallel"),
        name="ffn_up_weight",
    )(w_up, w_up)


FFN_EPILOGUE_ROWS = 256


def _ffn_up_kernel(h_ref, hp_ref, hn_ref, w_ref, cw_ref, cb_ref, o_ref, hbuf_ref, u_ref, *, bm, s, n_lat):
    i = pl.program_id(0)
    bn = o_ref.shape[1]

    @pl.when(pl.program_id(1) == 0)
    def _():
        hbuf_ref[0:HALO, :] = hp_ref[...]
        hbuf_ref[HALO:HALO + bm, :] = h_ref[...]
        hbuf_ref[HALO + bm:, :] = hn_ref[...]

    u_ref[...] = jnp.dot(hbuf_ref[...], w_ref[...], preferred_element_type=F32)
    cw = cw_ref[...]
    cb = cb_ref[...]
    er = min(FFN_EPILOGUE_ROWS, bm)
    for c in range(bm // er):
        r0 = HALO + c * er
        pos = lax.rem(i * bm + c * er + lax.broadcasted_iota(jnp.int32, (er, 1), 0), s)
        has_prev = jnp.where((pos == 0) | (pos == n_lat), 0.0, 1.0)
        has_next = jnp.where((pos == n_lat - 1) | (pos == s - 1), 0.0, 1.0)
        win = u_ref[r0 - 8:r0 + er + 8, :]
        prev = pltpu.roll(win, 1, 0)[8:8 + er]
        nxt = pltpu.roll(win, er + 16 - 1, 0)[8:8 + er]
        y = prev * has_prev * cw[0:1] + win[8:8 + er] * cw[1:2] + nxt * has_next * cw[2:3] + cb
        gate, val = y[:, :bn], y[:, bn:]
        o_ref[c * er:(c + 1) * er, :] = (gate * _sigmoid(gate) * val).astype(o_ref.dtype)


def ffn_up(h, w_up, conv_w, conv_b, s, n_lat):
    r, d = h.shape
    f = w_up.shape[1] // 2
    bm = _pick(r, (1024, 512, 256, 128))
    bn = FFN_BN
    assert f % bn == 0
    halo_blocks = r // HALO
    per_tile = bm // HALO
    return pl.pallas_call(
        functools.partial(_ffn_up_kernel, bm=bm, s=s, n_lat=n_lat),
        grid=(r // bm, f // bn),
        in_specs=[
            pl.BlockSpec((bm, d), lambda i, j: (i, 0), pipeline_mode=pl.Buffered(1)),
            pl.BlockSpec((HALO, d), lambda i, j: (jnp.maximum(i * per_tile - 1, 0), 0)),
            pl.BlockSpec((HALO, d), lambda i, j: (jnp.minimum((i + 1) * per_tile, halo_blocks - 1), 0)),
            pl.BlockSpec((d, 2 * bn), lambda i, j: (0, j)),
            pl.BlockSpec((3, 2 * bn), lambda i, j: (0, j)),
            pl.BlockSpec((1, 2 * bn), lambda i, j: (0, j)),
        ],
        out_specs=pl.BlockSpec((bm, bn), lambda i, j: (i, j)),
        out_shape=jax.ShapeDtypeStruct((r, f), BF16),
        scratch_shapes=[pltpu.VMEM((bm + 2 * HALO, d), BF16), pltpu.VMEM((bm + 2 * HALO, 2 * bn), F32)],
        compiler_params=_params("parallel", "arbitrary"),
        name="ffn_up",
    )(h, h, h, w_up, conv_w, conv_b)


W_IN_SPLIT_ROWS = 128


def _w_in_split_kernel(wt_ref, *out_refs, starts):
    for o_ref, start in zip(out_refs, starts):
        width = o_ref.shape[1]
        o_ref[...] = wt_ref[start:start + width, :].T.astype(o_ref.dtype)


def _split_w_in(w_in, layer, d, heads):
    mix = heads * HEAD_DIM
    q_rank = 3 * d // 16
    kv_rank = d // 8
    b_used = q_rank + kv_rank + MLA_ROPE
    b_width = -(-(q_rank + kv_rank + LANE) // 512) * 512
    sizes = [3 * mix, b_used, 5 * mix, 3 * mix, N_BRANCH * d]
    starts = [sum(sizes[:i]) for i in range(len(sizes))]
    total = w_in.shape[2]
    assert sum(sizes) == total and starts[1] + b_width <= total
    widths = [sizes[0], b_width, sizes[2], sizes[3], sizes[4]]
    assert all(s % 16 == 0 for s in starts) and all(w % LANE == 0 for w in widths)
    tk = W_IN_SPLIT_ROWS
    outs = pl.pallas_call(
        functools.partial(_w_in_split_kernel, starts=tuple(starts)),
        grid=(d // tk,),
        in_specs=[pl.BlockSpec((None, total, tk), lambda i: (layer, 0, i))],
        out_specs=[pl.BlockSpec((tk, w), lambda i: (i, 0)) for w in widths],
        out_shape=[jax.ShapeDtypeStruct((d, w), BF16) for w in widths],
        compiler_params=_params("parallel"),
        name="w_in_split",
    )(jnp.swapaxes(w_in, 1, 2))
    return (*outs, q_rank, kv_rank)


def _mla_q_weight(w_qb, heads):
    rq = w_qb.shape[0]
    w = w_qb.reshape(rq, heads, MLA_NOPE + MLA_ROPE)
    nope = w[:, :, :MLA_NOPE].reshape(rq, heads * MLA_NOPE)
    ropep = jnp.pad(w[:, :, MLA_NOPE:], ((0, 0), (0, 0), (0, LANE - MLA_ROPE))).reshape(rq, heads * LANE)
    return jnp.concatenate([nope, ropep], axis=1).astype(BF16)


def kernel(x, c, ctx, c_ctx, ada_w, ada_b, norm_mix_pre, norm_mix_post, norm_ffn_pre, norm_ffn_post, w_in,
           diff_lambda, diff_subln, mla_q_norm, mla_w_qb, mla_kv_norm, mla_w_kvb, hgrn_lb_logits, hgrn_norm,
           na_rpb, w_branch, w_out, ffn_w_up, ffn_conv_w, ffn_conv_b, ffn_w_down):
    bsz, n_lat, d = x.shape
    n_ctx = ctx.shape[1]
    s = n_lat + n_ctx
    r = bsz * s
    depth = ada_w.shape[0]
    heads = d // (N_BRANCH * HEAD_DIM)
    mix = heads * HEAD_DIM
    assert n_lat % 256 == 0 and n_ctx % 256 == 0 and s % n_ctx == 0 and n_lat % GRID_W == 0

    tables = _rope_tables(n_lat, n_ctx)
    lb_sm = jax.nn.softmax(hgrn_lb_logits.astype(F32), axis=0)
    lower_bounds = jnp.cumsum(lb_sm, axis=0) - lb_sm[0]

    c_rows = jnp.zeros((8, d), F32).at[:bsz].set(c).at[bsz].set(c_ctx)
    xa = jnp.concatenate([x, ctx], axis=1)

    def mod_rows(mod, idx):
        part = mod[:, idx * d:(idx + 1) * d]
        both = jnp.stack([part[:bsz], jnp.broadcast_to(part[bsz], (bsz, d))], axis=1)
        return both.reshape(2 * bsz, 1, d)

    mods = []
    for l in range(depth):
        mod = ada_modulation(c_rows, ada_w, ada_b, l)
        mods.append([mod_rows(mod, i) for i in range(6)])

    w_branch_bf, w_out_bf, w_down_bf = (w.astype(BF16) for w in (w_branch, w_out, ffn_w_down))

    h = norm_modulate(xa, norm_mix_pre[0], mods[0][0], mods[0][1], n_lat)
    for l in range(depth):
        with_ctx = l < depth - 1
        sh1, sc1, g1, sh2, sc2, g2 = mods[l]
        h2d = h.reshape(r, d)
        rows = s if with_ctx else n_lat

        w_a, w_b, w_c, w_d, w_g, q_rank, kv_rank = _split_w_in(w_in, l, d, heads)
        pa = matmul(h2d, w_a, BF16, "in_proj_a")
        pb = matmul(h2d, w_b, BF16, "in_proj_b")
        pc = matmul(h2d, w_c, BF16, "in_proj_c")
        pd = matmul(h2d, w_d, BF16, "in_proj_d")
        if with_ctx:
            pg = matmul(h2d, w_g, BF16, "in_proj_gate")
        else:
            pg = matmul_leading_rows(h, w_g, rows, BF16, "in_proj_gate")

        lam_init = 0.8 - 0.6 * math.exp(-0.3 * l)
        extras_a = [diff_lambda[l].astype(F32), diff_subln[l].reshape(1, HEAD_DIM).astype(F32)]

        def call_a(q_range, k_range, tag):
            return _attention_call(
                functools.partial(_diff_attn_kernel, lam_init=lam_init),
                [(pa, 0)], [(pa, heads, 1, False)], (pa, 2 * heads, 1), extras_a,
                bsz=bsz, heads=heads, s=s, q_range=q_range, k_range=k_range, tq=min(1024, q_range[1]), lanes=256,
                name="diff_attn_" + tag, q_row_tables=tables, k_row_tables=tables)

        oa = _attend_lat_and_ctx(call_a, n_lat, n_ctx, with_ctx)

        qn, cn, kr = mla_prep(pb, mla_q_norm[l], mla_kv_norm[l], tables, s)
        qb = matmul(qn, _mla_q_weight(mla_w_qb[l], heads), BF16, "mla_q_up")
        kvb = matmul(cn, mla_w_kvb[l].astype(BF16), BF16, "mla_kv_up")

        def call_b(q_range, k_range, tag):
            return _attention_call(
                functools.partial(_sdpa_kernel, n_qk=2, scale=(MLA_NOPE + MLA_ROPE) ** -0.5, rope_q=1),
                [(qb, 0), (qb, heads)], [(kvb, 0, 2, False), (kr, 0, 0, True)], (kvb, 1, 2), [],
                bsz=bsz, heads=heads, s=s, q_range=q_range, k_range=k_range, tq=min(2048, q_range[1]), lanes=256,
                name="mla_attn_" + tag, q_row_tables=tables)

        ob = _attend_lat_and_ctx(call_b, n_lat, n_ctx, with_ctx)

        o_fwd = hgrn_scan(pc, lower_bounds[l, 0], bsz, heads, n_lat, n_ctx, reverse=False)
        o_bwd = hgrn_scan(pc, lower_bounds[l, 1], bsz, heads, n_lat, n_ctx, reverse=True)
        oc = hgrn_readout(o_fwd, o_bwd, pc, hgrn_norm[l], bsz, s, rows)

        od = neighbourhood_attention(pd, _na_bias_table(na_rpb[l], n_lat // GRID_W), bsz, heads, n_lat, n_ctx)
        if with_ctx:
            od_ctx = _attention_call(
                functools.partial(_sdpa_kernel, n_qk=1, scale=HEAD_DIM ** -0.5, rope_q=None),
                [(pd, 0)], [(pd, heads, 1, False)], (pd, 2 * heads, 1), [],
                bsz=bsz, heads=heads, s=s, q_range=(n_lat, n_ctx), k_range=(n_lat, n_ctx), tq=256, lanes=256,
                name="na_ctx_attn").reshape(bsz, n_ctx, mix)
            od = jnp.concatenate([od.reshape(bsz, n_lat, mix), od_ctx], axis=1).reshape(r, mix)

        merged = merge_branches((oa, ob, oc, od), pg, w_branch_bf, l)
        mix_out = matmul(merged, w_out_bf, BF16, "out_proj", layer=l).reshape(bsz, rows, d)
        xa, h2 = residual_norm_modulate(xa, mix_out, g1, norm_mix_post[l], norm_ffn_pre[l], sh2, sc2, n_lat)

        gv = ffn_up(h2.reshape(bsz * rows, d), ffn_up_weight(ffn_w_up, l, FFN_BN),
                    _ffn_tile_pairs(ffn_conv_w[l], FFN_BN), _ffn_tile_pairs(ffn_conv_b[l][None], FFN_BN),
                    rows, n_lat)
        ffn = matmul(gv, w_down_bf, BF16, "ffn_down", layer=l, weights_outer=True).reshape(bsz, rows, d)
        if with_ctx:
            xa, h = residual_norm_modulate(xa, ffn, g2, norm_ffn_post[l], norm_mix_pre[l + 1],
                                           mods[l + 1][0], mods[l + 1][1], n_lat)
        else:
            xa = residual(xa, ffn, g2, norm_ffn_post[l], n_lat)
    return xa
```
